```python
import jax
import jax.numpy as jnp
from jax import lax
import numpy as np

D_MODEL = 1024
BATCH = 4
SEQ = 4096
DEPTH = 4

N_BRANCHES = 4
BRANCH_WIDTH = D_MODEL // 4
HEAD_DIM = 64
N_HEADS = BRANCH_WIDTH // HEAD_DIM
HGRN_CHUNK = 64
IDX_HEADS = 8
IDX_DIM = 32
TOPK_MAX = 256
Q_BLOCK = 128
ROPE_THETA = 10000.0
RWKV_W_LORA = 32
RWKV_A_LORA = 32
RWKV_V_LORA = 32
RWKV_G_LORA = 64
RWKV_COLS = 3 * BRANCH_WIDTH + RWKV_W_LORA + RWKV_A_LORA + RWKV_G_LORA
D_FF = 4 * D_MODEL
NORM_EPS = 1e-6
RWKV_GN_EPS = 64e-5
MASK_VALUE = -1e30
COL_SIZES = (
    BRANCH_WIDTH, BRANCH_WIDTH, BRANCH_WIDTH, BRANCH_WIDTH,
    BRANCH_WIDTH, BRANCH_WIDTH, BRANCH_WIDTH,
    IDX_HEADS * IDX_DIM, IDX_DIM, IDX_HEADS,
    RWKV_COLS,
    BRANCH_WIDTH, BRANCH_WIDTH, BRANCH_WIDTH, N_HEADS,
    N_BRANCHES * D_MODEL,
)
IN_COLS = sum(COL_SIZES)

kernel_name = 'hybrid_gated_hgrn2_dsa_rwkv7_fox'

F32 = jnp.float32


def rms_norm(t, g, eps=NORM_EPS):
    tf = t.astype(F32)
    y = tf * lax.rsqrt(jnp.mean(tf * tf, axis=-1, keepdims=True) + eps)
    return (y * g.astype(F32)).astype(t.dtype)


def split_cols(t, sizes):
    out, off = [], 0
    for s in sizes:
        out.append(t[..., off:off + s])
        off += s
    return out


def rope_tables(seq_len, dim):
    inv = 1.0 / (ROPE_THETA ** (jnp.arange(0, dim, 2, dtype=F32) / dim))
    ang = jnp.arange(seq_len, dtype=F32)[:, None] * inv[None, :]
    return jnp.cos(ang), jnp.sin(ang)


def apply_rope(t, cos, sin):
    half = t.shape[-1] // 2
    tf = t.astype(F32)
    t1, t2 = tf[..., :half], tf[..., half:]
    c, s = cos[None, :, None, :], sin[None, :, None, :]
    return jnp.concatenate([t1 * c - t2 * s, t2 * c + t1 * s], axis=-1).astype(t.dtype)


def hgrn2_mixer(q, f_logit, i_in, g_out, lb, norm_w):
    dt = q.dtype
    B_, T, _ = q.shape
    H, d, C = N_HEADS, HEAD_DIM, HGRN_CHUNK
    nc = T // C
    fl = f_logit.astype(F32)
    f = lb + (1.0 - lb) * jax.nn.sigmoid(fl)
    logf = jnp.log(f)
    k = (1.0 - lb) * jax.nn.sigmoid(-fl)

    def to_chunks(t):
        return t.astype(F32).reshape(B_, nc, C, H, d).transpose(1, 0, 3, 2, 4)

    qc = to_chunks(q) * (d ** -0.5)
    kc, vc, gc = to_chunks(k), to_chunks(i_in), to_chunks(logf)
    causal = jnp.tril(jnp.ones((C, C), dtype=bool))[:, :, None]

    def step(S, inp):
        qb, kb, vb, gb = inp
        b = jnp.cumsum(gb, axis=2)
        diff = b[:, :, :, None, :] - b[:, :, None, :, :]
        decay = jnp.where(causal, jnp.exp(jnp.where(causal, diff, 0.0)), 0.0)
        attn = jnp.einsum('bhtd,bhtsd,bhsd->bhts', qb, decay, kb)
        o = jnp.einsum('bhts,bhsv->bhtv', attn, vb) + jnp.einsum('bhtd,bhdv->bhtv', qb * jnp.exp(b), S)
        b_last = b[:, :, -1:, :]
        S = S * jnp.exp(b_last[:, :, 0, :, None]) + jnp.einsum('bhsd,bhsv->bhdv', kb * jnp.exp(b_last - b), vb)
        return S, o

    S0 = jnp.zeros((B_, H, d, d), F32)
    _, o = lax.scan(step, S0, (qc, kc, vc, gc))
    o = o.transpose(1, 0, 3, 2, 4).reshape(B_, T, H, d)
    o = o * lax.rsqrt(jnp.mean(o * o, axis=-1, keepdims=True) + NORM_EPS) * norm_w.astype(F32)
    gate = jax.nn.silu(g_out.astype(F32)).reshape(B_, T, H, d)
    return (o * gate).reshape(B_, T, H * d).astype(dt)


def dsa_mixer(q, k, v, q_idx, k_idx, w_idx, cos_h, sin_h, cos_i, sin_i):
    dt = q.dtype
    B_, T, _ = q.shape
    H, d = N_HEADS, HEAD_DIM
    qh = apply_rope(q.reshape(B_, T, H, d), cos_h, sin_h)
    kh = apply_rope(k.reshape(B_, T, H, d), cos_h, sin_h)
    vh = v.reshape(B_, T, H, d)
    qi = apply_rope(q_idx.reshape(B_, T, IDX_HEADS, IDX_DIM), cos_i, sin_i).astype(F32) * (IDX_DIM ** -0.5)
    ki = apply_rope(k_idx[:, :, None, :], cos_i, sin_i)[:, :, 0, :].astype(F32)
    wi = w_idx.astype(F32) * (IDX_HEADS ** -0.5)
    topk = min(TOPK_MAX, T // 4)
    key_pos = jnp.arange(T)

    def block(bi):
        start = bi * Q_BLOCK
        q_pos = start + jnp.arange(Q_BLOCK)
        qib = lax.dynamic_slice_in_dim(qi, start, Q_BLOCK, axis=1)
        wib = lax.dynamic_slice_in_dim(wi, start, Q_BLOCK, axis=1)
        qb = lax.dynamic_slice_in_dim(qh, start, Q_BLOCK, axis=1)
        score = jnp.einsum('bqhd,bsd->bqhs', qib, ki)
        score = jnp.einsum('bqhs,bqh->bqs', jax.nn.relu(score), wib)
        score = jnp.where((key_pos[None, :] <= q_pos[:, None])[None], score, MASK_VALUE)
        _, idx = lax.top_k(score, topk)
        valid = idx <= q_pos[None, :, None]
        k_sel = jax.vmap(lambda kk_, ii: kk_[ii])(kh, idx)
        v_sel = jax.vmap(lambda vv_, ii: vv_[ii])(vh, idx)
        logits = jnp.einsum('bqhd,bqkhd->bqhk', qb, k_sel).astype(F32) * (d ** -0.5)
        logits = jnp.where(valid[:, :, None, :], logits, MASK_VALUE)
        p = jax.nn.softmax(logits, axis=-1)
        return jnp.einsum('bqhk,bqkhd->bqhd', p.astype(v_sel.dtype), v_sel)

    out = lax.map(block, jnp.arange(T // Q_BLOCK))
    return out.transpose(1, 0, 2, 3, 4).reshape(B_, T, H * d).astype(dt)


def rwkv7_mixer(pc, mu, w0, w2, a0, a2, g2, k_k, k_a, r_k, ln_w, ln_b, v_first, v_gate):
    dt = pc.dtype
    B_, T, _ = pc.shape
    H, N = N_HEADS, HEAD_DIM
    p = pc.astype(F32)
    prev = jnp.pad(p, ((0, 0), (1, 0), (0, 0)))[:, :T]
    xm = p + (prev - p) * mu.astype(F32)
    r, k, v, w_in, a_in, g_in = split_cols(
        xm, (BRANCH_WIDTH, BRANCH_WIDTH, BRANCH_WIDTH, RWKV_W_LORA, RWKV_A_LORA, RWKV_G_LORA))
    w_log = -jax.nn.softplus(-(w0 + jnp.tanh(w_in) @ w2.astype(F32))) - 0.5
    decay = jnp.exp(-jnp.exp(w_log))
    a = jax.nn.sigmoid(a0 + a_in @ a2.astype(F32))
    g = jax.nn.sigmoid(g_in) @ g2.astype(F32)
    if v_gate is not None:
        v0, v1, v2 = v_gate
        v = v + (v_first - v) * jax.nn.sigmoid(v0 + (v @ v1.astype(F32)) @ v2.astype(F32))

    def heads(t):
        return t.reshape(B_, T, H, N)

    kk = heads(k * k_k)
    kk = kk * lax.rsqrt(jnp.maximum(jnp.sum(kk * kk, axis=-1, keepdims=True), 1e-24))
    k = k * (1.0 + (a - 1.0) * k_a)
    rh, wh, kh, vh, ah = heads(r), heads(decay), heads(k), heads(v), heads(a)
    xs = tuple(t.transpose(1, 0, 2, 3) for t in (rh, wh, kh, vh, kk, ah))

    def step(S, inp):
        rt, wt, kt, vt, kkt, at = inp
        sa = jnp.einsum('bhvi,bhi->bhv', S, -kkt)
        S = S * wt[:, :, None, :] + sa[..., None] * (kkt * at)[:, :, None, :] + vt[..., None] * kt[:, :, None, :]
        return S, jnp.einsum('bhvj,bhj->bhv', S, rt)

    S0 = jnp.zeros((B_, H, N, N), F32)
    _, y = lax.scan(step, S0, xs)
    y = y.transpose(1, 0, 2, 3)
    mean = jnp.mean(y, axis=-1, keepdims=True)
    var = jnp.mean(jnp.square(y - mean), axis=-1, keepdims=True)
    y = (y - mean) * lax.rsqrt(var + RWKV_GN_EPS) * ln_w.reshape(H, N) + ln_b.reshape(H, N)
    y = y + jnp.sum(rh * kh * r_k, axis=-1, keepdims=True) * vh
    return (y.reshape(B_, T, H * N) * g).astype(dt), v


def fox_mixer(q, k, v, f_logit, f_bias):
    dt = q.dtype
    B_, T, _ = q.shape
    H, d = N_HEADS, HEAD_DIM
    qh, kh, vh = q.reshape(B_, T, H, d), k.reshape(B_, T, H, d), v.reshape(B_, T, H, d)
    logf = jax.nn.log_sigmoid(f_logit.astype(F32) + f_bias.astype(F32))
    c = jnp.cumsum(logf, axis=1).transpose(0, 2, 1)
    key_pos = jnp.arange(T)

    def block(bi):
        start = bi * Q_BLOCK
        q_pos = start + jnp.arange(Q_BLOCK)
        qb = lax.dynamic_slice_in_dim(qh, start, Q_BLOCK, axis=1)
        cb = lax.dynamic_slice_in_dim(c, start, Q_BLOCK, axis=2)
        logits = jnp.einsum('bqhd,bshd->bhqs', qb, kh).astype(F32) * (d ** -0.5)
        logits = logits + cb[..., None] - c[:, :, None, :]
        logits = jnp.where(key_pos[None, :] <= q_pos[:, None], logits, MASK_VALUE)
        p = jax.nn.softmax(logits, axis=-1)
        return jnp.einsum('bhqs,bshd->bqhd', p.astype(vh.dtype), vh)

    out = lax.map(block, jnp.arange(T // Q_BLOCK))
    return out.transpose(1, 0, 2, 3, 4).reshape(B_, T, H * d).astype(dt)


def setup_inputs(seed: int = 0) -> dict:
    key = jax.random.key(seed)
    ks = jax.random.split(key, 32)
    W = BRANCH_WIDTH

    def nrm(k, shape, scale):
        return scale * jax.random.normal(k, shape, F32)

    def gain(k, n):
        return 1.0 + nrm(k, (DEPTH, n), 0.02)

    return {
        'x': nrm(ks[0], (BATCH, SEQ, D_MODEL), 1.0),
        'norm_mix_pre': gain(ks[1], D_MODEL),
        'norm_mix_post': gain(ks[2], D_MODEL),
        'norm_mlp_pre': gain(ks[3], D_MODEL),
        'norm_mlp_post': gain(ks[4], D_MODEL),
        'w_in': nrm(ks[5], (DEPTH, D_MODEL, IN_COLS), D_MODEL ** -0.5),
        'w_branch': nrm(ks[6], (DEPTH, N_BRANCHES, W, D_MODEL), W ** -0.5),
        'w_out': nrm(ks[7], (DEPTH, D_MODEL, D_MODEL), D_MODEL ** -0.5),
        'hgrn_lb_logits': nrm(ks[8], (DEPTH, W), 1.0),
        'hgrn_norm_w': gain(ks[9], HEAD_DIM),
        'fox_f_bias': 2.0 + nrm(ks[10], (DEPTH, N_HEADS), 0.5),
        'rwkv_mu': jax.random.uniform(ks[11], (DEPTH, RWKV_COLS), F32),
        'rwkv_w0': -3.0 + nrm(ks[12], (DEPTH, W), 1.0),
        'rwkv_w2': nrm(ks[13], (DEPTH, RWKV_W_LORA, W), 0.5 * RWKV_W_LORA ** -0.5),
        'rwkv_a0': nrm(ks[14], (DEPTH, W), 0.5),
        'rwkv_a2': nrm(ks[15], (DEPTH, RWKV_A_LORA, W), RWKV_A_LORA ** -0.5),
        'rwkv_g2': nrm(ks[16], (DEPTH, RWKV_G_LORA, W), RWKV_G_LORA ** -0.5),
        'rwkv_k_k': 0.85 + nrm(ks[17], (DEPTH, W), 0.05),
        'rwkv_k_a': 1.0 + nrm(ks[18], (DEPTH, W), 0.05),
        'rwkv_r_k': nrm(ks[19], (DEPTH, N_HEADS, HEAD_DIM), 0.1),
        'rwkv_ln_w': gain(ks[20], W),
        'rwkv_ln_b': nrm(ks[21], (DEPTH, W), 0.02),
        'rwkv_v0': nrm(ks[22], (DEPTH - 1, W), 0.5),
        'rwkv_v1': nrm(ks[23], (DEPTH - 1, W, RWKV_V_LORA), W ** -0.5),
        'rwkv_v2': nrm(ks[24], (DEPTH - 1, RWKV_V_LORA, W), RWKV_V_LORA ** -0.5),
        'w_up': nrm(ks[25], (DEPTH, D_MODEL, D_FF), D_MODEL ** -0.5),
        'w_down': nrm(ks[26], (DEPTH, D_FF, D_MODEL), D_FF ** -0.5),
    }


def reference(x, norm_mix_pre, norm_mix_post, norm_mlp_pre, norm_mlp_post, w_in, w_branch, w_out,
              hgrn_lb_logits, hgrn_norm_w, fox_f_bias, rwkv_mu, rwkv_w0, rwkv_w2, rwkv_a0, rwkv_a2,
              rwkv_g2, rwkv_k_k, rwkv_k_a, rwkv_r_k, rwkv_ln_w, rwkv_ln_b, rwkv_v0, rwkv_v1, rwkv_v2,
              w_up, w_down):
    B_, T, _ = x.shape
    cos_h, sin_h = rope_tables(T, HEAD_DIM)
    cos_i, sin_i = rope_tables(T, IDX_DIM)
    lb_soft = jax.nn.softmax(hgrn_lb_logits.astype(F32), axis=0)
    lower_bounds = jnp.cumsum(lb_soft, axis=0) - lb_soft[0:1]
    v_first = None
    for l in range(DEPTH):
        h = rms_norm(x, norm_mix_pre[l])
        proj = h @ w_in[l]
        (a_q, a_f, a_i, a_g, b_q, b_k, b_v, b_iq, b_ik, b_iw, c_all,
         d_q, d_k, d_v, d_f, gate_logits) = split_cols(proj, COL_SIZES)
        y_a = hgrn2_mixer(a_q, a_f, a_i, a_g, lower_bounds[l], hgrn_norm_w[l])
        y_b = dsa_mixer(b_q, b_k, b_v, b_iq, b_ik, b_iw, cos_h, sin_h, cos_i, sin_i)
        v_gate = None if l == 0 else (rwkv_v0[l - 1], rwkv_v1[l - 1], rwkv_v2[l - 1])
        y_c, v_c = rwkv7_mixer(c_all, rwkv_mu[l], rwkv_w0[l], rwkv_w2[l], rwkv_a0[l], rwkv_a2[l], rwkv_g2[l],
                               rwkv_k_k[l], rwkv_k_a[l], rwkv_r_k[l], rwkv_ln_w[l], rwkv_ln_b[l],
                               v_first, v_gate)
        if l == 0:
            v_first = v_c
        y_d = fox_mixer(d_q, d_k, d_v, d_f, fox_f_bias[l])
        gates = jax.nn.sigmoid(gate_logits.astype(F32)).reshape(B_, T, N_BRANCHES, D_MODEL)
        merged = (gates[:, :, 0] * (y_a @ w_branch[l, 0]).astype(F32)
                  + gates[:, :, 1] * (y_b @ w_branch[l, 1]).astype(F32)
                  + gates[:, :, 2] * (y_c @ w_branch[l, 2]).astype(F32)
                  + gates[:, :, 3] * (y_d @ w_branch[l, 3]).astype(F32))
        mix = merged.astype(x.dtype) @ w_out[l]
        x = x + rms_norm(mix, norm_mix_post[l])
        hm = rms_norm(x, norm_mlp_pre[l])
        u = jax.nn.relu(hm @ w_up[l])
        x = x + rms_norm(jnp.square(u) @ w_down[l], norm_mlp_post[l])
    return x
```

```python
import functools

import jax
import jax.numpy as jnp
from jax import lax
from jax.experimental import pallas as pl
from jax.experimental.pallas import tpu as pltpu

F32 = jnp.float32
BF16 = jnp.bfloat16

HEAD_DIM = 64
N_HEADS = 4
WIDTH = N_HEADS * HEAD_DIM
IDX_HEADS = 8
IDX_DIM = 32
TOPK_MAX = 256
ROPE_THETA = 10000.0
NORM_EPS = 1e-6
RWKV_GN_EPS = 64e-5
MASK_VALUE = -1e30
RWKV_LORA_LANES = 128
RWKV_COLS = 3 * WIDTH + RWKV_LORA_LANES

LANES = 128
VMEM_LIMIT = 48 * 1024 * 1024

NN = ((1,), (0,))
NT = ((1,), (1,))
TN = ((0,), (0,))


def _dot(a, b, dims=NN):
    return lax.dot_general(a, b, (dims, ((), ())), preferred_element_type=F32)


def _split2(x):
    hi = x.astype(BF16)
    lo = (x - hi.astype(F32)).astype(BF16)
    return hi, lo


def _split3(x):
    hi = x.astype(BF16)
    r1 = x - hi.astype(F32)
    mid = r1.astype(BF16)
    lo = (r1 - mid.astype(F32)).astype(BF16)
    return hi, mid, lo


def _dot3(a, b, dims=NN):
    ah, al = _split2(a)
    bh, bl = _split2(b)
    return _dot(ah, bh, dims) + (_dot(ah, bl, dims) + _dot(al, bh, dims))


def _dot_exact_lhs(a_bf16, b, dims=NN):
    b1, b2, b3 = _split3(b)
    return _dot(a_bf16, b1, dims) + (_dot(a_bf16, b2, dims) + _dot(a_bf16, b3, dims))


def _dot_exact_rhs(a, b_bf16, dims=NN):
    a1, a2 = _split2(a)
    return _dot(a1, b_bf16, dims) + _dot(a2, b_bf16, dims)


def _iota(shape, dim):
    return lax.broadcasted_iota(jnp.int32, shape, dim)


def _head_block_ones(n, group):
    return (_iota((n, n), 0) // group == _iota((n, n), 1) // group).astype(BF16)


def _lower_tri_ones(n):
    return (_iota((n, n), 0) >= _iota((n, n), 1)).astype(BF16)


def _sigmoid(x):
    return 1.0 / (1.0 + jnp.exp(-x))


def _softplus(x):
    return jnp.maximum(x, 0.0) + jnp.log(1.0 + jnp.exp(-jnp.abs(x)))


def _params(sem):
    return pltpu.CompilerParams(dimension_semantics=sem, vmem_limit_bytes=VMEM_LIMIT)


def _norm_rows(x, g):
    return x * lax.rsqrt(jnp.mean(x * x, axis=-1, keepdims=True) + NORM_EPS) * g


def _norm_matmul_kernel(x_ref, g_ref, w_ref, o_ref, h_ref):
    @pl.when(pl.program_id(1) == 0)
    def _():
        h_ref[...] = _norm_rows(x_ref[...], g_ref[...]).astype(BF16)

    o_ref[...] = _dot(h_ref[...], w_ref[...]).astype(o_ref.dtype)


def _norm_matmul3_kernel(x_ref, g_ref, wh_ref, wl_ref, o_ref, hh_ref, hl_ref):
    @pl.when(pl.program_id(1) == 0)
    def _():
        hh, hl = _split2(_norm_rows(x_ref[...], g_ref[...]))
        hh_ref[...] = hh
        hl_ref[...] = hl

    o_ref[...] = (_dot(hh_ref[...], wh_ref[...])
                  + (_dot(hh_ref[...], wl_ref[...]) + _dot(hl_ref[...], wh_ref[...]))).astype(o_ref.dtype)


def _norm_matmul(x2, g, w, out_dtype, tm, tn, precise=False):
    n, d = x2.shape
    cols = w.shape[1]
    grid = (n // tm, cols // tn)
    x_spec = pl.BlockSpec((tm, d), lambda i, j: (i, 0))
    g_spec = pl.BlockSpec((1, d), lambda i, j: (0, 0))
    w_spec = pl.BlockSpec((d, tn), lambda i, j: (0, j))
    o_spec = pl.BlockSpec((tm, tn), lambda i, j: (i, j))
    g2 = g.reshape(1, d)
    if precise:
        wh, wl = _split2(w)
        return pl.pallas_call(
            _norm_matmul3_kernel, grid=grid,
            in_specs=[x_spec, g_spec, w_spec, w_spec], out_specs=o_spec,
            out_shape=jax.ShapeDtypeStruct((n, cols), out_dtype),
            scratch_shapes=[pltpu.VMEM((tm, d), BF16), pltpu.VMEM((tm, d), BF16)],
            compiler_params=_params(("parallel", "arbitrary")), name="norm_proj_precise",
        )(x2, g2, wh, wl)
    return pl.pallas_call(
        _norm_matmul_kernel, grid=grid,
        in_specs=[x_spec, g_spec, w_spec], out_specs=o_spec,
        out_shape=jax.ShapeDtypeStruct((n, cols), out_dtype),
        scratch_shapes=[pltpu.VMEM((tm, d), BF16)],
        compiler_params=_params(("parallel", "arbitrary")), name="norm_proj",
    )(x2, g2, w.astype(BF16))


HGRN_CHUNK = 64
HGRN_GROUP = 8


def _hgrn_kernel(q_ref, f_ref, i_ref, g_ref, lb_ref, nw_ref, o_ref, st_ref, b_sc, k_sc, v_sc,
                 *, n_chunks):
    C = HGRN_CHUNK

    @pl.when(pl.program_id(1) == 0)
    def _():
        st_ref[...] = jnp.zeros_like(st_ref)

    lb = lb_ref[...]
    nw = nw_ref[...]
    tri = _lower_tri_ones(C)
    bd = _head_block_ones(WIDTH, HEAD_DIM)
    bd_mask = _iota((WIDTH, WIDTH), 0) // HEAD_DIM == _iota((WIDTH, WIDTH), 1) // HEAD_DIM
    row = _iota((C, 1), 0)

    for c in range(n_chunks):
        sl = pl.ds(c * C, C)
        fl = f_ref[0, sl, :]
        f = lb + (1.0 - lb) * _sigmoid(fl)
        k = (1.0 - lb) * _sigmoid(-fl)
        b = _dot_exact_lhs(tri, jnp.log(f))
        q = q_ref[0, sl, :] * (HEAD_DIM ** -0.5)
        v = i_ref[0, sl, :]
        b_sc[...] = b
        k_sc[...] = k
        v_sc[...] = v
        st = st_ref[...]

        o = _dot3(q * jnp.exp(b), st, NT)

        def group(gi, o):
            s0 = gi * HGRN_GROUP
            xs = []
            for u in range(HGRN_GROUP):
                bs = b_sc[pl.ds(s0 + u, 1), :]
                ks = k_sc[pl.ds(s0 + u, 1), :]
                e = jnp.exp(jnp.minimum(b - bs, 0.0))
                xs.append(jnp.where(row >= s0 + u, q * ks * e, 0.0))
            x = jnp.concatenate(xs, axis=0)
            r = _dot_exact_rhs(x, bd)
            for u in range(HGRN_GROUP):
                vs = v_sc[pl.ds(s0 + u, 1), :]
                o = o + r[u * C:(u + 1) * C] * vs
            return o

        o = lax.fori_loop(0, C // HGRN_GROUP, group, o)

        b_last = b[C - 1:C, :]
        upd = _dot3(v, k * jnp.exp(b_last - b), TN)
        st_ref[...] = st * jnp.exp(b_last) + jnp.where(bd_mask, upd, 0.0)

        ms = _dot_exact_rhs(o * o, bd) * (1.0 / HEAD_DIM)
        on = o * lax.rsqrt(ms + NORM_EPS) * nw
        gl = g_ref[0, sl, :]
        o_ref[0, sl, :] = (on * (gl * _sigmoid(gl))).astype(o_ref.dtype)


def _hgrn(main3, lb, norm_w, tb=256):
    bsz, t, _ = main3.shape
    col = lambda j: pl.BlockSpec((1, tb, WIDTH), lambda b, i, j=j: (b, i, j))
    vec = pl.BlockSpec((1, WIDTH), lambda b, i: (0, 0))
    return pl.pallas_call(
        functools.partial(_hgrn_kernel, n_chunks=tb // HGRN_CHUNK),
        grid=(bsz, t // tb),
        in_specs=[col(0), col(1), col(2), col(3), vec, vec],
        out_specs=pl.BlockSpec((1, tb, WIDTH), lambda b, i: (b, i, 0)),
        out_shape=jax.ShapeDtypeStruct((bsz, t, WIDTH), BF16),
        scratch_shapes=[pltpu.VMEM((WIDTH, WIDTH), F32)] + [pltpu.VMEM((HGRN_CHUNK, WIDTH), F32)] * 3,
        compiler_params=_params(("parallel", "arbitrary")), name="hgrn2",
    )(main3, main3, main3, main3, lb.reshape(1, WIDTH), jnp.tile(norm_w, N_HEADS).reshape(1, WIDTH))


RWKV_CHUNK = 64


def _rwkv_kernel(*refs, n_chunks, has_vgate):
    if has_vgate:
        (c_ref, vf_ref, mu_ref, w0_ref, w2_ref, a0_ref, a2_ref, g2_ref, kkw_ref, kaw_ref, rk_ref,
         lnw_ref, lnb_ref, v0_ref, v1_ref, v2_ref, y_ref, vout_ref, s_ref, prev_ref, y_sc) = refs
    else:
        (c_ref, mu_ref, w0_ref, w2_ref, a0_ref, a2_ref, g2_ref, kkw_ref, kaw_ref, rk_ref,
         lnw_ref, lnb_ref, y_ref, vout_ref, s_ref, prev_ref, y_sc) = refs
    C = RWKV_CHUNK
    W = WIDTH
    tb = n_chunks * C

    @pl.when(pl.program_id(1) == 0)
    def _():
        s_ref[...] = jnp.zeros_like(s_ref)
        prev_ref[...] = jnp.zeros_like(prev_ref)

    p = c_ref[0]
    shifted = jnp.where(_iota((tb, 1), 0) == 0, prev_ref[...], pltpu.roll(p, 1, axis=0))
    prev_ref[...] = p[tb - 1:tb, :]
    xm = p + (shifted - p) * mu_ref[...]
    r = xm[:, 0:W]
    k = xm[:, W:2 * W]
    v = xm[:, 2 * W:3 * W]
    lora = xm[:, 3 * W:3 * W + RWKV_LORA_LANES]

    bd = _head_block_ones(W, HEAD_DIM)
    w_log = -_softplus(-(w0_ref[...] + _dot3(jnp.tanh(lora), w2_ref[...]))) - 0.5
    log_decay = -jnp.exp(w_log)
    a = _sigmoid(a0_ref[...] + _dot3(lora, a2_ref[...]))
    g = _dot3(_sigmoid(lora), g2_ref[...])
    if has_vgate:
        vg = _dot3(_dot3(v, v1_ref[...]), v2_ref[...])
        v = v + (vf_ref[0] - v) * _sigmoid(v0_ref[...] + vg)
    vout_ref[0] = v
    kk = k * kkw_ref[...]
    kk = kk * lax.rsqrt(jnp.maximum(_dot_exact_rhs(kk * kk, bd), 1e-24))
    k = k * (1.0 + (a - 1.0) * kaw_ref[...])

    tri = _lower_tri_ones(C)
    lane_head = _iota((1, W), 1) // HEAD_DIM
    rt = _iota((N_HEADS * C, N_HEADS * C), 0) % C
    ct = _iota((N_HEADS * C, N_HEADS * C), 1) % C
    strict_lower = rt > ct
    lower = rt >= ct

    def stack(m):
        return jnp.concatenate([jnp.where(lane_head == h, m, 0.0) for h in range(N_HEADS)], axis=0)

    for c in range(n_chunks):
        sl = slice(c * C, (c + 1) * C)
        ld = log_decay[sl]
        cum = _dot_exact_lhs(tri, ld)
        pdec = jnp.exp(cum)
        pinv = jnp.exp(-cum)
        a_s = stack(-kk[sl] * jnp.exp(cum - ld))
        b_s = stack(kk[sl] * a[sl] * pinv)
        k_s = stack(k[sl] * pinv)
        r_s = stack(r[sl] * pdec)
        v_s = stack(v[sl])
        s = s_ref[...]

        m = jnp.where(strict_lower, _dot3(a_s, b_s, NT), 0.0)
        lak = jnp.where(strict_lower, _dot3(a_s, k_s, NT), 0.0)
        lrb = jnp.where(lower, _dot3(r_s, b_s, NT), 0.0)
        lrk = jnp.where(lower, _dot3(r_s, k_s, NT), 0.0)

        z = _dot3(a_s, s, NT) + _dot3(lak, v_s)
        z = z + _dot3(m, z)
        power = 2
        while power < C:
            m = _dot3(m, m)
            z = z + _dot3(m, z)
            power *= 2

        y = _dot3(r_s, s, NT) + _dot3(lrb, z) + _dot3(lrk, v_s)
        y_sc[sl, :] = sum(y[h * C:(h + 1) * C] for h in range(N_HEADS))
        s_ref[...] = (s + _dot3(z, b_s, TN) + _dot3(v_s, k_s, TN)) * pdec[C - 1:C, :]

    y = y_sc[...]
    inv_n = 1.0 / HEAD_DIM
    mean = _dot_exact_rhs(y, bd) * inv_n
    yc = y - mean
    var = _dot_exact_rhs(yc * yc, bd) * inv_n
    yn = yc * lax.rsqrt(var + RWKV_GN_EPS) * lnw_ref[...] + lnb_ref[...]
    yn = yn + _dot_exact_rhs(r * k * rk_ref[...], bd) * v
    y_ref[0] = (yn * g).astype(y_ref.dtype)


def _rwkv(c3, col_block, v_first, prm, tb=256):
    bsz, t, _ = c3.shape
    has_vgate = v_first is not None
    blk = lambda w: pl.BlockSpec((1, tb, w), lambda b, i: (b, i, 0))
    cblk = pl.BlockSpec((1, tb, RWKV_COLS), lambda b, i: (b, i, col_block))
    full = lambda a: pl.BlockSpec(a.shape, lambda b, i: (0,) * a.ndim)
    names = ["mu", "w0", "w2", "a0", "a2", "g2", "k_k", "k_a", "r_k", "ln_w", "ln_b"]
    if has_vgate:
        names += ["v0", "v1", "v2"]
    args = [c3] + ([v_first] if has_vgate else []) + [prm[n] for n in names]
    in_specs = [cblk] + ([blk(WIDTH)] if has_vgate else []) + [full(prm[n]) for n in names]
    return pl.pallas_call(
        functools.partial(_rwkv_kernel, n_chunks=tb // RWKV_CHUNK, has_vgate=has_vgate),
        grid=(bsz, t // tb),
        in_specs=in_specs,
        out_specs=[blk(WIDTH), blk(WIDTH)],
        out_shape=[jax.ShapeDtypeStruct((bsz, t, WIDTH), BF16), jax.ShapeDtypeStruct((bsz, t, WIDTH), F32)],
        scratch_shapes=[pltpu.VMEM((WIDTH, WIDTH), F32), pltpu.VMEM((1, RWKV_COLS), F32),
                        pltpu.VMEM((tb, WIDTH), F32)],
        compiler_params=_params(("parallel", "arbitrary")), name="rwkv7",
    )(*args)


def _fox_cum_kernel(f_ref, bias_ref, c_ref, carry_ref, *, tb):
    @pl.when(pl.program_id(1) == 0)
    def _():
        carry_ref[...] = jnp.zeros_like(carry_ref)

    logf = -_softplus(-(f_ref[0] + bias_ref[...]))
    c = _dot_exact_lhs(_lower_tri_ones(tb), logf) + carry_ref[...]
    c_ref[0] = c
    carry_ref[...] = c[tb - 1:tb, :]


def _fox_cum(main3, col_block, bias_row, tb=256):
    bsz, t, _ = main3.shape
    return pl.pallas_call(
        functools.partial(_fox_cum_kernel, tb=tb),
        grid=(bsz, t // tb),
        in_specs=[pl.BlockSpec((1, tb, LANES), lambda b, i: (b, i, col_block)),
                  pl.BlockSpec((1, LANES), lambda b, i: (0, 0))],
        out_specs=pl.BlockSpec((1, tb, LANES), lambda b, i: (b, i, 0)),
        out_shape=jax.ShapeDtypeStruct((bsz, t, LANES), F32),
        scratch_shapes=[pltpu.VMEM((1, LANES), F32)],
        compiler_params=_params(("parallel", "arbitrary")), name="fox_cumgate",
    )(main3, bias_row)


def _fox_kernel(q_ref, k_ref, v_ref, cq_ref, ck_ref, o_ref, *, tq):
    i = pl.program_id(1)
    q = (q_ref[0].astype(F32) * (HEAD_DIM ** -0.5)).astype(BF16)
    cq_all = cq_ref[0]
    qpos = i * tq + _iota((tq, 1), 0)
    outs = []
    for h in range(N_HEADS):
        hs = slice(h * HEAD_DIM, (h + 1) * HEAD_DIM)
        qh = q[:, hs]
        cq = cq_all[:, h:h + 1]

        def body(j, carry, hs=hs, qh=qh, cq=cq, h=h):
            m, l, acc = carry
            ks = pl.ds(pl.multiple_of(j * tq, tq), tq)
            s = _dot(qh, k_ref[0, ks, hs], NT) + (cq - ck_ref[0, h:h + 1, ks])
            s = jnp.where(j * tq + _iota((1, tq), 1) <= qpos, s, MASK_VALUE)
            m_new = jnp.maximum(m, jnp.max(s, axis=-1, keepdims=True))
            alpha = jnp.exp(m - m_new)
            pr = jnp.exp(s - m_new)
            l = alpha * l + jnp.sum(pr, axis=-1, keepdims=True)
            acc = alpha * acc + _dot(pr.astype(BF16), v_ref[0, ks, hs])
            return m_new, l, acc

        init = (jnp.full((tq, 1), MASK_VALUE, F32), jnp.zeros((tq, 1), F32), jnp.zeros((tq, HEAD_DIM), F32))
        _, l, acc = lax.fori_loop(0, i + 1, body, init)
        outs.append(acc / l)
    o_ref[0] = jnp.concatenate(outs, axis=1).astype(o_ref.dtype)


def _fox(qkv3, c_col, c_row, tq=256):
    bsz, t, _ = qkv3.shape
    return pl.pallas_call(
        functools.partial(_fox_kernel, tq=tq),
        grid=(bsz, t // tq),
        in_specs=[pl.BlockSpec((1, tq, WIDTH), lambda b, i: (b, i, 0)),
                  pl.BlockSpec((1, t, WIDTH), lambda b, i: (b, 0, 1)),
                  pl.BlockSpec((1, t, WIDTH), lambda b, i: (b, 0, 2)),
                  pl.BlockSpec((1, tq, LANES), lambda b, i: (b, i, 0)),
                  pl.BlockSpec((1, 8, t), lambda b, i: (b, 0, 0))],
        out_specs=pl.BlockSpec((1, tq, WIDTH), lambda b, i: (b, i, 0)),
        out_shape=jax.ShapeDtypeStruct((bsz, t, WIDTH), BF16),
        compiler_params=_params(("parallel", "arbitrary")), name="fox_attention",
    )(qkv3, qkv3, qkv3, c_col, c_row)


DSA_QB = 128
DSA_KB = 512


def _swap_halves(x, half):
    n = x.shape[-1]
    lower = (_iota((1, n), 1) % (2 * half)) < half
    return jnp.where(lower, pltpu.roll(x, n - half, axis=1), pltpu.roll(x, half, axis=1))


def _dsa_prep_kernel(q_ref, k_ref, v_ref, iq_ref, ik_ref, iw_ref, ch_ref, sh_ref, ci_ref, si_ref,
                     qo_ref, ko_ref, vo_ref, qih_ref, qil_ref, kih_ref, kil_ref, wi_ref):
    ch, sh, ci, si = ch_ref[...], sh_ref[...], ci_ref[...], si_ref[...]

    def rope(x, c, s, half):
        return x * c + _swap_halves(x, half) * s

    qo_ref[0] = (rope(q_ref[0], ch, sh, HEAD_DIM // 2) * (HEAD_DIM ** -0.5)).astype(BF16)
    ko_ref[0] = rope(k_ref[0], ch, sh, HEAD_DIM // 2).astype(BF16)
    vo_ref[0] = v_ref[0].astype(BF16)
    qh, ql = _split2(rope(iq_ref[0], ci, si, IDX_DIM // 2) * (IDX_DIM ** -0.5))
    qih_ref[0] = qh
    qil_ref[0] = ql
    kh, kl = _split2(rope(ik_ref[0], ci, si, IDX_DIM // 2))
    kih_ref[0] = kh
    kil_ref[0] = kl
    wi_ref[0] = iw_ref[0] * (IDX_HEADS ** -0.5)


def _dsa_prep(main3, idx3, tables, qkv_block0, tb=512):
    bsz, t, _ = main3.shape
    mcol = lambda j: pl.BlockSpec((1, tb, WIDTH), lambda b, i, j=j: (b, i, qkv_block0 + j))
    icol = lambda j, w: pl.BlockSpec((1, tb, w), lambda b, i, j=j: (b, i, j))
    tab = pl.BlockSpec((tb, WIDTH), lambda b, i: (i, 0))
    out = pl.BlockSpec((1, tb, WIDTH), lambda b, i: (b, i, 0))
    shape = lambda dt, w=WIDTH: jax.ShapeDtypeStruct((bsz, t, w), dt)
    return pl.pallas_call(
        _dsa_prep_kernel, grid=(bsz, t // tb),
        in_specs=[mcol(0), mcol(1), mcol(2), icol(0, WIDTH), icol(1, WIDTH),
                  pl.BlockSpec((1, tb, LANES), lambda b, i: (b, i, 2 * WIDTH // LANES)),
                  tab, tab, tab, tab],
        out_specs=[out] * 7 + [pl.BlockSpec((1, tb, LANES), lambda b, i: (b, i, 0))],
        out_shape=[shape(BF16)] * 7 + [shape(F32, LANES)],
        compiler_params=_params(("parallel", "parallel")), name="dsa_prep",
    )(main3, main3, main3, idx3, idx3, idx3, *tables)


def _dsa_kernel(q_ref, k_ref, v_ref, qih_ref, qil_ref, wi_ref, kih_ref, kil_ref, o_ref, key_sc,
                *, topk):
    QB, KB = DSA_QB, DSA_KB
    n_lt = KB // LANES
    i = pl.program_id(1)
    nkb = (i * QB) // KB + 1
    qpos = i * QB + _iota((QB, 1), 0)
    int_min = jnp.int32(-2 ** 31)

    idx_head = _iota((1, WIDTH), 1) // IDX_DIM
    qih = qih_ref[0]
    qil = qil_ref[0]
    zero = jnp.zeros_like(qih)
    qsh = jnp.concatenate([jnp.where(idx_head == h, qih, zero) for h in range(IDX_HEADS)], axis=0)
    qsl = jnp.concatenate([jnp.where(idx_head == h, qil, zero) for h in range(IDX_HEADS)], axis=0)
    wi = wi_ref[0]
    wcols = [jnp.broadcast_to(wi[:, h:h + 1], (QB, KB)) for h in range(IDX_HEADS)]

    def score_block(j, carry):
        ks = pl.ds(pl.multiple_of(j * KB, KB), KB)
        kh = kih_ref[0, ks, :]
        kl = kil_ref[0, ks, :]
        r = _dot(qsh, kh, NT) + (_dot(qsh, kl, NT) + _dot(qsl, kh, NT))
        sc = jnp.zeros((QB, KB), F32)
        for h in range(IDX_HEADS):
            sc = sc + jnp.maximum(r[h * QB:(h + 1) * QB], 0.0) * wcols[h]
        sc = jnp.where(j * KB + _iota((1, KB), 1) <= qpos, sc, MASK_VALUE)
        sc = jnp.where(sc == 0.0, 0.0, sc)
        bits = pltpu.bitcast(sc, jnp.int32)
        key_sc[:, ks] = jnp.where(bits < 0, bits ^ jnp.int32(0x7FFFFFFF), bits)
        return carry

    lax.fori_loop(0, nkb, score_block, 0)

    ones_l = jnp.ones((LANES, LANES), BF16)

    def count_ge(cand):
        def blk(j, acc):
            base = pl.multiple_of(j * KB, KB)
            for c in range(n_lt):
                kt = key_sc[:, pl.ds(base + c * LANES, LANES)]
                acc = acc + jnp.where(kt >= cand, 1.0, 0.0)
            return acc
        acc = lax.fori_loop(0, nkb, blk, jnp.zeros((QB, LANES), F32))
        return _dot(acc.astype(BF16), ones_l)

    kf = float(topk)

    def bit_step(step, v):
        trial = jnp.where(step == 0, jnp.zeros_like(v), v | (jnp.int32(1) << (31 - step)))
        return jnp.where(count_ge(trial) >= kf, trial, v)

    thr = lax.fori_loop(0, 32, bit_step, jnp.full((QB, LANES), int_min, jnp.int32))
    need = kf - count_ge(thr + 1)

    ut = (_iota((KB, KB), 0) <= _iota((KB, KB), 1)).astype(BF16)
    ones_k = jnp.ones((KB, LANES), BF16)
    q = q_ref[0]
    thr_w = jnp.concatenate([thr] * n_lt, axis=1)
    need_w = jnp.concatenate([need] * n_lt, axis=1)

    def attend(j, carry):
        tie_seen, ms, ls, accs = carry
        ks = pl.ds(pl.multiple_of(j * KB, KB), KB)
        key = key_sc[:, ks]
        tie = key == thr_w
        tie_b = jnp.where(tie, 1.0, 0.0).astype(BF16)
        rank = jnp.concatenate([tie_seen] * n_lt, axis=1) + _dot(tie_b, ut)
        sel = (key > thr_w) | (tie & (rank <= need_w))
        sel = sel & (j * KB + _iota((1, KB), 1) <= qpos)
        tie_seen = tie_seen + _dot(tie_b, ones_k)
        kb = k_ref[0, ks, :]
        vb = v_ref[0, ks, :]
        ms2, ls2, accs2 = [], [], []
        for h in range(N_HEADS):
            hs = slice(h * HEAD_DIM, (h + 1) * HEAD_DIM)
            s = jnp.where(sel, _dot(q[:, hs], kb[:, hs], NT), MASK_VALUE)
            m_new = jnp.maximum(ms[h], jnp.max(s, axis=-1, keepdims=True))
            alpha = jnp.exp(ms[h] - m_new)
            pr = jnp.where(sel, jnp.exp(s - m_new), 0.0)
            ls2.append(alpha * ls[h] + jnp.sum(pr, axis=-1, keepdims=True))
            accs2.append(alpha * accs[h] + _dot(pr.astype(BF16), vb[:, hs]))
            ms2.append(m_new)
        return tie_seen, tuple(ms2), tuple(ls2), tuple(accs2)

    init = (jnp.zeros((QB, LANES), F32),
            tuple(jnp.full((QB, 1), MASK_VALUE, F32) for _ in range(N_HEADS)),
            tuple(jnp.zeros((QB, 1), F32) for _ in range(N_HEADS)),
            tuple(jnp.zeros((QB, HEAD_DIM), F32) for _ in range(N_HEADS)))
    _, _, ls, accs = lax.fori_loop(0, nkb, attend, init)
    o_ref[0] = jnp.concatenate([accs[h] / ls[h] for h in range(N_HEADS)], axis=1).astype(o_ref.dtype)


def _dsa(q, k, v, qih, qil, wi, kih, kil, topk):
    bsz, t, _ = q.shape
    qblk = lambda w: pl.BlockSpec((1, DSA_QB, w), lambda b, i: (b, i, 0))
    seq = pl.BlockSpec((1, t, WIDTH), lambda b, i: (b, 0, 0))
    return pl.pallas_call(
        functools.partial(_dsa_kernel, topk=topk),
        grid=(bsz, t // DSA_QB),
        in_specs=[qblk(WIDTH), seq, seq, qblk(WIDTH), qblk(WIDTH), qblk(LANES), seq, seq],
        out_specs=qblk(WIDTH),
        out_shape=jax.ShapeDtypeStruct((bsz, t, WIDTH), BF16),
        scratch_shapes=[pltpu.VMEM((DSA_QB, t), jnp.int32)],
        compiler_params=_params(("parallel", "arbitrary")), name="dsa_attention",
    )(q, k, v, qih, qil, wi, kih, kil)


def _merge_kernel(x_ref, ya_ref, yb_ref, yc_ref, yd_ref, ga_ref, gb_ref, gc_ref, gd_ref,
                  wb_ref, wo_ref, g_ref, o_ref):
    merged = None
    for n, (y_ref, gate_ref) in enumerate(((ya_ref, ga_ref), (yb_ref, gb_ref), (yc_ref, gc_ref), (yd_ref, gd_ref))):
        term = _sigmoid(gate_ref[...]) * _dot(y_ref[...], wb_ref[n])
        merged = term if merged is None else merged + term
    mix = _dot(merged.astype(BF16), wo_ref[...])
    o_ref[...] = x_ref[...] + _norm_rows(mix, g_ref[...])


def _merge(x2, ys, main2, gate_block0, w_branch, w_out, g_post, tm=256):
    n, d = x2.shape
    row = lambda w: pl.BlockSpec((tm, w), lambda i: (i, 0))
    gate = lambda j: pl.BlockSpec((tm, d), lambda i, j=j: (i, gate_block0 + j))
    return pl.pallas_call(
        _merge_kernel, grid=(n // tm,),
        in_specs=[row(d)] + [row(WIDTH)] * 4 + [gate(j) for j in range(4)]
        + [pl.BlockSpec(w_branch.shape, lambda i: (0, 0, 0)), pl.BlockSpec(w_out.shape, lambda i: (0, 0)),
           pl.BlockSpec((1, d), lambda i: (0, 0))],
        out_specs=row(d),
        out_shape=jax.ShapeDtypeStruct((n, d), F32),
        compiler_params=_params(("parallel",)), name="gated_merge",
    )(x2, *ys, main2, main2, main2, main2, w_branch.astype(BF16), w_out.astype(BF16), g_post.reshape(1, d))


def _mlp_kernel(x_ref, gpre_ref, wu_ref, wd_ref, gpost_ref, o_ref, h_ref, acc_ref):
    kf = pl.program_id(1)

    @pl.when(kf == 0)
    def _():
        h_ref[...] = _norm_rows(x_ref[...], gpre_ref[...]).astype(BF16)
        acc_ref[...] = jnp.zeros_like(acc_ref)

    u = jnp.maximum(_dot(h_ref[...], wu_ref[...]), 0.0)
    acc_ref[...] += _dot((u * u).astype(BF16), wd_ref[...])

    @pl.when(kf == pl.num_programs(1) - 1)
    def _():
        o_ref[...] = x_ref[...] + _norm_rows(acc_ref[...], gpost_ref[...])


def _mlp(x2, g_pre, w_up, w_down, g_post, tm=1024, tf=1024):
    n, d = x2.shape
    dff = w_up.shape[1]
    return pl.pallas_call(
        _mlp_kernel, grid=(n // tm, dff // tf),
        in_specs=[pl.BlockSpec((tm, d), lambda i, k: (i, 0)), pl.BlockSpec((1, d), lambda i, k: (0, 0)),
                  pl.BlockSpec((d, tf), lambda i, k: (0, k)), pl.BlockSpec((tf, d), lambda i, k: (k, 0)),
                  pl.BlockSpec((1, d), lambda i, k: (0, 0))],
        out_specs=pl.BlockSpec((tm, d), lambda i, k: (i, 0)),
        out_shape=jax.ShapeDtypeStruct((n, d), F32),
        scratch_shapes=[pltpu.VMEM((tm, d), BF16), pltpu.VMEM((tm, d), F32)],
        compiler_params=_params(("parallel", "arbitrary")), name="mlp",
    )(x2, g_pre.reshape(1, d), w_up.astype(BF16), w_down.astype(BF16), g_post.reshape(1, d))


def _rope_tables(t, dim, groups):
    inv = 1.0 / (ROPE_THETA ** (jnp.arange(0, dim, 2, dtype=F32) / dim))
    ang = jnp.arange(t, dtype=F32)[:, None] * inv[None, :]
    cos, sin = jnp.cos(ang), jnp.sin(ang)
    return jnp.tile(jnp.concatenate([cos, cos], axis=1), (1, groups)), jnp.tile(jnp.concatenate([-sin, sin], axis=1), (1, groups))


def _pad_rows(w, rows, offset):
    return jnp.zeros((rows, w.shape[1]), w.dtype).at[offset:offset + w.shape[0]].set(w)


def _pad_cols(w, cols):
    return jnp.pad(w, ((0, 0), (0, cols - w.shape[1])))


def kernel(x, norm_mix_pre, norm_mix_post, norm_mlp_pre, norm_mlp_post, w_in, w_branch, w_out, hgrn_lb_logits, hgrn_norm_w, fox_f_bias, rwkv_mu, rwkv_w0, rwkv_w2, rwkv_a0, rwkv_a2, rwkv_g2, rwkv_k_k, rwkv_k_a, rwkv_r_k, rwkv_ln_w, rwkv_ln_b, rwkv_v0, rwkv_v1, rwkv_v2, w_up, w_down):
    bsz, t, d = x.shape
    n = bsz * t
    depth = w_in.shape[0]
    W = WIDTH
    topk = min(TOPK_MAX, t // 4)

    lb_soft = jax.nn.softmax(hgrn_lb_logits.astype(F32), axis=0)
    lower_bounds = jnp.cumsum(lb_soft, axis=0) - lb_soft[0:1]
    tables = _rope_tables(t, HEAD_DIM, N_HEADS) + _rope_tables(t, IDX_DIM, IDX_HEADS)

    o_a, o_b, o_iq = 0, 4 * W, 7 * W
    o_ik, o_iw = o_iq + IDX_HEADS * IDX_DIM, o_iq + IDX_HEADS * IDX_DIM + IDX_DIM
    o_c = o_iw + IDX_HEADS
    c_cols = 3 * W + 128
    o_d = o_c + c_cols
    o_df = o_d + 3 * W
    o_g = o_df + N_HEADS
    main_blocks = dict(b_qkv=4, c=2, df=(7 * W + c_cols) // LANES, gates=(7 * W + c_cols + LANES + 256) // d)
    assert main_blocks["c"] * c_cols == 7 * W and main_blocks["gates"] * d == 7 * W + c_cols + LANES + 256

    x2 = x.reshape(n, d)
    v_first = None
    for l in range(depth):
        wl = w_in[l]
        w_main = jnp.concatenate([
            wl[:, o_a:o_a + 7 * W], wl[:, o_c:o_c + c_cols], _pad_cols(wl[:, o_df:o_df + N_HEADS], LANES),
            jnp.zeros((d, 256), F32), wl[:, o_g:o_g + 4 * d]], axis=1)
        w_idx = jnp.concatenate([
            wl[:, o_iq:o_iq + W], jnp.tile(wl[:, o_ik:o_ik + IDX_DIM], (1, IDX_HEADS)),
            _pad_cols(wl[:, o_iw:o_iw + IDX_HEADS], LANES)], axis=1)
        w_fox = wl[:, o_d:o_d + 3 * W]

        main2 = _norm_matmul(x2, norm_mix_pre[l], w_main, F32, tm=1024, tn=512)
        idx2 = _norm_matmul(x2, norm_mix_pre[l], w_idx, F32, tm=1024, tn=w_idx.shape[1], precise=True)
        fox2 = _norm_matmul(x2, norm_mix_pre[l], w_fox, BF16, tm=1024, tn=3 * W)
        main3 = main2.reshape(bsz, t, -1)

        y_a = _hgrn(main3, lower_bounds[l], hgrn_norm_w[l])

        prep = _dsa_prep(main3, idx2.reshape(bsz, t, -1), tables, main_blocks["b_qkv"])
        y_b = _dsa(*prep[:5], prep[7], prep[5], prep[6], topk)

        prm = dict(
            mu=_pad_cols(rwkv_mu[l][None, :], RWKV_COLS), w0=rwkv_w0[l][None, :],
            w2=_pad_rows(rwkv_w2[l], RWKV_LORA_LANES, 0), a0=rwkv_a0[l][None, :],
            a2=_pad_rows(rwkv_a2[l], RWKV_LORA_LANES, 32), g2=_pad_rows(rwkv_g2[l], RWKV_LORA_LANES, 64),
            k_k=rwkv_k_k[l][None, :], k_a=rwkv_k_a[l][None, :], r_k=rwkv_r_k[l].reshape(1, W),
            ln_w=rwkv_ln_w[l][None, :], ln_b=rwkv_ln_b[l][None, :])
        if l > 0:
            prm.update(v0=rwkv_v0[l - 1][None, :], v1=_pad_cols(rwkv_v1[l - 1], LANES),
                       v2=_pad_rows(rwkv_v2[l - 1], LANES, 0))
        y_c, v_c = _rwkv(main3, main_blocks["c"], v_first, prm)
        if l == 0:
            v_first = v_c

        c_col = _fox_cum(main3, main_blocks["df"], _pad_cols(fox_f_bias[l][None, :], LANES))
        c_row = jnp.transpose(c_col[:, :, :8], (0, 2, 1))
        y_d = _fox(fox2.reshape(bsz, t, 3 * W), c_col, c_row)

        ys = [y.reshape(n, W) for y in (y_a, y_b, y_c, y_d)]
        x2 = _merge(x2, ys, main2, main_blocks["gates"], w_branch[l], w_out[l], norm_mix_post[l])
        x2 = _mlp(x2, norm_mlp_pre[l], w_up[l], w_down[l], norm_mlp_post[l])
    return x2.reshape(bsz, t, d)
```

```python
import functools

import jax
import jax.numpy as jnp
from jax import lax
from jax.experimental import pallas as pl
from jax.experimental.pallas import tpu as pltpu

F32 = jnp.float32
BF16 = jnp.bfloat16

HEAD_DIM = 64
N_HEADS = 4
WIDTH = N_HEADS * HEAD_DIM
IDX_HEADS = 8
IDX_DIM = 32
TOPK_MAX = 256
ROPE_THETA = 10000.0
NORM_EPS = 1e-6
RWKV_GN_EPS = 64e-5
MASK_VALUE = -1e30
RWKV_LORA_LANES = 128
RWKV_COLS = 3 * WIDTH + RWKV_LORA_LANES

LANES = 128
VMEM_LIMIT = 48 * 1024 * 1024

NN = ((1,), (0,))
NT = ((1,), (1,))
TN = ((0,), (0,))


def _dot(a, b, dims=NN):
    return lax.dot_general(a, b, (dims, ((), ())), preferred_element_type=F32)


def _split2(x):
    hi = x.astype(BF16)
    lo = (x - hi.astype(F32)).astype(BF16)
    return hi, lo


def _split3(x):
    hi = x.astype(BF16)
    r1 = x - hi.astype(F32)
    mid = r1.astype(BF16)
    lo = (r1 - mid.astype(F32)).astype(BF16)
    return hi, mid, lo


def _dot3(a, b, dims=NN):
    ah, al = _split2(a)
    bh, bl = _split2(b)
    return _dot(ah, bh, dims) + (_dot(ah, bl, dims) + _dot(al, bh, dims))


def _dot_exact_lhs(a_bf16, b, dims=NN):
    b1, b2, b3 = _split3(b)
    return _dot(a_bf16, b1, dims) + (_dot(a_bf16, b2, dims) + _dot(a_bf16, b3, dims))


def _dot_exact_rhs(a, b_bf16, dims=NN):
    a1, a2 = _split2(a)
    return _dot(a1, b_bf16, dims) + _dot(a2, b_bf16, dims)


def _iota(shape, dim):
    return lax.broadcasted_iota(jnp.int32, shape, dim)


def _head_block_ones(n, group):
    return (_iota((n, n), 0) // group == _iota((n, n), 1) // group).astype(BF16)


def _lower_tri_ones(n):
    return (_iota((n, n), 0) >= _iota((n, n), 1)).astype(BF16)


def _sigmoid(x):
    return 1.0 / (1.0 + jnp.exp(-x))


def _softplus(x):
    return jnp.maximum(x, 0.0) + jnp.log(1.0 + jnp.exp(-jnp.abs(x)))


def _params(sem):
    return pltpu.CompilerParams(dimension_semantics=sem, vmem_limit_bytes=VMEM_LIMIT)


def _norm_rows(x, g):
    return x * lax.rsqrt(jnp.mean(x * x, axis=-1, keepdims=True) + NORM_EPS) * g


def _norm_matmul_kernel(x_ref, g_ref, w_ref, o_ref, h_ref):
    @pl.when(pl.program_id(1) == 0)
    def _():
        h_ref[...] = _norm_rows(x_ref[...], g_ref[...]).astype(BF16)

    o_ref[...] = _dot(h_ref[...], w_ref[...]).astype(o_ref.dtype)


def _norm_matmul3_kernel(x_ref, g_ref, wh_ref, wl_ref, o_ref, hh_ref, hl_ref):
    @pl.when(pl.program_id(1) == 0)
    def _():
        hh, hl = _split2(_norm_rows(x_ref[...], g_ref[...]))
        hh_ref[...] = hh
        hl_ref[...] = hl

    o_ref[...] = (_dot(hh_ref[...], wh_ref[...])
                  + (_dot(hh_ref[...], wl_ref[...]) + _dot(hl_ref[...], wh_ref[...]))).astype(o_ref.dtype)


def _norm_matmul(x2, g, w, out_dtype, tm, tn, precise=False):
    n, d = x2.shape
    cols = w.shape[1]
    grid = (n // tm, cols // tn)
    x_spec = pl.BlockSpec((tm, d), lambda i, j: (i, 0))
    g_spec = pl.BlockSpec((1, d), lambda i, j: (0, 0))
    w_spec = pl.BlockSpec((d, tn), lambda i, j: (0, j))
    o_spec = pl.BlockSpec((tm, tn), lambda i, j: (i, j))
    g2 = g.reshape(1, d)
    if precise:
        wh, wl = _split2(w)
        return pl.pallas_call(
            _norm_matmul3_kernel, grid=grid,
            in_specs=[x_spec, g_spec, w_spec, w_spec], out_specs=o_spec,
            out_shape=jax.ShapeDtypeStruct((n, cols), out_dtype),
            scratch_shapes=[pltpu.VMEM((tm, d), BF16), pltpu.VMEM((tm, d), BF16)],
            compiler_params=_params(("parallel", "arbitrary")), name="norm_proj_precise",
        )(x2, g2, wh, wl)
    return pl.pallas_call(
        _norm_matmul_kernel, grid=grid,
        in_specs=[x_spec, g_spec, w_spec], out_specs=o_spec,
        out_shape=jax.ShapeDtypeStruct((n, cols), out_dtype),
        scratch_shapes=[pltpu.VMEM((tm, d), BF16)],
        compiler_params=_params(("parallel", "arbitrary")), name="norm_proj",
    )(x2, g2, w.astype(BF16))


HGRN_CHUNK = 64
HGRN_GROUP = 8


def _dot1(a, b, dims=NN):
    return _dot(a.astype(BF16), b.astype(BF16), dims)


def _hgrn_kernel(q_ref, f_ref, i_ref, g_ref, lb_ref, nw_ref, o_ref, st_ref, o_sc, *, n_chunks):
    C, G = HGRN_CHUNK, HGRN_GROUP

    @pl.when(pl.program_id(1) == 0)
    def _():
        st_ref[...] = jnp.zeros_like(st_ref)

    lb = lb_ref[...]
    nw = nw_ref[...]
    tri = _lower_tri_ones(C)
    bd = _head_block_ones(WIDTH, HEAD_DIM)
    bd_mask = _iota((WIDTH, WIDTH), 0) // HEAD_DIM == _iota((WIDTH, WIDTH), 1) // HEAD_DIM
    row_top = _iota((G, 1), 0)

    def chunk(c, carry):
        sl = pl.ds(pl.multiple_of(c * C, C), C)
        fl = f_ref[0, sl, :]
        f = lb + (1.0 - lb) * _sigmoid(fl)
        k = (1.0 - lb) * _sigmoid(-fl)
        b = _dot_exact_lhs(tri, jnp.log(f))
        q = q_ref[0, sl, :] * (HEAD_DIM ** -0.5)
        v = i_ref[0, sl, :]
        st = st_ref[...]
        o_sc[...] = _dot1(q * jnp.exp(b), st, NT)

        for s0 in range(0, C, G):
            rows = C - s0
            q_top, b_top = q[s0:s0 + G], b[s0:s0 + G]
            q_rest, b_rest = q[s0 + G:], b[s0 + G:]
            pieces = []
            for u in range(G):
                bs = b[s0 + u:s0 + u + 1]
                ks = k[s0 + u:s0 + u + 1]
                top = q_top * ks * jnp.exp(jnp.minimum(b_top - bs, 0.0))
                pieces.append(jnp.where(row_top >= u, top, 0.0))
                if rows > G:
                    pieces.append(q_rest * ks * jnp.exp(b_rest - bs))
            x = jnp.concatenate(pieces, axis=0).astype(BF16)
            r = _dot(x, bd)
            og = jnp.zeros((rows, WIDTH), F32)
            for u in range(G):
                og = og + r[u * rows:(u + 1) * rows] * v[s0 + u:s0 + u + 1]
            o_sc[s0:, :] += og

        b_last = b[C - 1:C, :]
        upd = _dot1(v, k * jnp.exp(b_last - b), TN)
        st_ref[...] = st * jnp.exp(b_last) + jnp.where(bd_mask, upd, 0.0)

        o = o_sc[...]
        ms = _dot_exact_rhs(o * o, bd) * (1.0 / HEAD_DIM)
        on = o * lax.rsqrt(ms + NORM_EPS) * nw
        gl = g_ref[0, sl, :]
        o_ref[0, sl, :] = (on * (gl * _sigmoid(gl))).astype(o_ref.dtype)
        return carry

    lax.fori_loop(0, n_chunks, chunk, 0)


def _hgrn(main3, lb, norm_w, tb=256):
    bsz, t, _ = main3.shape
    col = lambda j: pl.BlockSpec((1, tb, WIDTH), lambda b, i, j=j: (b, i, j))
    vec = pl.BlockSpec((1, WIDTH), lambda b, i: (0, 0))
    return pl.pallas_call(
        functools.partial(_hgrn_kernel, n_chunks=tb // HGRN_CHUNK),
        grid=(bsz, t // tb),
        in_specs=[col(0), col(1), col(2), col(3), vec, vec],
        out_specs=pl.BlockSpec((1, tb, WIDTH), lambda b, i: (b, i, 0)),
        out_shape=jax.ShapeDtypeStruct((bsz, t, WIDTH), BF16),
        scratch_shapes=[pltpu.VMEM((WIDTH, WIDTH), F32), pltpu.VMEM((HGRN_CHUNK, WIDTH), F32)],
        compiler_params=_params(("parallel", "arbitrary")), name="hgrn2",
    )(main3, main3, main3, main3, lb.reshape(1, WIDTH), jnp.tile(norm_w, N_HEADS).reshape(1, WIDTH))


RWKV_CHUNK = 64


def _rwkv_kernel(*refs, n_chunks, has_vgate):
    if has_vgate:
        (c_ref, vf_ref, mu_ref, w0_ref, w2_ref, a0_ref, a2_ref, g2_ref, kkw_ref, kaw_ref, rk_ref,
         lnw_ref, lnb_ref, v0_ref, v1_ref, v2_ref, y_ref, vout_ref, s_ref, prev_ref, y_sc) = refs
    else:
        (c_ref, mu_ref, w0_ref, w2_ref, a0_ref, a2_ref, g2_ref, kkw_ref, kaw_ref, rk_ref,
         lnw_ref, lnb_ref, y_ref, vout_ref, s_ref, prev_ref, y_sc) = refs
    C = RWKV_CHUNK
    W = WIDTH
    tb = n_chunks * C

    @pl.when(pl.program_id(1) == 0)
    def _():
        s_ref[...] = jnp.zeros_like(s_ref)
        prev_ref[...] = jnp.zeros_like(prev_ref)

    p = c_ref[0]
    shifted = jnp.where(_iota((tb, 1), 0) == 0, prev_ref[...], pltpu.roll(p, 1, axis=0))
    prev_ref[...] = p[tb - 1:tb, :]
    xm = p + (shifted - p) * mu_ref[...]
    r = xm[:, 0:W]
    k = xm[:, W:2 * W]
    v = xm[:, 2 * W:3 * W]
    lora = xm[:, 3 * W:3 * W + RWKV_LORA_LANES]

    bd = _head_block_ones(W, HEAD_DIM)
    w_log = -_softplus(-(w0_ref[...] + _dot3(jnp.tanh(lora), w2_ref[...]))) - 0.5
    log_decay = -jnp.exp(w_log)
    a = _sigmoid(a0_ref[...] + _dot3(lora, a2_ref[...]))
    g = _dot3(_sigmoid(lora), g2_ref[...])
    if has_vgate:
        vg = _dot3(_dot3(v, v1_ref[...]), v2_ref[...])
        v = v + (vf_ref[0] - v) * _sigmoid(v0_ref[...] + vg)
    vout_ref[0] = v
    kk = k * kkw_ref[...]
    kk = kk * lax.rsqrt(jnp.maximum(_dot_exact_rhs(kk * kk, bd), 1e-24))
    k = k * (1.0 + (a - 1.0) * kaw_ref[...])

    tri = _lower_tri_ones(C)
    lane_head = _iota((1, W), 1) // HEAD_DIM
    rt = _iota((N_HEADS * C, N_HEADS * C), 0) % C
    ct = _iota((N_HEADS * C, N_HEADS * C), 1) % C
    strict_lower = rt > ct
    lower = rt >= ct

    def stack(m):
        return jnp.concatenate([jnp.where(lane_head == h, m, 0.0) for h in range(N_HEADS)], axis=0)

    for c in range(n_chunks):
        sl = slice(c * C, (c + 1) * C)
        ld = log_decay[sl]
        cum = _dot_exact_lhs(tri, ld)
        pdec = jnp.exp(cum)
        pinv = jnp.exp(-cum)
        a_s = stack(-kk[sl] * jnp.exp(cum - ld))
        b_s = stack(kk[sl] * a[sl] * pinv)
        k_s = stack(k[sl] * pinv)
        r_s = stack(r[sl] * pdec)
        v_s = stack(v[sl])
        s = s_ref[...]

        m = jnp.where(strict_lower, _dot1(a_s, b_s, NT), 0.0)
        lak = jnp.where(strict_lower, _dot1(a_s, k_s, NT), 0.0)
        lrb = jnp.where(lower, _dot1(r_s, b_s, NT), 0.0)
        lrk = jnp.where(lower, _dot1(r_s, k_s, NT), 0.0)

        z = _dot1(a_s, s, NT) + _dot1(lak, v_s)
        z = z + _dot1(m, z)
        power = 2
        while power < C:
            m = _dot1(m, m)
            z = z + _dot1(m, z)
            power *= 2

        y = _dot1(r_s, s, NT) + _dot1(lrb, z) + _dot1(lrk, v_s)
        y_sc[sl, :] = sum(y[h * C:(h + 1) * C] for h in range(N_HEADS))
        s_ref[...] = (s + _dot1(z, b_s, TN) + _dot1(v_s, k_s, TN)) * pdec[C - 1:C, :]

    y = y_sc[...]
    inv_n = 1.0 / HEAD_DIM
    mean = _dot_exact_rhs(y, bd) * inv_n
    yc = y - mean
    var = _dot_exact_rhs(yc * yc, bd) * inv_n
    yn = yc * lax.rsqrt(var + RWKV_GN_EPS) * lnw_ref[...] + lnb_ref[...]
    yn = yn + _dot_exact_rhs(r * k * rk_ref[...], bd) * v
    y_ref[0] = (yn * g).astype(y_ref.dtype)


def _rwkv(c3, col_block, v_first, prm, tb=256):
    bsz, t, _ = c3.shape
    has_vgate = v_first is not None
    blk = lambda w: pl.BlockSpec((1, tb, w), lambda b, i: (b, i, 0))
    cblk = pl.BlockSpec((1, tb, RWKV_COLS), lambda b, i: (b, i, col_block))
    full = lambda a: pl.BlockSpec(a.shape, lambda b, i: (0,) * a.ndim)
    names = ["mu", "w0", "w2", "a0", "a2", "g2", "k_k", "k_a", "r_k", "ln_w", "ln_b"]
    if has_vgate:
        names += ["v0", "v1", "v2"]
    args = [c3] + ([v_first] if has_vgate else []) + [prm[n] for n in names]
    in_specs = [cblk] + ([blk(WIDTH)] if has_vgate else []) + [full(prm[n]) for n in names]
    return pl.pallas_call(
        functools.partial(_rwkv_kernel, n_chunks=tb // RWKV_CHUNK, has_vgate=has_vgate),
        grid=(bsz, t // tb),
        in_specs=in_specs,
        out_specs=[blk(WIDTH), blk(WIDTH)],
        out_shape=[jax.ShapeDtypeStruct((bsz, t, WIDTH), BF16), jax.ShapeDtypeStruct((bsz, t, WIDTH), F32)],
        scratch_shapes=[pltpu.VMEM((WIDTH, WIDTH), F32), pltpu.VMEM((1, RWKV_COLS), F32),
                        pltpu.VMEM((tb, WIDTH), F32)],
        compiler_params=_params(("parallel", "arbitrary")), name="rwkv7",
    )(*args)


def _fox_cum_kernel(f_ref, bias_ref, c_ref, carry_ref, *, tb):
    @pl.when(pl.program_id(1) == 0)
    def _():
        carry_ref[...] = jnp.zeros_like(carry_ref)

    logf = -_softplus(-(f_ref[0] + bias_ref[...]))
    c = _dot_exact_lhs(_lower_tri_ones(tb), logf) + carry_ref[...]
    c_ref[0] = c
    carry_ref[...] = c[tb - 1:tb, :]


def _fox_cum(main3, col_block, bias_row, tb=256):
    bsz, t, _ = main3.shape
    return pl.pallas_call(
        functools.partial(_fox_cum_kernel, tb=tb),
        grid=(bsz, t // tb),
        in_specs=[pl.BlockSpec((1, tb, LANES), lambda b, i: (b, i, col_block)),
                  pl.BlockSpec((1, LANES), lambda b, i: (0, 0))],
        out_specs=pl.BlockSpec((1, tb, LANES), lambda b, i: (b, i, 0)),
        out_shape=jax.ShapeDtypeStruct((bsz, t, LANES), F32),
        scratch_shapes=[pltpu.VMEM((1, LANES), F32)],
        compiler_params=_params(("parallel", "arbitrary")), name="fox_cumgate",
    )(main3, bias_row)


FOX_QB = 128
FOX_KB = 512


def _fox_kernel(q_ref, k_ref, v_ref, ck_ref, o_ref, m_sc, l_sc, acc_sc):
    QB, KB = FOX_QB, FOX_KB
    i = pl.program_id(1)
    lane_head = _iota((1, WIDTH), 1) // HEAD_DIM
    q = (q_ref[0].astype(F32) * (HEAD_DIM ** -0.5)).astype(BF16)
    zero = jnp.zeros_like(q)
    qstack = jnp.concatenate([jnp.where(lane_head == h, q, zero) for h in range(N_HEADS)], axis=0)
    qpos = i * QB + _iota((N_HEADS * QB, 1), 0) % QB
    m_sc[...] = jnp.full_like(m_sc, MASK_VALUE)
    l_sc[...] = jnp.zeros_like(l_sc)
    acc_sc[...] = jnp.zeros_like(acc_sc)

    def step(j, masked):
        ks = pl.ds(pl.multiple_of(j * KB, KB), KB)
        s = _dot(qstack, k_ref[0, ks, :], NT)
        ck = ck_ref[0, :, ks]
        s = jnp.concatenate([s[h * QB:(h + 1) * QB] - ck[h:h + 1, :] for h in range(N_HEADS)], axis=0)
        if masked:
            s = jnp.where(j * KB + _iota((1, KB), 1) <= qpos, s, MASK_VALUE)
        m_old = m_sc[...]
        m_new = jnp.maximum(m_old, jnp.max(s, axis=-1, keepdims=True))
        alpha = jnp.exp(m_old - m_new)
        pr = jnp.exp(s - m_new)
        l_sc[...] = alpha * l_sc[...] + jnp.sum(pr, axis=-1, keepdims=True)
        acc_sc[...] = alpha * acc_sc[...] + _dot(pr.astype(BF16), v_ref[0, ks, :])
        m_sc[...] = m_new

    n_full = (i * QB) // KB

    def full_step(j, carry):
        step(j, False)
        return carry

    lax.fori_loop(0, n_full, full_step, 0)
    step(n_full, True)
    res = acc_sc[...] / l_sc[...]
    out = jnp.zeros((QB, WIDTH), F32)
    for h in range(N_HEADS):
        out = out + jnp.where(lane_head == h, res[h * QB:(h + 1) * QB], 0.0)
    o_ref[0] = out.astype(o_ref.dtype)


def _fox(qkv3, c_row):
    bsz, t, _ = qkv3.shape
    rows = N_HEADS * FOX_QB
    return pl.pallas_call(
        _fox_kernel,
        grid=(bsz, t // FOX_QB),
        in_specs=[pl.BlockSpec((1, FOX_QB, WIDTH), lambda b, i: (b, i, 0)),
                  pl.BlockSpec((1, t, WIDTH), lambda b, i: (b, 0, 1)),
                  pl.BlockSpec((1, t, WIDTH), lambda b, i: (b, 0, 2)),
                  pl.BlockSpec((1, 8, t), lambda b, i: (b, 0, 0))],
        out_specs=pl.BlockSpec((1, FOX_QB, WIDTH), lambda b, i: (b, i, 0)),
        out_shape=jax.ShapeDtypeStruct((bsz, t, WIDTH), BF16),
        scratch_shapes=[pltpu.VMEM((rows, 1), F32), pltpu.VMEM((rows, 1), F32), pltpu.VMEM((rows, WIDTH), F32)],
        compiler_params=_params(("parallel", "arbitrary")), name="fox_attention",
    )(qkv3, qkv3, qkv3, c_row)


DSA_QB = 128
DSA_KB = 512


def _swap_halves(x, half):
    n = x.shape[-1]
    lower = (_iota((1, n), 1) % (2 * half)) < half
    return jnp.where(lower, pltpu.roll(x, n - half, axis=1), pltpu.roll(x, half, axis=1))


def _dsa_prep_kernel(q_ref, k_ref, v_ref, iq_ref, ik_ref, iw_ref, ch_ref, sh_ref, ci_ref, si_ref,
                     selh_ref, sell_ref, qo_ref, ko_ref, vo_ref, q3_ref, k3_ref, wi_ref):
    ch, sh, ci, si = ch_ref[...], sh_ref[...], ci_ref[...], si_ref[...]

    def rope(x, c, s, half):
        return x * c + _swap_halves(x, half) * s

    qo_ref[0] = (rope(q_ref[0], ch, sh, HEAD_DIM // 2) * (HEAD_DIM ** -0.5)).astype(BF16)
    ko_ref[0] = rope(k_ref[0], ch, sh, HEAD_DIM // 2).astype(BF16)
    vo_ref[0] = v_ref[0].astype(BF16)
    qh, ql = _split2(rope(iq_ref[0], ci, si, IDX_DIM // 2) * (IDX_DIM ** -0.5))
    q3_ref[0] = (_dot(qh, selh_ref[...]) + _dot(ql, sell_ref[...])).astype(BF16)
    kh, kl = _split2(rope(ik_ref[0], ci[:, :LANES], si[:, :LANES], IDX_DIM // 2))
    seg = _iota((1, LANES), 1) // IDX_DIM
    k3_ref[0] = jnp.where(seg < 2, kh, jnp.where(seg == 2, kl, jnp.zeros_like(kl)))
    wi_ref[0] = iw_ref[0] * (IDX_HEADS ** -0.5)


def _dsa_prep(main3, idx3, tables, qkv_block0, tb=512):
    bsz, t, _ = main3.shape
    iq_w = IDX_HEADS * IDX_DIM
    r = jnp.arange(iq_w)[:, None]
    c = jnp.arange(IDX_HEADS * LANES)[None, :]
    same = (r // IDX_DIM == c // LANES) & (r % IDX_DIM == c % IDX_DIM)
    seg = (c % LANES) // IDX_DIM
    sel_hi = (same & ((seg == 0) | (seg == 2))).astype(BF16)
    sel_lo = (same & (seg == 1)).astype(BF16)
    mcol = lambda j: pl.BlockSpec((1, tb, WIDTH), lambda b, i, j=j: (b, i, qkv_block0 + j))
    tab = pl.BlockSpec((tb, WIDTH), lambda b, i: (i, 0))
    sel = pl.BlockSpec(sel_hi.shape, lambda b, i: (0, 0))
    out = lambda w: pl.BlockSpec((1, tb, w), lambda b, i: (b, i, 0))
    shape = lambda dt, w: jax.ShapeDtypeStruct((bsz, t, w), dt)
    return pl.pallas_call(
        _dsa_prep_kernel, grid=(bsz, t // tb),
        in_specs=[mcol(0), mcol(1), mcol(2), out(iq_w),
                  pl.BlockSpec((1, tb, LANES), lambda b, i: (b, i, iq_w // LANES)),
                  pl.BlockSpec((1, tb, LANES), lambda b, i: (b, i, iq_w // LANES + 1)),
                  tab, tab, tab, tab, sel, sel],
        out_specs=[out(WIDTH)] * 3 + [out(IDX_HEADS * LANES), out(LANES), out(LANES)],
        out_shape=[shape(BF16, WIDTH)] * 3 + [shape(BF16, IDX_HEADS * LANES), shape(BF16, LANES), shape(F32, LANES)],
        compiler_params=_params(("parallel", "parallel")), name="dsa_prep",
    )(main3, main3, main3, idx3, idx3, idx3, *tables, sel_hi, sel_lo)


def _dsa_kernel(q_ref, k_ref, v_ref, q3_ref, k3_ref, wi_ref, o_ref, key_sc, m_sc, l_sc, acc_sc,
                *, topk):
    QB, KB = DSA_QB, DSA_KB
    n_lt = KB // LANES
    i = pl.program_id(1)
    nkb = (i * QB) // KB + 1
    qpos = i * QB + _iota((QB, 1), 0)
    int_min = jnp.int32(-2 ** 31)

    q3 = q3_ref[0]
    qs = jnp.concatenate([q3[:, h * LANES:(h + 1) * LANES] for h in range(IDX_HEADS)], axis=0)
    wi = wi_ref[0]
    wcols = [jnp.broadcast_to(wi[:, h:h + 1], (QB, KB)) for h in range(IDX_HEADS)]

    def score_block(j, carry):
        ks = pl.ds(pl.multiple_of(j * KB, KB), KB)
        r = _dot(qs, k3_ref[0, ks, :], NT)
        sc = jnp.zeros((QB, KB), F32)
        for h in range(IDX_HEADS):
            sc = sc + jnp.maximum(r[h * QB:(h + 1) * QB], 0.0) * wcols[h]
        sc = jnp.where(j * KB + _iota((1, KB), 1) <= qpos, sc, MASK_VALUE)
        sc = jnp.where(sc == 0.0, 0.0, sc)
        bits = pltpu.bitcast(sc, jnp.int32)
        key_sc[:, ks] = jnp.where(bits < 0, bits ^ jnp.int32(0x7FFFFFFF), bits)
        return carry

    lax.fori_loop(0, nkb, score_block, 0)

    ones_l = jnp.ones((LANES, LANES), BF16)

    def count_ge(cand):
        def blk(j, acc):
            base = pl.multiple_of(j * KB, KB)
            for c in range(n_lt):
                kt = key_sc[:, pl.ds(base + c * LANES, LANES)]
                acc = acc + jnp.where(kt >= cand, 1.0, 0.0)
            return acc
        acc = lax.fori_loop(0, nkb, blk, jnp.zeros((QB, LANES), F32))
        return _dot(acc.astype(BF16), ones_l)

    kf = float(topk)

    def bit_step(step, v):
        trial = jnp.where(step == 0, jnp.zeros_like(v), v | (jnp.int32(1) << (31 - step)))
        return jnp.where(count_ge(trial) >= kf, trial, v)

    thr = lax.fori_loop(0, 32, bit_step, jnp.full((QB, LANES), int_min, jnp.int32))
    need = kf - count_ge(thr + 1)

    ut = (_iota((KB, KB), 0) <= _iota((KB, KB), 1)).astype(BF16)
    ones_k = jnp.ones((KB, LANES), BF16)
    lane_head = _iota((1, WIDTH), 1) // HEAD_DIM
    q = q_ref[0]
    zero = jnp.zeros_like(q)
    qstack = jnp.concatenate([jnp.where(lane_head == h, q, zero) for h in range(N_HEADS)], axis=0)
    thr_w = jnp.concatenate([thr] * n_lt, axis=1)
    need_w = jnp.concatenate([need] * n_lt, axis=1)
    m_sc[...] = jnp.full_like(m_sc, MASK_VALUE)
    l_sc[...] = jnp.zeros_like(l_sc)
    acc_sc[...] = jnp.zeros_like(acc_sc)

    def attend(j, tie_seen):
        ks = pl.ds(pl.multiple_of(j * KB, KB), KB)
        key = key_sc[:, ks]
        tie = key == thr_w
        tie_b = jnp.where(tie, 1.0, 0.0).astype(BF16)
        rank = jnp.concatenate([tie_seen] * n_lt, axis=1) + _dot(tie_b, ut)
        sel = (key > thr_w) | (tie & (rank <= need_w))
        sel = sel & (j * KB + _iota((1, KB), 1) <= qpos)
        bias = jnp.where(sel, 0.0, MASK_VALUE)
        s = _dot(qstack, k_ref[0, ks, :], NT) + jnp.concatenate([bias] * N_HEADS, axis=0)
        m_old = m_sc[...]
        m_new = jnp.maximum(m_old, jnp.max(s, axis=-1, keepdims=True))
        alpha = jnp.exp(m_old - m_new)
        pr = jnp.exp(s - m_new)
        l_sc[...] = alpha * l_sc[...] + jnp.sum(pr, axis=-1, keepdims=True)
        acc_sc[...] = alpha * acc_sc[...] + _dot(pr.astype(BF16), v_ref[0, ks, :])
        m_sc[...] = m_new
        return tie_seen + _dot(tie_b, ones_k)

    lax.fori_loop(0, nkb, attend, jnp.zeros((QB, LANES), F32))
    res = acc_sc[...] / l_sc[...]
    out = jnp.zeros((QB, WIDTH), F32)
    for h in range(N_HEADS):
        out = out + jnp.where(lane_head == h, res[h * QB:(h + 1) * QB], 0.0)
    o_ref[0] = out.astype(o_ref.dtype)


def _dsa(q, k, v, q3, k3, wi, topk):
    bsz, t, _ = q.shape
    qblk = lambda w: pl.BlockSpec((1, DSA_QB, w), lambda b, i: (b, i, 0))
    seq = lambda w: pl.BlockSpec((1, t, w), lambda b, i: (b, 0, 0))
    rows = N_HEADS * DSA_QB
    return pl.pallas_call(
        functools.partial(_dsa_kernel, topk=topk),
        grid=(bsz, t // DSA_QB),
        in_specs=[qblk(WIDTH), seq(WIDTH), seq(WIDTH), qblk(IDX_HEADS * LANES), seq(LANES), qblk(LANES)],
        out_specs=qblk(WIDTH),
        out_shape=jax.ShapeDtypeStruct((bsz, t, WIDTH), BF16),
        scratch_shapes=[pltpu.VMEM((DSA_QB, t), jnp.int32), pltpu.VMEM((rows, 1), F32),
                        pltpu.VMEM((rows, 1), F32), pltpu.VMEM((rows, WIDTH), F32)],
        compiler_params=_params(("parallel", "arbitrary")), name="dsa_attention",
    )(q, k, v, q3, k3, wi)


def _merge_kernel(x_ref, ya_ref, yb_ref, yc_ref, yd_ref, ga_ref, gb_ref, gc_ref, gd_ref,
                  wb_ref, wo_ref, g_ref, o_ref):
    merged = None
    for n, (y_ref, gate_ref) in enumerate(((ya_ref, ga_ref), (yb_ref, gb_ref), (yc_ref, gc_ref), (yd_ref, gd_ref))):
        term = _sigmoid(gate_ref[...]) * _dot(y_ref[...], wb_ref[n])
        merged = term if merged is None else merged + term
    mix = _dot(merged.astype(BF16), wo_ref[...])
    o_ref[...] = x_ref[...] + _norm_rows(mix, g_ref[...])


def _merge(x2, ys, main2, gate_block0, w_branch, w_out, g_post, tm=256):
    n, d = x2.shape
    row = lambda w: pl.BlockSpec((tm, w), lambda i: (i, 0))
    gate = lambda j: pl.BlockSpec((tm, d), lambda i, j=j: (i, gate_block0 + j))
    return pl.pallas_call(
        _merge_kernel, grid=(n // tm,),
        in_specs=[row(d)] + [row(WIDTH)] * 4 + [gate(j) for j in range(4)]
        + [pl.BlockSpec(w_branch.shape, lambda i: (0, 0, 0)), pl.BlockSpec(w_out.shape, lambda i: (0, 0)),
           pl.BlockSpec((1, d), lambda i: (0, 0))],
        out_specs=row(d),
        out_shape=jax.ShapeDtypeStruct((n, d), F32),
        compiler_params=_params(("parallel",)), name="gated_merge",
    )(x2, *ys, main2, main2, main2, main2, w_branch.astype(BF16), w_out.astype(BF16), g_post.reshape(1, d))


def _mlp_kernel(x_ref, gpre_ref, wu_ref, wd_ref, gpost_ref, o_ref, h_ref, acc_ref):
    kf = pl.program_id(1)

    @pl.when(kf == 0)
    def _():
        h_ref[...] = _norm_rows(x_ref[...], gpre_ref[...]).astype(BF16)
        acc_ref[...] = jnp.zeros_like(acc_ref)

    u = jnp.maximum(_dot(h_ref[...], wu_ref[...]), 0.0)
    acc_ref[...] += _dot((u * u).astype(BF16), wd_ref[...])

    @pl.when(kf == pl.num_programs(1) - 1)
    def _():
        o_ref[...] = x_ref[...] + _norm_rows(acc_ref[...], gpost_ref[...])


def _mlp(x2, g_pre, w_up, w_down, g_post, tm=1024, tf=1024):
    n, d = x2.shape
    dff = w_up.shape[1]
    return pl.pallas_call(
        _mlp_kernel, grid=(n // tm, dff // tf),
        in_specs=[pl.BlockSpec((tm, d), lambda i, k: (i, 0)), pl.BlockSpec((1, d), lambda i, k: (0, 0)),
                  pl.BlockSpec((d, tf), lambda i, k: (0, k)), pl.BlockSpec((tf, d), lambda i, k: (k, 0)),
                  pl.BlockSpec((1, d), lambda i, k: (0, 0))],
        out_specs=pl.BlockSpec((tm, d), lambda i, k: (i, 0)),
        out_shape=jax.ShapeDtypeStruct((n, d), F32),
        scratch_shapes=[pltpu.VMEM((tm, d), BF16), pltpu.VMEM((tm, d), F32)],
        compiler_params=_params(("parallel", "arbitrary")), name="mlp",
    )(x2, g_pre.reshape(1, d), w_up.astype(BF16), w_down.astype(BF16), g_post.reshape(1, d))


def _rope_tables(t, dim, groups):
    inv = 1.0 / (ROPE_THETA ** (jnp.arange(0, dim, 2, dtype=F32) / dim))
    ang = jnp.arange(t, dtype=F32)[:, None] * inv[None, :]
    cos, sin = jnp.cos(ang), jnp.sin(ang)
    return jnp.tile(jnp.concatenate([cos, cos], axis=1), (1, groups)), jnp.tile(jnp.concatenate([-sin, sin], axis=1), (1, groups))


def _pad_rows(w, rows, offset):
    return jnp.zeros((rows, w.shape[1]), w.dtype).at[offset:offset + w.shape[0]].set(w)


def _pad_cols(w, cols):
    return jnp.pad(w, ((0, 0), (0, cols - w.shape[1])))


def kernel(x, norm_mix_pre, norm_mix_post, norm_mlp_pre, norm_mlp_post, w_in, w_branch, w_out, hgrn_lb_logits, hgrn_norm_w, fox_f_bias, rwkv_mu, rwkv_w0, rwkv_w2, rwkv_a0, rwkv_a2, rwkv_g2, rwkv_k_k, rwkv_k_a, rwkv_r_k, rwkv_ln_w, rwkv_ln_b, rwkv_v0, rwkv_v1, rwkv_v2, w_up, w_down):
    bsz, t, d = x.shape
    n = bsz * t
    depth = w_in.shape[0]
    W = WIDTH
    topk = min(TOPK_MAX, t // 4)

    lb_soft = jax.nn.softmax(hgrn_lb_logits.astype(F32), axis=0)
    lower_bounds = jnp.cumsum(lb_soft, axis=0) - lb_soft[0:1]
    tables = _rope_tables(t, HEAD_DIM, N_HEADS) + _rope_tables(t, IDX_DIM, IDX_HEADS)

    o_a, o_b, o_iq = 0, 4 * W, 7 * W
    o_ik, o_iw = o_iq + IDX_HEADS * IDX_DIM, o_iq + IDX_HEADS * IDX_DIM + IDX_DIM
    o_c = o_iw + IDX_HEADS
    c_cols = 3 * W + 128
    o_d = o_c + c_cols
    o_df = o_d + 3 * W
    o_g = o_df + N_HEADS
    main_blocks = dict(b_qkv=4, c=2, df=(7 * W + c_cols) // LANES, gates=(7 * W + c_cols + LANES + 256) // d)
    assert main_blocks["c"] * c_cols == 7 * W and main_blocks["gates"] * d == 7 * W + c_cols + LANES + 256

    x2 = x.reshape(n, d)
    v_first = None
    for l in range(depth):
        wl = w_in[l]
        w_main = jnp.concatenate([
            wl[:, o_a:o_a + 7 * W], wl[:, o_c:o_c + c_cols], _pad_cols(wl[:, o_df:o_df + N_HEADS], LANES),
            jnp.zeros((d, 256), F32), wl[:, o_g:o_g + 4 * d]], axis=1)
        w_idx = jnp.concatenate([
            wl[:, o_iq:o_iq + W], jnp.tile(wl[:, o_ik:o_ik + IDX_DIM], (1, LANES // IDX_DIM)),
            _pad_cols(wl[:, o_iw:o_iw + IDX_HEADS], LANES)], axis=1)
        w_fox = wl[:, o_d:o_d + 3 * W]

        main2 = _norm_matmul(x2, norm_mix_pre[l], w_main, F32, tm=1024, tn=1024)
        idx2 = _norm_matmul(x2, norm_mix_pre[l], w_idx, F32, tm=1024, tn=w_idx.shape[1], precise=True)
        fox2 = _norm_matmul(x2, norm_mix_pre[l], w_fox, BF16, tm=1024, tn=3 * W)
        main3 = main2.reshape(bsz, t, -1)

        y_a = _hgrn(main3, lower_bounds[l], hgrn_norm_w[l])

        prep = _dsa_prep(main3, idx2.reshape(bsz, t, -1), tables, main_blocks["b_qkv"])
        y_b = _dsa(*prep, topk)

        prm = dict(
            mu=_pad_cols(rwkv_mu[l][None, :], RWKV_COLS), w0=rwkv_w0[l][None, :],
            w2=_pad_rows(rwkv_w2[l], RWKV_LORA_LANES, 0), a0=rwkv_a0[l][None, :],
            a2=_pad_rows(rwkv_a2[l], RWKV_LORA_LANES, 32), g2=_pad_rows(rwkv_g2[l], RWKV_LORA_LANES, 64),
            k_k=rwkv_k_k[l][None, :], k_a=rwkv_k_a[l][None, :], r_k=rwkv_r_k[l].reshape(1, W),
            ln_w=rwkv_ln_w[l][None, :], ln_b=rwkv_ln_b[l][None, :])
        if l > 0:
            prm.update(v0=rwkv_v0[l - 1][None, :], v1=_pad_cols(rwkv_v1[l - 1], LANES),
                       v2=_pad_rows(rwkv_v2[l - 1], LANES, 0))
        y_c, v_c = _rwkv(main3, main_blocks["c"], v_first, prm)
        if l == 0:
            v_first = v_c

        c_col = _fox_cum(main3, main_blocks["df"], _pad_cols(fox_f_bias[l][None, :], LANES))
        c_row = jnp.transpose(c_col[:, :, :8], (0, 2, 1))
        y_d = _fox(fox2.reshape(bsz, t, 3 * W), c_row)

        ys = [y.reshape(n, W) for y in (y_a, y_b, y_c, y_d)]
        x2 = _merge(x2, ys, main2, main_blocks["gates"], w_branch[l], w_out[l], norm_mix_post[l])
        x2 = _mlp(x2, norm_mlp_pre[l], w_up[l], w_down[l], norm_mlp_post[l])
    return x2.reshape(bsz, t, d)
```

```python
import functools

import jax
import jax.numpy as jnp
from jax import lax
from jax.experimental import pallas as pl
from jax.experimental.pallas import tpu as pltpu

F32 = jnp.float32
BF16 = jnp.bfloat16

HEAD_DIM = 64
N_HEADS = 4
WIDTH = N_HEADS * HEAD_DIM
IDX_HEADS = 8
IDX_DIM = 32
TOPK_MAX = 256
ROPE_THETA = 10000.0
NORM_EPS = 1e-6
RWKV_GN_EPS = 64e-5
MASK_VALUE = -1e30
LOG2E = 1.4426950408889634
RWKV_LORA_LANES = 128
RWKV_COLS = 3 * WIDTH + RWKV_LORA_LANES

LANES = 128
VMEM_LIMIT = 48 * 1024 * 1024

NN = ((1,), (0,))
NT = ((1,), (1,))
TN = ((0,), (0,))


def _dot(a, b, dims=NN):
    return lax.dot_general(a, b, (dims, ((), ())), preferred_element_type=F32)


def _split2(x):
    hi = x.astype(BF16)
    lo = (x - hi.astype(F32)).astype(BF16)
    return hi, lo


def _split3(x):
    hi = x.astype(BF16)
    r1 = x - hi.astype(F32)
    mid = r1.astype(BF16)
    lo = (r1 - mid.astype(F32)).astype(BF16)
    return hi, mid, lo


def _dot3(a, b, dims=NN):
    ah, al = _split2(a)
    bh, bl = _split2(b)
    return _dot(ah, bh, dims) + (_dot(ah, bl, dims) + _dot(al, bh, dims))


def _dot_exact_lhs(a_bf16, b, dims=NN):
    b1, b2, b3 = _split3(b)
    return _dot(a_bf16, b1, dims) + (_dot(a_bf16, b2, dims) + _dot(a_bf16, b3, dims))


def _dot_exact_rhs(a, b_bf16, dims=NN):
    a1, a2 = _split2(a)
    return _dot(a1, b_bf16, dims) + _dot(a2, b_bf16, dims)


def _iota(shape, dim):
    return lax.broadcasted_iota(jnp.int32, shape, dim)


def _head_block_ones(n, group):
    return (_iota((n, n), 0) // group == _iota((n, n), 1) // group).astype(BF16)


def _lower_tri_ones(n):
    return (_iota((n, n), 0) >= _iota((n, n), 1)).astype(BF16)


def _sigmoid(x):
    return 1.0 / (1.0 + jnp.exp(-x))


def _softplus(x):
    return jnp.maximum(x, 0.0) + jnp.log(1.0 + jnp.exp(-jnp.abs(x)))


def _params(sem):
    return pltpu.CompilerParams(dimension_semantics=sem, vmem_limit_bytes=VMEM_LIMIT)


def _norm_rows(x, g):
    return x * lax.rsqrt(jnp.mean(x * x, axis=-1, keepdims=True) + NORM_EPS) * g


def _norm_matmul_kernel(x_ref, g_ref, w_ref, o_ref, h_ref):
    @pl.when(pl.program_id(1) == 0)
    def _():
        h_ref[...] = _norm_rows(x_ref[...], g_ref[...]).astype(BF16)

    o_ref[...] = _dot(h_ref[...], w_ref[...]).astype(o_ref.dtype)


def _norm_matmul3_kernel(x_ref, g_ref, wh_ref, wl_ref, o_ref, hh_ref, hl_ref):
    @pl.when(pl.program_id(1) == 0)
    def _():
        hh, hl = _split2(_norm_rows(x_ref[...], g_ref[...]))
        hh_ref[...] = hh
        hl_ref[...] = hl

    o_ref[...] = (_dot(hh_ref[...], wh_ref[...])
                  + (_dot(hh_ref[...], wl_ref[...]) + _dot(hl_ref[...], wh_ref[...]))).astype(o_ref.dtype)


def _norm_matmul(x2, g, w, out_dtype, tm, tn, precise=False):
    n, d = x2.shape
    cols = w.shape[1]
    grid = (n // tm, cols // tn)
    x_spec = pl.BlockSpec((tm, d), lambda i, j: (i, 0))
    g_spec = pl.BlockSpec((1, d), lambda i, j: (0, 0))
    w_spec = pl.BlockSpec((d, tn), lambda i, j: (0, j))
    o_spec = pl.BlockSpec((tm, tn), lambda i, j: (i, j))
    g2 = g.reshape(1, d)
    if precise:
        wh, wl = _split2(w)
        return pl.pallas_call(
            _norm_matmul3_kernel, grid=grid,
            in_specs=[x_spec, g_spec, w_spec, w_spec], out_specs=o_spec,
            out_shape=jax.ShapeDtypeStruct((n, cols), out_dtype),
            scratch_shapes=[pltpu.VMEM((tm, d), BF16), pltpu.VMEM((tm, d), BF16)],
            compiler_params=_params(("parallel", "arbitrary")), name="norm_proj_precise",
        )(x2, g2, wh, wl)
    return pl.pallas_call(
        _norm_matmul_kernel, grid=grid,
        in_specs=[x_spec, g_spec, w_spec], out_specs=o_spec,
        out_shape=jax.ShapeDtypeStruct((n, cols), out_dtype),
        scratch_shapes=[pltpu.VMEM((tm, d), BF16)],
        compiler_params=_params(("parallel", "arbitrary")), name="norm_proj",
    )(x2, g2, w.astype(BF16))


HGRN_CHUNK = 64
HGRN_GROUP = 8


def _dot1(a, b, dims=NN):
    return _dot(a.astype(BF16), b.astype(BF16), dims)


def _hgrn_kernel(q_ref, f_ref, i_ref, g_ref, lb_ref, nw_ref, o_ref, st_ref, o_sc, *, n_chunks):
    C, G = HGRN_CHUNK, HGRN_GROUP

    @pl.when(pl.program_id(1) == 0)
    def _():
        st_ref[...] = jnp.zeros_like(st_ref)

    lb = lb_ref[...]
    nw = nw_ref[...]
    tri = _lower_tri_ones(C)
    bd = _head_block_ones(WIDTH, HEAD_DIM)
    bd_mask = _iota((WIDTH, WIDTH), 0) // HEAD_DIM == _iota((WIDTH, WIDTH), 1) // HEAD_DIM
    row_top = _iota((G, 1), 0)

    def chunk(c, carry):
        sl = pl.ds(pl.multiple_of(c * C, C), C)
        fl = f_ref[0, sl, :]
        f = lb + (1.0 - lb) * _sigmoid(fl)
        k = (1.0 - lb) * _sigmoid(-fl)
        b = _dot_exact_lhs(tri, jnp.log(f))
        q = q_ref[0, sl, :] * (HEAD_DIM ** -0.5)
        v = i_ref[0, sl, :]
        st = st_ref[...]
        o_sc[...] = _dot1(q * jnp.exp(b), st, NT)

        for s0 in range(0, C, G):
            rows = C - s0
            q_top, b_top = q[s0:s0 + G], b[s0:s0 + G]
            q_rest, b_rest = q[s0 + G:], b[s0 + G:]
            pieces = []
            for u in range(G):
                bs = b[s0 + u:s0 + u + 1]
                ks = k[s0 + u:s0 + u + 1]
                top = q_top * ks * jnp.exp(jnp.minimum(b_top - bs, 0.0))
                pieces.append(jnp.where(row_top >= u, top, 0.0))
                if rows > G:
                    pieces.append(q_rest * ks * jnp.exp(b_rest - bs))
            x = jnp.concatenate(pieces, axis=0).astype(BF16)
            r = _dot(x, bd)
            og = jnp.zeros((rows, WIDTH), F32)
            for u in range(G):
                og = og + r[u * rows:(u + 1) * rows] * v[s0 + u:s0 + u + 1]
            o_sc[s0:, :] += og

        b_last = b[C - 1:C, :]
        upd = _dot1(v, k * jnp.exp(b_last - b), TN)
        st_ref[...] = st * jnp.exp(b_last) + jnp.where(bd_mask, upd, 0.0)

        o = o_sc[...]
        ms = _dot_exact_rhs(o * o, bd) * (1.0 / HEAD_DIM)
        on = o * lax.rsqrt(ms + NORM_EPS) * nw
        gl = g_ref[0, sl, :]
        o_ref[0, sl, :] = (on * (gl * _sigmoid(gl))).astype(o_ref.dtype)
        return carry

    lax.fori_loop(0, n_chunks, chunk, 0)


def _hgrn(main3, lb, norm_w, tb=256):
    bsz, t, _ = main3.shape
    col = lambda j: pl.BlockSpec((1, tb, WIDTH), lambda b, i, j=j: (b, i, j))
    vec = pl.BlockSpec((1, WIDTH), lambda b, i: (0, 0))
    return pl.pallas_call(
        functools.partial(_hgrn_kernel, n_chunks=tb // HGRN_CHUNK),
        grid=(bsz, t // tb),
        in_specs=[col(0), col(1), col(2), col(3), vec, vec],
        out_specs=pl.BlockSpec((1, tb, WIDTH), lambda b, i: (b, i, 0)),
        out_shape=jax.ShapeDtypeStruct((bsz, t, WIDTH), BF16),
        scratch_shapes=[pltpu.VMEM((WIDTH, WIDTH), F32), pltpu.VMEM((HGRN_CHUNK, WIDTH), F32)],
        compiler_params=_params(("parallel", "arbitrary")), name="hgrn2",
    )(main3, main3, main3, main3, lb.reshape(1, WIDTH), jnp.tile(norm_w, N_HEADS).reshape(1, WIDTH))


RWKV_CHUNK = 64


def _rwkv_kernel(*refs, n_chunks, has_vgate):
    if has_vgate:
        (c_ref, vf_ref, mu_ref, w0_ref, w2_ref, a0_ref, a2_ref, g2_ref, kkw_ref, kaw_ref, rk_ref,
         lnw_ref, lnb_ref, v0_ref, v1_ref, v2_ref, y_ref, vout_ref, s_ref, prev_ref, y_sc) = refs
    else:
        (c_ref, mu_ref, w0_ref, w2_ref, a0_ref, a2_ref, g2_ref, kkw_ref, kaw_ref, rk_ref,
         lnw_ref, lnb_ref, y_ref, vout_ref, s_ref, prev_ref, y_sc) = refs
    C = RWKV_CHUNK
    W = WIDTH
    tb = n_chunks * C

    @pl.when(pl.program_id(1) == 0)
    def _():
        s_ref[...] = jnp.zeros_like(s_ref)
        prev_ref[...] = jnp.zeros_like(prev_ref)

    p = c_ref[0]
    shifted = jnp.where(_iota((tb, 1), 0) == 0, prev_ref[...], pltpu.roll(p, 1, axis=0))
    prev_ref[...] = p[tb - 1:tb, :]
    xm = p + (shifted - p) * mu_ref[...]
    r = xm[:, 0:W]
    k = xm[:, W:2 * W]
    v = xm[:, 2 * W:3 * W]
    lora = xm[:, 3 * W:3 * W + RWKV_LORA_LANES]

    bd = _head_block_ones(W, HEAD_DIM)
    w_log = -_softplus(-(w0_ref[...] + _dot3(jnp.tanh(lora), w2_ref[...]))) - 0.5
    log_decay = -jnp.exp(w_log)
    a = _sigmoid(a0_ref[...] + _dot3(lora, a2_ref[...]))
    g = _dot3(_sigmoid(lora), g2_ref[...])
    if has_vgate:
        vg = _dot3(_dot3(v, v1_ref[...]), v2_ref[...])
        v = v + (vf_ref[0] - v) * _sigmoid(v0_ref[...] + vg)
    vout_ref[0] = v
    kk = k * kkw_ref[...]
    kk = kk * lax.rsqrt(jnp.maximum(_dot_exact_rhs(kk * kk, bd), 1e-24))
    k = k * (1.0 + (a - 1.0) * kaw_ref[...])

    tri = _lower_tri_ones(C)
    lane_head = _iota((1, W), 1) // HEAD_DIM
    rt = _iota((N_HEADS * C, N_HEADS * C), 0) % C
    ct = _iota((N_HEADS * C, N_HEADS * C), 1) % C
    strict_lower = rt > ct
    lower = rt >= ct

    def stack(m):
        return jnp.concatenate([jnp.where(lane_head == h, m, 0.0) for h in range(N_HEADS)], axis=0)

    for c in range(n_chunks):
        sl = slice(c * C, (c + 1) * C)
        ld = log_decay[sl]
        cum = _dot_exact_lhs(tri, ld)
        pdec = jnp.exp(cum)
        pinv = jnp.exp(-cum)
        a_s = stack(-kk[sl] * jnp.exp(cum - ld))
        b_s = stack(kk[sl] * a[sl] * pinv)
        k_s = stack(k[sl] * pinv)
        r_s = stack(r[sl] * pdec)
        v_s = stack(v[sl])
        s = s_ref[...]

        m = jnp.where(strict_lower, _dot1(a_s, b_s, NT), 0.0)
        lak = jnp.where(strict_lower, _dot1(a_s, k_s, NT), 0.0)
        lrb = jnp.where(lower, _dot1(r_s, b_s, NT), 0.0)
        lrk = jnp.where(lower, _dot1(r_s, k_s, NT), 0.0)

        z = _dot1(a_s, s, NT) + _dot1(lak, v_s)
        z = z + _dot1(m, z)
        power = 2
        while power < C:
            m = _dot1(m, m)
            z = z + _dot1(m, z)
            power *= 2

        y = _dot1(r_s, s, NT) + _dot1(lrb, z) + _dot1(lrk, v_s)
        y_sc[sl, :] = sum(y[h * C:(h + 1) * C] for h in range(N_HEADS))
        s_ref[...] = (s + _dot1(z, b_s, TN) + _dot1(v_s, k_s, TN)) * pdec[C - 1:C, :]

    y = y_sc[...]
    inv_n = 1.0 / HEAD_DIM
    mean = _dot_exact_rhs(y, bd) * inv_n
    yc = y - mean
    var = _dot_exact_rhs(yc * yc, bd) * inv_n
    yn = yc * lax.rsqrt(var + RWKV_GN_EPS) * lnw_ref[...] + lnb_ref[...]
    yn = yn + _dot_exact_rhs(r * k * rk_ref[...], bd) * v
    y_ref[0] = (yn * g).astype(y_ref.dtype)


def _rwkv(c3, col_block, v_first, prm, tb=256):
    bsz, t, _ = c3.shape
    has_vgate = v_first is not None
    blk = lambda w: pl.BlockSpec((1, tb, w), lambda b, i: (b, i, 0))
    cblk = pl.BlockSpec((1, tb, RWKV_COLS), lambda b, i: (b, i, col_block))
    full = lambda a: pl.BlockSpec(a.shape, lambda b, i: (0,) * a.ndim)
    names = ["mu", "w0", "w2", "a0", "a2", "g2", "k_k", "k_a", "r_k", "ln_w", "ln_b"]
    if has_vgate:
        names += ["v0", "v1", "v2"]
    args = [c3] + ([v_first] if has_vgate else []) + [prm[n] for n in names]
    in_specs = [cblk] + ([blk(WIDTH)] if has_vgate else []) + [full(prm[n]) for n in names]
    return pl.pallas_call(
        functools.partial(_rwkv_kernel, n_chunks=tb // RWKV_CHUNK, has_vgate=has_vgate),
        grid=(bsz, t // tb),
        in_specs=in_specs,
        out_specs=[blk(WIDTH), blk(WIDTH)],
        out_shape=[jax.ShapeDtypeStruct((bsz, t, WIDTH), BF16), jax.ShapeDtypeStruct((bsz, t, WIDTH), F32)],
        scratch_shapes=[pltpu.VMEM((WIDTH, WIDTH), F32), pltpu.VMEM((1, RWKV_COLS), F32),
                        pltpu.VMEM((tb, WIDTH), F32)],
        compiler_params=_params(("parallel", "arbitrary")), name="rwkv7",
    )(*args)


def _fox_cum_kernel(f_ref, bias_ref, c_ref, carry_ref, *, tb):
    @pl.when(pl.program_id(1) == 0)
    def _():
        carry_ref[...] = jnp.zeros_like(carry_ref)

    logf = -_softplus(-(f_ref[0] + bias_ref[...]))
    c = _dot_exact_lhs(_lower_tri_ones(tb), logf) + carry_ref[...]
    carry_ref[...] = c[tb - 1:tb, :]
    for h in range(N_HEADS):
        pick = (_iota((LANES, LANES), 0) == h).astype(BF16)
        c1, c2, c3 = _split3(c)
        c_ref[0, h] = LOG2E * (_dot(c1, pick) + (_dot(c2, pick) + _dot(c3, pick)))


def _fox_cum(main3, col_block, bias_row, tb=256):
    bsz, t, _ = main3.shape
    return pl.pallas_call(
        functools.partial(_fox_cum_kernel, tb=tb),
        grid=(bsz, t // tb),
        in_specs=[pl.BlockSpec((1, tb, LANES), lambda b, i: (b, i, col_block)),
                  pl.BlockSpec((1, LANES), lambda b, i: (0, 0))],
        out_specs=pl.BlockSpec((1, N_HEADS, tb, LANES), lambda b, i: (b, 0, i, 0)),
        out_shape=jax.ShapeDtypeStruct((bsz, N_HEADS, t, LANES), F32),
        scratch_shapes=[pltpu.VMEM((1, LANES), F32)],
        compiler_params=_params(("parallel", "arbitrary")), name="fox_cumgate",
    )(main3, bias_row)


ATT_QB = 128
ATT_KB = 512


def _stack_heads(q):
    lane_head = _iota((1, WIDTH), 1) // HEAD_DIM
    zero = jnp.zeros_like(q)
    return jnp.concatenate([jnp.where(lane_head == h, q, zero) for h in range(N_HEADS)], axis=0)


def _softmax_stage_t(s_t, m_sc, l_sc):
    m_old = m_sc[...]
    m_new = jnp.maximum(m_old, jnp.max(s_t, axis=0, keepdims=True))
    alpha = jnp.exp2(m_old - m_new)
    p_t = jnp.exp2(s_t - m_new)
    l_sc[...] = alpha * l_sc[...] + jnp.sum(p_t, axis=0, keepdims=True)
    m_sc[...] = m_new
    return p_t.astype(BF16), alpha


def _pv_stage_t(vt_blk, p_t, alpha, acc_sc):
    for h in range(N_HEADS):
        cs = slice(h * ATT_QB, (h + 1) * ATT_QB)
        pv = _dot(vt_blk[h * HEAD_DIM:(h + 1) * HEAD_DIM, :], p_t[:, cs])
        acc_sc[h] = alpha[:, cs] * acc_sc[h] + pv


def _attention_t(n_blocks, qk, post, post_last, vt_ref, bufs, m_sc, l_sc, acc_sc):
    KB = ATT_KB
    s_buf, p_buf, a_buf = bufs[0:2], bufs[2:4], bufs[4:6]
    _softmax_init_t(m_sc, l_sc, acc_sc)

    def vt_blk(j):
        return vt_ref[0, :, pl.ds(pl.multiple_of(j * KB, KB), KB)]

    def pending(j, par):
        _pv_stage_t(vt_blk(jnp.maximum(j - 1, 0)), p_buf[1 - par][...], a_buf[1 - par][...], acc_sc)

    def step(j, par):
        raw = s_buf[par][...]
        s_buf[1 - par][...] = qk(j + 1)
        p_t, alpha = _softmax_stage_t(post(j, raw), m_sc, l_sc)
        pending(j, par)
        p_buf[par][...] = p_t
        a_buf[par][...] = alpha

    def final(j, par):
        p_t, alpha = _softmax_stage_t(post_last(j, s_buf[par][...]), m_sc, l_sc)
        pending(j, par)
        _pv_stage_t(vt_blk(j), p_t, alpha, acc_sc)

    s_buf[0][...] = qk(0)
    p_buf[1][...] = jnp.zeros_like(p_buf[1])
    a_buf[1][...] = jnp.ones_like(a_buf[1])
    pairs = (n_blocks - 1) // 2

    def pair(t, carry):
        step(2 * t, 0)
        step(2 * t + 1, 1)
        return carry

    lax.fori_loop(0, pairs, pair, 0)
    odd_left = (n_blocks - 1) % 2 == 1

    @pl.when(odd_left)
    def _():
        step(2 * pairs, 0)
        final(2 * pairs + 1, 1)

    @pl.when(jnp.logical_not(odd_left))
    def _():
        final(2 * pairs, 0)


def _attention_bufs():
    cols = N_HEADS * ATT_QB
    return ([pltpu.VMEM((ATT_KB, cols), F32)] * 2 + [pltpu.VMEM((ATT_KB, cols), BF16)] * 2
            + [pltpu.VMEM((1, cols), F32)] * 2)


def _softmax_init_t(m_sc, l_sc, acc_sc):
    m_sc[...] = jnp.full_like(m_sc, MASK_VALUE)
    l_sc[...] = jnp.zeros_like(l_sc)
    acc_sc[...] = jnp.zeros_like(acc_sc)


def _softmax_finish_t(l_sc, acc_sc):
    l = l_sc[...]
    out_t = jnp.concatenate([acc_sc[h] / l[:, h * ATT_QB:(h + 1) * ATT_QB] for h in range(N_HEADS)], axis=0)
    return out_t.T


def _fox_kernel(q_ref, k_ref, vt_ref, ck_ref, o_ref, m_sc, l_sc, acc_sc, *bufs):
    QB, KB = ATT_QB, ATT_KB
    i = pl.program_id(1)
    qstack = _stack_heads(q_ref[0])
    qpos = i * QB + _iota((1, N_HEADS * QB), 1) % QB

    def qk(j):
        return _dot(k_ref[0, pl.ds(pl.multiple_of(j * KB, KB), KB), :], qstack, NT)

    def post(j, s_t):
        ks = pl.ds(pl.multiple_of(j * KB, KB), KB)
        return jnp.concatenate([s_t[:, h * QB:(h + 1) * QB] - ck_ref[0, h, ks, :] for h in range(N_HEADS)], axis=1)

    def post_last(j, s_t):
        return jnp.where(j * KB + _iota((KB, 1), 0) <= qpos, post(j, s_t), MASK_VALUE)

    _attention_t((i * QB) // KB + 1, qk, post, post_last, vt_ref, bufs, m_sc, l_sc, acc_sc)
    o_ref[0] = _softmax_finish_t(l_sc, acc_sc).astype(o_ref.dtype)


def _fox(qkv3, v_t, c_rep):
    bsz, t, _ = qkv3.shape
    cols = N_HEADS * ATT_QB
    return pl.pallas_call(
        _fox_kernel,
        grid=(bsz, t // ATT_QB),
        in_specs=[pl.BlockSpec((1, ATT_QB, WIDTH), lambda b, i: (b, i, 0)),
                  pl.BlockSpec((1, t, WIDTH), lambda b, i: (b, 0, 1)),
                  pl.BlockSpec((1, WIDTH, t), lambda b, i: (b, 0, 0)),
                  pl.BlockSpec((1, N_HEADS, t, LANES), lambda b, i: (b, 0, 0, 0))],
        out_specs=pl.BlockSpec((1, ATT_QB, WIDTH), lambda b, i: (b, i, 0)),
        out_shape=jax.ShapeDtypeStruct((bsz, t, WIDTH), BF16),
        scratch_shapes=[pltpu.VMEM((1, cols), F32), pltpu.VMEM((1, cols), F32),
                        pltpu.VMEM((N_HEADS, HEAD_DIM, ATT_QB), F32)] + _attention_bufs(),
        compiler_params=_params(("parallel", "arbitrary")), name="fox_attention",
    )(qkv3, qkv3, v_t, c_rep)


def _swap_halves(x, half):
    n = x.shape[-1]
    lower = (_iota((1, n), 1) % (2 * half)) < half
    return jnp.where(lower, pltpu.roll(x, n - half, axis=1), pltpu.roll(x, half, axis=1))


def _dsa_prep_kernel(q_ref, k_ref, v_ref, iq_ref, ik_ref, iw_ref, ch_ref, sh_ref, ci_ref, si_ref,
                     selh_ref, sell_ref, qo_ref, ko_ref, vo_ref, q3_ref, k3_ref, wi_ref):
    ch, sh, ci, si = ch_ref[...], sh_ref[...], ci_ref[...], si_ref[...]

    def rope(x, c, s, half):
        return x * c + _swap_halves(x, half) * s

    qo_ref[0] = (rope(q_ref[0], ch, sh, HEAD_DIM // 2) * (LOG2E * HEAD_DIM ** -0.5)).astype(BF16)
    ko_ref[0] = rope(k_ref[0], ch, sh, HEAD_DIM // 2).astype(BF16)
    vo_ref[0] = v_ref[0].astype(BF16)
    qh, ql = _split2(rope(iq_ref[0], ci, si, IDX_DIM // 2) * (IDX_DIM ** -0.5))
    q3_ref[0] = (_dot(qh, selh_ref[...]) + _dot(ql, sell_ref[...])).astype(BF16)
    kh, kl = _split2(rope(ik_ref[0], ci[:, :LANES], si[:, :LANES], IDX_DIM // 2))
    seg = _iota((1, LANES), 1) // IDX_DIM
    k3_ref[0] = jnp.where(seg < 2, kh, jnp.where(seg == 2, kl, jnp.zeros_like(kl)))
    wi_ref[0] = iw_ref[0] * (IDX_HEADS ** -0.5)


def _dsa_prep(main3, idx3, tables, qkv_block0, tb=512):
    bsz, t, _ = main3.shape
    iq_w = IDX_HEADS * IDX_DIM
    r = jnp.arange(iq_w)[:, None]
    c = jnp.arange(IDX_HEADS * LANES)[None, :]
    same = (r // IDX_DIM == c // LANES) & (r % IDX_DIM == c % IDX_DIM)
    seg = (c % LANES) // IDX_DIM
    sel_hi = (same & ((seg == 0) | (seg == 2))).astype(BF16)
    sel_lo = (same & (seg == 1)).astype(BF16)
    mcol = lambda j: pl.BlockSpec((1, tb, WIDTH), lambda b, i, j=j: (b, i, qkv_block0 + j))
    tab = pl.BlockSpec((tb, WIDTH), lambda b, i: (i, 0))
    sel = pl.BlockSpec(sel_hi.shape, lambda b, i: (0, 0))
    out = lambda w: pl.BlockSpec((1, tb, w), lambda b, i: (b, i, 0))
    shape = lambda dt, w: jax.ShapeDtypeStruct((bsz, t, w), dt)
    return pl.pallas_call(
        _dsa_prep_kernel, grid=(bsz, t // tb),
        in_specs=[mcol(0), mcol(1), mcol(2), out(iq_w),
                  pl.BlockSpec((1, tb, LANES), lambda b, i: (b, i, iq_w // LANES)),
                  pl.BlockSpec((1, tb, LANES), lambda b, i: (b, i, iq_w // LANES + 1)),
                  tab, tab, tab, tab, sel, sel],
        out_specs=[out(WIDTH)] * 3 + [out(IDX_HEADS * LANES), out(LANES), out(LANES)],
        out_shape=[shape(BF16, WIDTH)] * 3 + [shape(BF16, IDX_HEADS * LANES), shape(BF16, LANES), shape(F32, LANES)],
        compiler_params=_params(("parallel", "parallel")), name="dsa_prep",
    )(main3, main3, main3, idx3, idx3, idx3, *tables, sel_hi, sel_lo)


def _dsa_kernel(q_ref, k_ref, vt_ref, q3_ref, k3_ref, wt_ref, o_ref, key_sc, tie_sc, m_sc, l_sc, acc_sc,
                *bufs, topk):
    QB, KB = ATT_QB, ATT_KB
    i = pl.program_id(1)
    nkb = (i * QB) // KB + 1
    qpos = i * QB + _iota((1, QB), 1)
    int_min = jnp.int32(-2 ** 31)

    q3 = q3_ref[0]
    qs = jnp.concatenate([q3[:, h * LANES:(h + 1) * LANES] for h in range(IDX_HEADS)], axis=0)
    wt = wt_ref[0]

    def score_block(j, carry):
        ks = pl.ds(pl.multiple_of(j * KB, KB), KB)
        r = _dot(k3_ref[0, ks, :], qs, NT)
        sc = jnp.zeros((KB, QB), F32)
        for h in range(IDX_HEADS):
            sc = sc + jnp.maximum(r[:, h * QB:(h + 1) * QB], 0.0) * wt[h:h + 1, :]
        sc = jnp.where(j * KB + _iota((KB, 1), 0) <= qpos, sc, MASK_VALUE)
        sc = jnp.where(sc == 0.0, 0.0, sc)
        bits = pltpu.bitcast(sc, jnp.int32)
        key_sc[ks, :] = jnp.where(bits < 0, bits ^ jnp.int32(0x7FFFFFFF), bits)
        return carry

    lax.fori_loop(0, nkb, score_block, 0)

    def count_ge(cand):
        def blk(j, acc):
            base = pl.multiple_of(j * KB, KB)
            for r0 in range(0, KB, 64):
                acc = acc + jnp.where(key_sc[pl.ds(base + r0, 64), :] >= cand, 1.0, 0.0)
            return acc
        acc = lax.fori_loop(0, nkb, blk, jnp.zeros((64, QB), F32))
        return jnp.sum(acc, axis=0, keepdims=True)

    kf = float(topk)

    def bit_step(step, v):
        trial = jnp.where(step == 0, jnp.zeros_like(v), v | (jnp.int32(1) << (31 - step)))
        return jnp.where(count_ge(trial) >= kf, trial, v)

    thr = lax.fori_loop(0, 32, bit_step, jnp.full((1, QB), int_min, jnp.int32))
    need = kf - count_ge(thr + 1)

    lt = _lower_tri_ones(KB)
    qstack = _stack_heads(q_ref[0])

    def qk(j):
        return _dot(k_ref[0, pl.ds(pl.multiple_of(j * KB, KB), KB), :], qstack, NT)

    def post(j, s_t):
        ks = pl.ds(pl.multiple_of(j * KB, KB), KB)
        key = key_sc[ks, :]
        tie = key == thr
        tie_b = jnp.where(tie, 1.0, 0.0).astype(BF16)
        rank = tie_sc[...] + _dot(lt, tie_b)
        tie_sc[...] = rank[KB - 1:KB, :]
        sel = (key > thr) | (tie & (rank <= need))
        sel = sel & (j * KB + _iota((KB, 1), 0) <= qpos)
        bias = jnp.where(sel, 0.0, MASK_VALUE)
        return s_t + jnp.concatenate([bias] * N_HEADS, axis=1)

    tie_sc[...] = jnp.zeros_like(tie_sc)
    _attention_t(nkb, qk, post, post, vt_ref, bufs, m_sc, l_sc, acc_sc)
    o_ref[0] = _softmax_finish_t(l_sc, acc_sc).astype(o_ref.dtype)


def _dsa(q, k, v_t, q3, k3, w_t, topk):
    bsz, t, _ = q.shape
    qblk = lambda w: pl.BlockSpec((1, ATT_QB, w), lambda b, i: (b, i, 0))
    seq = lambda w: pl.BlockSpec((1, t, w), lambda b, i: (b, 0, 0))
    cols = N_HEADS * ATT_QB
    return pl.pallas_call(
        functools.partial(_dsa_kernel, topk=topk),
        grid=(bsz, t // ATT_QB),
        in_specs=[qblk(WIDTH), seq(WIDTH), pl.BlockSpec((1, WIDTH, t), lambda b, i: (b, 0, 0)),
                  qblk(IDX_HEADS * LANES), seq(LANES),
                  pl.BlockSpec((1, IDX_HEADS, ATT_QB), lambda b, i: (b, 0, i))],
        out_specs=qblk(WIDTH),
        out_shape=jax.ShapeDtypeStruct((bsz, t, WIDTH), BF16),
        scratch_shapes=[pltpu.VMEM((t, ATT_QB), jnp.int32), pltpu.VMEM((1, ATT_QB), F32), pltpu.VMEM((1, cols), F32),
                        pltpu.VMEM((1, cols), F32), pltpu.VMEM((N_HEADS, HEAD_DIM, ATT_QB), F32)] + _attention_bufs(),
        compiler_params=_params(("parallel", "arbitrary")), name="dsa_attention",
    )(q, k, v_t, q3, k3, w_t)


def _merge_kernel(x_ref, ya_ref, yb_ref, yc_ref, yd_ref, ga_ref, gb_ref, gc_ref, gd_ref,
                  wb_ref, wo_ref, g_ref, o_ref):
    merged = None
    for n, (y_ref, gate_ref) in enumerate(((ya_ref, ga_ref), (yb_ref, gb_ref), (yc_ref, gc_ref), (yd_ref, gd_ref))):
        term = _sigmoid(gate_ref[...]) * _dot(y_ref[...], wb_ref[n])
        merged = term if merged is None else merged + term
    mix = _dot(merged.astype(BF16), wo_ref[...])
    o_ref[...] = x_ref[...] + _norm_rows(mix, g_ref[...])


def _merge(x2, ys, main2, gate_block0, w_branch, w_out, g_post, tm=256):
    n, d = x2.shape
    row = lambda w: pl.BlockSpec((tm, w), lambda i: (i, 0))
    gate = lambda j: pl.BlockSpec((tm, d), lambda i, j=j: (i, gate_block0 + j))
    return pl.pallas_call(
        _merge_kernel, grid=(n // tm,),
        in_specs=[row(d)] + [row(WIDTH)] * 4 + [gate(j) for j in range(4)]
        + [pl.BlockSpec(w_branch.shape, lambda i: (0, 0, 0)), pl.BlockSpec(w_out.shape, lambda i: (0, 0)),
           pl.BlockSpec((1, d), lambda i: (0, 0))],
        out_specs=row(d),
        out_shape=jax.ShapeDtypeStruct((n, d), F32),
        compiler_params=_params(("parallel",)), name="gated_merge",
    )(x2, *ys, main2, main2, main2, main2, w_branch.astype(BF16), w_out.astype(BF16), g_post.reshape(1, d))


def _mlp_kernel(x_ref, gpre_ref, wu_ref, wd_ref, gpost_ref, o_ref, h_ref, acc_ref):
    kf = pl.program_id(1)

    @pl.when(kf == 0)
    def _():
        h_ref[...] = _norm_rows(x_ref[...], gpre_ref[...]).astype(BF16)
        acc_ref[...] = jnp.zeros_like(acc_ref)

    u = jnp.maximum(_dot(h_ref[...], wu_ref[...]), 0.0)
    acc_ref[...] += _dot((u * u).astype(BF16), wd_ref[...])

    @pl.when(kf == pl.num_programs(1) - 1)
    def _():
        o_ref[...] = x_ref[...] + _norm_rows(acc_ref[...], gpost_ref[...])


def _mlp(x2, g_pre, w_up, w_down, g_post, tm=1024, tf=1024):
    n, d = x2.shape
    dff = w_up.shape[1]
    return pl.pallas_call(
        _mlp_kernel, grid=(n // tm, dff // tf),
        in_specs=[pl.BlockSpec((tm, d), lambda i, k: (i, 0)), pl.BlockSpec((1, d), lambda i, k: (0, 0)),
                  pl.BlockSpec((d, tf), lambda i, k: (0, k)), pl.BlockSpec((tf, d), lambda i, k: (k, 0)),
                  pl.BlockSpec((1, d), lambda i, k: (0, 0))],
        out_specs=pl.BlockSpec((tm, d), lambda i, k: (i, 0)),
        out_shape=jax.ShapeDtypeStruct((n, d), F32),
        scratch_shapes=[pltpu.VMEM((tm, d), BF16), pltpu.VMEM((tm, d), F32)],
        compiler_params=_params(("parallel", "arbitrary")), name="mlp",
    )(x2, g_pre.reshape(1, d), w_up.astype(BF16), w_down.astype(BF16), g_post.reshape(1, d))


def _rope_tables(t, dim, groups):
    inv = 1.0 / (ROPE_THETA ** (jnp.arange(0, dim, 2, dtype=F32) / dim))
    ang = jnp.arange(t, dtype=F32)[:, None] * inv[None, :]
    cos, sin = jnp.cos(ang), jnp.sin(ang)
    return jnp.tile(jnp.concatenate([cos, cos], axis=1), (1, groups)), jnp.tile(jnp.concatenate([-sin, sin], axis=1), (1, groups))


def _pad_rows(w, rows, offset):
    return jnp.zeros((rows, w.shape[1]), w.dtype).at[offset:offset + w.shape[0]].set(w)


def _pad_cols(w, cols):
    return jnp.pad(w, ((0, 0), (0, cols - w.shape[1])))


def kernel(x, norm_mix_pre, norm_mix_post, norm_mlp_pre, norm_mlp_post, w_in, w_branch, w_out, hgrn_lb_logits, hgrn_norm_w, fox_f_bias, rwkv_mu, rwkv_w0, rwkv_w2, rwkv_a0, rwkv_a2, rwkv_g2, rwkv_k_k, rwkv_k_a, rwkv_r_k, rwkv_ln_w, rwkv_ln_b, rwkv_v0, rwkv_v1, rwkv_v2, w_up, w_down):
    bsz, t, d = x.shape
    n = bsz * t
    depth = w_in.shape[0]
    W = WIDTH
    topk = min(TOPK_MAX, t // 4)

    lb_soft = jax.nn.softmax(hgrn_lb_logits.astype(F32), axis=0)
    lower_bounds = jnp.cumsum(lb_soft, axis=0) - lb_soft[0:1]
    tables = _rope_tables(t, HEAD_DIM, N_HEADS) + _rope_tables(t, IDX_DIM, IDX_HEADS)

    o_a, o_b, o_iq = 0, 4 * W, 7 * W
    o_ik, o_iw = o_iq + IDX_HEADS * IDX_DIM, o_iq + IDX_HEADS * IDX_DIM + IDX_DIM
    o_c = o_iw + IDX_HEADS
    c_cols = 3 * W + 128
    o_d = o_c + c_cols
    o_df = o_d + 3 * W
    o_g = o_df + N_HEADS
    main_blocks = dict(b_qkv=4, c=2, df=(7 * W + c_cols) // LANES, gates=(7 * W + c_cols + LANES + 256) // d)
    assert main_blocks["c"] * c_cols == 7 * W and main_blocks["gates"] * d == 7 * W + c_cols + LANES + 256

    x2 = x.reshape(n, d)
    v_first = None
    for l in range(depth):
        wl = w_in[l]
        w_main = jnp.concatenate([
            wl[:, o_a:o_a + 7 * W], wl[:, o_c:o_c + c_cols], _pad_cols(wl[:, o_df:o_df + N_HEADS], LANES),
            jnp.zeros((d, 256), F32), wl[:, o_g:o_g + 4 * d]], axis=1)
        w_idx = jnp.concatenate([
            wl[:, o_iq:o_iq + W], jnp.tile(wl[:, o_ik:o_ik + IDX_DIM], (1, LANES // IDX_DIM)),
            _pad_cols(wl[:, o_iw:o_iw + IDX_HEADS], LANES)], axis=1)
        w_fox = jnp.concatenate([wl[:, o_d:o_d + W] * (LOG2E * HEAD_DIM ** -0.5), wl[:, o_d + W:o_d + 3 * W]], axis=1)

        main2 = _norm_matmul(x2, norm_mix_pre[l], w_main, F32, tm=1024, tn=1024)
        idx2 = _norm_matmul(x2, norm_mix_pre[l], w_idx, F32, tm=1024, tn=w_idx.shape[1], precise=True)
        fox2 = _norm_matmul(x2, norm_mix_pre[l], w_fox, BF16, tm=1024, tn=3 * W)
        main3 = main2.reshape(bsz, t, -1)

        y_a = _hgrn(main3, lower_bounds[l], hgrn_norm_w[l])

        q_b, k_b, v_b, q3, k3, wi = _dsa_prep(main3, idx2.reshape(bsz, t, -1), tables, main_blocks["b_qkv"])
        y_b = _dsa(q_b, k_b, jnp.transpose(v_b, (0, 2, 1)), q3, k3,
                   jnp.transpose(wi[:, :, :IDX_HEADS], (0, 2, 1)), topk)

        prm = dict(
            mu=_pad_cols(rwkv_mu[l][None, :], RWKV_COLS), w0=rwkv_w0[l][None, :],
            w2=_pad_rows(rwkv_w2[l], RWKV_LORA_LANES, 0), a0=rwkv_a0[l][None, :],
            a2=_pad_rows(rwkv_a2[l], RWKV_LORA_LANES, 32), g2=_pad_rows(rwkv_g2[l], RWKV_LORA_LANES, 64),
            k_k=rwkv_k_k[l][None, :], k_a=rwkv_k_a[l][None, :], r_k=rwkv_r_k[l].reshape(1, W),
            ln_w=rwkv_ln_w[l][None, :], ln_b=rwkv_ln_b[l][None, :])
        if l > 0:
            prm.update(v0=rwkv_v0[l - 1][None, :], v1=_pad_cols(rwkv_v1[l - 1], LANES),
                       v2=_pad_rows(rwkv_v2[l - 1], LANES, 0))
        y_c, v_c = _rwkv(main3, main_blocks["c"], v_first, prm)
        if l == 0:
            v_first = v_c

        c_rep = _fox_cum(main3, main_blocks["df"], _pad_cols(fox_f_bias[l][None, :], LANES))
        fox3 = fox2.reshape(bsz, t, 3 * W)
        y_d = _fox(fox3, jnp.transpose(fox3[:, :, 2 * W:], (0, 2, 1)), c_rep)

        ys = [y.reshape(n, W) for y in (y_a, y_b, y_c, y_d)]
        x2 = _merge(x2, ys, main2, main_blocks["gates"], w_branch[l], w_out[l], norm_mix_post[l])
        x2 = _mlp(x2, norm_mlp_pre[l], w_up[l], w_down[l], norm_mlp_post[l])
    return x2.reshape(bsz, t, d)
```

```python
import functools

import jax
import jax.numpy as jnp
from jax import lax
from jax.experimental import pallas as pl
from jax.experimental.pallas import tpu as pltpu

F32 = jnp.float32
BF16 = jnp.bfloat16

HEAD_DIM = 64
N_HEADS = 4
WIDTH = N_HEADS * HEAD_DIM
IDX_HEADS = 8
IDX_DIM = 32
TOPK_MAX = 256
ROPE_THETA = 10000.0
NORM_EPS = 1e-6
RWKV_GN_EPS = 64e-5
MASK_VALUE = -1e30
LOG2E = 1.4426950408889634
RWKV_LORA_LANES = 128
RWKV_COLS = 3 * WIDTH + RWKV_LORA_LANES

LANES = 128
VMEM_LIMIT = 48 * 1024 * 1024

NN = ((1,), (0,))
NT = ((1,), (1,))
TN = ((0,), (0,))


def _dot(a, b, dims=NN):
    return lax.dot_general(a, b, (dims, ((), ())), preferred_element_type=F32)


def _split2(x):
    hi = x.astype(BF16)
    lo = (x - hi.astype(F32)).astype(BF16)
    return hi, lo


def _split3(x):
    hi = x.astype(BF16)
    r1 = x - hi.astype(F32)
    mid = r1.astype(BF16)
    lo = (r1 - mid.astype(F32)).astype(BF16)
    return hi, mid, lo


def _dot3(a, b, dims=NN):
    ah, al = _split2(a)
    bh, bl = _split2(b)
    return _dot(ah, bh, dims) + (_dot(ah, bl, dims) + _dot(al, bh, dims))


def _dot_exact_lhs(a_bf16, b, dims=NN):
    b1, b2, b3 = _split3(b)
    return _dot(a_bf16, b1, dims) + (_dot(a_bf16, b2, dims) + _dot(a_bf16, b3, dims))


def _dot_exact_rhs(a, b_bf16, dims=NN):
    a1, a2 = _split2(a)
    return _dot(a1, b_bf16, dims) + _dot(a2, b_bf16, dims)


def _iota(shape, dim):
    return lax.broadcasted_iota(jnp.int32, shape, dim)


def _head_block_ones(n, group):
    return (_iota((n, n), 0) // group == _iota((n, n), 1) // group).astype(BF16)


def _lower_tri_ones(n):
    return (_iota((n, n), 0) >= _iota((n, n), 1)).astype(BF16)


def _sigmoid(x):
    return 1.0 / (1.0 + jnp.exp(-x))


def _softplus(x):
    return jnp.maximum(x, 0.0) + jnp.log(1.0 + jnp.exp(-jnp.abs(x)))


def _params(sem):
    return pltpu.CompilerParams(dimension_semantics=sem, vmem_limit_bytes=VMEM_LIMIT)


def _norm_rows(x, g):
    return x * lax.rsqrt(jnp.mean(x * x, axis=-1, keepdims=True) + NORM_EPS) * g


def _norm_matmul_kernel(x_ref, g_ref, w_ref, o_ref, h_ref):
    @pl.when(pl.program_id(1) == 0)
    def _():
        h_ref[...] = _norm_rows(x_ref[...], g_ref[...]).astype(BF16)

    o_ref[...] = _dot(h_ref[...], w_ref[...]).astype(o_ref.dtype)


def _norm_matmul3_kernel(x_ref, g_ref, wh_ref, wl_ref, o_ref, hh_ref, hl_ref):
    @pl.when(pl.program_id(1) == 0)
    def _():
        hh, hl = _split2(_norm_rows(x_ref[...], g_ref[...]))
        hh_ref[...] = hh
        hl_ref[...] = hl

    o_ref[...] = (_dot(hh_ref[...], wh_ref[...])
                  + (_dot(hh_ref[...], wl_ref[...]) + _dot(hl_ref[...], wh_ref[...]))).astype(o_ref.dtype)


def _norm_matmul(x2, g, w, out_dtype, tm, tn, precise=False):
    n, d = x2.shape
    cols = w.shape[1]
    grid = (n // tm, cols // tn)
    x_spec = pl.BlockSpec((tm, d), lambda i, j: (i, 0))
    g_spec = pl.BlockSpec((1, d), lambda i, j: (0, 0))
    w_spec = pl.BlockSpec((d, tn), lambda i, j: (0, j))
    o_spec = pl.BlockSpec((tm, tn), lambda i, j: (i, j))
    g2 = g.reshape(1, d)
    if precise:
        wh, wl = _split2(w)
        return pl.pallas_call(
            _norm_matmul3_kernel, grid=grid,
            in_specs=[x_spec, g_spec, w_spec, w_spec], out_specs=o_spec,
            out_shape=jax.ShapeDtypeStruct((n, cols), out_dtype),
            scratch_shapes=[pltpu.VMEM((tm, d), BF16), pltpu.VMEM((tm, d), BF16)],
            compiler_params=_params(("parallel", "arbitrary")), name="norm_proj_precise",
        )(x2, g2, wh, wl)
    return pl.pallas_call(
        _norm_matmul_kernel, grid=grid,
        in_specs=[x_spec, g_spec, w_spec], out_specs=o_spec,
        out_shape=jax.ShapeDtypeStruct((n, cols), out_dtype),
        scratch_shapes=[pltpu.VMEM((tm, d), BF16)],
        compiler_params=_params(("parallel", "arbitrary")), name="norm_proj",
    )(x2, g2, w.astype(BF16))


HGRN_CHUNK = 64
HGRN_GROUP = 8


def _dot1(a, b, dims=NN):
    return _dot(a.astype(BF16), b.astype(BF16), dims)


def _hgrn_kernel(q_ref, f_ref, i_ref, g_ref, lb_ref, nw_ref, o_ref, st_ref, o_sc, *, n_chunks):
    C, G = HGRN_CHUNK, HGRN_GROUP

    @pl.when(pl.program_id(1) == 0)
    def _():
        st_ref[...] = jnp.zeros_like(st_ref)

    lb = lb_ref[...]
    nw = nw_ref[...]
    tri = _lower_tri_ones(C)
    bd = _head_block_ones(WIDTH, HEAD_DIM)
    bd_mask = _iota((WIDTH, WIDTH), 0) // HEAD_DIM == _iota((WIDTH, WIDTH), 1) // HEAD_DIM
    row_top = _iota((G, 1), 0)

    def chunk(c, carry):
        sl = pl.ds(pl.multiple_of(c * C, C), C)
        fl = f_ref[0, sl, :]
        f = lb + (1.0 - lb) * _sigmoid(fl)
        k = (1.0 - lb) * _sigmoid(-fl)
        b = _dot_exact_lhs(tri, jnp.log(f))
        q = q_ref[0, sl, :] * (HEAD_DIM ** -0.5)
        v = i_ref[0, sl, :]
        st = st_ref[...]
        o_sc[...] = _dot1(q * jnp.exp(b), st, NT)

        for s0 in range(0, C, G):
            rows = C - s0
            q_top, b_top = q[s0:s0 + G], b[s0:s0 + G]
            q_rest, b_rest = q[s0 + G:], b[s0 + G:]
            pieces = []
            for u in range(G):
                bs = b[s0 + u:s0 + u + 1]
                ks = k[s0 + u:s0 + u + 1]
                top = q_top * ks * jnp.exp(jnp.minimum(b_top - bs, 0.0))
                pieces.append(jnp.where(row_top >= u, top, 0.0))
                if rows > G:
                    pieces.append(q_rest * ks * jnp.exp(b_rest - bs))
            x = jnp.concatenate(pieces, axis=0).astype(BF16)
            r = _dot(x, bd)
            og = jnp.zeros((rows, WIDTH), F32)
            for u in range(G):
                og = og + r[u * rows:(u + 1) * rows] * v[s0 + u:s0 + u + 1]
            o_sc[s0:, :] += og

        b_last = b[C - 1:C, :]
        upd = _dot1(v, k * jnp.exp(b_last - b), TN)
        st_ref[...] = st * jnp.exp(b_last) + jnp.where(bd_mask, upd, 0.0)

        o = o_sc[...]
        ms = _dot_exact_rhs(o * o, bd) * (1.0 / HEAD_DIM)
        on = o * lax.rsqrt(ms + NORM_EPS) * nw
        gl = g_ref[0, sl, :]
        o_ref[0, sl, :] = (on * (gl * _sigmoid(gl))).astype(o_ref.dtype)
        return carry

    lax.fori_loop(0, n_chunks, chunk, 0)


def _hgrn(main3, lb, norm_w, tb=256):
    bsz, t, _ = main3.shape
    col = lambda j: pl.BlockSpec((1, tb, WIDTH), lambda b, i, j=j: (b, i, j))
    vec = pl.BlockSpec((1, WIDTH), lambda b, i: (0, 0))
    return pl.pallas_call(
        functools.partial(_hgrn_kernel, n_chunks=tb // HGRN_CHUNK),
        grid=(bsz, t // tb),
        in_specs=[col(0), col(1), col(2), col(3), vec, vec],
        out_specs=pl.BlockSpec((1, tb, WIDTH), lambda b, i: (b, i, 0)),
        out_shape=jax.ShapeDtypeStruct((bsz, t, WIDTH), BF16),
        scratch_shapes=[pltpu.VMEM((WIDTH, WIDTH), F32), pltpu.VMEM((HGRN_CHUNK, WIDTH), F32)],
        compiler_params=_params(("parallel", "arbitrary")), name="hgrn2",
    )(main3, main3, main3, main3, lb.reshape(1, WIDTH), jnp.tile(norm_w, N_HEADS).reshape(1, WIDTH))


RWKV_CHUNK = 64


def _rwkv_kernel(*refs, n_chunks, has_vgate):
    if has_vgate:
        (c_ref, vf_ref, mu_ref, w0_ref, w2_ref, a0_ref, a2_ref, g2_ref, kkw_ref, kaw_ref, rk_ref,
         lnw_ref, lnb_ref, v0_ref, v1_ref, v2_ref, y_ref, vout_ref, s_ref, prev_ref, y_sc) = refs
    else:
        (c_ref, mu_ref, w0_ref, w2_ref, a0_ref, a2_ref, g2_ref, kkw_ref, kaw_ref, rk_ref,
         lnw_ref, lnb_ref, y_ref, vout_ref, s_ref, prev_ref, y_sc) = refs
    C = RWKV_CHUNK
    W = WIDTH
    tb = n_chunks * C

    @pl.when(pl.program_id(1) == 0)
    def _():
        s_ref[...] = jnp.zeros_like(s_ref)
        prev_ref[...] = jnp.zeros_like(prev_ref)

    p = c_ref[0]
    shifted = jnp.where(_iota((tb, 1), 0) == 0, prev_ref[...], pltpu.roll(p, 1, axis=0))
    prev_ref[...] = p[tb - 1:tb, :]
    xm = p + (shifted - p) * mu_ref[...]
    r = xm[:, 0:W]
    k = xm[:, W:2 * W]
    v = xm[:, 2 * W:3 * W]
    lora = xm[:, 3 * W:3 * W + RWKV_LORA_LANES]

    bd = _head_block_ones(W, HEAD_DIM)
    w_log = -_softplus(-(w0_ref[...] + _dot3(jnp.tanh(lora), w2_ref[...]))) - 0.5
    log_decay = -jnp.exp(w_log)
    a = _sigmoid(a0_ref[...] + _dot3(lora, a2_ref[...]))
    g = _dot3(_sigmoid(lora), g2_ref[...])
    if has_vgate:
        vg = _dot3(_dot3(v, v1_ref[...]), v2_ref[...])
        v = v + (vf_ref[0] - v) * _sigmoid(v0_ref[...] + vg)
    vout_ref[0] = v
    kk = k * kkw_ref[...]
    kk = kk * lax.rsqrt(jnp.maximum(_dot_exact_rhs(kk * kk, bd), 1e-24))
    k = k * (1.0 + (a - 1.0) * kaw_ref[...])

    tri = _lower_tri_ones(C)
    lane_head = _iota((1, W), 1) // HEAD_DIM
    hc = N_HEADS * C
    rt = _iota((hc, hc), 0) % C
    ct = _iota((hc, hc), 1) % C
    strict_lower = rt > ct
    lower = rt >= ct
    eye = (_iota((hc, hc), 0) == _iota((hc, hc), 1)).astype(F32)

    def stack(m):
        return jnp.concatenate([jnp.where(lane_head == h, m, 0.0) for h in range(N_HEADS)], axis=0)

    chunks = range(n_chunks)
    a_s, b_s, k_s, r_s, v_s, p_last, m, lak, lrb, lrk = ([None] * n_chunks for _ in range(10))
    for c in chunks:
        sl = slice(c * C, (c + 1) * C)
        ld = log_decay[sl]
        cum = _dot_exact_lhs(tri, ld)
        pdec = jnp.exp(cum)
        pinv = jnp.exp(-cum)
        p_last[c] = pdec[C - 1:C, :]
        a_s[c] = stack(-kk[sl] * jnp.exp(cum - ld)).astype(BF16)
        b_s[c] = stack(kk[sl] * a[sl] * pinv).astype(BF16)
        k_s[c] = stack(k[sl] * pinv).astype(BF16)
        r_s[c] = stack(r[sl] * pdec)
        v_s[c] = stack(v[sl]).astype(BF16)
        gram = _dot(jnp.concatenate([a_s[c], r_s[c].astype(BF16)], axis=0),
                    jnp.concatenate([b_s[c], k_s[c]], axis=0), NT)
        m[c] = jnp.where(strict_lower, gram[:hc, :hc], 0.0)
        lak[c] = jnp.where(strict_lower, gram[:hc, hc:], 0.0)
        lrb[c] = jnp.where(lower, gram[hc:, :hc], 0.0)
        lrk[c] = jnp.where(lower, gram[hc:, hc:], 0.0)

    x = [eye + m[c] for c in chunks]
    power = 2
    while power < C:
        m = [_dot1(m[c], m[c]) for c in chunks]
        x = [x[c] + _dot1(x[c], m[c]) for c in chunks]
        power *= 2
    lakv = [_dot1(lak[c], v_s[c]) for c in chunks]
    lrkv = [_dot1(lrk[c], v_s[c]) for c in chunks]
    w1 = [_dot1(x[c], lakv[c]) for c in chunks]
    a2 = [_dot1(x[c], a_s[c]) for c in chunks]
    r2 = [r_s[c] + _dot1(lrb[c], a2[c]) for c in chunks]
    y0 = [_dot1(lrb[c], w1[c]) + lrkv[c] for c in chunks]
    t1 = [_dot1(a2[c], b_s[c], TN) for c in chunks]
    t0 = [_dot1(w1[c], b_s[c], TN) + _dot(v_s[c], k_s[c], TN) for c in chunks]

    s = s_ref[...]
    for c in chunks:
        y = _dot1(r2[c], s, NT) + y0[c]
        y_sc[c * C:(c + 1) * C, :] = sum(y[h * C:(h + 1) * C] for h in range(N_HEADS))
        s = (s + _dot1(s, t1[c]) + t0[c]) * p_last[c]
    s_ref[...] = s

    y = y_sc[...]
    inv_n = 1.0 / HEAD_DIM
    mean = _dot_exact_rhs(y, bd) * inv_n
    yc = y - mean
    var = _dot_exact_rhs(yc * yc, bd) * inv_n
    yn = yc * lax.rsqrt(var + RWKV_GN_EPS) * lnw_ref[...] + lnb_ref[...]
    yn = yn + _dot_exact_rhs(r * k * rk_ref[...], bd) * v
    y_ref[0] = (yn * g).astype(y_ref.dtype)


def _rwkv(c3, col_block, v_first, prm, tb=256):
    bsz, t, _ = c3.shape
    has_vgate = v_first is not None
    blk = lambda w: pl.BlockSpec((1, tb, w), lambda b, i: (b, i, 0))
    cblk = pl.BlockSpec((1, tb, RWKV_COLS), lambda b, i: (b, i, col_block))
    full = lambda a: pl.BlockSpec(a.shape, lambda b, i: (0,) * a.ndim)
    names = ["mu", "w0", "w2", "a0", "a2", "g2", "k_k", "k_a", "r_k", "ln_w", "ln_b"]
    if has_vgate:
        names += ["v0", "v1", "v2"]
    args = [c3] + ([v_first] if has_vgate else []) + [prm[n] for n in names]
    in_specs = [cblk] + ([blk(WIDTH)] if has_vgate else []) + [full(prm[n]) for n in names]
    return pl.pallas_call(
        functools.partial(_rwkv_kernel, n_chunks=tb // RWKV_CHUNK, has_vgate=has_vgate),
        grid=(bsz, t // tb),
        in_specs=in_specs,
        out_specs=[blk(WIDTH), blk(WIDTH)],
        out_shape=[jax.ShapeDtypeStruct((bsz, t, WIDTH), BF16), jax.ShapeDtypeStruct((bsz, t, WIDTH), F32)],
        scratch_shapes=[pltpu.VMEM((WIDTH, WIDTH), F32), pltpu.VMEM((1, RWKV_COLS), F32),
                        pltpu.VMEM((tb, WIDTH), F32)],
        compiler_params=_params(("parallel", "arbitrary")), name="rwkv7",
    )(*args)


def _fox_cum_kernel(f_ref, bias_ref, c_ref, carry_ref, *, tb):
    @pl.when(pl.program_id(1) == 0)
    def _():
        carry_ref[...] = jnp.zeros_like(carry_ref)

    logf = -_softplus(-(f_ref[0] + bias_ref[...]))
    c = _dot_exact_lhs(_lower_tri_ones(tb), logf) + carry_ref[...]
    carry_ref[...] = c[tb - 1:tb, :]
    for h in range(N_HEADS):
        pick = (_iota((LANES, LANES), 0) == h).astype(BF16)
        c1, c2, c3 = _split3(c)
        c_ref[0, h] = LOG2E * (_dot(c1, pick) + (_dot(c2, pick) + _dot(c3, pick)))


def _fox_cum(main3, col_block, bias_row, tb=256):
    bsz, t, _ = main3.shape
    return pl.pallas_call(
        functools.partial(_fox_cum_kernel, tb=tb),
        grid=(bsz, t // tb),
        in_specs=[pl.BlockSpec((1, tb, LANES), lambda b, i: (b, i, col_block)),
                  pl.BlockSpec((1, LANES), lambda b, i: (0, 0))],
        out_specs=pl.BlockSpec((1, N_HEADS, tb, LANES), lambda b, i: (b, 0, i, 0)),
        out_shape=jax.ShapeDtypeStruct((bsz, N_HEADS, t, LANES), F32),
        scratch_shapes=[pltpu.VMEM((1, LANES), F32)],
        compiler_params=_params(("parallel", "arbitrary")), name="fox_cumgate",
    )(main3, bias_row)


ATT_QB = 128
ATT_KB = 512


def _stack_heads(q):
    lane_head = _iota((1, WIDTH), 1) // HEAD_DIM
    zero = jnp.zeros_like(q)
    return jnp.concatenate([jnp.where(lane_head == h, q, zero) for h in range(N_HEADS)], axis=0)


def _softmax_stage_t(s_t, m_sc, l_sc):
    m_old = m_sc[...]
    m_new = jnp.maximum(m_old, jnp.max(s_t, axis=0, keepdims=True))
    alpha = jnp.exp2(m_old - m_new)
    p_t = jnp.exp2(s_t - m_new)
    l_sc[...] = alpha * l_sc[...] + jnp.sum(p_t, axis=0, keepdims=True)
    m_sc[...] = m_new
    return p_t.astype(BF16), alpha


def _pv_stage_t(vt_blk, p_t, alpha, acc_sc):
    for h in range(N_HEADS):
        cs = slice(h * ATT_QB, (h + 1) * ATT_QB)
        pv = _dot(vt_blk[h * HEAD_DIM:(h + 1) * HEAD_DIM, :], p_t[:, cs])
        acc_sc[h] = alpha[:, cs] * acc_sc[h] + pv


def _attention_t(n_blocks, qk, post, post_last, vt_ref, bufs, m_sc, l_sc, acc_sc):
    KB = ATT_KB
    s_buf, p_buf, a_buf = bufs[0:2], bufs[2:4], bufs[4:6]
    _softmax_init_t(m_sc, l_sc, acc_sc)

    def vt_blk(j):
        return vt_ref[0, :, pl.ds(pl.multiple_of(j * KB, KB), KB)]

    def pending(j, par):
        _pv_stage_t(vt_blk(jnp.maximum(j - 1, 0)), p_buf[1 - par][...], a_buf[1 - par][...], acc_sc)

    def step(j, par):
        raw = s_buf[par][...]
        s_buf[1 - par][...] = qk(j + 1)
        p_t, alpha = _softmax_stage_t(post(j, raw), m_sc, l_sc)
        pending(j, par)
        p_buf[par][...] = p_t
        a_buf[par][...] = alpha

    def final(j, par):
        p_t, alpha = _softmax_stage_t(post_last(j, s_buf[par][...]), m_sc, l_sc)
        pending(j, par)
        _pv_stage_t(vt_blk(j), p_t, alpha, acc_sc)

    s_buf[0][...] = qk(0)
    p_buf[1][...] = jnp.zeros_like(p_buf[1])
    a_buf[1][...] = jnp.ones_like(a_buf[1])
    pairs = (n_blocks - 1) // 2

    def pair(t, carry):
        step(2 * t, 0)
        step(2 * t + 1, 1)
        return carry

    lax.fori_loop(0, pairs, pair, 0)
    odd_left = (n_blocks - 1) % 2 == 1

    @pl.when(odd_left)
    def _():
        step(2 * pairs, 0)
        final(2 * pairs + 1, 1)

    @pl.when(jnp.logical_not(odd_left))
    def _():
        final(2 * pairs, 0)


def _attention_bufs():
    cols = N_HEADS * ATT_QB
    return ([pltpu.VMEM((ATT_KB, cols), F32)] * 2 + [pltpu.VMEM((ATT_KB, cols), BF16)] * 2
            + [pltpu.VMEM((1, cols), F32)] * 2)


def _softmax_init_t(m_sc, l_sc, acc_sc):
    m_sc[...] = jnp.full_like(m_sc, MASK_VALUE)
    l_sc[...] = jnp.zeros_like(l_sc)
    acc_sc[...] = jnp.zeros_like(acc_sc)


def _softmax_finish_t(l_sc, acc_sc):
    l = l_sc[...]
    out_t = jnp.concatenate([acc_sc[h] / l[:, h * ATT_QB:(h + 1) * ATT_QB] for h in range(N_HEADS)], axis=0)
    return out_t.T


def _fox_kernel(q_ref, k_ref, vt_ref, ck_ref, o_ref, m_sc, l_sc, acc_sc, *bufs):
    QB, KB = ATT_QB, ATT_KB
    i = pl.program_id(1)
    qstack = _stack_heads(q_ref[0])
    qpos = i * QB + _iota((1, N_HEADS * QB), 1) % QB

    def qk(j):
        return _dot(k_ref[0, pl.ds(pl.multiple_of(j * KB, KB), KB), :], qstack, NT)

    def post(j, s_t):
        ks = pl.ds(pl.multiple_of(j * KB, KB), KB)
        return jnp.concatenate([s_t[:, h * QB:(h + 1) * QB] - ck_ref[0, h, ks, :] for h in range(N_HEADS)], axis=1)

    def post_last(j, s_t):
        return jnp.where(j * KB + _iota((KB, 1), 0) <= qpos, post(j, s_t), MASK_VALUE)

    _attention_t((i * QB) // KB + 1, qk, post, post_last, vt_ref, bufs, m_sc, l_sc, acc_sc)
    o_ref[0] = _softmax_finish_t(l_sc, acc_sc).astype(o_ref.dtype)


def _fox(qkv3, v_t, c_rep):
    bsz, t, _ = qkv3.shape
    cols = N_HEADS * ATT_QB
    return pl.pallas_call(
        _fox_kernel,
        grid=(bsz, t // ATT_QB),
        in_specs=[pl.BlockSpec((1, ATT_QB, WIDTH), lambda b, i: (b, i, 0)),
                  pl.BlockSpec((1, t, WIDTH), lambda b, i: (b, 0, 1)),
                  pl.BlockSpec((1, WIDTH, t), lambda b, i: (b, 0, 0)),
                  pl.BlockSpec((1, N_HEADS, t, LANES), lambda b, i: (b, 0, 0, 0))],
        out_specs=pl.BlockSpec((1, ATT_QB, WIDTH), lambda b, i: (b, i, 0)),
        out_shape=jax.ShapeDtypeStruct((bsz, t, WIDTH), BF16),
        scratch_shapes=[pltpu.VMEM((1, cols), F32), pltpu.VMEM((1, cols), F32),
                        pltpu.VMEM((N_HEADS, HEAD_DIM, ATT_QB), F32)] + _attention_bufs(),
        compiler_params=_params(("parallel", "arbitrary")), name="fox_attention",
    )(qkv3, qkv3, v_t, c_rep)


def _swap_halves(x, half):
    n = x.shape[-1]
    lower = (_iota((1, n), 1) % (2 * half)) < half
    return jnp.where(lower, pltpu.roll(x, n - half, axis=1), pltpu.roll(x, half, axis=1))


def _dsa_prep_kernel(q_ref, k_ref, v_ref, iq_ref, ik_ref, iw_ref, ch_ref, sh_ref, ci_ref, si_ref,
                     selh_ref, sell_ref, qo_ref, ko_ref, vo_ref, q3_ref, k3_ref, wi_ref):
    ch, sh, ci, si = ch_ref[...], sh_ref[...], ci_ref[...], si_ref[...]

    def rope(x, c, s, half):
        return x * c + _swap_halves(x, half) * s

    qo_ref[0] = (rope(q_ref[0], ch, sh, HEAD_DIM // 2) * (LOG2E * HEAD_DIM ** -0.5)).astype(BF16)
    ko_ref[0] = rope(k_ref[0], ch, sh, HEAD_DIM // 2).astype(BF16)
    vo_ref[0] = v_ref[0].astype(BF16)
    qh, ql = _split2(rope(iq_ref[0], ci, si, IDX_DIM // 2) * (IDX_DIM ** -0.5))
    q3_ref[0] = (_dot(qh, selh_ref[...]) + _dot(ql, sell_ref[...])).astype(BF16)
    kh, kl = _split2(rope(ik_ref[0], ci[:, :LANES], si[:, :LANES], IDX_DIM // 2))
    seg = _iota((1, LANES), 1) // IDX_DIM
    k3_ref[0] = jnp.where(seg < 2, kh, jnp.where(seg == 2, kl, jnp.zeros_like(kl)))
    wi_ref[0] = iw_ref[0] * (IDX_HEADS ** -0.5)


def _dsa_prep(main3, idx3, tables, qkv_block0, tb=512):
    bsz, t, _ = main3.shape
    iq_w = IDX_HEADS * IDX_DIM
    r = jnp.arange(iq_w)[:, None]
    c = jnp.arange(IDX_HEADS * LANES)[None, :]
    same = (r // IDX_DIM == c // LANES) & (r % IDX_DIM == c % IDX_DIM)
    seg = (c % LANES) // IDX_DIM
    sel_hi = (same & ((seg == 0) | (seg == 2))).astype(BF16)
    sel_lo = (same & (seg == 1)).astype(BF16)
    mcol = lambda j: pl.BlockSpec((1, tb, WIDTH), lambda b, i, j=j: (b, i, qkv_block0 + j))
    tab = pl.BlockSpec((tb, WIDTH), lambda b, i: (i, 0))
    sel = pl.BlockSpec(sel_hi.shape, lambda b, i: (0, 0))
    out = lambda w: pl.BlockSpec((1, tb, w), lambda b, i: (b, i, 0))
    shape = lambda dt, w: jax.ShapeDtypeStruct((bsz, t, w), dt)
    return pl.pallas_call(
        _dsa_prep_kernel, grid=(bsz, t // tb),
        in_specs=[mcol(0), mcol(1), mcol(2), out(iq_w),
                  pl.BlockSpec((1, tb, LANES), lambda b, i: (b, i, iq_w // LANES)),
                  pl.BlockSpec((1, tb, LANES), lambda b, i: (b, i, iq_w // LANES + 1)),
                  tab, tab, tab, tab, sel, sel],
        out_specs=[out(WIDTH)] * 3 + [out(IDX_HEADS * LANES), out(LANES), out(LANES)],
        out_shape=[shape(BF16, WIDTH)] * 3 + [shape(BF16, IDX_HEADS * LANES), shape(BF16, LANES), shape(F32, LANES)],
        compiler_params=_params(("parallel", "parallel")), name="dsa_prep",
    )(main3, main3, main3, idx3, idx3, idx3, *tables, sel_hi, sel_lo)


def _dsa_kernel(q_ref, k_ref, vt_ref, q3_ref, k3_ref, wt_ref, o_ref, key_sc, tie_sc, m_sc, l_sc, acc_sc,
                *bufs, topk):
    QB, KB = ATT_QB, ATT_KB
    i = pl.program_id(1)
    nkb = (i * QB) // KB + 1
    qpos = i * QB + _iota((1, QB), 1)
    int_min = jnp.int32(-2 ** 31)

    q3 = q3_ref[0]
    qs = jnp.concatenate([q3[:, h * LANES:(h + 1) * LANES] for h in range(IDX_HEADS)], axis=0)
    wt = wt_ref[0]

    def score_block(j, carry):
        ks = pl.ds(pl.multiple_of(j * KB, KB), KB)
        r = _dot(k3_ref[0, ks, :], qs, NT)
        sc = jnp.zeros((KB, QB), F32)
        for h in range(IDX_HEADS):
            sc = sc + jnp.maximum(r[:, h * QB:(h + 1) * QB], 0.0) * wt[h:h + 1, :]
        sc = jnp.where(j * KB + _iota((KB, 1), 0) <= qpos, sc, MASK_VALUE)
        sc = jnp.where(sc == 0.0, 0.0, sc)
        bits = pltpu.bitcast(sc, jnp.int32)
        key_sc[ks, :] = jnp.where(bits < 0, bits ^ jnp.int32(0x7FFFFFFF), bits)
        return carry

    lax.fori_loop(0, nkb, score_block, 0)

    def count_ge(cand):
        def blk(j, acc):
            base = pl.multiple_of(j * KB, KB)
            for r0 in range(0, KB, 64):
                acc = acc + jnp.where(key_sc[pl.ds(base + r0, 64), :] >= cand, 1.0, 0.0)
            return acc
        acc = lax.fori_loop(0, nkb, blk, jnp.zeros((64, QB), F32))
        return jnp.sum(acc, axis=0, keepdims=True)

    kf = float(topk)

    def bit_step(step, v):
        trial = jnp.where(step == 0, jnp.zeros_like(v), v | (jnp.int32(1) << (31 - step)))
        return jnp.where(count_ge(trial) >= kf, trial, v)

    thr = lax.fori_loop(0, 32, bit_step, jnp.full((1, QB), int_min, jnp.int32))
    need = kf - count_ge(thr + 1)

    lt = _lower_tri_ones(KB)
    qstack = _stack_heads(q_ref[0])

    def qk(j):
        return _dot(k_ref[0, pl.ds(pl.multiple_of(j * KB, KB), KB), :], qstack, NT)

    def post(j, s_t):
        ks = pl.ds(pl.multiple_of(j * KB, KB), KB)
        key = key_sc[ks, :]
        tie = key == thr
        tie_b = jnp.where(tie, 1.0, 0.0).astype(BF16)
        rank = tie_sc[...] + _dot(lt, tie_b)
        tie_sc[...] = rank[KB - 1:KB, :]
        sel = (key > thr) | (tie & (rank <= need))
        sel = sel & (j * KB + _iota((KB, 1), 0) <= qpos)
        bias = jnp.where(sel, 0.0, MASK_VALUE)
        return s_t + jnp.concatenate([bias] * N_HEADS, axis=1)

    tie_sc[...] = jnp.zeros_like(tie_sc)
    _attention_t(nkb, qk, post, post, vt_ref, bufs, m_sc, l_sc, acc_sc)
    o_ref[0] = _softmax_finish_t(l_sc, acc_sc).astype(o_ref.dtype)


def _dsa(q, k, v_t, q3, k3, w_t, topk):
    bsz, t, _ = q.shape
    qblk = lambda w: pl.BlockSpec((1, ATT_QB, w), lambda b, i: (b, i, 0))
    seq = lambda w: pl.BlockSpec((1, t, w), lambda b, i: (b, 0, 0))
    cols = N_HEADS * ATT_QB
    return pl.pallas_call(
        functools.partial(_dsa_kernel, topk=topk),
        grid=(bsz, t // ATT_QB),
        in_specs=[qblk(WIDTH), seq(WIDTH), pl.BlockSpec((1, WIDTH, t), lambda b, i: (b, 0, 0)),
                  qblk(IDX_HEADS * LANES), seq(LANES),
                  pl.BlockSpec((1, IDX_HEADS, ATT_QB), lambda b, i: (b, 0, i))],
        out_specs=qblk(WIDTH),
        out_shape=jax.ShapeDtypeStruct((bsz, t, WIDTH), BF16),
        scratch_shapes=[pltpu.VMEM((t, ATT_QB), jnp.int32), pltpu.VMEM((1, ATT_QB), F32), pltpu.VMEM((1, cols), F32),
                        pltpu.VMEM((1, cols), F32), pltpu.VMEM((N_HEADS, HEAD_DIM, ATT_QB), F32)] + _attention_bufs(),
        compiler_params=_params(("parallel", "arbitrary")), name="dsa_attention",
    )(q, k, v_t, q3, k3, w_t)


def _merge_kernel(x_ref, ya_ref, yb_ref, yc_ref, yd_ref, gpre_ref, wg_ref, wb_ref, wo_ref, g_ref, o_ref):
    d = x_ref.shape[-1]
    x = x_ref[...]
    h = _norm_rows(x, gpre_ref[...]).astype(BF16)
    merged = None
    for n, y_ref in enumerate((ya_ref, yb_ref, yc_ref, yd_ref)):
        gate = _sigmoid(_dot(h, wg_ref[:, n * d:(n + 1) * d]))
        term = gate * _dot(y_ref[...], wb_ref[n])
        merged = term if merged is None else merged + term
    mix = _dot(merged.astype(BF16), wo_ref[...])
    o_ref[...] = x + _norm_rows(mix, g_ref[...])


def _merge(x2, ys, g_pre, w_gate, w_branch, w_out, g_post, tm=256):
    n, d = x2.shape
    row = lambda w: pl.BlockSpec((tm, w), lambda i: (i, 0))
    vec = pl.BlockSpec((1, d), lambda i: (0, 0))
    return pl.pallas_call(
        _merge_kernel, grid=(n // tm,),
        in_specs=[row(d)] + [row(WIDTH)] * 4 + [vec, pl.BlockSpec(w_gate.shape, lambda i: (0, 0)),
                                                 pl.BlockSpec(w_branch.shape, lambda i: (0, 0, 0)),
                                                 pl.BlockSpec(w_out.shape, lambda i: (0, 0)), vec],
        out_specs=row(d),
        out_shape=jax.ShapeDtypeStruct((n, d), F32),
        compiler_params=_params(("parallel",)), name="gated_merge",
    )(x2, *ys, g_pre.reshape(1, d), w_gate.astype(BF16), w_branch.astype(BF16), w_out.astype(BF16),
      g_post.reshape(1, d))


def _mlp_kernel(x_ref, gpre_ref, wu_ref, wd_ref, gpost_ref, o_ref, h_ref, acc_ref):
    kf = pl.program_id(1)

    @pl.when(kf == 0)
    def _():
        h_ref[...] = _norm_rows(x_ref[...], gpre_ref[...]).astype(BF16)
        acc_ref[...] = jnp.zeros_like(acc_ref)

    u = jnp.maximum(_dot(h_ref[...], wu_ref[...]), 0.0)
    acc_ref[...] += _dot((u * u).astype(BF16), wd_ref[...])

    @pl.when(kf == pl.num_programs(1) - 1)
    def _():
        o_ref[...] = x_ref[...] + _norm_rows(acc_ref[...], gpost_ref[...])


def _mlp(x2, g_pre, w_up, w_down, g_post, tm=1024, tf=1024):
    n, d = x2.shape
    dff = w_up.shape[1]
    return pl.pallas_call(
        _mlp_kernel, grid=(n // tm, dff // tf),
        in_specs=[pl.BlockSpec((tm, d), lambda i, k: (i, 0)), pl.BlockSpec((1, d), lambda i, k: (0, 0)),
                  pl.BlockSpec((d, tf), lambda i, k: (0, k)), pl.BlockSpec((tf, d), lambda i, k: (k, 0)),
                  pl.BlockSpec((1, d), lambda i, k: (0, 0))],
        out_specs=pl.BlockSpec((tm, d), lambda i, k: (i, 0)),
        out_shape=jax.ShapeDtypeStruct((n, d), F32),
        scratch_shapes=[pltpu.VMEM((tm, d), BF16), pltpu.VMEM((tm, d), F32)],
        compiler_params=_params(("parallel", "arbitrary")), name="mlp",
    )(x2, g_pre.reshape(1, d), w_up.astype(BF16), w_down.astype(BF16), g_post.reshape(1, d))


def _rope_tables(t, dim, groups):
    inv = 1.0 / (ROPE_THETA ** (jnp.arange(0, dim, 2, dtype=F32) / dim))
    ang = jnp.arange(t, dtype=F32)[:, None] * inv[None, :]
    cos, sin = jnp.cos(ang), jnp.sin(ang)
    return jnp.tile(jnp.concatenate([cos, cos], axis=1), (1, groups)), jnp.tile(jnp.concatenate([-sin, sin], axis=1), (1, groups))


def _pad_rows(w, rows, offset):
    return jnp.zeros((rows, w.shape[1]), w.dtype).at[offset:offset + w.shape[0]].set(w)


def _pad_cols(w, cols):
    return jnp.pad(w, ((0, 0), (0, cols - w.shape[1])))


def kernel(x, norm_mix_pre, norm_mix_post, norm_mlp_pre, norm_mlp_post, w_in, w_branch, w_out, hgrn_lb_logits, hgrn_norm_w, fox_f_bias, rwkv_mu, rwkv_w0, rwkv_w2, rwkv_a0, rwkv_a2, rwkv_g2, rwkv_k_k, rwkv_k_a, rwkv_r_k, rwkv_ln_w, rwkv_ln_b, rwkv_v0, rwkv_v1, rwkv_v2, w_up, w_down):
    bsz, t, d = x.shape
    n = bsz * t
    depth = w_in.shape[0]
    W = WIDTH
    topk = min(TOPK_MAX, t // 4)

    lb_soft = jax.nn.softmax(hgrn_lb_logits.astype(F32), axis=0)
    lower_bounds = jnp.cumsum(lb_soft, axis=0) - lb_soft[0:1]
    tables = _rope_tables(t, HEAD_DIM, N_HEADS) + _rope_tables(t, IDX_DIM, IDX_HEADS)

    o_a, o_b, o_iq = 0, 4 * W, 7 * W
    o_ik, o_iw = o_iq + IDX_HEADS * IDX_DIM, o_iq + IDX_HEADS * IDX_DIM + IDX_DIM
    o_c = o_iw + IDX_HEADS
    c_cols = 3 * W + 128
    o_d = o_c + c_cols
    o_df = o_d + 3 * W
    o_g = o_df + N_HEADS
    main_blocks = dict(b_qkv=4, c=2, df=(7 * W + c_cols) // LANES)
    assert main_blocks["c"] * c_cols == 7 * W

    x2 = x.reshape(n, d)
    v_first = None
    for l in range(depth):
        wl = w_in[l]
        w_main = jnp.concatenate([
            wl[:, o_a:o_a + 7 * W], wl[:, o_c:o_c + c_cols], _pad_cols(wl[:, o_df:o_df + N_HEADS], LANES),
            jnp.zeros((d, 256), F32)], axis=1)
        w_idx = jnp.concatenate([
            wl[:, o_iq:o_iq + W], jnp.tile(wl[:, o_ik:o_ik + IDX_DIM], (1, LANES // IDX_DIM)),
            _pad_cols(wl[:, o_iw:o_iw + IDX_HEADS], LANES)], axis=1)
        w_fox = jnp.concatenate([wl[:, o_d:o_d + W] * (LOG2E * HEAD_DIM ** -0.5), wl[:, o_d + W:o_d + 3 * W]], axis=1)

        main2 = _norm_matmul(x2, norm_mix_pre[l], w_main, F32, tm=1024, tn=1024)
        idx2 = _norm_matmul(x2, norm_mix_pre[l], w_idx, F32, tm=1024, tn=w_idx.shape[1], precise=True)
        fox2 = _norm_matmul(x2, norm_mix_pre[l], w_fox, BF16, tm=1024, tn=3 * W)
        main3 = main2.reshape(bsz, t, -1)

        y_a = _hgrn(main3, lower_bounds[l], hgrn_norm_w[l])

        q_b, k_b, v_b, q3, k3, wi = _dsa_prep(main3, idx2.reshape(bsz, t, -1), tables, main_blocks["b_qkv"])
        y_b = _dsa(q_b, k_b, jnp.transpose(v_b, (0, 2, 1)), q3, k3,
                   jnp.transpose(wi[:, :, :IDX_HEADS], (0, 2, 1)), topk)

        prm = dict(
            mu=_pad_cols(rwkv_mu[l][None, :], RWKV_COLS), w0=rwkv_w0[l][None, :],
            w2=_pad_rows(rwkv_w2[l], RWKV_LORA_LANES, 0), a0=rwkv_a0[l][None, :],
            a2=_pad_rows(rwkv_a2[l], RWKV_LORA_LANES, 32), g2=_pad_rows(rwkv_g2[l], RWKV_LORA_LANES, 64),
            k_k=rwkv_k_k[l][None, :], k_a=rwkv_k_a[l][None, :], r_k=rwkv_r_k[l].reshape(1, W),
            ln_w=rwkv_ln_w[l][None, :], ln_b=rwkv_ln_b[l][None, :])
        if l > 0:
            prm.update(v0=rwkv_v0[l - 1][None, :], v1=_pad_cols(rwkv_v1[l - 1], LANES),
                       v2=_pad_rows(rwkv_v2[l - 1], LANES, 0))
        y_c, v_c = _rwkv(main3, main_blocks["c"], v_first, prm)
        if l == 0:
            v_first = v_c

        c_rep = _fox_cum(main3, main_blocks["df"], _pad_cols(fox_f_bias[l][None, :], LANES))
        fox3 = fox2.reshape(bsz, t, 3 * W)
        y_d = _fox(fox3, jnp.transpose(fox3[:, :, 2 * W:], (0, 2, 1)), c_rep)

        ys = [y.reshape(n, W) for y in (y_a, y_b, y_c, y_d)]
        x2 = _merge(x2, ys, norm_mix_pre[l], wl[:, o_g:o_g + 4 * d], w_branch[l], w_out[l], norm_mix_post[l])
        x2 = _mlp(x2, norm_mlp_pre[l], w_up[l], w_down[l], norm_mlp_post[l])
    return x2.reshape(bsz, t, d)
```

```python
import functools

import jax
import jax.numpy as jnp
from jax import lax
from jax.experimental import pallas as pl
from jax.experimental.pallas import tpu as pltpu

F32 = jnp.float32
BF16 = jnp.bfloat16

HEAD_DIM = 64
N_HEADS = 4
WIDTH = N_HEADS * HEAD_DIM
IDX_HEADS = 8
IDX_DIM = 32
TOPK_MAX = 256
ROPE_THETA = 10000.0
NORM_EPS = 1e-6
RWKV_GN_EPS = 64e-5
MASK_VALUE = -1e30
LOG2E = 1.4426950408889634
RWKV_LORA_LANES = 128
RWKV_COLS = 3 * WIDTH + RWKV_LORA_LANES

LANES = 128
VMEM_LIMIT = 48 * 1024 * 1024

NN = ((1,), (0,))
NT = ((1,), (1,))
TN = ((0,), (0,))


def _dot(a, b, dims=NN):
    return lax.dot_general(a, b, (dims, ((), ())), preferred_element_type=F32)


def _split2(x):
    hi = x.astype(BF16)
    lo = (x - hi.astype(F32)).astype(BF16)
    return hi, lo


def _split3(x):
    hi = x.astype(BF16)
    r1 = x - hi.astype(F32)
    mid = r1.astype(BF16)
    lo = (r1 - mid.astype(F32)).astype(BF16)
    return hi, mid, lo


def _dot3(a, b, dims=NN):
    ah, al = _split2(a)
    bh, bl = _split2(b)
    return _dot(ah, bh, dims) + (_dot(ah, bl, dims) + _dot(al, bh, dims))


def _dot_exact_lhs(a_bf16, b, dims=NN):
    b1, b2, b3 = _split3(b)
    return _dot(a_bf16, b1, dims) + (_dot(a_bf16, b2, dims) + _dot(a_bf16, b3, dims))


def _dot_exact_rhs(a, b_bf16, dims=NN):
    a1, a2 = _split2(a)
    return _dot(a1, b_bf16, dims) + _dot(a2, b_bf16, dims)


def _iota(shape, dim):
    return lax.broadcasted_iota(jnp.int32, shape, dim)


def _head_block_ones(n, group):
    return (_iota((n, n), 0) // group == _iota((n, n), 1) // group).astype(BF16)


def _lower_tri_ones(n):
    return (_iota((n, n), 0) >= _iota((n, n), 1)).astype(BF16)


def _sigmoid(x):
    return 1.0 / (1.0 + jnp.exp(-x))


def _softplus(x):
    return jnp.maximum(x, 0.0) + jnp.log(1.0 + jnp.exp(-jnp.abs(x)))


def _params(sem):
    return pltpu.CompilerParams(dimension_semantics=sem, vmem_limit_bytes=VMEM_LIMIT)


def _norm_rows(x, g):
    return x * lax.rsqrt(jnp.mean(x * x, axis=-1, keepdims=True) + NORM_EPS) * g


def _norm_matmul_kernel(x_ref, g_ref, w_ref, o_ref, h_ref):
    @pl.when(pl.program_id(1) == 0)
    def _():
        h_ref[...] = _norm_rows(x_ref[...], g_ref[...]).astype(BF16)

    o_ref[...] = _dot(h_ref[...], w_ref[...]).astype(o_ref.dtype)


def _norm_matmul3_kernel(x_ref, g_ref, wh_ref, wl_ref, o_ref, hh_ref, hl_ref):
    @pl.when(pl.program_id(1) == 0)
    def _():
        hh, hl = _split2(_norm_rows(x_ref[...], g_ref[...]))
        hh_ref[...] = hh
        hl_ref[...] = hl

    o_ref[...] = (_dot(hh_ref[...], wh_ref[...])
                  + (_dot(hh_ref[...], wl_ref[...]) + _dot(hl_ref[...], wh_ref[...]))).astype(o_ref.dtype)


def _norm_matmul(x2, g, w, out_dtype, tm, tn, precise=False):
    n, d = x2.shape
    cols = w.shape[1]
    grid = (n // tm, cols // tn)
    x_spec = pl.BlockSpec((tm, d), lambda i, j: (i, 0))
    g_spec = pl.BlockSpec((1, d), lambda i, j: (0, 0))
    w_spec = pl.BlockSpec((d, tn), lambda i, j: (0, j))
    o_spec = pl.BlockSpec((tm, tn), lambda i, j: (i, j))
    g2 = g.reshape(1, d)
    if precise:
        wh, wl = _split2(w)
        return pl.pallas_call(
            _norm_matmul3_kernel, grid=grid,
            in_specs=[x_spec, g_spec, w_spec, w_spec], out_specs=o_spec,
            out_shape=jax.ShapeDtypeStruct((n, cols), out_dtype),
            scratch_shapes=[pltpu.VMEM((tm, d), BF16), pltpu.VMEM((tm, d), BF16)],
            compiler_params=_params(("parallel", "arbitrary")), name="norm_proj_precise",
        )(x2, g2, wh, wl)
    return pl.pallas_call(
        _norm_matmul_kernel, grid=grid,
        in_specs=[x_spec, g_spec, w_spec], out_specs=o_spec,
        out_shape=jax.ShapeDtypeStruct((n, cols), out_dtype),
        scratch_shapes=[pltpu.VMEM((tm, d), BF16)],
        compiler_params=_params(("parallel", "arbitrary")), name="norm_proj",
    )(x2, g2, w.astype(BF16))


HGRN_CHUNK = 64
HGRN_GROUP = 8


def _dot1(a, b, dims=NN):
    return _dot(a.astype(BF16), b.astype(BF16), dims)


def _hgrn_kernel(q_ref, f_ref, i_ref, g_ref, lb_ref, nw_ref, o_ref, st_ref, *, n_chunks):
    C, G = HGRN_CHUNK, HGRN_GROUP

    @pl.when(pl.program_id(1) == 0)
    def _():
        st_ref[...] = jnp.zeros_like(st_ref)

    lb = lb_ref[...]
    nw = nw_ref[...]
    tri = _lower_tri_ones(C)
    bd = _head_block_ones(WIDTH, HEAD_DIM)
    bd_mask = _iota((WIDTH, WIDTH), 0) // HEAD_DIM == _iota((WIDTH, WIDTH), 1) // HEAD_DIM
    row_top = _iota((G, 1), 0)
    lane_head = _iota((1, WIDTH), 1) // HEAD_DIM

    def chunk(c, carry):
        sl = pl.ds(pl.multiple_of(c * C, C), C)
        fl = f_ref[0, sl, :]
        f = lb + (1.0 - lb) * _sigmoid(fl)
        k = (1.0 - lb) * _sigmoid(-fl)
        b = _dot_exact_lhs(tri, jnp.log(f))
        q = q_ref[0, sl, :] * (HEAD_DIM ** -0.5)
        v = i_ref[0, sl, :]
        st = st_ref[...]
        o_inter = _dot1(q * jnp.exp(b), st, NT)

        n_groups = C // G
        o_rows = [o_inter[g * G:(g + 1) * G] for g in range(n_groups)]
        pieces = []
        for g in range(n_groups):
            s0 = g * G
            q_top, b_top, k_top = q[s0:s0 + G], b[s0:s0 + G], k[s0:s0 + G]
            for u in range(G):
                top = q_top * k_top[u:u + 1] * jnp.exp(jnp.minimum(b_top - b_top[u:u + 1], 0.0))
                pieces.append(jnp.where(row_top >= u, top, 0.0))
        r = _dot(jnp.concatenate(pieces, axis=0).astype(BF16), bd)
        att, v_st = [], []
        for g in range(n_groups - 1):
            s0 = g * G
            b_top, k_top, v_top = b[s0:s0 + G], k[s0:s0 + G], v[s0:s0 + G]
            b_ref = b_top[G - 1:G]
            qp = (q[s0 + G:] * jnp.exp(b[s0 + G:] - b_ref)).astype(BF16)
            kp = k_top * jnp.exp(b_ref - b_top)
            k_st = jnp.concatenate([jnp.where(lane_head == h, kp, 0.0) for h in range(N_HEADS)], axis=0)
            v_st.append(jnp.concatenate([jnp.where(lane_head == h, v_top, 0.0) for h in range(N_HEADS)], axis=0))
            att.append(_dot(qp, k_st.astype(BF16), NT))
        for g in range(n_groups):
            v_top = v[g * G:(g + 1) * G]
            for u in range(G):
                o_rows[g] = o_rows[g] + r[(g * G + u) * G:(g * G + u + 1) * G] * v_top[u:u + 1]
        for g in range(n_groups - 1):
            rest = _dot(att[g].astype(BF16), v_st[g].astype(BF16))
            for g2 in range(g + 1, n_groups):
                o_rows[g2] = o_rows[g2] + rest[(g2 - g - 1) * G:(g2 - g) * G]
        o = jnp.concatenate(o_rows, axis=0)

        b_last = b[C - 1:C, :]
        upd = _dot1(v, k * jnp.exp(b_last - b), TN)
        st_ref[...] = st * jnp.exp(b_last) + jnp.where(bd_mask, upd, 0.0)

        ms = _dot_exact_rhs(o * o, bd) * (1.0 / HEAD_DIM)
        on = o * lax.rsqrt(ms + NORM_EPS) * nw
        gl = g_ref[0, sl, :]
        o_ref[0, sl, :] = (on * (gl * _sigmoid(gl))).astype(o_ref.dtype)
        return carry

    lax.fori_loop(0, n_chunks, chunk, 0)


def _hgrn(main3, lb, norm_w, tb=256):
    bsz, t, _ = main3.shape
    col = lambda j: pl.BlockSpec((1, tb, WIDTH), lambda b, i, j=j: (b, i, j))
    vec = pl.BlockSpec((1, WIDTH), lambda b, i: (0, 0))
    return pl.pallas_call(
        functools.partial(_hgrn_kernel, n_chunks=tb // HGRN_CHUNK),
        grid=(bsz, t // tb),
        in_specs=[col(0), col(1), col(2), col(3), vec, vec],
        out_specs=pl.BlockSpec((1, tb, WIDTH), lambda b, i: (b, i, 0)),
        out_shape=jax.ShapeDtypeStruct((bsz, t, WIDTH), BF16),
        scratch_shapes=[pltpu.VMEM((WIDTH, WIDTH), F32)],
        compiler_params=_params(("parallel", "arbitrary")), name="hgrn2",
    )(main3, main3, main3, main3, lb.reshape(1, WIDTH), jnp.tile(norm_w, N_HEADS).reshape(1, WIDTH))


RWKV_CHUNK = 64


def _rwkv_kernel(*refs, n_chunks, has_vgate):
    if has_vgate:
        (c_ref, vf_ref, mu_ref, w0_ref, w2_ref, a0_ref, a2_ref, g2_ref, kkw_ref, kaw_ref, rk_ref,
         lnw_ref, lnb_ref, v0_ref, v1_ref, v2_ref, y_ref, vout_ref, s_ref, prev_ref, y_sc) = refs
    else:
        (c_ref, mu_ref, w0_ref, w2_ref, a0_ref, a2_ref, g2_ref, kkw_ref, kaw_ref, rk_ref,
         lnw_ref, lnb_ref, y_ref, vout_ref, s_ref, prev_ref, y_sc) = refs
    C = RWKV_CHUNK
    W = WIDTH
    tb = n_chunks * C

    @pl.when(pl.program_id(1) == 0)
    def _():
        s_ref[...] = jnp.zeros_like(s_ref)
        prev_ref[...] = jnp.zeros_like(prev_ref)

    p = c_ref[0]
    shifted = jnp.where(_iota((tb, 1), 0) == 0, prev_ref[...], pltpu.roll(p, 1, axis=0))
    prev_ref[...] = p[tb - 1:tb, :]
    xm = p + (shifted - p) * mu_ref[...]
    r = xm[:, 0:W]
    k = xm[:, W:2 * W]
    v = xm[:, 2 * W:3 * W]
    lora = xm[:, 3 * W:3 * W + RWKV_LORA_LANES]

    bd = _head_block_ones(W, HEAD_DIM)
    w_log = -_softplus(-(w0_ref[...] + _dot3(jnp.tanh(lora), w2_ref[...]))) - 0.5
    log_decay = -jnp.exp(w_log)
    a = _sigmoid(a0_ref[...] + _dot3(lora, a2_ref[...]))
    g = _dot3(_sigmoid(lora), g2_ref[...])
    if has_vgate:
        vg = _dot3(_dot3(v, v1_ref[...]), v2_ref[...])
        v = v + (vf_ref[0] - v) * _sigmoid(v0_ref[...] + vg)
    vout_ref[0] = v
    kk = k * kkw_ref[...]
    kk = kk * lax.rsqrt(jnp.maximum(_dot_exact_rhs(kk * kk, bd), 1e-24))
    k = k * (1.0 + (a - 1.0) * kaw_ref[...])

    tri = _lower_tri_ones(C)
    lane_head = _iota((1, W), 1) // HEAD_DIM
    hc = N_HEADS * C
    rt = _iota((hc, hc), 0) % C
    ct = _iota((hc, hc), 1) % C
    strict_lower = rt > ct
    lower = rt >= ct
    eye = (_iota((hc, hc), 0) == _iota((hc, hc), 1)).astype(F32)

    def stack(m):
        return jnp.concatenate([jnp.where(lane_head == h, m, 0.0) for h in range(N_HEADS)], axis=0)

    chunks = range(n_chunks)
    a_s, b_s, k_s, r_s, v_s, p_last, m, lak, lrb, lrk = ([None] * n_chunks for _ in range(10))
    for c in chunks:
        sl = slice(c * C, (c + 1) * C)
        ld = log_decay[sl]
        cum = _dot_exact_lhs(tri, ld)
        pdec = jnp.exp(cum)
        pinv = jnp.exp(-cum)
        p_last[c] = pdec[C - 1:C, :]
        a_s[c] = stack(-kk[sl] * jnp.exp(cum - ld)).astype(BF16)
        b_s[c] = stack(kk[sl] * a[sl] * pinv).astype(BF16)
        k_s[c] = stack(k[sl] * pinv).astype(BF16)
        r_s[c] = stack(r[sl] * pdec)
        v_s[c] = stack(v[sl]).astype(BF16)
        gram = _dot(jnp.concatenate([a_s[c], r_s[c].astype(BF16)], axis=0),
                    jnp.concatenate([b_s[c], k_s[c]], axis=0), NT)
        m[c] = jnp.where(strict_lower, gram[:hc, :hc], 0.0)
        lak[c] = jnp.where(strict_lower, gram[:hc, hc:], 0.0)
        lrb[c] = jnp.where(lower, gram[hc:, :hc], 0.0)
        lrk[c] = jnp.where(lower, gram[hc:, hc:], 0.0)

    x = [eye + m[c] for c in chunks]
    power = 2
    while power < C:
        m = [_dot1(m[c], m[c]) for c in chunks]
        x = [x[c] + _dot1(x[c], m[c]) for c in chunks]
        power *= 2
    lakv = [_dot1(lak[c], v_s[c]) for c in chunks]
    lrkv = [_dot1(lrk[c], v_s[c]) for c in chunks]
    w1 = [_dot1(x[c], lakv[c]) for c in chunks]
    a2 = [_dot1(x[c], a_s[c]) for c in chunks]
    r2 = [r_s[c] + _dot1(lrb[c], a2[c]) for c in chunks]
    y0 = [_dot1(lrb[c], w1[c]) + lrkv[c] for c in chunks]
    t1 = [_dot1(a2[c], b_s[c], TN) for c in chunks]
    t0 = [_dot1(w1[c], b_s[c], TN) + _dot(v_s[c], k_s[c], TN) for c in chunks]

    s = s_ref[...]
    for c in chunks:
        y = _dot1(r2[c], s, NT) + y0[c]
        y_sc[c * C:(c + 1) * C, :] = sum(y[h * C:(h + 1) * C] for h in range(N_HEADS))
        s = (s + _dot1(s, t1[c]) + t0[c]) * p_last[c]
    s_ref[...] = s

    y = y_sc[...]
    inv_n = 1.0 / HEAD_DIM
    mean = _dot_exact_rhs(y, bd) * inv_n
    yc = y - mean
    var = _dot_exact_rhs(yc * yc, bd) * inv_n
    yn = yc * lax.rsqrt(var + RWKV_GN_EPS) * lnw_ref[...] + lnb_ref[...]
    yn = yn + _dot_exact_rhs(r * k * rk_ref[...], bd) * v
    y_ref[0] = (yn * g).astype(y_ref.dtype)


def _rwkv(c3, col_block, v_first, prm, tb=256):
    bsz, t, _ = c3.shape
    has_vgate = v_first is not None
    blk = lambda w: pl.BlockSpec((1, tb, w), lambda b, i: (b, i, 0))
    cblk = pl.BlockSpec((1, tb, RWKV_COLS), lambda b, i: (b, i, col_block))
    full = lambda a: pl.BlockSpec(a.shape, lambda b, i: (0,) * a.ndim)
    names = ["mu", "w0", "w2", "a0", "a2", "g2", "k_k", "k_a", "r_k", "ln_w", "ln_b"]
    if has_vgate:
        names += ["v0", "v1", "v2"]
    args = [c3] + ([v_first] if has_vgate else []) + [prm[n] for n in names]
    in_specs = [cblk] + ([blk(WIDTH)] if has_vgate else []) + [full(prm[n]) for n in names]
    return pl.pallas_call(
        functools.partial(_rwkv_kernel, n_chunks=tb // RWKV_CHUNK, has_vgate=has_vgate),
        grid=(bsz, t // tb),
        in_specs=in_specs,
        out_specs=[blk(WIDTH), blk(WIDTH)],
        out_shape=[jax.ShapeDtypeStruct((bsz, t, WIDTH), BF16), jax.ShapeDtypeStruct((bsz, t, WIDTH), F32)],
        scratch_shapes=[pltpu.VMEM((WIDTH, WIDTH), F32), pltpu.VMEM((1, RWKV_COLS), F32),
                        pltpu.VMEM((tb, WIDTH), F32)],
        compiler_params=_params(("parallel", "arbitrary")), name="rwkv7",
    )(*args)


def _fox_cum_kernel(f_ref, bias_ref, c_ref, carry_ref, *, tb):
    @pl.when(pl.program_id(1) == 0)
    def _():
        carry_ref[...] = jnp.zeros_like(carry_ref)

    logf = -_softplus(-(f_ref[0] + bias_ref[...]))
    c = _dot_exact_lhs(_lower_tri_ones(tb), logf) + carry_ref[...]
    carry_ref[...] = c[tb - 1:tb, :]
    for h in range(N_HEADS):
        pick = (_iota((LANES, LANES), 0) == h).astype(BF16)
        c1, c2, c3 = _split3(c)
        c_ref[0, h] = LOG2E * (_dot(c1, pick) + (_dot(c2, pick) + _dot(c3, pick)))


def _fox_cum(main3, col_block, bias_row, tb=256):
    bsz, t, _ = main3.shape
    return pl.pallas_call(
        functools.partial(_fox_cum_kernel, tb=tb),
        grid=(bsz, t // tb),
        in_specs=[pl.BlockSpec((1, tb, LANES), lambda b, i: (b, i, col_block)),
                  pl.BlockSpec((1, LANES), lambda b, i: (0, 0))],
        out_specs=pl.BlockSpec((1, N_HEADS, tb, LANES), lambda b, i: (b, 0, i, 0)),
        out_shape=jax.ShapeDtypeStruct((bsz, N_HEADS, t, LANES), F32),
        scratch_shapes=[pltpu.VMEM((1, LANES), F32)],
        compiler_params=_params(("parallel", "arbitrary")), name="fox_cumgate",
    )(main3, bias_row)


ATT_QB = 128
ATT_KB = 512


def _stack_heads(q):
    lane_head = _iota((1, WIDTH), 1) // HEAD_DIM
    zero = jnp.zeros_like(q)
    return jnp.concatenate([jnp.where(lane_head == h, q, zero) for h in range(N_HEADS)], axis=0)


def _softmax_stage_t(s_t, m_sc, l_sc):
    m_old = m_sc[...]
    m_new = jnp.maximum(m_old, jnp.max(s_t, axis=0, keepdims=True))
    alpha = jnp.exp2(m_old - m_new)
    p_t = jnp.exp2(s_t - m_new)
    l_sc[...] = alpha * l_sc[...] + jnp.sum(p_t, axis=0, keepdims=True)
    m_sc[...] = m_new
    return p_t.astype(BF16), alpha


def _pv_stage_t(vt_blk, p_t, alpha, acc_sc):
    for h in range(N_HEADS):
        cs = slice(h * ATT_QB, (h + 1) * ATT_QB)
        pv = _dot(vt_blk[h * HEAD_DIM:(h + 1) * HEAD_DIM, :], p_t[:, cs])
        acc_sc[h] = alpha[:, cs] * acc_sc[h] + pv


def _attention_t(n_blocks, qk, post, post_last, vt_ref, bufs, m_sc, l_sc, acc_sc):
    KB = ATT_KB
    s_buf, p_buf, a_buf = bufs[0:2], bufs[2:4], bufs[4:6]
    _softmax_init_t(m_sc, l_sc, acc_sc)

    def vt_blk(j):
        return vt_ref[0, :, pl.ds(pl.multiple_of(j * KB, KB), KB)]

    def pending(j, par):
        _pv_stage_t(vt_blk(jnp.maximum(j - 1, 0)), p_buf[1 - par][...], a_buf[1 - par][...], acc_sc)

    def step(j, par):
        raw = s_buf[par][...]
        s_buf[1 - par][...] = qk(j + 1)
        p_t, alpha = _softmax_stage_t(post(j, raw), m_sc, l_sc)
        pending(j, par)
        p_buf[par][...] = p_t
        a_buf[par][...] = alpha

    def final(j, par):
        p_t, alpha = _softmax_stage_t(post_last(j, s_buf[par][...]), m_sc, l_sc)
        pending(j, par)
        _pv_stage_t(vt_blk(j), p_t, alpha, acc_sc)

    s_buf[0][...] = qk(0)
    p_buf[1][...] = jnp.zeros_like(p_buf[1])
    a_buf[1][...] = jnp.ones_like(a_buf[1])
    pairs = (n_blocks - 1) // 2

    def pair(t, carry):
        step(2 * t, 0)
        step(2 * t + 1, 1)
        return carry

    lax.fori_loop(0, pairs, pair, 0)
    odd_left = (n_blocks - 1) % 2 == 1

    @pl.when(odd_left)
    def _():
        step(2 * pairs, 0)
        final(2 * pairs + 1, 1)

    @pl.when(jnp.logical_not(odd_left))
    def _():
        final(2 * pairs, 0)


def _attention_bufs():
    cols = N_HEADS * ATT_QB
    return ([pltpu.VMEM((ATT_KB, cols), F32)] * 2 + [pltpu.VMEM((ATT_KB, cols), BF16)] * 2
            + [pltpu.VMEM((1, cols), F32)] * 2)


def _softmax_init_t(m_sc, l_sc, acc_sc):
    m_sc[...] = jnp.full_like(m_sc, MASK_VALUE)
    l_sc[...] = jnp.zeros_like(l_sc)
    acc_sc[...] = jnp.zeros_like(acc_sc)


def _softmax_finish_t(l_sc, acc_sc):
    l = l_sc[...]
    out_t = jnp.concatenate([acc_sc[h] / l[:, h * ATT_QB:(h + 1) * ATT_QB] for h in range(N_HEADS)], axis=0)
    return out_t.T


def _fox_kernel(q_ref, k_ref, vt_ref, ck_ref, o_ref, m_sc, l_sc, acc_sc, *bufs):
    QB, KB = ATT_QB, ATT_KB
    i = pl.program_id(1)
    qstack = _stack_heads(q_ref[0])
    qpos = i * QB + _iota((1, N_HEADS * QB), 1) % QB

    def qk(j):
        return _dot(k_ref[0, pl.ds(pl.multiple_of(j * KB, KB), KB), :], qstack, NT)

    def post(j, s_t):
        ks = pl.ds(pl.multiple_of(j * KB, KB), KB)
        return jnp.concatenate([s_t[:, h * QB:(h + 1) * QB] - ck_ref[0, h, ks, :] for h in range(N_HEADS)], axis=1)

    def post_last(j, s_t):
        return jnp.where(j * KB + _iota((KB, 1), 0) <= qpos, post(j, s_t), MASK_VALUE)

    _attention_t((i * QB) // KB + 1, qk, post, post_last, vt_ref, bufs, m_sc, l_sc, acc_sc)
    o_ref[0] = _softmax_finish_t(l_sc, acc_sc).astype(o_ref.dtype)


def _fox(qkv3, v_t, c_rep):
    bsz, t, _ = qkv3.shape
    cols = N_HEADS * ATT_QB
    return pl.pallas_call(
        _fox_kernel,
        grid=(bsz, t // ATT_QB),
        in_specs=[pl.BlockSpec((1, ATT_QB, WIDTH), lambda b, i: (b, i, 0)),
                  pl.BlockSpec((1, t, WIDTH), lambda b, i: (b, 0, 1)),
                  pl.BlockSpec((1, WIDTH, t), lambda b, i: (b, 0, 0)),
                  pl.BlockSpec((1, N_HEADS, t, LANES), lambda b, i: (b, 0, 0, 0))],
        out_specs=pl.BlockSpec((1, ATT_QB, WIDTH), lambda b, i: (b, i, 0)),
        out_shape=jax.ShapeDtypeStruct((bsz, t, WIDTH), BF16),
        scratch_shapes=[pltpu.VMEM((1, cols), F32), pltpu.VMEM((1, cols), F32),
                        pltpu.VMEM((N_HEADS, HEAD_DIM, ATT_QB), F32)] + _attention_bufs(),
        compiler_params=_params(("parallel", "arbitrary")), name="fox_attention",
    )(qkv3, qkv3, v_t, c_rep)


def _swap_halves(x, half):
    n = x.shape[-1]
    lower = (_iota((1, n), 1) % (2 * half)) < half
    return jnp.where(lower, pltpu.roll(x, n - half, axis=1), pltpu.roll(x, half, axis=1))


def _dsa_prep_kernel(q_ref, k_ref, v_ref, iq_ref, ik_ref, iw_ref, ch_ref, sh_ref, ci_ref, si_ref,
                     selh_ref, sell_ref, qo_ref, ko_ref, vo_ref, q3_ref, k3_ref, wi_ref):
    ch, sh, ci, si = ch_ref[...], sh_ref[...], ci_ref[...], si_ref[...]

    def rope(x, c, s, half):
        return x * c + _swap_halves(x, half) * s

    qo_ref[0] = (rope(q_ref[0], ch, sh, HEAD_DIM // 2) * (LOG2E * HEAD_DIM ** -0.5)).astype(BF16)
    ko_ref[0] = rope(k_ref[0], ch, sh, HEAD_DIM // 2).astype(BF16)
    vo_ref[0] = v_ref[0].astype(BF16)
    qh, ql = _split2(rope(iq_ref[0], ci, si, IDX_DIM // 2) * (IDX_DIM ** -0.5))
    q3_ref[0] = (_dot(qh, selh_ref[...]) + _dot(ql, sell_ref[...])).astype(BF16)
    kh, kl = _split2(rope(ik_ref[0], ci[:, :LANES], si[:, :LANES], IDX_DIM // 2))
    seg = _iota((1, LANES), 1) // IDX_DIM
    k3_ref[0] = jnp.where(seg < 2, kh, jnp.where(seg == 2, kl, jnp.zeros_like(kl)))
    wi_ref[0] = iw_ref[0] * (IDX_HEADS ** -0.5)


def _dsa_prep(main3, idx3, tables, qkv_block0, tb=512):
    bsz, t, _ = main3.shape
    iq_w = IDX_HEADS * IDX_DIM
    r = jnp.arange(iq_w)[:, None]
    c = jnp.arange(IDX_HEADS * LANES)[None, :]
    same = (r // IDX_DIM == c // LANES) & (r % IDX_DIM == c % IDX_DIM)
    seg = (c % LANES) // IDX_DIM
    sel_hi = (same & ((seg == 0) | (seg == 2))).astype(BF16)
    sel_lo = (same & (seg == 1)).astype(BF16)
    mcol = lambda j: pl.BlockSpec((1, tb, WIDTH), lambda b, i, j=j: (b, i, qkv_block0 + j))
    tab = pl.BlockSpec((tb, WIDTH), lambda b, i: (i, 0))
    sel = pl.BlockSpec(sel_hi.shape, lambda b, i: (0, 0))
    out = lambda w: pl.BlockSpec((1, tb, w), lambda b, i: (b, i, 0))
    shape = lambda dt, w: jax.ShapeDtypeStruct((bsz, t, w), dt)
    return pl.pallas_call(
        _dsa_prep_kernel, grid=(bsz, t // tb),
        in_specs=[mcol(0), mcol(1), mcol(2), out(iq_w),
                  pl.BlockSpec((1, tb, LANES), lambda b, i: (b, i, iq_w // LANES)),
                  pl.BlockSpec((1, tb, LANES), lambda b, i: (b, i, iq_w // LANES + 1)),
                  tab, tab, tab, tab, sel, sel],
        out_specs=[out(WIDTH)] * 3 + [out(IDX_HEADS * LANES), out(LANES), out(LANES)],
        out_shape=[shape(BF16, WIDTH)] * 3 + [shape(BF16, IDX_HEADS * LANES), shape(BF16, LANES), shape(F32, LANES)],
        compiler_params=_params(("parallel", "parallel")), name="dsa_prep",
    )(main3, main3, main3, idx3, idx3, idx3, *tables, sel_hi, sel_lo)


def _dsa_kernel(q_ref, k_ref, vt_ref, q3_ref, k3_ref, wt_ref, o_ref, key_sc, tie_sc, m_sc, l_sc, acc_sc,
                *bufs, topk):
    QB, KB = ATT_QB, ATT_KB
    i = pl.program_id(1)
    nkb = (i * QB) // KB + 1
    qpos = i * QB + _iota((1, QB), 1)
    int_min = jnp.int32(-2 ** 31)

    q3 = q3_ref[0]
    qs = jnp.concatenate([q3[:, h * LANES:(h + 1) * LANES] for h in range(IDX_HEADS)], axis=0)
    wt = wt_ref[0]

    def score_block(j, carry):
        ks = pl.ds(pl.multiple_of(j * KB, KB), KB)
        r = _dot(k3_ref[0, ks, :], qs, NT)
        sc = jnp.zeros((KB, QB), F32)
        for h in range(IDX_HEADS):
            sc = sc + jnp.maximum(r[:, h * QB:(h + 1) * QB], 0.0) * wt[h:h + 1, :]
        sc = jnp.where(j * KB + _iota((KB, 1), 0) <= qpos, sc, MASK_VALUE)
        sc = jnp.where(sc == 0.0, 0.0, sc)
        bits = pltpu.bitcast(sc, jnp.int32)
        key_sc[ks, :] = jnp.where(bits < 0, bits ^ jnp.int32(0x7FFFFFFF), bits)
        return carry

    lax.fori_loop(0, nkb, score_block, 0)

    def count_ge(cand):
        def blk(j, acc):
            base = pl.multiple_of(j * KB, KB)
            for r0 in range(0, KB, 64):
                acc = acc + jnp.where(key_sc[pl.ds(base + r0, 64), :] >= cand, 1.0, 0.0)
            return acc
        acc = lax.fori_loop(0, nkb, blk, jnp.zeros((64, QB), F32))
        return jnp.sum(acc, axis=0, keepdims=True)

    kf = float(topk)

    def slab(j, gm):
        base = pl.multiple_of(j * KB, KB)
        for r0 in range(0, KB, topk):
            gm = jnp.maximum(gm, key_sc[pl.ds(base + r0, topk), :])
        return gm

    gm = lax.fori_loop(0, nkb, slab, jnp.full((topk, QB), int_min, jnp.int32))
    lo0 = jnp.min(gm, axis=0, keepdims=True)
    hi0 = jnp.max(gm, axis=0, keepdims=True)
    width = hi0 - lo0
    steps = jnp.where(width < 0, 32.0, jnp.ceil(jnp.log2(width.astype(F32) + 1.0)) + 1.0)
    n_steps = jnp.minimum(jnp.max(steps), 32.0).astype(jnp.int32)

    def halve(_, carry):
        lo, hi = carry
        mid = (lo | hi) - ((lo ^ hi) >> 1)
        ok = count_ge(mid) >= kf
        return jnp.where(ok, mid, lo), jnp.where(ok, hi, mid - 1)

    thr, _ = lax.fori_loop(0, n_steps, halve, (lo0, hi0))
    need = kf - count_ge(thr + 1)

    lt = _lower_tri_ones(KB)
    qstack = _stack_heads(q_ref[0])

    def qk(j):
        return _dot(k_ref[0, pl.ds(pl.multiple_of(j * KB, KB), KB), :], qstack, NT)

    def post(j, s_t):
        ks = pl.ds(pl.multiple_of(j * KB, KB), KB)
        key = key_sc[ks, :]
        tie = key == thr
        tie_b = jnp.where(tie, 1.0, 0.0).astype(BF16)
        rank = tie_sc[...] + _dot(lt, tie_b)
        tie_sc[...] = rank[KB - 1:KB, :]
        sel = (key > thr) | (tie & (rank <= need))
        sel = sel & (j * KB + _iota((KB, 1), 0) <= qpos)
        bias = jnp.where(sel, 0.0, MASK_VALUE)
        return s_t + jnp.concatenate([bias] * N_HEADS, axis=1)

    tie_sc[...] = jnp.zeros_like(tie_sc)
    _attention_t(nkb, qk, post, post, vt_ref, bufs, m_sc, l_sc, acc_sc)
    o_ref[0] = _softmax_finish_t(l_sc, acc_sc).astype(o_ref.dtype)


def _dsa(q, k, v_t, q3, k3, w_t, topk):
    bsz, t, _ = q.shape
    qblk = lambda w: pl.BlockSpec((1, ATT_QB, w), lambda b, i: (b, i, 0))
    seq = lambda w: pl.BlockSpec((1, t, w), lambda b, i: (b, 0, 0))
    cols = N_HEADS * ATT_QB
    return pl.pallas_call(
        functools.partial(_dsa_kernel, topk=topk),
        grid=(bsz, t // ATT_QB),
        in_specs=[qblk(WIDTH), seq(WIDTH), pl.BlockSpec((1, WIDTH, t), lambda b, i: (b, 0, 0)),
                  qblk(IDX_HEADS * LANES), seq(LANES),
                  pl.BlockSpec((1, IDX_HEADS, ATT_QB), lambda b, i: (b, 0, i))],
        out_specs=qblk(WIDTH),
        out_shape=jax.ShapeDtypeStruct((bsz, t, WIDTH), BF16),
        scratch_shapes=[pltpu.VMEM((t, ATT_QB), jnp.int32), pltpu.VMEM((1, ATT_QB), F32), pltpu.VMEM((1, cols), F32),
                        pltpu.VMEM((1, cols), F32), pltpu.VMEM((N_HEADS, HEAD_DIM, ATT_QB), F32)] + _attention_bufs(),
        compiler_params=_params(("parallel", "arbitrary")), name="dsa_attention",
    )(q, k, v_t, q3, k3, w_t)


def _merge_kernel(x_ref, ya_ref, yb_ref, yc_ref, yd_ref, gpre_ref, wg_ref, wb_ref, wo_ref, g_ref, o_ref):
    d = x_ref.shape[-1]
    x = x_ref[...]
    h = _norm_rows(x, gpre_ref[...]).astype(BF16)
    merged = None
    for n, y_ref in enumerate((ya_ref, yb_ref, yc_ref, yd_ref)):
        gate = _sigmoid(_dot(h, wg_ref[:, n * d:(n + 1) * d]))
        term = gate * _dot(y_ref[...], wb_ref[n])
        merged = term if merged is None else merged + term
    mix = _dot(merged.astype(BF16), wo_ref[...])
    o_ref[...] = x + _norm_rows(mix, g_ref[...])


def _merge(x2, ys, g_pre, w_gate, w_branch, w_out, g_post, tm=256):
    n, d = x2.shape
    row = lambda w: pl.BlockSpec((tm, w), lambda i: (i, 0))
    vec = pl.BlockSpec((1, d), lambda i: (0, 0))
    return pl.pallas_call(
        _merge_kernel, grid=(n // tm,),
        in_specs=[row(d)] + [row(WIDTH)] * 4 + [vec, pl.BlockSpec(w_gate.shape, lambda i: (0, 0)),
                                                 pl.BlockSpec(w_branch.shape, lambda i: (0, 0, 0)),
                                                 pl.BlockSpec(w_out.shape, lambda i: (0, 0)), vec],
        out_specs=row(d),
        out_shape=jax.ShapeDtypeStruct((n, d), F32),
        compiler_params=_params(("parallel",)), name="gated_merge",
    )(x2, *ys, g_pre.reshape(1, d), w_gate.astype(BF16), w_branch.astype(BF16), w_out.astype(BF16),
      g_post.reshape(1, d))


def _mlp_kernel(x_ref, gpre_ref, wu_ref, wd_ref, gpost_ref, o_ref, h_ref, acc_ref):
    kf = pl.program_id(1)

    @pl.when(kf == 0)
    def _():
        h_ref[...] = _norm_rows(x_ref[...], gpre_ref[...]).astype(BF16)
        acc_ref[...] = jnp.zeros_like(acc_ref)

    u = jnp.maximum(_dot(h_ref[...], wu_ref[...]), 0.0)
    acc_ref[...] += _dot((u * u).astype(BF16), wd_ref[...])

    @pl.when(kf == pl.num_programs(1) - 1)
    def _():
        o_ref[...] = x_ref[...] + _norm_rows(acc_ref[...], gpost_ref[...])


def _mlp(x2, g_pre, w_up, w_down, g_post, tm=1024, tf=1024):
    n, d = x2.shape
    dff = w_up.shape[1]
    return pl.pallas_call(
        _mlp_kernel, grid=(n // tm, dff // tf),
        in_specs=[pl.BlockSpec((tm, d), lambda i, k: (i, 0)), pl.BlockSpec((1, d), lambda i, k: (0, 0)),
                  pl.BlockSpec((d, tf), lambda i, k: (0, k)), pl.BlockSpec((tf, d), lambda i, k: (k, 0)),
                  pl.BlockSpec((1, d), lambda i, k: (0, 0))],
        out_specs=pl.BlockSpec((tm, d), lambda i, k: (i, 0)),
        out_shape=jax.ShapeDtypeStruct((n, d), F32),
        scratch_shapes=[pltpu.VMEM((tm, d), BF16), pltpu.VMEM((tm, d), F32)],
        compiler_params=_params(("parallel", "arbitrary")), name="mlp",
    )(x2, g_pre.reshape(1, d), w_up.astype(BF16), w_down.astype(BF16), g_post.reshape(1, d))


def _rope_tables(t, dim, groups):
    inv = 1.0 / (ROPE_THETA ** (jnp.arange(0, dim, 2, dtype=F32) / dim))
    ang = jnp.arange(t, dtype=F32)[:, None] * inv[None, :]
    cos, sin = jnp.cos(ang), jnp.sin(ang)
    return jnp.tile(jnp.concatenate([cos, cos], axis=1), (1, groups)), jnp.tile(jnp.concatenate([-sin, sin], axis=1), (1, groups))


def _pad_rows(w, rows, offset):
    return jnp.zeros((rows, w.shape[1]), w.dtype).at[offset:offset + w.shape[0]].set(w)


def _pad_cols(w, cols):
    return jnp.pad(w, ((0, 0), (0, cols - w.shape[1])))


def kernel(x, norm_mix_pre, norm_mix_post, norm_mlp_pre, norm_mlp_post, w_in, w_branch, w_out, hgrn_lb_logits, hgrn_norm_w, fox_f_bias, rwkv_mu, rwkv_w0, rwkv_w2, rwkv_a0, rwkv_a2, rwkv_g2, rwkv_k_k, rwkv_k_a, rwkv_r_k, rwkv_ln_w, rwkv_ln_b, rwkv_v0, rwkv_v1, rwkv_v2, w_up, w_down):
    bsz, t, d = x.shape
    n = bsz * t
    depth = w_in.shape[0]
    W = WIDTH
    topk = min(TOPK_MAX, t // 4)

    lb_soft = jax.nn.softmax(hgrn_lb_logits.astype(F32), axis=0)
    lower_bounds = jnp.cumsum(lb_soft, axis=0) - lb_soft[0:1]
    tables = _rope_tables(t, HEAD_DIM, N_HEADS) + _rope_tables(t, IDX_DIM, IDX_HEADS)

    o_a, o_b, o_iq = 0, 4 * W, 7 * W
    o_ik, o_iw = o_iq + IDX_HEADS * IDX_DIM, o_iq + IDX_HEADS * IDX_DIM + IDX_DIM
    o_c = o_iw + IDX_HEADS
    c_cols = 3 * W + 128
    o_d = o_c + c_cols
    o_df = o_d + 3 * W
    o_g = o_df + N_HEADS
    main_blocks = dict(b_qkv=4, c=2, df=(7 * W + c_cols) // LANES)
    assert main_blocks["c"] * c_cols == 7 * W

    x2 = x.reshape(n, d)
    v_first = None
    for l in range(depth):
        wl = w_in[l]
        w_main = jnp.concatenate([
            wl[:, o_a:o_a + 7 * W], wl[:, o_c:o_c + c_cols], _pad_cols(wl[:, o_df:o_df + N_HEADS], LANES),
            jnp.zeros((d, 256), F32)], axis=1)
        w_idx = jnp.concatenate([
            wl[:, o_iq:o_iq + W], jnp.tile(wl[:, o_ik:o_ik + IDX_DIM], (1, LANES // IDX_DIM)),
            _pad_cols(wl[:, o_iw:o_iw + IDX_HEADS], LANES)], axis=1)
        w_fox = jnp.concatenate([wl[:, o_d:o_d + W] * (LOG2E * HEAD_DIM ** -0.5), wl[:, o_d + W:o_d + 3 * W]], axis=1)

        main2 = _norm_matmul(x2, norm_mix_pre[l], w_main, F32, tm=1024, tn=1024)
        idx2 = _norm_matmul(x2, norm_mix_pre[l], w_idx, F32, tm=1024, tn=w_idx.shape[1], precise=True)
        fox2 = _norm_matmul(x2, norm_mix_pre[l], w_fox, BF16, tm=1024, tn=3 * W)
        main3 = main2.reshape(bsz, t, -1)

        y_a = _hgrn(main3, lower_bounds[l], hgrn_norm_w[l])

        q_b, k_b, v_b, q3, k3, wi = _dsa_prep(main3, idx2.reshape(bsz, t, -1), tables, main_blocks["b_qkv"])
        y_b = _dsa(q_b, k_b, jnp.transpose(v_b, (0, 2, 1)), q3, k3,
                   jnp.transpose(wi[:, :, :IDX_HEADS], (0, 2, 1)), topk)

        prm = dict(
            mu=_pad_cols(rwkv_mu[l][None, :], RWKV_COLS), w0=rwkv_w0[l][None, :],
            w2=_pad_rows(rwkv_w2[l], RWKV_LORA_LANES, 0), a0=rwkv_a0[l][None, :],
            a2=_pad_rows(rwkv_a2[l], RWKV_LORA_LANES, 32), g2=_pad_rows(rwkv_g2[l], RWKV_LORA_LANES, 64),
            k_k=rwkv_k_k[l][None, :], k_a=rwkv_k_a[l][None, :], r_k=rwkv_r_k[l].reshape(1, W),
            ln_w=rwkv_ln_w[l][None, :], ln_b=rwkv_ln_b[l][None, :])
        if l > 0:
            prm.update(v0=rwkv_v0[l - 1][None, :], v1=_pad_cols(rwkv_v1[l - 1], LANES),
                       v2=_pad_rows(rwkv_v2[l - 1], LANES, 0))
        y_c, v_c = _rwkv(main3, main_blocks["c"], v_first, prm)
        if l == 0:
            v_first = v_c

        c_rep = _fox_cum(main3, main_blocks["df"], _pad_cols(fox_f_bias[l][None, :], LANES))
        fox3 = fox2.reshape(bsz, t, 3 * W)
        y_d = _fox(fox3, jnp.transpose(fox3[:, :, 2 * W:], (0, 2, 1)), c_rep)

        ys = [y.reshape(n, W) for y in (y_a, y_b, y_c, y_d)]
        x2 = _merge(x2, ys, norm_mix_pre[l], wl[:, o_g:o_g + 4 * d], w_branch[l], w_out[l], norm_mix_post[l])
        x2 = _mlp(x2, norm_mlp_pre[l], w_up[l], w_down[l], norm_mlp_post[l])
    return x2.reshape(bsz, t, d)
```

```python
import functools

import jax
import jax.numpy as jnp
from jax import lax
from jax.experimental import pallas as pl
from jax.experimental.pallas import tpu as pltpu

F32 = jnp.float32
BF16 = jnp.bfloat16

HEAD_DIM = 64
N_HEADS = 4
WIDTH = N_HEADS * HEAD_DIM
IDX_HEADS = 8
IDX_DIM = 32
TOPK_MAX = 256
ROPE_THETA = 10000.0
NORM_EPS = 1e-6
RWKV_GN_EPS = 64e-5
MASK_VALUE = -1e30
TINY = 1.1754944e-38
LOG2E = 1.4426950408889634
RWKV_LORA_LANES = 128
RWKV_COLS = 3 * WIDTH + RWKV_LORA_LANES

LANES = 128
VMEM_LIMIT = 48 * 1024 * 1024

NN = ((1,), (0,))
NT = ((1,), (1,))
TN = ((0,), (0,))


def _dot(a, b, dims=NN):
    return lax.dot_general(a, b, (dims, ((), ())), preferred_element_type=F32)


def _split2(x):
    hi = x.astype(BF16)
    lo = (x - hi.astype(F32)).astype(BF16)
    return hi, lo


def _split3(x):
    hi = x.astype(BF16)
    r1 = x - hi.astype(F32)
    mid = r1.astype(BF16)
    lo = (r1 - mid.astype(F32)).astype(BF16)
    return hi, mid, lo


def _dot3(a, b, dims=NN):
    ah, al = _split2(a)
    bh, bl = _split2(b)
    return _dot(ah, bh, dims) + (_dot(ah, bl, dims) + _dot(al, bh, dims))


def _dot_exact_lhs(a_bf16, b, dims=NN):
    b1, b2, b3 = _split3(b)
    return _dot(a_bf16, b1, dims) + (_dot(a_bf16, b2, dims) + _dot(a_bf16, b3, dims))


def _dot_exact_rhs(a, b_bf16, dims=NN):
    a1, a2 = _split2(a)
    return _dot(a1, b_bf16, dims) + _dot(a2, b_bf16, dims)


def _iota(shape, dim):
    return lax.broadcasted_iota(jnp.int32, shape, dim)


def _head_block_ones(n, group):
    return (_iota((n, n), 0) // group == _iota((n, n), 1) // group).astype(BF16)


def _lower_tri_ones(n):
    return (_iota((n, n), 0) >= _iota((n, n), 1)).astype(BF16)


def _sigmoid(x):
    return 1.0 / (1.0 + jnp.exp(-x))


def _softplus(x):
    return jnp.maximum(x, 0.0) + jnp.log(1.0 + jnp.exp(-jnp.abs(x)))


def _params(sem):
    return pltpu.CompilerParams(dimension_semantics=sem, vmem_limit_bytes=VMEM_LIMIT)


def _norm_rows(x, g):
    return x * lax.rsqrt(jnp.mean(x * x, axis=-1, keepdims=True) + NORM_EPS) * g


def _norm_matmul_kernel(x_ref, g_ref, w_ref, o_ref, h_ref):
    @pl.when(pl.program_id(1) == 0)
    def _():
        h_ref[...] = _norm_rows(x_ref[...], g_ref[...]).astype(BF16)

    o_ref[...] = _dot(h_ref[...], w_ref[...]).astype(o_ref.dtype)


def _norm_matmul3_kernel(x_ref, g_ref, wh_ref, wl_ref, o_ref, hh_ref, hl_ref):
    @pl.when(pl.program_id(1) == 0)
    def _():
        hh, hl = _split2(_norm_rows(x_ref[...], g_ref[...]))
        hh_ref[...] = hh
        hl_ref[...] = hl

    o_ref[...] = (_dot(hh_ref[...], wh_ref[...])
                  + (_dot(hh_ref[...], wl_ref[...]) + _dot(hl_ref[...], wh_ref[...]))).astype(o_ref.dtype)


def _norm_matmul(x2, g, w, out_dtype, tm, tn, precise=False):
    n, d = x2.shape
    cols = w.shape[1]
    grid = (n // tm, cols // tn)
    x_spec = pl.BlockSpec((tm, d), lambda i, j: (i, 0))
    g_spec = pl.BlockSpec((1, d), lambda i, j: (0, 0))
    w_spec = pl.BlockSpec((d, tn), lambda i, j: (0, j))
    o_spec = pl.BlockSpec((tm, tn), lambda i, j: (i, j))
    g2 = g.reshape(1, d)
    if precise:
        wh, wl = _split2(w)
        return pl.pallas_call(
            _norm_matmul3_kernel, grid=grid,
            in_specs=[x_spec, g_spec, w_spec, w_spec], out_specs=o_spec,
            out_shape=jax.ShapeDtypeStruct((n, cols), out_dtype),
            scratch_shapes=[pltpu.VMEM((tm, d), BF16), pltpu.VMEM((tm, d), BF16)],
            compiler_params=_params(("parallel", "arbitrary")), name="norm_proj_precise",
        )(x2, g2, wh, wl)
    return pl.pallas_call(
        _norm_matmul_kernel, grid=grid,
        in_specs=[x_spec, g_spec, w_spec], out_specs=o_spec,
        out_shape=jax.ShapeDtypeStruct((n, cols), out_dtype),
        scratch_shapes=[pltpu.VMEM((tm, d), BF16)],
        compiler_params=_params(("parallel", "arbitrary")), name="norm_proj",
    )(x2, g2, w.astype(BF16))


HGRN_CHUNK = 64
HGRN_GROUP = 8


def _dot1(a, b, dims=NN):
    return _dot(a.astype(BF16), b.astype(BF16), dims)


def _hgrn_kernel(q_ref, f_ref, i_ref, g_ref, lb_ref, nw_ref, o_ref, st_ref, *, n_chunks):
    C, G = HGRN_CHUNK, HGRN_GROUP

    @pl.when(pl.program_id(1) == 0)
    def _():
        st_ref[...] = jnp.zeros_like(st_ref)

    lb = lb_ref[...]
    nw = nw_ref[...]
    tri = _lower_tri_ones(C)
    bd = _head_block_ones(WIDTH, HEAD_DIM)
    bd_mask = _iota((WIDTH, WIDTH), 0) // HEAD_DIM == _iota((WIDTH, WIDTH), 1) // HEAD_DIM
    row_top = _iota((G, 1), 0)
    lane_head = _iota((1, WIDTH), 1) // HEAD_DIM

    def chunk(c, carry):
        sl = pl.ds(pl.multiple_of(c * C, C), C)
        fl = f_ref[0, sl, :]
        f = lb + (1.0 - lb) * _sigmoid(fl)
        k = (1.0 - lb) * _sigmoid(-fl)
        b = _dot_exact_lhs(tri, jnp.log(f))
        q = q_ref[0, sl, :] * (HEAD_DIM ** -0.5)
        v = i_ref[0, sl, :]
        st = st_ref[...]
        o_inter = _dot1(q * jnp.exp(b), st, NT)

        n_groups = C // G
        o_rows = [o_inter[g * G:(g + 1) * G] for g in range(n_groups)]
        pieces = []
        for g in range(n_groups):
            s0 = g * G
            q_top, b_top, k_top = q[s0:s0 + G], b[s0:s0 + G], k[s0:s0 + G]
            for u in range(G):
                top = q_top * k_top[u:u + 1] * jnp.exp(jnp.minimum(b_top - b_top[u:u + 1], 0.0))
                pieces.append(jnp.where(row_top >= u, top, 0.0))
        r = _dot(jnp.concatenate(pieces, axis=0).astype(BF16), bd)
        att, v_st = [], []
        for g in range(n_groups - 1):
            s0 = g * G
            b_top, k_top, v_top = b[s0:s0 + G], k[s0:s0 + G], v[s0:s0 + G]
            b_ref = b_top[G - 1:G]
            qp = (q[s0 + G:] * jnp.exp(b[s0 + G:] - b_ref)).astype(BF16)
            kp = k_top * jnp.exp(b_ref - b_top)
            k_st = jnp.concatenate([jnp.where(lane_head == h, kp, 0.0) for h in range(N_HEADS)], axis=0)
            v_st.append(jnp.concatenate([jnp.where(lane_head == h, v_top, 0.0) for h in range(N_HEADS)], axis=0))
            att.append(_dot(qp, k_st.astype(BF16), NT))
        for g in range(n_groups):
            v_top = v[g * G:(g + 1) * G]
            for u in range(G):
                o_rows[g] = o_rows[g] + r[(g * G + u) * G:(g * G + u + 1) * G] * v_top[u:u + 1]
        for g in range(n_groups - 1):
            rest = _dot(att[g].astype(BF16), v_st[g].astype(BF16))
            for g2 in range(g + 1, n_groups):
                o_rows[g2] = o_rows[g2] + rest[(g2 - g - 1) * G:(g2 - g) * G]
        o = jnp.concatenate(o_rows, axis=0)

        b_last = b[C - 1:C, :]
        upd = _dot1(v, k * jnp.exp(b_last - b), TN)
        st_ref[...] = st * jnp.exp(b_last) + jnp.where(bd_mask, upd, 0.0)

        ms = _dot_exact_rhs(o * o, bd) * (1.0 / HEAD_DIM)
        on = o * lax.rsqrt(ms + NORM_EPS) * nw
        gl = g_ref[0, sl, :]
        o_ref[0, sl, :] = (on * (gl * _sigmoid(gl))).astype(o_ref.dtype)
        return carry

    lax.fori_loop(0, n_chunks, chunk, 0)


def _hgrn(main3, lb, norm_w, tb=256):
    bsz, t, _ = main3.shape
    col = lambda j: pl.BlockSpec((1, tb, WIDTH), lambda b, i, j=j: (b, i, j))
    vec = pl.BlockSpec((1, WIDTH), lambda b, i: (0, 0))
    return pl.pallas_call(
        functools.partial(_hgrn_kernel, n_chunks=tb // HGRN_CHUNK),
        grid=(bsz, t // tb),
        in_specs=[col(0), col(1), col(2), col(3), vec, vec],
        out_specs=pl.BlockSpec((1, tb, WIDTH), lambda b, i: (b, i, 0)),
        out_shape=jax.ShapeDtypeStruct((bsz, t, WIDTH), BF16),
        scratch_shapes=[pltpu.VMEM((WIDTH, WIDTH), F32)],
        compiler_params=_params(("parallel", "arbitrary")), name="hgrn2",
    )(main3, main3, main3, main3, lb.reshape(1, WIDTH), jnp.tile(norm_w, N_HEADS).reshape(1, WIDTH))


RWKV_CHUNK = 64


def _rwkv_kernel(*refs, n_chunks, has_vgate):
    if has_vgate:
        (c_ref, vf_ref, mu_ref, w0_ref, w2_ref, a0_ref, a2_ref, g2_ref, kkw_ref, kaw_ref, rk_ref,
         lnw_ref, lnb_ref, v0_ref, v1_ref, v2_ref, y_ref, vout_ref, s_ref, prev_ref, y_sc) = refs
    else:
        (c_ref, mu_ref, w0_ref, w2_ref, a0_ref, a2_ref, g2_ref, kkw_ref, kaw_ref, rk_ref,
         lnw_ref, lnb_ref, y_ref, vout_ref, s_ref, prev_ref, y_sc) = refs
    C = RWKV_CHUNK
    W = WIDTH
    tb = n_chunks * C

    @pl.when(pl.program_id(1) == 0)
    def _():
        s_ref[...] = jnp.zeros_like(s_ref)
        prev_ref[...] = jnp.zeros_like(prev_ref)

    p = c_ref[0]
    shifted = jnp.where(_iota((tb, 1), 0) == 0, prev_ref[...], pltpu.roll(p, 1, axis=0))
    prev_ref[...] = p[tb - 1:tb, :]
    xm = p + (shifted - p) * mu_ref[...]
    r = xm[:, 0:W]
    k = xm[:, W:2 * W]
    v = xm[:, 2 * W:3 * W]
    lora = xm[:, 3 * W:3 * W + RWKV_LORA_LANES]

    bd = _head_block_ones(W, HEAD_DIM)
    w_log = -_softplus(-(w0_ref[...] + _dot3(jnp.tanh(lora), w2_ref[...]))) - 0.5
    log_decay = -jnp.exp(w_log)
    a = _sigmoid(a0_ref[...] + _dot3(lora, a2_ref[...]))
    g = _dot3(_sigmoid(lora), g2_ref[...])
    if has_vgate:
        vg = _dot3(_dot3(v, v1_ref[...]), v2_ref[...])
        v = v + (vf_ref[0] - v) * _sigmoid(v0_ref[...] + vg)
    vout_ref[0] = v
    kk = k * kkw_ref[...]
    kk = kk * lax.rsqrt(jnp.maximum(_dot_exact_rhs(kk * kk, bd), 1e-24))
    k = k * (1.0 + (a - 1.0) * kaw_ref[...])

    tri = _lower_tri_ones(C)
    lane_head = _iota((1, W), 1) // HEAD_DIM
    hc = N_HEADS * C
    rt = _iota((hc, hc), 0) % C
    ct = _iota((hc, hc), 1) % C
    strict_lower = rt > ct
    lower = rt >= ct
    eye = (_iota((hc, hc), 0) == _iota((hc, hc), 1)).astype(F32)

    def stack(m):
        return jnp.concatenate([jnp.where(lane_head == h, m, 0.0) for h in range(N_HEADS)], axis=0)

    chunks = range(n_chunks)
    a_s, b_s, k_s, r_s, v_s, p_last, m, lak, lrb, lrk = ([None] * n_chunks for _ in range(10))
    for c in chunks:
        sl = slice(c * C, (c + 1) * C)
        ld = log_decay[sl]
        cum = _dot_exact_lhs(tri, ld)
        pdec = jnp.exp(cum)
        pinv = jnp.exp(-cum)
        p_last[c] = pdec[C - 1:C, :]
        a_s[c] = stack(-kk[sl] * jnp.exp(cum - ld)).astype(BF16)
        b_s[c] = stack(kk[sl] * a[sl] * pinv).astype(BF16)
        k_s[c] = stack(k[sl] * pinv).astype(BF16)
        r_s[c] = stack(r[sl] * pdec)
        v_s[c] = stack(v[sl]).astype(BF16)
        gram = _dot(jnp.concatenate([a_s[c], r_s[c].astype(BF16)], axis=0),
                    jnp.concatenate([b_s[c], k_s[c]], axis=0), NT)
        m[c] = jnp.where(strict_lower, gram[:hc, :hc], 0.0)
        lak[c] = jnp.where(strict_lower, gram[:hc, hc:], 0.0)
        lrb[c] = jnp.where(lower, gram[hc:, :hc], 0.0)
        lrk[c] = jnp.where(lower, gram[hc:, hc:], 0.0)

    x = [eye + m[c] for c in chunks]
    power = 2
    while power < C:
        m = [_dot1(m[c], m[c]) for c in chunks]
        x = [x[c] + _dot1(x[c], m[c]) for c in chunks]
        power *= 2
    lakv = [_dot1(lak[c], v_s[c]) for c in chunks]
    lrkv = [_dot1(lrk[c], v_s[c]) for c in chunks]
    w1 = [_dot1(x[c], lakv[c]) for c in chunks]
    a2 = [_dot1(x[c], a_s[c]) for c in chunks]
    r2 = [r_s[c] + _dot1(lrb[c], a2[c]) for c in chunks]
    y0 = [_dot1(lrb[c], w1[c]) + lrkv[c] for c in chunks]
    t1 = [_dot1(a2[c], b_s[c], TN) for c in chunks]
    t0 = [_dot1(w1[c], b_s[c], TN) + _dot(v_s[c], k_s[c], TN) for c in chunks]

    s = s_ref[...]
    for c in chunks:
        y = _dot1(r2[c], s, NT) + y0[c]
        y_sc[c * C:(c + 1) * C, :] = sum(y[h * C:(h + 1) * C] for h in range(N_HEADS))
        s = (s + _dot1(s, t1[c]) + t0[c]) * p_last[c]
    s_ref[...] = s

    y = y_sc[...]
    inv_n = 1.0 / HEAD_DIM
    mean = _dot_exact_rhs(y, bd) * inv_n
    yc = y - mean
    var = _dot_exact_rhs(yc * yc, bd) * inv_n
    yn = yc * lax.rsqrt(var + RWKV_GN_EPS) * lnw_ref[...] + lnb_ref[...]
    yn = yn + _dot_exact_rhs(r * k * rk_ref[...], bd) * v
    y_ref[0] = (yn * g).astype(y_ref.dtype)


def _rwkv(c3, col_block, v_first, prm, tb=256):
    bsz, t, _ = c3.shape
    has_vgate = v_first is not None
    blk = lambda w: pl.BlockSpec((1, tb, w), lambda b, i: (b, i, 0))
    cblk = pl.BlockSpec((1, tb, RWKV_COLS), lambda b, i: (b, i, col_block))
    full = lambda a: pl.BlockSpec(a.shape, lambda b, i: (0,) * a.ndim)
    names = ["mu", "w0", "w2", "a0", "a2", "g2", "k_k", "k_a", "r_k", "ln_w", "ln_b"]
    if has_vgate:
        names += ["v0", "v1", "v2"]
    args = [c3] + ([v_first] if has_vgate else []) + [prm[n] for n in names]
    in_specs = [cblk] + ([blk(WIDTH)] if has_vgate else []) + [full(prm[n]) for n in names]
    return pl.pallas_call(
        functools.partial(_rwkv_kernel, n_chunks=tb // RWKV_CHUNK, has_vgate=has_vgate),
        grid=(bsz, t // tb),
        in_specs=in_specs,
        out_specs=[blk(WIDTH), blk(WIDTH)],
        out_shape=[jax.ShapeDtypeStruct((bsz, t, WIDTH), BF16), jax.ShapeDtypeStruct((bsz, t, WIDTH), F32)],
        scratch_shapes=[pltpu.VMEM((WIDTH, WIDTH), F32), pltpu.VMEM((1, RWKV_COLS), F32),
                        pltpu.VMEM((tb, WIDTH), F32)],
        compiler_params=_params(("parallel", "arbitrary")), name="rwkv7",
    )(*args)


def _fox_cum_kernel(f_ref, bias_ref, c_ref, carry_ref, *, tb):
    @pl.when(pl.program_id(1) == 0)
    def _():
        carry_ref[...] = jnp.zeros_like(carry_ref)

    logf = -_softplus(-(f_ref[0] + bias_ref[...]))
    c = _dot_exact_lhs(_lower_tri_ones(tb), logf) + carry_ref[...]
    carry_ref[...] = c[tb - 1:tb, :]
    for h in range(N_HEADS):
        pick = (_iota((LANES, LANES), 0) == h).astype(BF16)
        c1, c2, c3 = _split3(c)
        c_ref[0, h] = LOG2E * (_dot(c1, pick) + (_dot(c2, pick) + _dot(c3, pick)))


def _fox_cum(main3, col_block, bias_row, tb=256):
    bsz, t, _ = main3.shape
    return pl.pallas_call(
        functools.partial(_fox_cum_kernel, tb=tb),
        grid=(bsz, t // tb),
        in_specs=[pl.BlockSpec((1, tb, LANES), lambda b, i: (b, i, col_block)),
                  pl.BlockSpec((1, LANES), lambda b, i: (0, 0))],
        out_specs=pl.BlockSpec((1, N_HEADS, tb, LANES), lambda b, i: (b, 0, i, 0)),
        out_shape=jax.ShapeDtypeStruct((bsz, N_HEADS, t, LANES), F32),
        scratch_shapes=[pltpu.VMEM((1, LANES), F32)],
        compiler_params=_params(("parallel", "arbitrary")), name="fox_cumgate",
    )(main3, bias_row)


ATT_QB = 128
ATT_KB = 512


def _stack_heads(q):
    lane_head = _iota((1, WIDTH), 1) // HEAD_DIM
    zero = jnp.zeros_like(q)
    return jnp.concatenate([jnp.where(lane_head == h, q, zero) for h in range(N_HEADS)], axis=0)


def _softmax_stage_t(s_t, m_sc, l_sc):
    m_old = m_sc[...]
    m_new = jnp.maximum(m_old, jnp.max(s_t, axis=0, keepdims=True))
    alpha = jnp.exp2(m_old - m_new)
    p_t = jnp.exp2(s_t - m_new)
    l_sc[...] = alpha * l_sc[...] + jnp.sum(p_t, axis=0, keepdims=True)
    m_sc[...] = m_new
    return p_t.astype(BF16), alpha


def _pv_stage_t(vt_blk, p_t, alpha, acc_sc):
    for h in range(N_HEADS):
        cs = slice(h * ATT_QB, (h + 1) * ATT_QB)
        pv = _dot(vt_blk[h * HEAD_DIM:(h + 1) * HEAD_DIM, :], p_t[:, cs])
        acc_sc[h] = alpha[:, cs] * acc_sc[h] + pv


def _attention_t(first, n_blocks, qk, post, post_last, vt_ref, bufs, m_sc, l_sc, acc_sc):
    KB = ATT_KB
    s_buf, p_buf, a_buf = bufs[0:2], bufs[2:4], bufs[4:6]
    _softmax_init_t(m_sc, l_sc, acc_sc)

    def vt_blk(j):
        return vt_ref[0, :, pl.ds(pl.multiple_of(j * KB, KB), KB)]

    def pending(j, par):
        _pv_stage_t(vt_blk(jnp.maximum(j - 1, first)), p_buf[1 - par][...], a_buf[1 - par][...], acc_sc)

    def step(j, par):
        raw = s_buf[par][...]
        s_buf[1 - par][...] = qk(j + 1)
        p_t, alpha = _softmax_stage_t(post(j, raw), m_sc, l_sc)
        pending(j, par)
        p_buf[par][...] = p_t
        a_buf[par][...] = alpha

    def final(j, par):
        p_t, alpha = _softmax_stage_t(post_last(j, s_buf[par][...]), m_sc, l_sc)
        pending(j, par)
        _pv_stage_t(vt_blk(j), p_t, alpha, acc_sc)

    s_buf[0][...] = qk(first)
    p_buf[1][...] = jnp.zeros_like(p_buf[1])
    a_buf[1][...] = jnp.ones_like(a_buf[1])
    pairs = (n_blocks - 1) // 2

    def pair(t, carry):
        step(first + 2 * t, 0)
        step(first + 2 * t + 1, 1)
        return carry

    lax.fori_loop(0, pairs, pair, 0)
    odd_left = (n_blocks - 1) % 2 == 1

    @pl.when(odd_left)
    def _():
        step(first + 2 * pairs, 0)
        final(first + 2 * pairs + 1, 1)

    @pl.when(jnp.logical_not(odd_left))
    def _():
        final(first + 2 * pairs, 0)


def _attention_bufs():
    cols = N_HEADS * ATT_QB
    return ([pltpu.VMEM((ATT_KB, cols), F32)] * 2 + [pltpu.VMEM((ATT_KB, cols), BF16)] * 2
            + [pltpu.VMEM((1, cols), F32)] * 2)


def _softmax_init_t(m_sc, l_sc, acc_sc):
    m_sc[...] = jnp.full_like(m_sc, MASK_VALUE)
    l_sc[...] = jnp.zeros_like(l_sc)
    acc_sc[...] = jnp.zeros_like(acc_sc)


def _softmax_finish_t(l_sc, acc_sc):
    l = l_sc[...]
    out_t = jnp.concatenate([acc_sc[h] / l[:, h * ATT_QB:(h + 1) * ATT_QB] for h in range(N_HEADS)], axis=0)
    return out_t.T


FOX_NEGLIGIBLE_LOG2 = -200.0


def _fox_kernel(q_ref, k_ref, vt_ref, ck_ref, o_ref, m_sc, l_sc, acc_sc, kn_sc, *bufs):
    QB, KB = ATT_QB, ATT_KB
    i = pl.program_id(1)
    t_len = k_ref.shape[1]

    def sq_norm_max(x):
        xf = x.astype(F32)
        return jnp.max(jnp.sum(xf * xf, axis=1, keepdims=True), axis=0, keepdims=True)

    @pl.when(i == 0)
    def _():
        def blk(j, best):
            return jnp.maximum(best, sq_norm_max(k_ref[0, pl.ds(pl.multiple_of(j * KB, KB), KB), :]))
        kn_sc[...] = lax.fori_loop(0, t_len // KB, blk, jnp.zeros((1, 1), F32))

    q = q_ref[0]
    qstack = _stack_heads(q)
    qpos = i * QB + _iota((1, N_HEADS * QB), 1) % QB
    last = (i * QB) // KB

    qk_bound = 2.02 * jnp.sqrt(sq_norm_max(q) * kn_sc[...])
    skipped = jnp.zeros((1, LANES), F32)
    for j in range(t_len // KB - 1):
        neg = None
        for h in range(N_HEADS):
            gap = ck_ref[0, h, pl.ds(i * QB, 1), :] - ck_ref[0, h, (j + 1) * KB - 1:(j + 1) * KB, :] + qk_bound
            ok = gap < FOX_NEGLIGIBLE_LOG2
            neg = ok if neg is None else neg & ok
        skipped = skipped + jnp.where(neg & (j < last), 1.0, 0.0)
    first = jnp.max(skipped).astype(jnp.int32)

    def qk(j):
        return _dot(k_ref[0, pl.ds(pl.multiple_of(j * KB, KB), KB), :], qstack, NT)

    def post(j, s_t):
        ks = pl.ds(pl.multiple_of(j * KB, KB), KB)
        return jnp.concatenate([s_t[:, h * QB:(h + 1) * QB] - ck_ref[0, h, ks, :] for h in range(N_HEADS)], axis=1)

    def post_last(j, s_t):
        return jnp.where(j * KB + _iota((KB, 1), 0) <= qpos, post(j, s_t), MASK_VALUE)

    _attention_t(first, last - first + 1, qk, post, post_last, vt_ref, bufs, m_sc, l_sc, acc_sc)
    o_ref[0] = _softmax_finish_t(l_sc, acc_sc).astype(o_ref.dtype)


def _fox(qkv3, v_t, c_rep):
    bsz, t, _ = qkv3.shape
    cols = N_HEADS * ATT_QB
    return pl.pallas_call(
        _fox_kernel,
        grid=(bsz, t // ATT_QB),
        in_specs=[pl.BlockSpec((1, ATT_QB, WIDTH), lambda b, i: (b, i, 0)),
                  pl.BlockSpec((1, t, WIDTH), lambda b, i: (b, 0, 1)),
                  pl.BlockSpec((1, WIDTH, t), lambda b, i: (b, 0, 0)),
                  pl.BlockSpec((1, N_HEADS, t, LANES), lambda b, i: (b, 0, 0, 0))],
        out_specs=pl.BlockSpec((1, ATT_QB, WIDTH), lambda b, i: (b, i, 0)),
        out_shape=jax.ShapeDtypeStruct((bsz, t, WIDTH), BF16),
        scratch_shapes=[pltpu.VMEM((1, cols), F32), pltpu.VMEM((1, cols), F32),
                        pltpu.VMEM((N_HEADS, HEAD_DIM, ATT_QB), F32), pltpu.VMEM((1, 1), F32)] + _attention_bufs(),
        compiler_params=_params(("parallel", "arbitrary")), name="fox_attention",
    )(qkv3, qkv3, v_t, c_rep)


def _swap_halves(x, half):
    n = x.shape[-1]
    lower = (_iota((1, n), 1) % (2 * half)) < half
    return jnp.where(lower, pltpu.roll(x, n - half, axis=1), pltpu.roll(x, half, axis=1))


def _dsa_prep_kernel(q_ref, k_ref, v_ref, iq_ref, ik_ref, iw_ref, ch_ref, sh_ref, ci_ref, si_ref,
                     selh_ref, sell_ref, qo_ref, ko_ref, vo_ref, q3_ref, k3_ref, wi_ref):
    ch, sh, ci, si = ch_ref[...], sh_ref[...], ci_ref[...], si_ref[...]

    def rope(x, c, s, half):
        return x * c + _swap_halves(x, half) * s

    qo_ref[0] = (rope(q_ref[0], ch, sh, HEAD_DIM // 2) * (LOG2E * HEAD_DIM ** -0.5)).astype(BF16)
    ko_ref[0] = rope(k_ref[0], ch, sh, HEAD_DIM // 2).astype(BF16)
    vo_ref[0] = v_ref[0].astype(BF16)
    qh, ql = _split2(rope(iq_ref[0], ci, si, IDX_DIM // 2) * (IDX_DIM ** -0.5))
    q3_ref[0] = (_dot(qh, selh_ref[...]) + _dot(ql, sell_ref[...])).astype(BF16)
    kh, kl = _split2(rope(ik_ref[0], ci[:, :LANES], si[:, :LANES], IDX_DIM // 2))
    seg = _iota((1, LANES), 1) // IDX_DIM
    k3_ref[0] = jnp.where(seg < 2, kh, jnp.where(seg == 2, kl, jnp.zeros_like(kl)))
    wi_ref[0] = iw_ref[0] * (IDX_HEADS ** -0.5)


def _dsa_prep(main3, idx3, tables, qkv_block0, tb=512):
    bsz, t, _ = main3.shape
    iq_w = IDX_HEADS * IDX_DIM
    r = jnp.arange(iq_w)[:, None]
    c = jnp.arange(IDX_HEADS * LANES)[None, :]
    same = (r // IDX_DIM == c // LANES) & (r % IDX_DIM == c % IDX_DIM)
    seg = (c % LANES) // IDX_DIM
    sel_hi = (same & ((seg == 0) | (seg == 2))).astype(BF16)
    sel_lo = (same & (seg == 1)).astype(BF16)
    mcol = lambda j: pl.BlockSpec((1, tb, WIDTH), lambda b, i, j=j: (b, i, qkv_block0 + j))
    tab = pl.BlockSpec((tb, WIDTH), lambda b, i: (i, 0))
    sel = pl.BlockSpec(sel_hi.shape, lambda b, i: (0, 0))
    out = lambda w: pl.BlockSpec((1, tb, w), lambda b, i: (b, i, 0))
    shape = lambda dt, w: jax.ShapeDtypeStruct((bsz, t, w), dt)
    return pl.pallas_call(
        _dsa_prep_kernel, grid=(bsz, t // tb),
        in_specs=[mcol(0), mcol(1), mcol(2), out(iq_w),
                  pl.BlockSpec((1, tb, LANES), lambda b, i: (b, i, iq_w // LANES)),
                  pl.BlockSpec((1, tb, LANES), lambda b, i: (b, i, iq_w // LANES + 1)),
                  tab, tab, tab, tab, sel, sel],
        out_specs=[out(WIDTH)] * 3 + [out(IDX_HEADS * LANES), out(LANES), out(LANES)],
        out_shape=[shape(BF16, WIDTH)] * 3 + [shape(BF16, IDX_HEADS * LANES), shape(BF16, LANES), shape(F32, LANES)],
        compiler_params=_params(("parallel", "parallel")), name="dsa_prep",
    )(main3, main3, main3, idx3, idx3, idx3, *tables, sel_hi, sel_lo)


def _dsa_kernel(q_ref, k_ref, vt_ref, q3_ref, k3_ref, wt_ref, o_ref, key_sc, top_sc, tie_sc, m_sc, l_sc, acc_sc,
                *bufs, topk):
    QB, KB = ATT_QB, ATT_KB
    i = pl.program_id(1)
    nkb = (i * QB) // KB + 1
    qpos = i * QB + _iota((1, QB), 1)

    q3 = q3_ref[0]
    qs = jnp.concatenate([q3[:, h * LANES:(h + 1) * LANES] for h in range(IDX_HEADS)], axis=0)
    wt = wt_ref[0]

    def score_block(j, carry):
        ks = pl.ds(pl.multiple_of(j * KB, KB), KB)
        r = _dot(k3_ref[0, ks, :], qs, NT)
        sc = jnp.zeros((KB, QB), F32)
        for h in range(IDX_HEADS):
            sc = sc + jnp.maximum(r[:, h * QB:(h + 1) * QB], 0.0) * wt[h:h + 1, :]
        sc = jnp.where(j * KB + _iota((KB, 1), 0) <= qpos, sc, MASK_VALUE)
        sc = jnp.where(jnp.abs(sc) < TINY, 0.0, sc)
        bits = pltpu.bitcast(sc, jnp.int32)
        key_sc[ks, :] = jnp.where(bits < 0, bits ^ jnp.int32(0x7FFFFFFF), bits)
        top_sc[ks, :] = pltpu.bitcast(bits & jnp.int32(-65536), F32).astype(BF16)
        return carry

    lax.fori_loop(0, nkb, score_block, 0)

    kf = float(topk)

    def count_ge(cand):
        def blk(j, acc):
            base = pl.multiple_of(j * KB, KB)
            for r0 in range(0, KB, 64):
                acc = acc + jnp.where(key_sc[pl.ds(base + r0, 64), :] >= cand, 1.0, 0.0)
            return acc
        acc = lax.fori_loop(0, nkb, blk, jnp.zeros((64, QB), F32))
        return jnp.sum(acc, axis=0, keepdims=True)

    def count_top_ge(cand):
        one, zero = jnp.ones((), BF16), jnp.zeros((), BF16)

        def blk(j, acc):
            base = pl.multiple_of(j * KB, KB)
            for r0 in range(0, KB, 128):
                acc = acc + jnp.where(top_sc[pl.ds(base + r0, 128), :] >= cand, one, zero)
            return acc
        acc = lax.fori_loop(0, nkb, blk, jnp.zeros((128, QB), BF16))
        return jnp.sum(acc.astype(F32), axis=0, keepdims=True)

    def top_step(step, v):
        trial = jnp.where(step == 0, jnp.zeros_like(v), v | (jnp.int32(1) << (15 - step)))
        mag = jnp.where(trial < 0, trial ^ jnp.int32(0x7FFF), trial) & jnp.int32(0x7FFF)
        denormal = (mag & jnp.int32(0x7F80)) == 0
        mag = jnp.where(denormal, jnp.where((trial > 0), jnp.int32(0x0080), jnp.int32(0)), mag)
        sign = jnp.where((trial < 0) & jnp.logical_not(denormal), jnp.int32(0x8000), jnp.int32(0))
        cand = pltpu.bitcast((sign | mag) << 16, F32).astype(BF16)
        return jnp.where(count_top_ge(cand) >= kf, trial, v)

    def low_step(step, v):
        trial = v | (jnp.int32(1) << (15 - step))
        return jnp.where(count_ge(trial) >= kf, trial, v)

    top = lax.fori_loop(0, 16, top_step, jnp.full((1, QB), -32768, jnp.int32))
    thr = lax.fori_loop(0, 16, low_step, top << 16)
    need = kf - count_ge(thr + 1)

    lt = _lower_tri_ones(KB)
    qstack = _stack_heads(q_ref[0])

    def qk(j):
        return _dot(k_ref[0, pl.ds(pl.multiple_of(j * KB, KB), KB), :], qstack, NT)

    def post(j, s_t):
        ks = pl.ds(pl.multiple_of(j * KB, KB), KB)
        key = key_sc[ks, :]
        tie = key == thr
        tie_b = jnp.where(tie, 1.0, 0.0).astype(BF16)
        rank = tie_sc[...] + _dot(lt, tie_b)
        tie_sc[...] = rank[KB - 1:KB, :]
        sel = (key > thr) | (tie & (rank <= need))
        sel = sel & (j * KB + _iota((KB, 1), 0) <= qpos)
        bias = jnp.where(sel, 0.0, MASK_VALUE)
        return s_t + jnp.concatenate([bias] * N_HEADS, axis=1)

    tie_sc[...] = jnp.zeros_like(tie_sc)
    _attention_t(0, nkb, qk, post, post, vt_ref, bufs, m_sc, l_sc, acc_sc)
    o_ref[0] = _softmax_finish_t(l_sc, acc_sc).astype(o_ref.dtype)


def _dsa(q, k, v_t, q3, k3, w_t, topk):
    bsz, t, _ = q.shape
    qblk = lambda w: pl.BlockSpec((1, ATT_QB, w), lambda b, i: (b, i, 0))
    seq = lambda w: pl.BlockSpec((1, t, w), lambda b, i: (b, 0, 0))
    cols = N_HEADS * ATT_QB
    return pl.pallas_call(
        functools.partial(_dsa_kernel, topk=topk),
        grid=(bsz, t // ATT_QB),
        in_specs=[qblk(WIDTH), seq(WIDTH), pl.BlockSpec((1, WIDTH, t), lambda b, i: (b, 0, 0)),
                  qblk(IDX_HEADS * LANES), seq(LANES),
                  pl.BlockSpec((1, IDX_HEADS, ATT_QB), lambda b, i: (b, 0, i))],
        out_specs=qblk(WIDTH),
        out_shape=jax.ShapeDtypeStruct((bsz, t, WIDTH), BF16),
        scratch_shapes=[pltpu.VMEM((t, ATT_QB), jnp.int32), pltpu.VMEM((t, ATT_QB), BF16),
                        pltpu.VMEM((1, ATT_QB), F32), pltpu.VMEM((1, cols), F32),
                        pltpu.VMEM((1, cols), F32), pltpu.VMEM((N_HEADS, HEAD_DIM, ATT_QB), F32)] + _attention_bufs(),
        compiler_params=_params(("parallel", "arbitrary")), name="dsa_attention",
    )(q, k, v_t, q3, k3, w_t)


def _merge_kernel(x_ref, ya_ref, yb_ref, yc_ref, yd_ref, gpre_ref, wg_ref, wb_ref, wo_ref, g_ref, o_ref):
    d = x_ref.shape[-1]
    x = x_ref[...]
    h = _norm_rows(x, gpre_ref[...]).astype(BF16)
    merged = None
    for n, y_ref in enumerate((ya_ref, yb_ref, yc_ref, yd_ref)):
        gate = _sigmoid(_dot(h, wg_ref[:, n * d:(n + 1) * d]))
        term = gate * _dot(y_ref[...], wb_ref[n])
        merged = term if merged is None else merged + term
    mix = _dot(merged.astype(BF16), wo_ref[...])
    o_ref[...] = x + _norm_rows(mix, g_ref[...])


def _merge(x2, ys, g_pre, w_gate, w_branch, w_out, g_post, tm=256):
    n, d = x2.shape
    row = lambda w: pl.BlockSpec((tm, w), lambda i: (i, 0))
    vec = pl.BlockSpec((1, d), lambda i: (0, 0))
    return pl.pallas_call(
        _merge_kernel, grid=(n // tm,),
        in_specs=[row(d)] + [row(WIDTH)] * 4 + [vec, pl.BlockSpec(w_gate.shape, lambda i: (0, 0)),
                                                 pl.BlockSpec(w_branch.shape, lambda i: (0, 0, 0)),
                                                 pl.BlockSpec(w_out.shape, lambda i: (0, 0)), vec],
        out_specs=row(d),
        out_shape=jax.ShapeDtypeStruct((n, d), F32),
        compiler_params=_params(("parallel",)), name="gated_merge",
    )(x2, *ys, g_pre.reshape(1, d), w_gate.astype(BF16), w_branch.astype(BF16), w_out.astype(BF16),
      g_post.reshape(1, d))


def _mlp_kernel(x_ref, gpre_ref, wu_ref, wd_ref, gpost_ref, o_ref, h_ref, acc_ref):
    kf = pl.program_id(1)

    @pl.when(kf == 0)
    def _():
        h_ref[...] = _norm_rows(x_ref[...], gpre_ref[...]).astype(BF16)
        acc_ref[...] = jnp.zeros_like(acc_ref)

    u = jnp.maximum(_dot(h_ref[...], wu_ref[...]), 0.0)
    acc_ref[...] += _dot((u * u).astype(BF16), wd_ref[...])

    @pl.when(kf == pl.num_programs(1) - 1)
    def _():
        o_ref[...] = x_ref[...] + _norm_rows(acc_ref[...], gpost_ref[...])


def _mlp(x2, g_pre, w_up, w_down, g_post, tm=1024, tf=1024):
    n, d = x2.shape
    dff = w_up.shape[1]
    return pl.pallas_call(
        _mlp_kernel, grid=(n // tm, dff // tf),
        in_specs=[pl.BlockSpec((tm, d), lambda i, k: (i, 0)), pl.BlockSpec((1, d), lambda i, k: (0, 0)),
                  pl.BlockSpec((d, tf), lambda i, k: (0, k)), pl.BlockSpec((tf, d), lambda i, k: (k, 0)),
                  pl.BlockSpec((1, d), lambda i, k: (0, 0))],
        out_specs=pl.BlockSpec((tm, d), lambda i, k: (i, 0)),
        out_shape=jax.ShapeDtypeStruct((n, d), F32),
        scratch_shapes=[pltpu.VMEM((tm, d), BF16), pltpu.VMEM((tm, d), F32)],
        compiler_params=_params(("parallel", "arbitrary")), name="mlp",
    )(x2, g_pre.reshape(1, d), w_up.astype(BF16), w_down.astype(BF16), g_post.reshape(1, d))


def _rope_tables(t, dim, groups):
    inv = 1.0 / (ROPE_THETA ** (jnp.arange(0, dim, 2, dtype=F32) / dim))
    ang = jnp.arange(t, dtype=F32)[:, None] * inv[None, :]
    cos, sin = jnp.cos(ang), jnp.sin(ang)
    return jnp.tile(jnp.concatenate([cos, cos], axis=1), (1, groups)), jnp.tile(jnp.concatenate([-sin, sin], axis=1), (1, groups))


def _pad_rows(w, rows, offset):
    return jnp.zeros((rows, w.shape[1]), w.dtype).at[offset:offset + w.shape[0]].set(w)


def _pad_cols(w, cols):
    return jnp.pad(w, ((0, 0), (0, cols - w.shape[1])))


def kernel(x, norm_mix_pre, norm_mix_post, norm_mlp_pre, norm_mlp_post, w_in, w_branch, w_out, hgrn_lb_logits, hgrn_norm_w, fox_f_bias, rwkv_mu, rwkv_w0, rwkv_w2, rwkv_a0, rwkv_a2, rwkv_g2, rwkv_k_k, rwkv_k_a, rwkv_r_k, rwkv_ln_w, rwkv_ln_b, rwkv_v0, rwkv_v1, rwkv_v2, w_up, w_down):
    bsz, t, d = x.shape
    n = bsz * t
    depth = w_in.shape[0]
    W = WIDTH
    topk = min(TOPK_MAX, t // 4)

    lb_soft = jax.nn.softmax(hgrn_lb_logits.astype(F32), axis=0)
    lower_bounds = jnp.cumsum(lb_soft, axis=0) - lb_soft[0:1]
    tables = _rope_tables(t, HEAD_DIM, N_HEADS) + _rope_tables(t, IDX_DIM, IDX_HEADS)

    o_a, o_b, o_iq = 0, 4 * W, 7 * W
    o_ik, o_iw = o_iq + IDX_HEADS * IDX_DIM, o_iq + IDX_HEADS * IDX_DIM + IDX_DIM
    o_c = o_iw + IDX_HEADS
    c_cols = 3 * W + 128
    o_d = o_c + c_cols
    o_df = o_d + 3 * W
    o_g = o_df + N_HEADS
    main_blocks = dict(b_qkv=4, c=2, df=(7 * W + c_cols) // LANES)
    assert main_blocks["c"] * c_cols == 7 * W

    x2 = x.reshape(n, d)
    v_first = None
    for l in range(depth):
        wl = w_in[l]
        w_main = jnp.concatenate([
            wl[:, o_a:o_a + 7 * W], wl[:, o_c:o_c + c_cols], _pad_cols(wl[:, o_df:o_df + N_HEADS], LANES),
            jnp.zeros((d, 256), F32)], axis=1)
        w_idx = jnp.concatenate([
            wl[:, o_iq:o_iq + W], jnp.tile(wl[:, o_ik:o_ik + IDX_DIM], (1, LANES // IDX_DIM)),
            _pad_cols(wl[:, o_iw:o_iw + IDX_HEADS], LANES)], axis=1)
        w_fox = jnp.concatenate([wl[:, o_d:o_d + W] * (LOG2E * HEAD_DIM ** -0.5), wl[:, o_d + W:o_d + 3 * W]], axis=1)

        main2 = _norm_matmul(x2, norm_mix_pre[l], w_main, F32, tm=1024, tn=1024)
        idx2 = _norm_matmul(x2, norm_mix_pre[l], w_idx, F32, tm=1024, tn=w_idx.shape[1], precise=True)
        fox2 = _norm_matmul(x2, norm_mix_pre[l], w_fox, BF16, tm=1024, tn=3 * W)
        main3 = main2.reshape(bsz, t, -1)

        y_a = _hgrn(main3, lower_bounds[l], hgrn_norm_w[l])

        q_b, k_b, v_b, q3, k3, wi = _dsa_prep(main3, idx2.reshape(bsz, t, -1), tables, main_blocks["b_qkv"])
        y_b = _dsa(q_b, k_b, jnp.transpose(v_b, (0, 2, 1)), q3, k3,
                   jnp.transpose(wi[:, :, :IDX_HEADS], (0, 2, 1)), topk)

        prm = dict(
            mu=_pad_cols(rwkv_mu[l][None, :], RWKV_COLS), w0=rwkv_w0[l][None, :],
            w2=_pad_rows(rwkv_w2[l], RWKV_LORA_LANES, 0), a0=rwkv_a0[l][None, :],
            a2=_pad_rows(rwkv_a2[l], RWKV_LORA_LANES, 32), g2=_pad_rows(rwkv_g2[l], RWKV_LORA_LANES, 64),
            k_k=rwkv_k_k[l][None, :], k_a=rwkv_k_a[l][None, :], r_k=rwkv_r_k[l].reshape(1, W),
            ln_w=rwkv_ln_w[l][None, :], ln_b=rwkv_ln_b[l][None, :])
        if l > 0:
            prm.update(v0=rwkv_v0[l - 1][None, :], v1=_pad_cols(rwkv_v1[l - 1], LANES),
                       v2=_pad_rows(rwkv_v2[l - 1], LANES, 0))
        y_c, v_c = _rwkv(main3, main_blocks["c"], v_first, prm)
        if l == 0:
            v_first = v_c

        c_rep = _fox_cum(main3, main_blocks["df"], _pad_cols(fox_f_bias[l][None, :], LANES))
        fox3 = fox2.reshape(bsz, t, 3 * W)
        y_d = _fox(fox3, jnp.transpose(fox3[:, :, 2 * W:], (0, 2, 1)), c_rep)

        ys = [y.reshape(n, W) for y in (y_a, y_b, y_c, y_d)]
        x2 = _merge(x2, ys, norm_mix_pre[l], wl[:, o_g:o_g + 4 * d], w_branch[l], w_out[l], norm_mix_post[l])
        x2 = _mlp(x2, norm_mlp_pre[l], w_up[l], w_down[l], norm_mlp_post[l])
    return x2.reshape(bsz, t, d)
```

```python
import functools

import jax
import jax.numpy as jnp
from jax import lax
from jax.experimental import pallas as pl
from jax.experimental.pallas import tpu as pltpu

F32 = jnp.float32
BF16 = jnp.bfloat16

HEAD_DIM = 64
N_HEADS = 4
WIDTH = N_HEADS * HEAD_DIM
IDX_HEADS = 8
IDX_DIM = 32
TOPK_MAX = 256
ROPE_THETA = 10000.0
NORM_EPS = 1e-6
RWKV_GN_EPS = 64e-5
MASK_VALUE = -1e30
LOG2E = 1.4426950408889634
RWKV_LORA_LANES = 128
RWKV_COLS = 3 * WIDTH + RWKV_LORA_LANES

LANES = 128
VMEM_LIMIT = 48 * 1024 * 1024

NN = ((1,), (0,))
NT = ((1,), (1,))
TN = ((0,), (0,))


def _dot(a, b, dims=NN):
    return lax.dot_general(a, b, (dims, ((), ())), preferred_element_type=F32)


def _split2(x):
    hi = x.astype(BF16)
    lo = (x - hi.astype(F32)).astype(BF16)
    return hi, lo


def _split3(x):
    hi = x.astype(BF16)
    r1 = x - hi.astype(F32)
    mid = r1.astype(BF16)
    lo = (r1 - mid.astype(F32)).astype(BF16)
    return hi, mid, lo


def _dot3(a, b, dims=NN):
    ah, al = _split2(a)
    bh, bl = _split2(b)
    return _dot(ah, bh, dims) + (_dot(ah, bl, dims) + _dot(al, bh, dims))


def _dot_exact_lhs(a_bf16, b, dims=NN):
    b1, b2, b3 = _split3(b)
    return _dot(a_bf16, b1, dims) + (_dot(a_bf16, b2, dims) + _dot(a_bf16, b3, dims))


def _dot_exact_rhs(a, b_bf16, dims=NN):
    a1, a2 = _split2(a)
    return _dot(a1, b_bf16, dims) + _dot(a2, b_bf16, dims)


def _iota(shape, dim):
    return lax.broadcasted_iota(jnp.int32, shape, dim)


def _head_block_ones(n, group):
    return (_iota((n, n), 0) // group == _iota((n, n), 1) // group).astype(BF16)


def _lower_tri_ones(n):
    return (_iota((n, n), 0) >= _iota((n, n), 1)).astype(BF16)


def _sigmoid(x):
    return 1.0 / (1.0 + jnp.exp(-x))


def _softplus(x):
    return jnp.maximum(x, 0.0) + jnp.log(1.0 + jnp.exp(-jnp.abs(x)))


def _params(sem):
    return pltpu.CompilerParams(dimension_semantics=sem, vmem_limit_bytes=VMEM_LIMIT)


def _norm_rows(x, g):
    return x * lax.rsqrt(jnp.mean(x * x, axis=-1, keepdims=True) + NORM_EPS) * g


def _norm_matmul_kernel(x_ref, g_ref, w_ref, o_ref, h_ref):
    @pl.when(pl.program_id(1) == 0)
    def _():
        h_ref[...] = _norm_rows(x_ref[...], g_ref[...]).astype(BF16)

    o_ref[...] = _dot(h_ref[...], w_ref[...]).astype(o_ref.dtype)


def _norm_matmul3_kernel(x_ref, g_ref, wh_ref, wl_ref, o_ref, hh_ref, hl_ref):
    @pl.when(pl.program_id(1) == 0)
    def _():
        hh, hl = _split2(_norm_rows(x_ref[...], g_ref[...]))
        hh_ref[...] = hh
        hl_ref[...] = hl

    o_ref[...] = (_dot(hh_ref[...], wh_ref[...])
                  + (_dot(hh_ref[...], wl_ref[...]) + _dot(hl_ref[...], wh_ref[...]))).astype(o_ref.dtype)


def _norm_matmul(x2, g, w, out_dtype, tm, tn, precise=False):
    n, d = x2.shape
    cols = w.shape[1]
    grid = (n // tm, cols // tn)
    x_spec = pl.BlockSpec((tm, d), lambda i, j: (i, 0))
    g_spec = pl.BlockSpec((1, d), lambda i, j: (0, 0))
    w_spec = pl.BlockSpec((d, tn), lambda i, j: (0, j))
    o_spec = pl.BlockSpec((tm, tn), lambda i, j: (i, j))
    g2 = g.reshape(1, d)
    if precise:
        wh, wl = _split2(w)
        return pl.pallas_call(
            _norm_matmul3_kernel, grid=grid,
            in_specs=[x_spec, g_spec, w_spec, w_spec], out_specs=o_spec,
            out_shape=jax.ShapeDtypeStruct((n, cols), out_dtype),
            scratch_shapes=[pltpu.VMEM((tm, d), BF16), pltpu.VMEM((tm, d), BF16)],
            compiler_params=_params(("parallel", "arbitrary")), name="norm_proj_precise",
        )(x2, g2, wh, wl)
    return pl.pallas_call(
        _norm_matmul_kernel, grid=grid,
        in_specs=[x_spec, g_spec, w_spec], out_specs=o_spec,
        out_shape=jax.ShapeDtypeStruct((n, cols), out_dtype),
        scratch_shapes=[pltpu.VMEM((tm, d), BF16)],
        compiler_params=_params(("parallel", "arbitrary")), name="norm_proj",
    )(x2, g2, w.astype(BF16))


HGRN_CHUNK = 64
HGRN_GROUP = 8


def _dot1(a, b, dims=NN):
    return _dot(a.astype(BF16), b.astype(BF16), dims)


def _hgrn_kernel(q_ref, f_ref, i_ref, g_ref, lb_ref, nw_ref, o_ref, st_ref, *, n_chunks):
    C, G = HGRN_CHUNK, HGRN_GROUP

    @pl.when(pl.program_id(1) == 0)
    def _():
        st_ref[...] = jnp.zeros_like(st_ref)

    lb = lb_ref[...]
    nw = nw_ref[...]
    tri = _lower_tri_ones(C)
    bd = _head_block_ones(WIDTH, HEAD_DIM)
    bd_mask = _iota((WIDTH, WIDTH), 0) // HEAD_DIM == _iota((WIDTH, WIDTH), 1) // HEAD_DIM
    row_top = _iota((G, 1), 0)
    lane_head = _iota((1, WIDTH), 1) // HEAD_DIM

    def chunk(c, carry):
        sl = pl.ds(pl.multiple_of(c * C, C), C)
        fl = f_ref[0, sl, :]
        f = lb + (1.0 - lb) * _sigmoid(fl)
        k = (1.0 - lb) * _sigmoid(-fl)
        b = _dot_exact_lhs(tri, jnp.log(f))
        q = q_ref[0, sl, :] * (HEAD_DIM ** -0.5)
        v = i_ref[0, sl, :]
        st = st_ref[...]
        o_inter = _dot1(q * jnp.exp(b), st, NT)

        n_groups = C // G
        o_rows = [o_inter[g * G:(g + 1) * G] for g in range(n_groups)]
        pieces = []
        for g in range(n_groups):
            s0 = g * G
            q_top, b_top, k_top = q[s0:s0 + G], b[s0:s0 + G], k[s0:s0 + G]
            for u in range(G):
                top = q_top * k_top[u:u + 1] * jnp.exp(jnp.minimum(b_top - b_top[u:u + 1], 0.0))
                pieces.append(jnp.where(row_top >= u, top, 0.0))
        r = _dot(jnp.concatenate(pieces, axis=0).astype(BF16), bd)
        att, v_st = [], []
        for g in range(n_groups - 1):
            s0 = g * G
            b_top, k_top, v_top = b[s0:s0 + G], k[s0:s0 + G], v[s0:s0 + G]
            b_ref = b_top[G - 1:G]
            qp = (q[s0 + G:] * jnp.exp(b[s0 + G:] - b_ref)).astype(BF16)
            kp = k_top * jnp.exp(b_ref - b_top)
            k_st = jnp.concatenate([jnp.where(lane_head == h, kp, 0.0) for h in range(N_HEADS)], axis=0)
            v_st.append(jnp.concatenate([jnp.where(lane_head == h, v_top, 0.0) for h in range(N_HEADS)], axis=0))
            att.append(_dot(qp, k_st.astype(BF16), NT))
        for g in range(n_groups):
            v_top = v[g * G:(g + 1) * G]
            for u in range(G):
                o_rows[g] = o_rows[g] + r[(g * G + u) * G:(g * G + u + 1) * G] * v_top[u:u + 1]
        for g in range(n_groups - 1):
            rest = _dot(att[g].astype(BF16), v_st[g].astype(BF16))
            for g2 in range(g + 1, n_groups):
                o_rows[g2] = o_rows[g2] + rest[(g2 - g - 1) * G:(g2 - g) * G]
        o = jnp.concatenate(o_rows, axis=0)

        b_last = b[C - 1:C, :]
        upd = _dot1(v, k * jnp.exp(b_last - b), TN)
        st_ref[...] = st * jnp.exp(b_last) + jnp.where(bd_mask, upd, 0.0)

        ms = _dot_exact_rhs(o * o, bd) * (1.0 / HEAD_DIM)
        on = o * lax.rsqrt(ms + NORM_EPS) * nw
        gl = g_ref[0, sl, :]
        o_ref[0, sl, :] = (on * (gl * _sigmoid(gl))).astype(o_ref.dtype)
        return carry

    lax.fori_loop(0, n_chunks, chunk, 0)


def _hgrn(main3, lb, norm_w, tb=256):
    bsz, t, _ = main3.shape
    col = lambda j: pl.BlockSpec((1, tb, WIDTH), lambda b, i, j=j: (b, i, j))
    vec = pl.BlockSpec((1, WIDTH), lambda b, i: (0, 0))
    return pl.pallas_call(
        functools.partial(_hgrn_kernel, n_chunks=tb // HGRN_CHUNK),
        grid=(bsz, t // tb),
        in_specs=[col(0), col(1), col(2), col(3), vec, vec],
        out_specs=pl.BlockSpec((1, tb, WIDTH), lambda b, i: (b, i, 0)),
        out_shape=jax.ShapeDtypeStruct((bsz, t, WIDTH), BF16),
        scratch_shapes=[pltpu.VMEM((WIDTH, WIDTH), F32)],
        compiler_params=_params(("parallel", "arbitrary")), name="hgrn2",
    )(main3, main3, main3, main3, lb.reshape(1, WIDTH), jnp.tile(norm_w, N_HEADS).reshape(1, WIDTH))


RWKV_CHUNK = 64


def _rwkv_kernel(*refs, n_chunks, has_vgate):
    if has_vgate:
        (c_ref, vf_ref, mu_ref, w0_ref, w2_ref, a0_ref, a2_ref, g2_ref, kkw_ref, kaw_ref, rk_ref,
         lnw_ref, lnb_ref, v0_ref, v1_ref, v2_ref, y_ref, vout_ref, s_ref, prev_ref, y_sc) = refs
    else:
        (c_ref, mu_ref, w0_ref, w2_ref, a0_ref, a2_ref, g2_ref, kkw_ref, kaw_ref, rk_ref,
         lnw_ref, lnb_ref, y_ref, vout_ref, s_ref, prev_ref, y_sc) = refs
    C = RWKV_CHUNK
    W = WIDTH
    tb = n_chunks * C

    @pl.when(pl.program_id(1) == 0)
    def _():
        s_ref[...] = jnp.zeros_like(s_ref)
        prev_ref[...] = jnp.zeros_like(prev_ref)

    p = c_ref[0]
    shifted = jnp.where(_iota((tb, 1), 0) == 0, prev_ref[...], pltpu.roll(p, 1, axis=0))
    prev_ref[...] = p[tb - 1:tb, :]
    xm = p + (shifted - p) * mu_ref[...]
    r = xm[:, 0:W]
    k = xm[:, W:2 * W]
    v = xm[:, 2 * W:3 * W]
    lora = xm[:, 3 * W:3 * W + RWKV_LORA_LANES]

    bd = _head_block_ones(W, HEAD_DIM)
    w_log = -_softplus(-(w0_ref[...] + _dot3(jnp.tanh(lora), w2_ref[...]))) - 0.5
    log_decay = -jnp.exp(w_log)
    a = _sigmoid(a0_ref[...] + _dot3(lora, a2_ref[...]))
    g = _dot3(_sigmoid(lora), g2_ref[...])
    if has_vgate:
        vg = _dot3(_dot3(v, v1_ref[...]), v2_ref[...])
        v = v + (vf_ref[0] - v) * _sigmoid(v0_ref[...] + vg)
    vout_ref[0] = v
    kk = k * kkw_ref[...]
    kk = kk * lax.rsqrt(jnp.maximum(_dot_exact_rhs(kk * kk, bd), 1e-24))
    k = k * (1.0 + (a - 1.0) * kaw_ref[...])

    tri = _lower_tri_ones(C)
    lane_head = _iota((1, W), 1) // HEAD_DIM
    hc = N_HEADS * C
    rt = _iota((hc, hc), 0) % C
    ct = _iota((hc, hc), 1) % C
    strict_lower = rt > ct
    lower = rt >= ct
    eye = (_iota((hc, hc), 0) == _iota((hc, hc), 1)).astype(F32)

    def stack(m):
        return jnp.concatenate([jnp.where(lane_head == h, m, 0.0) for h in range(N_HEADS)], axis=0)

    chunks = range(n_chunks)
    a_s, b_s, k_s, r_s, v_s, p_last, m, lak, lrb, lrk = ([None] * n_chunks for _ in range(10))
    for c in chunks:
        sl = slice(c * C, (c + 1) * C)
        ld = log_decay[sl]
        cum = _dot_exact_lhs(tri, ld)
        pdec = jnp.exp(cum)
        pinv = jnp.exp(-cum)
        p_last[c] = pdec[C - 1:C, :]
        a_s[c] = stack(-kk[sl] * jnp.exp(cum - ld)).astype(BF16)
        b_s[c] = stack(kk[sl] * a[sl] * pinv).astype(BF16)
        k_s[c] = stack(k[sl] * pinv).astype(BF16)
        r_s[c] = stack(r[sl] * pdec)
        v_s[c] = stack(v[sl]).astype(BF16)
        gram = _dot(jnp.concatenate([a_s[c], r_s[c].astype(BF16)], axis=0),
                    jnp.concatenate([b_s[c], k_s[c]], axis=0), NT)
        m[c] = jnp.where(strict_lower, gram[:hc, :hc], 0.0)
        lak[c] = jnp.where(strict_lower, gram[:hc, hc:], 0.0)
        lrb[c] = jnp.where(lower, gram[hc:, :hc], 0.0)
        lrk[c] = jnp.where(lower, gram[hc:, hc:], 0.0)

    x = [eye + m[c] for c in chunks]
    power = 2
    while power < C:
        m = [_dot1(m[c], m[c]) for c in chunks]
        x = [x[c] + _dot1(x[c], m[c]) for c in chunks]
        power *= 2
    lakv = [_dot1(lak[c], v_s[c]) for c in chunks]
    lrkv = [_dot1(lrk[c], v_s[c]) for c in chunks]
    w1 = [_dot1(x[c], lakv[c]) for c in chunks]
    a2 = [_dot1(x[c], a_s[c]) for c in chunks]
    r2 = [r_s[c] + _dot1(lrb[c], a2[c]) for c in chunks]
    y0 = [_dot1(lrb[c], w1[c]) + lrkv[c] for c in chunks]
    t1 = [_dot1(a2[c], b_s[c], TN) for c in chunks]
    t0 = [_dot1(w1[c], b_s[c], TN) + _dot(v_s[c], k_s[c], TN) for c in chunks]

    s = s_ref[...]
    for c in chunks:
        y = _dot1(r2[c], s, NT) + y0[c]
        y_sc[c * C:(c + 1) * C, :] = sum(y[h * C:(h + 1) * C] for h in range(N_HEADS))
        s = (s + _dot1(s, t1[c]) + t0[c]) * p_last[c]
    s_ref[...] = s

    y = y_sc[...]
    inv_n = 1.0 / HEAD_DIM
    mean = _dot_exact_rhs(y, bd) * inv_n
    yc = y - mean
    var = _dot_exact_rhs(yc * yc, bd) * inv_n
    yn = yc * lax.rsqrt(var + RWKV_GN_EPS) * lnw_ref[...] + lnb_ref[...]
    yn = yn + _dot_exact_rhs(r * k * rk_ref[...], bd) * v
    y_ref[0] = (yn * g).astype(y_ref.dtype)


def _rwkv(c3, col_block, v_first, prm, tb=256):
    bsz, t, _ = c3.shape
    has_vgate = v_first is not None
    blk = lambda w: pl.BlockSpec((1, tb, w), lambda b, i: (b, i, 0))
    cblk = pl.BlockSpec((1, tb, RWKV_COLS), lambda b, i: (b, i, col_block))
    full = lambda a: pl.BlockSpec(a.shape, lambda b, i: (0,) * a.ndim)
    names = ["mu", "w0", "w2", "a0", "a2", "g2", "k_k", "k_a", "r_k", "ln_w", "ln_b"]
    if has_vgate:
        names += ["v0", "v1", "v2"]
    args = [c3] + ([v_first] if has_vgate else []) + [prm[n] for n in names]
    in_specs = [cblk] + ([blk(WIDTH)] if has_vgate else []) + [full(prm[n]) for n in names]
    return pl.pallas_call(
        functools.partial(_rwkv_kernel, n_chunks=tb // RWKV_CHUNK, has_vgate=has_vgate),
        grid=(bsz, t // tb),
        in_specs=in_specs,
        out_specs=[blk(WIDTH), blk(WIDTH)],
        out_shape=[jax.ShapeDtypeStruct((bsz, t, WIDTH), BF16), jax.ShapeDtypeStruct((bsz, t, WIDTH), F32)],
        scratch_shapes=[pltpu.VMEM((WIDTH, WIDTH), F32), pltpu.VMEM((1, RWKV_COLS), F32),
                        pltpu.VMEM((tb, WIDTH), F32)],
        compiler_params=_params(("parallel", "arbitrary")), name="rwkv7",
    )(*args)


def _fox_cum_kernel(f_ref, bias_ref, c_ref, carry_ref, *, tb):
    @pl.when(pl.program_id(1) == 0)
    def _():
        carry_ref[...] = jnp.zeros_like(carry_ref)

    logf = -_softplus(-(f_ref[0] + bias_ref[...]))
    c = _dot_exact_lhs(_lower_tri_ones(tb), logf) + carry_ref[...]
    carry_ref[...] = c[tb - 1:tb, :]
    for h in range(N_HEADS):
        pick = (_iota((LANES, LANES), 0) == h).astype(BF16)
        c1, c2, c3 = _split3(c)
        c_ref[0, h] = LOG2E * (_dot(c1, pick) + (_dot(c2, pick) + _dot(c3, pick)))


def _fox_cum(main3, col_block, bias_row, tb=256):
    bsz, t, _ = main3.shape
    return pl.pallas_call(
        functools.partial(_fox_cum_kernel, tb=tb),
        grid=(bsz, t // tb),
        in_specs=[pl.BlockSpec((1, tb, LANES), lambda b, i: (b, i, col_block)),
                  pl.BlockSpec((1, LANES), lambda b, i: (0, 0))],
        out_specs=pl.BlockSpec((1, N_HEADS, tb, LANES), lambda b, i: (b, 0, i, 0)),
        out_shape=jax.ShapeDtypeStruct((bsz, N_HEADS, t, LANES), F32),
        scratch_shapes=[pltpu.VMEM((1, LANES), F32)],
        compiler_params=_params(("parallel", "arbitrary")), name="fox_cumgate",
    )(main3, bias_row)


ATT_QB = 256
ATT_KB = 512


def _stack_heads(q):
    lane_head = _iota((1, WIDTH), 1) // HEAD_DIM
    zero = jnp.zeros_like(q)
    return jnp.concatenate([jnp.where(lane_head == h, q, zero) for h in range(N_HEADS)], axis=0)


def _softmax_stage_t(s_t, m_sc, l_sc):
    m_old = m_sc[...]
    m_new = jnp.maximum(m_old, jnp.max(s_t, axis=0, keepdims=True))
    alpha = jnp.exp2(m_old - m_new)
    p_t = jnp.exp2(s_t - m_new)
    l_sc[...] = alpha * l_sc[...] + jnp.sum(p_t, axis=0, keepdims=True)
    m_sc[...] = m_new
    return p_t.astype(BF16), alpha


def _pv_stage_t(vt_blk, p_t, alpha, acc_sc):
    for h in range(N_HEADS):
        cs = slice(h * ATT_QB, (h + 1) * ATT_QB)
        pv = _dot(vt_blk[h * HEAD_DIM:(h + 1) * HEAD_DIM, :], p_t[:, cs])
        acc_sc[h] = alpha[:, cs] * acc_sc[h] + pv


def _attention_t(n_blocks, qk, post, post_last, vt_ref, bufs, m_sc, l_sc, acc_sc):
    KB = ATT_KB
    s_buf, p_buf, a_buf = bufs[0:2], bufs[2:4], bufs[4:6]
    _softmax_init_t(m_sc, l_sc, acc_sc)

    def vt_blk(j):
        return vt_ref[0, :, pl.ds(pl.multiple_of(j * KB, KB), KB)]

    def pending(j, par):
        _pv_stage_t(vt_blk(jnp.maximum(j - 1, 0)), p_buf[1 - par][...], a_buf[1 - par][...], acc_sc)

    def step(j, par):
        raw = s_buf[par][...]
        s_buf[1 - par][...] = qk(j + 1)
        p_t, alpha = _softmax_stage_t(post(j, raw), m_sc, l_sc)
        pending(j, par)
        p_buf[par][...] = p_t
        a_buf[par][...] = alpha

    def final(j, par):
        p_t, alpha = _softmax_stage_t(post_last(j, s_buf[par][...]), m_sc, l_sc)
        pending(j, par)
        _pv_stage_t(vt_blk(j), p_t, alpha, acc_sc)

    s_buf[0][...] = qk(0)
    p_buf[1][...] = jnp.zeros_like(p_buf[1])
    a_buf[1][...] = jnp.ones_like(a_buf[1])
    pairs = (n_blocks - 1) // 2

    def pair(t, carry):
        step(2 * t, 0)
        step(2 * t + 1, 1)
        return carry

    lax.fori_loop(0, pairs, pair, 0)
    odd_left = (n_blocks - 1) % 2 == 1

    @pl.when(odd_left)
    def _():
        step(2 * pairs, 0)
        final(2 * pairs + 1, 1)

    @pl.when(jnp.logical_not(odd_left))
    def _():
        final(2 * pairs, 0)


def _attention_bufs():
    cols = N_HEADS * ATT_QB
    return ([pltpu.VMEM((ATT_KB, cols), F32)] * 2 + [pltpu.VMEM((ATT_KB, cols), BF16)] * 2
            + [pltpu.VMEM((1, cols), F32)] * 2)


def _softmax_init_t(m_sc, l_sc, acc_sc):
    m_sc[...] = jnp.full_like(m_sc, MASK_VALUE)
    l_sc[...] = jnp.zeros_like(l_sc)
    acc_sc[...] = jnp.zeros_like(acc_sc)


def _softmax_finish_t(l_sc, acc_sc):
    l = l_sc[...]
    out_t = jnp.concatenate([acc_sc[h] / l[:, h * ATT_QB:(h + 1) * ATT_QB] for h in range(N_HEADS)], axis=0)
    return out_t.T


def _fox_kernel(q_ref, k_ref, vt_ref, ck_ref, o_ref, m_sc, l_sc, acc_sc, *bufs):
    QB, KB = ATT_QB, ATT_KB
    i = pl.program_id(1)
    qstack = _stack_heads(q_ref[0])
    qpos = i * QB + _iota((1, N_HEADS * QB), 1) % QB

    def qk(j):
        return _dot(k_ref[0, pl.ds(pl.multiple_of(j * KB, KB), KB), :], qstack, NT)

    def post(j, s_t):
        ks = pl.ds(pl.multiple_of(j * KB, KB), KB)
        rep = lambda c: jnp.concatenate([c] * (QB // LANES), axis=1)
        return jnp.concatenate([s_t[:, h * QB:(h + 1) * QB] - rep(ck_ref[0, h, ks, :]) for h in range(N_HEADS)], axis=1)

    def post_last(j, s_t):
        return jnp.where(j * KB + _iota((KB, 1), 0) <= qpos, post(j, s_t), MASK_VALUE)

    _attention_t((i * QB) // KB + 1, qk, post, post_last, vt_ref, bufs, m_sc, l_sc, acc_sc)
    o_ref[0] = _softmax_finish_t(l_sc, acc_sc).astype(o_ref.dtype)


def _fox(qkv3, v_t, c_rep):
    bsz, t, _ = qkv3.shape
    cols = N_HEADS * ATT_QB
    return pl.pallas_call(
        _fox_kernel,
        grid=(bsz, t // ATT_QB),
        in_specs=[pl.BlockSpec((1, ATT_QB, WIDTH), lambda b, i: (b, i, 0)),
                  pl.BlockSpec((1, t, WIDTH), lambda b, i: (b, 0, 1)),
                  pl.BlockSpec((1, WIDTH, t), lambda b, i: (b, 0, 0)),
                  pl.BlockSpec((1, N_HEADS, t, LANES), lambda b, i: (b, 0, 0, 0))],
        out_specs=pl.BlockSpec((1, ATT_QB, WIDTH), lambda b, i: (b, i, 0)),
        out_shape=jax.ShapeDtypeStruct((bsz, t, WIDTH), BF16),
        scratch_shapes=[pltpu.VMEM((1, cols), F32), pltpu.VMEM((1, cols), F32),
                        pltpu.VMEM((N_HEADS, HEAD_DIM, ATT_QB), F32)] + _attention_bufs(),
        compiler_params=_params(("parallel", "arbitrary")), name="fox_attention",
    )(qkv3, qkv3, v_t, c_rep)


def _swap_halves(x, half):
    n = x.shape[-1]
    lower = (_iota((1, n), 1) % (2 * half)) < half
    return jnp.where(lower, pltpu.roll(x, n - half, axis=1), pltpu.roll(x, half, axis=1))


def _dsa_prep_kernel(q_ref, k_ref, v_ref, iq_ref, ik_ref, iw_ref, ch_ref, sh_ref, ci_ref, si_ref,
                     selh_ref, sell_ref, qo_ref, ko_ref, vo_ref, q3_ref, k3_ref, wi_ref):
    ch, sh, ci, si = ch_ref[...], sh_ref[...], ci_ref[...], si_ref[...]

    def rope(x, c, s, half):
        return x * c + _swap_halves(x, half) * s

    qo_ref[0] = (rope(q_ref[0], ch, sh, HEAD_DIM // 2) * (LOG2E * HEAD_DIM ** -0.5)).astype(BF16)
    ko_ref[0] = rope(k_ref[0], ch, sh, HEAD_DIM // 2).astype(BF16)
    vo_ref[0] = v_ref[0].astype(BF16)
    qh, ql = _split2(rope(iq_ref[0], ci, si, IDX_DIM // 2) * (IDX_DIM ** -0.5))
    q3_ref[0] = (_dot(qh, selh_ref[...]) + _dot(ql, sell_ref[...])).astype(BF16)
    kh, kl = _split2(rope(ik_ref[0], ci[:, :LANES], si[:, :LANES], IDX_DIM // 2))
    seg = _iota((1, LANES), 1) // IDX_DIM
    k3_ref[0] = jnp.where(seg < 2, kh, jnp.where(seg == 2, kl, jnp.zeros_like(kl)))
    wi_ref[0] = iw_ref[0] * (IDX_HEADS ** -0.5)


def _dsa_prep(main3, idx3, tables, qkv_block0, tb=512):
    bsz, t, _ = main3.shape
    iq_w = IDX_HEADS * IDX_DIM
    r = jnp.arange(iq_w)[:, None]
    c = jnp.arange(IDX_HEADS * LANES)[None, :]
    same = (r // IDX_DIM == c // LANES) & (r % IDX_DIM == c % IDX_DIM)
    seg = (c % LANES) // IDX_DIM
    sel_hi = (same & ((seg == 0) | (seg == 2))).astype(BF16)
    sel_lo = (same & (seg == 1)).astype(BF16)
    mcol = lambda j: pl.BlockSpec((1, tb, WIDTH), lambda b, i, j=j: (b, i, qkv_block0 + j))
    tab = pl.BlockSpec((tb, WIDTH), lambda b, i: (i, 0))
    sel = pl.BlockSpec(sel_hi.shape, lambda b, i: (0, 0))
    out = lambda w: pl.BlockSpec((1, tb, w), lambda b, i: (b, i, 0))
    shape = lambda dt, w: jax.ShapeDtypeStruct((bsz, t, w), dt)
    return pl.pallas_call(
        _dsa_prep_kernel, grid=(bsz, t // tb),
        in_specs=[mcol(0), mcol(1), mcol(2), out(iq_w),
                  pl.BlockSpec((1, tb, LANES), lambda b, i: (b, i, iq_w // LANES)),
                  pl.BlockSpec((1, tb, LANES), lambda b, i: (b, i, iq_w // LANES + 1)),
                  tab, tab, tab, tab, sel, sel],
        out_specs=[out(WIDTH)] * 3 + [out(IDX_HEADS * LANES), out(LANES), out(LANES)],
        out_shape=[shape(BF16, WIDTH)] * 3 + [shape(BF16, IDX_HEADS * LANES), shape(BF16, LANES), shape(F32, LANES)],
        compiler_params=_params(("parallel", "parallel")), name="dsa_prep",
    )(main3, main3, main3, idx3, idx3, idx3, *tables, sel_hi, sel_lo)


def _dsa_kernel(q_ref, k_ref, vt_ref, q3_ref, k3_ref, wt_ref, o_ref, key_sc, tie_sc, m_sc, l_sc, acc_sc,
                *bufs, topk):
    QB, KB = ATT_QB, ATT_KB
    i = pl.program_id(1)
    nkb = (i * QB) // KB + 1
    qpos = i * QB + _iota((1, QB), 1)
    int_min = jnp.int32(-2 ** 31)

    q3 = q3_ref[0]
    qs = jnp.concatenate([q3[:, h * LANES:(h + 1) * LANES] for h in range(IDX_HEADS)], axis=0)
    wt = wt_ref[0]

    def score_block(j, carry):
        ks = pl.ds(pl.multiple_of(j * KB, KB), KB)
        r = _dot(k3_ref[0, ks, :], qs, NT)
        sc = jnp.zeros((KB, QB), F32)
        for h in range(IDX_HEADS):
            sc = sc + jnp.maximum(r[:, h * QB:(h + 1) * QB], 0.0) * wt[h:h + 1, :]
        sc = jnp.where(j * KB + _iota((KB, 1), 0) <= qpos, sc, MASK_VALUE)
        sc = jnp.where(sc == 0.0, 0.0, sc)
        bits = pltpu.bitcast(sc, jnp.int32)
        key_sc[ks, :] = jnp.where(bits < 0, bits ^ jnp.int32(0x7FFFFFFF), bits)
        return carry

    lax.fori_loop(0, nkb, score_block, 0)

    def count_ge(cand):
        def blk(j, acc):
            base = pl.multiple_of(j * KB, KB)
            for r0 in range(0, KB, 64):
                acc = acc + jnp.where(key_sc[pl.ds(base + r0, 64), :] >= cand, 1.0, 0.0)
            return acc
        acc = lax.fori_loop(0, nkb, blk, jnp.zeros((64, QB), F32))
        return jnp.sum(acc, axis=0, keepdims=True)

    kf = float(topk)

    def bit_step(step, v):
        trial = jnp.where(step == 0, jnp.zeros_like(v), v | (jnp.int32(1) << (31 - step)))
        return jnp.where(count_ge(trial) >= kf, trial, v)

    thr = lax.fori_loop(0, 32, bit_step, jnp.full((1, QB), int_min, jnp.int32))
    need = kf - count_ge(thr + 1)

    lt = _lower_tri_ones(KB)
    qstack = _stack_heads(q_ref[0])

    def qk(j):
        return _dot(k_ref[0, pl.ds(pl.multiple_of(j * KB, KB), KB), :], qstack, NT)

    def post(j, s_t):
        ks = pl.ds(pl.multiple_of(j * KB, KB), KB)
        key = key_sc[ks, :]
        tie = key == thr
        tie_b = jnp.where(tie, 1.0, 0.0).astype(BF16)
        rank = tie_sc[...] + _dot(lt, tie_b)
        tie_sc[...] = rank[KB - 1:KB, :]
        sel = (key > thr) | (tie & (rank <= need))
        sel = sel & (j * KB + _iota((KB, 1), 0) <= qpos)
        bias = jnp.where(sel, 0.0, MASK_VALUE)
        return s_t + jnp.concatenate([bias] * N_HEADS, axis=1)

    tie_sc[...] = jnp.zeros_like(tie_sc)
    _attention_t(nkb, qk, post, post, vt_ref, bufs, m_sc, l_sc, acc_sc)
    o_ref[0] = _softmax_finish_t(l_sc, acc_sc).astype(o_ref.dtype)


def _dsa(q, k, v_t, q3, k3, w_t, topk):
    bsz, t, _ = q.shape
    qblk = lambda w: pl.BlockSpec((1, ATT_QB, w), lambda b, i: (b, i, 0))
    seq = lambda w: pl.BlockSpec((1, t, w), lambda b, i: (b, 0, 0))
    cols = N_HEADS * ATT_QB
    return pl.pallas_call(
        functools.partial(_dsa_kernel, topk=topk),
        grid=(bsz, t // ATT_QB),
        in_specs=[qblk(WIDTH), seq(WIDTH), pl.BlockSpec((1, WIDTH, t), lambda b, i: (b, 0, 0)),
                  qblk(IDX_HEADS * LANES), seq(LANES),
                  pl.BlockSpec((1, IDX_HEADS, ATT_QB), lambda b, i: (b, 0, i))],
        out_specs=qblk(WIDTH),
        out_shape=jax.ShapeDtypeStruct((bsz, t, WIDTH), BF16),
        scratch_shapes=[pltpu.VMEM((t, ATT_QB), jnp.int32), pltpu.VMEM((1, ATT_QB), F32), pltpu.VMEM((1, cols), F32),
                        pltpu.VMEM((1, cols), F32), pltpu.VMEM((N_HEADS, HEAD_DIM, ATT_QB), F32)] + _attention_bufs(),
        compiler_params=_params(("parallel", "arbitrary")), name="dsa_attention",
    )(q, k, v_t, q3, k3, w_t)


def _merge_kernel(x_ref, ya_ref, yb_ref, yc_ref, yd_ref, gpre_ref, wg_ref, wb_ref, wo_ref, g_ref, o_ref):
    d = x_ref.shape[-1]
    x = x_ref[...]
    h = _norm_rows(x, gpre_ref[...]).astype(BF16)
    merged = None
    for n, y_ref in enumerate((ya_ref, yb_ref, yc_ref, yd_ref)):
        gate = _sigmoid(_dot(h, wg_ref[:, n * d:(n + 1) * d]))
        term = gate * _dot(y_ref[...], wb_ref[n])
        merged = term if merged is None else merged + term
    mix = _dot(merged.astype(BF16), wo_ref[...])
    o_ref[...] = x + _norm_rows(mix, g_ref[...])


def _merge(x2, ys, g_pre, w_gate, w_branch, w_out, g_post, tm=256):
    n, d = x2.shape
    row = lambda w: pl.BlockSpec((tm, w), lambda i: (i, 0))
    vec = pl.BlockSpec((1, d), lambda i: (0, 0))
    return pl.pallas_call(
        _merge_kernel, grid=(n // tm,),
        in_specs=[row(d)] + [row(WIDTH)] * 4 + [vec, pl.BlockSpec(w_gate.shape, lambda i: (0, 0)),
                                                 pl.BlockSpec(w_branch.shape, lambda i: (0, 0, 0)),
                                                 pl.BlockSpec(w_out.shape, lambda i: (0, 0)), vec],
        out_specs=row(d),
        out_shape=jax.ShapeDtypeStruct((n, d), F32),
        compiler_params=_params(("parallel",)), name="gated_merge",
    )(x2, *ys, g_pre.reshape(1, d), w_gate.astype(BF16), w_branch.astype(BF16), w_out.astype(BF16),
      g_post.reshape(1, d))


def _mlp_kernel(x_ref, gpre_ref, wu_ref, wd_ref, gpost_ref, o_ref, h_ref, acc_ref):
    kf = pl.program_id(1)

    @pl.when(kf == 0)
    def _():
        h_ref[...] = _norm_rows(x_ref[...], gpre_ref[...]).astype(BF16)
        acc_ref[...] = jnp.zeros_like(acc_ref)

    u = jnp.maximum(_dot(h_ref[...], wu_ref[...]), 0.0)
    acc_ref[...] += _dot((u * u).astype(BF16), wd_ref[...])

    @pl.when(kf == pl.num_programs(1) - 1)
    def _():
        o_ref[...] = x_ref[...] + _norm_rows(acc_ref[...], gpost_ref[...])


def _mlp(x2, g_pre, w_up, w_down, g_post, tm=1024, tf=1024):
    n, d = x2.shape
    dff = w_up.shape[1]
    return pl.pallas_call(
        _mlp_kernel, grid=(n // tm, dff // tf),
        in_specs=[pl.BlockSpec((tm, d), lambda i, k: (i, 0)), pl.BlockSpec((1, d), lambda i, k: (0, 0)),
                  pl.BlockSpec((d, tf), lambda i, k: (0, k)), pl.BlockSpec((tf, d), lambda i, k: (k, 0)),
                  pl.BlockSpec((1, d), lambda i, k: (0, 0))],
        out_specs=pl.BlockSpec((tm, d), lambda i, k: (i, 0)),
        out_shape=jax.ShapeDtypeStruct((n, d), F32),
        scratch_shapes=[pltpu.VMEM((tm, d), BF16), pltpu.VMEM((tm, d), F32)],
        compiler_params=_params(("parallel", "arbitrary")), name="mlp",
    )(x2, g_pre.reshape(1, d), w_up.astype(BF16), w_down.astype(BF16), g_post.reshape(1, d))


def _rope_tables(t, dim, groups):
    inv = 1.0 / (ROPE_THETA ** (jnp.arange(0, dim, 2, dtype=F32) / dim))
    ang = jnp.arange(t, dtype=F32)[:, None] * inv[None, :]
    cos, sin = jnp.cos(ang), jnp.sin(ang)
    return jnp.tile(jnp.concatenate([cos, cos], axis=1), (1, groups)), jnp.tile(jnp.concatenate([-sin, sin], axis=1), (1, groups))


def _pad_rows(w, rows, offset):
    return jnp.zeros((rows, w.shape[1]), w.dtype).at[offset:offset + w.shape[0]].set(w)


def _pad_cols(w, cols):
    return jnp.pad(w, ((0, 0), (0, cols - w.shape[1])))


def kernel(x, norm_mix_pre, norm_mix_post, norm_mlp_pre, norm_mlp_post, w_in, w_branch, w_out, hgrn_lb_logits, hgrn_norm_w, fox_f_bias, rwkv_mu, rwkv_w0, rwkv_w2, rwkv_a0, rwkv_a2, rwkv_g2, rwkv_k_k, rwkv_k_a, rwkv_r_k, rwkv_ln_w, rwkv_ln_b, rwkv_v0, rwkv_v1, rwkv_v2, w_up, w_down):
    bsz, t, d = x.shape
    n = bsz * t
    depth = w_in.shape[0]
    W = WIDTH
    topk = min(TOPK_MAX, t // 4)

    lb_soft = jax.nn.softmax(hgrn_lb_logits.astype(F32), axis=0)
    lower_bounds = jnp.cumsum(lb_soft, axis=0) - lb_soft[0:1]
    tables = _rope_tables(t, HEAD_DIM, N_HEADS) + _rope_tables(t, IDX_DIM, IDX_HEADS)

    o_a, o_b, o_iq = 0, 4 * W, 7 * W
    o_ik, o_iw = o_iq + IDX_HEADS * IDX_DIM, o_iq + IDX_HEADS * IDX_DIM + IDX_DIM
    o_c = o_iw + IDX_HEADS
    c_cols = 3 * W + 128
    o_d = o_c + c_cols
    o_df = o_d + 3 * W
    o_g = o_df + N_HEADS
    main_blocks = dict(b_qkv=4, c=2, df=(7 * W + c_cols) // LANES)
    assert main_blocks["c"] * c_cols == 7 * W

    x2 = x.reshape(n, d)
    v_first = None
    for l in range(depth):
        wl = w_in[l]
        w_main = jnp.concatenate([
            wl[:, o_a:o_a + 7 * W], wl[:, o_c:o_c + c_cols], _pad_cols(wl[:, o_df:o_df + N_HEADS], LANES),
            jnp.zeros((d, 256), F32)], axis=1)
        w_idx = jnp.concatenate([
            wl[:, o_iq:o_iq + W], jnp.tile(wl[:, o_ik:o_ik + IDX_DIM], (1, LANES // IDX_DIM)),
            _pad_cols(wl[:, o_iw:o_iw + IDX_HEADS], LANES)], axis=1)
        w_fox = jnp.concatenate([wl[:, o_d:o_d + W] * (LOG2E * HEAD_DIM ** -0.5), wl[:, o_d + W:o_d + 3 * W]], axis=1)

        main2 = _norm_matmul(x2, norm_mix_pre[l], w_main, F32, tm=1024, tn=1024)
        idx2 = _norm_matmul(x2, norm_mix_pre[l], w_idx, F32, tm=1024, tn=w_idx.shape[1], precise=True)
        fox2 = _norm_matmul(x2, norm_mix_pre[l], w_fox, BF16, tm=1024, tn=3 * W)
        main3 = main2.reshape(bsz, t, -1)

        y_a = _hgrn(main3, lower_bounds[l], hgrn_norm_w[l])

        q_b, k_b, v_b, q3, k3, wi = _dsa_prep(main3, idx2.reshape(bsz, t, -1), tables, main_blocks["b_qkv"])
        y_b = _dsa(q_b, k_b, jnp.transpose(v_b, (0, 2, 1)), q3, k3,
                   jnp.transpose(wi[:, :, :IDX_HEADS], (0, 2, 1)), topk)

        prm = dict(
            mu=_pad_cols(rwkv_mu[l][None, :], RWKV_COLS), w0=rwkv_w0[l][None, :],
            w2=_pad_rows(rwkv_w2[l], RWKV_LORA_LANES, 0), a0=rwkv_a0[l][None, :],
            a2=_pad_rows(rwkv_a2[l], RWKV_LORA_LANES, 32), g2=_pad_rows(rwkv_g2[l], RWKV_LORA_LANES, 64),
            k_k=rwkv_k_k[l][None, :], k_a=rwkv_k_a[l][None, :], r_k=rwkv_r_k[l].reshape(1, W),
            ln_w=rwkv_ln_w[l][None, :], ln_b=rwkv_ln_b[l][None, :])
        if l > 0:
            prm.update(v0=rwkv_v0[l - 1][None, :], v1=_pad_cols(rwkv_v1[l - 1], LANES),
                       v2=_pad_rows(rwkv_v2[l - 1], LANES, 0))
        y_c, v_c = _rwkv(main3, main_blocks["c"], v_first, prm)
        if l == 0:
            v_first = v_c

        c_rep = _fox_cum(main3, main_blocks["df"], _pad_cols(fox_f_bias[l][None, :], LANES))
        fox3 = fox2.reshape(bsz, t, 3 * W)
        y_d = _fox(fox3, jnp.transpose(fox3[:, :, 2 * W:], (0, 2, 1)), c_rep)

        ys = [y.reshape(n, W) for y in (y_a, y_b, y_c, y_d)]
        x2 = _merge(x2, ys, norm_mix_pre[l], wl[:, o_g:o_g + 4 * d], w_branch[l], w_out[l], norm_mix_post[l])
        x2 = _mlp(x2, norm_mlp_pre[l], w_up[l], w_down[l], norm_mlp_post[l])
    return x2.reshape(bsz, t, d)
```

```python
import functools

import jax
import jax.numpy as jnp
from jax import lax
from jax.experimental import pallas as pl
from jax.experimental.pallas import tpu as pltpu

F32 = jnp.float32
BF16 = jnp.bfloat16

HEAD_DIM = 64
N_HEADS = 4
WIDTH = N_HEADS * HEAD_DIM
IDX_HEADS = 8
IDX_DIM = 32
TOPK_MAX = 256
ROPE_THETA = 10000.0
NORM_EPS = 1e-6
RWKV_GN_EPS = 64e-5
MASK_VALUE = -1e30
LOG2E = 1.4426950408889634
RWKV_LORA_LANES = 128
RWKV_COLS = 3 * WIDTH + RWKV_LORA_LANES

LANES = 128
VMEM_LIMIT = 48 * 1024 * 1024

NN = ((1,), (0,))
NT = ((1,), (1,))
TN = ((0,), (0,))


def _dot(a, b, dims=NN):
    return lax.dot_general(a, b, (dims, ((), ())), preferred_element_type=F32)


def _split2(x):
    hi = x.astype(BF16)
    lo = (x - hi.astype(F32)).astype(BF16)
    return hi, lo


def _split3(x):
    hi = x.astype(BF16)
    r1 = x - hi.astype(F32)
    mid = r1.astype(BF16)
    lo = (r1 - mid.astype(F32)).astype(BF16)
    return hi, mid, lo


def _dot3(a, b, dims=NN):
    ah, al = _split2(a)
    bh, bl = _split2(b)
    return _dot(ah, bh, dims) + (_dot(ah, bl, dims) + _dot(al, bh, dims))


def _dot_exact_lhs(a_bf16, b, dims=NN):
    b1, b2, b3 = _split3(b)
    return _dot(a_bf16, b1, dims) + (_dot(a_bf16, b2, dims) + _dot(a_bf16, b3, dims))


def _dot_exact_rhs(a, b_bf16, dims=NN):
    a1, a2 = _split2(a)
    return _dot(a1, b_bf16, dims) + _dot(a2, b_bf16, dims)


def _iota(shape, dim):
    return lax.broadcasted_iota(jnp.int32, shape, dim)


def _head_block_ones(n, group):
    return (_iota((n, n), 0) // group == _iota((n, n), 1) // group).astype(BF16)


def _lower_tri_ones(n):
    return (_iota((n, n), 0) >= _iota((n, n), 1)).astype(BF16)


def _sigmoid(x):
    return 1.0 / (1.0 + jnp.exp(-x))


def _softplus(x):
    return jnp.maximum(x, 0.0) + jnp.log(1.0 + jnp.exp(-jnp.abs(x)))


def _params(sem):
    return pltpu.CompilerParams(dimension_semantics=sem, vmem_limit_bytes=VMEM_LIMIT)


def _norm_rows(x, g):
    return x * lax.rsqrt(jnp.mean(x * x, axis=-1, keepdims=True) + NORM_EPS) * g


def _norm_matmul_kernel(x_ref, g_ref, w_ref, o_ref, h_ref):
    @pl.when(pl.program_id(1) == 0)
    def _():
        h_ref[...] = _norm_rows(x_ref[...], g_ref[...]).astype(BF16)

    o_ref[...] = _dot(h_ref[...], w_ref[...]).astype(o_ref.dtype)


def _norm_matmul3_kernel(x_ref, g_ref, wh_ref, wl_ref, o_ref, hh_ref, hl_ref):
    @pl.when(pl.program_id(1) == 0)
    def _():
        hh, hl = _split2(_norm_rows(x_ref[...], g_ref[...]))
        hh_ref[...] = hh
        hl_ref[...] = hl

    o_ref[...] = (_dot(hh_ref[...], wh_ref[...])
                  + (_dot(hh_ref[...], wl_ref[...]) + _dot(hl_ref[...], wh_ref[...]))).astype(o_ref.dtype)


def _norm_matmul(x2, g, w, out_dtype, tm, tn, precise=False):
    n, d = x2.shape
    cols = w.shape[1]
    grid = (n // tm, cols // tn)
    x_spec = pl.BlockSpec((tm, d), lambda i, j: (i, 0))
    g_spec = pl.BlockSpec((1, d), lambda i, j: (0, 0))
    w_spec = pl.BlockSpec((d, tn), lambda i, j: (0, j))
    o_spec = pl.BlockSpec((tm, tn), lambda i, j: (i, j))
    g2 = g.reshape(1, d)
    if precise:
        wh, wl = _split2(w)
        return pl.pallas_call(
            _norm_matmul3_kernel, grid=grid,
            in_specs=[x_spec, g_spec, w_spec, w_spec], out_specs=o_spec,
            out_shape=jax.ShapeDtypeStruct((n, cols), out_dtype),
            scratch_shapes=[pltpu.VMEM((tm, d), BF16), pltpu.VMEM((tm, d), BF16)],
            compiler_params=_params(("parallel", "arbitrary")), name="norm_proj_precise",
        )(x2, g2, wh, wl)
    return pl.pallas_call(
        _norm_matmul_kernel, grid=grid,
        in_specs=[x_spec, g_spec, w_spec], out_specs=o_spec,
        out_shape=jax.ShapeDtypeStruct((n, cols), out_dtype),
        scratch_shapes=[pltpu.VMEM((tm, d), BF16)],
        compiler_params=_params(("parallel", "arbitrary")), name="norm_proj",
    )(x2, g2, w.astype(BF16))


HGRN_CHUNK = 64
HGRN_GROUP = 8


def _dot1(a, b, dims=NN):
    return _dot(a.astype(BF16), b.astype(BF16), dims)


def _hgrn_kernel(q_ref, f_ref, i_ref, g_ref, lb_ref, nw_ref, o_ref, st_ref, *, n_chunks):
    C, G = HGRN_CHUNK, HGRN_GROUP

    @pl.when(pl.program_id(1) == 0)
    def _():
        st_ref[...] = jnp.zeros_like(st_ref)

    lb = lb_ref[...]
    nw = nw_ref[...]
    tri = _lower_tri_ones(C)
    bd = _head_block_ones(WIDTH, HEAD_DIM)
    bd_mask = _iota((WIDTH, WIDTH), 0) // HEAD_DIM == _iota((WIDTH, WIDTH), 1) // HEAD_DIM
    row_top = _iota((G, 1), 0)
    lane_head = _iota((1, WIDTH), 1) // HEAD_DIM

    def chunk(c, carry):
        sl = pl.ds(pl.multiple_of(c * C, C), C)
        fl = f_ref[0, sl, :]
        f = lb + (1.0 - lb) * _sigmoid(fl)
        k = (1.0 - lb) * _sigmoid(-fl)
        b = _dot_exact_lhs(tri, jnp.log(f))
        q = q_ref[0, sl, :] * (HEAD_DIM ** -0.5)
        v = i_ref[0, sl, :]
        st = st_ref[...]
        o_inter = _dot1(q * jnp.exp(b), st, NT)

        n_groups = C // G
        o_rows = [o_inter[g * G:(g + 1) * G] for g in range(n_groups)]
        pieces = []
        for g in range(n_groups):
            s0 = g * G
            q_top, b_top, k_top = q[s0:s0 + G], b[s0:s0 + G], k[s0:s0 + G]
            for u in range(G):
                top = q_top * k_top[u:u + 1] * jnp.exp(jnp.minimum(b_top - b_top[u:u + 1], 0.0))
                pieces.append(jnp.where(row_top >= u, top, 0.0))
        r = _dot(jnp.concatenate(pieces, axis=0).astype(BF16), bd)
        att, v_st = [], []
        for g in range(n_groups - 1):
            s0 = g * G
            b_top, k_top, v_top = b[s0:s0 + G], k[s0:s0 + G], v[s0:s0 + G]
            b_ref = b_top[G - 1:G]
            qp = (q[s0 + G:] * jnp.exp(b[s0 + G:] - b_ref)).astype(BF16)
            kp = k_top * jnp.exp(b_ref - b_top)
            k_st = jnp.concatenate([jnp.where(lane_head == h, kp, 0.0) for h in range(N_HEADS)], axis=0)
            v_st.append(jnp.concatenate([jnp.where(lane_head == h, v_top, 0.0) for h in range(N_HEADS)], axis=0))
            att.append(_dot(qp, k_st.astype(BF16), NT))
        for g in range(n_groups):
            v_top = v[g * G:(g + 1) * G]
            for u in range(G):
                o_rows[g] = o_rows[g] + r[(g * G + u) * G:(g * G + u + 1) * G] * v_top[u:u + 1]
        for g in range(n_groups - 1):
            rest = _dot(att[g].astype(BF16), v_st[g].astype(BF16))
            for g2 in range(g + 1, n_groups):
                o_rows[g2] = o_rows[g2] + rest[(g2 - g - 1) * G:(g2 - g) * G]
        o = jnp.concatenate(o_rows, axis=0)

        b_last = b[C - 1:C, :]
        upd = _dot1(v, k * jnp.exp(b_last - b), TN)
        st_ref[...] = st * jnp.exp(b_last) + jnp.where(bd_mask, upd, 0.0)

        ms = _dot_exact_rhs(o * o, bd) * (1.0 / HEAD_DIM)
        on = o * lax.rsqrt(ms + NORM_EPS) * nw
        gl = g_ref[0, sl, :]
        o_ref[0, sl, :] = (on * (gl * _sigmoid(gl))).astype(o_ref.dtype)
        return carry

    lax.fori_loop(0, n_chunks, chunk, 0)


def _hgrn(main3, lb, norm_w, tb=256):
    bsz, t, _ = main3.shape
    col = lambda j: pl.BlockSpec((1, tb, WIDTH), lambda b, i, j=j: (b, i, j))
    vec = pl.BlockSpec((1, WIDTH), lambda b, i: (0, 0))
    return pl.pallas_call(
        functools.partial(_hgrn_kernel, n_chunks=tb // HGRN_CHUNK),
        grid=(bsz, t // tb),
        in_specs=[col(0), col(1), col(2), col(3), vec, vec],
        out_specs=pl.BlockSpec((1, tb, WIDTH), lambda b, i: (b, i, 0)),
        out_shape=jax.ShapeDtypeStruct((bsz, t, WIDTH), BF16),
        scratch_shapes=[pltpu.VMEM((WIDTH, WIDTH), F32)],
        compiler_params=_params(("parallel", "arbitrary")), name="hgrn2",
    )(main3, main3, main3, main3, lb.reshape(1, WIDTH), jnp.tile(norm_w, N_HEADS).reshape(1, WIDTH))


RWKV_CHUNK = 64


def _rwkv_kernel(*refs, n_chunks, has_vgate):
    if has_vgate:
        (c_ref, vf_ref, mu_ref, w0_ref, w2_ref, a0_ref, a2_ref, g2_ref, kkw_ref, kaw_ref, rk_ref,
         lnw_ref, lnb_ref, v0_ref, v1_ref, v2_ref, y_ref, vout_ref, s_ref, prev_ref, y_sc) = refs
    else:
        (c_ref, mu_ref, w0_ref, w2_ref, a0_ref, a2_ref, g2_ref, kkw_ref, kaw_ref, rk_ref,
         lnw_ref, lnb_ref, y_ref, vout_ref, s_ref, prev_ref, y_sc) = refs
    C = RWKV_CHUNK
    W = WIDTH
    tb = n_chunks * C

    @pl.when(pl.program_id(1) == 0)
    def _():
        s_ref[...] = jnp.zeros_like(s_ref)
        prev_ref[...] = jnp.zeros_like(prev_ref)

    p = c_ref[0]
    shifted = jnp.where(_iota((tb, 1), 0) == 0, prev_ref[...], pltpu.roll(p, 1, axis=0))
    prev_ref[...] = p[tb - 1:tb, :]
    xm = p + (shifted - p) * mu_ref[...]
    r = xm[:, 0:W]
    k = xm[:, W:2 * W]
    v = xm[:, 2 * W:3 * W]
    lora = xm[:, 3 * W:3 * W + RWKV_LORA_LANES]

    bd = _head_block_ones(W, HEAD_DIM)
    w_log = -_softplus(-(w0_ref[...] + _dot3(jnp.tanh(lora), w2_ref[...]))) - 0.5
    log_decay = -jnp.exp(w_log)
    a = _sigmoid(a0_ref[...] + _dot3(lora, a2_ref[...]))
    g = _dot3(_sigmoid(lora), g2_ref[...])
    if has_vgate:
        vg = _dot3(_dot3(v, v1_ref[...]), v2_ref[...])
        v = v + (vf_ref[0] - v) * _sigmoid(v0_ref[...] + vg)
    vout_ref[0] = v
    kk = k * kkw_ref[...]
    kk = kk * lax.rsqrt(jnp.maximum(_dot_exact_rhs(kk * kk, bd), 1e-24))
    k = k * (1.0 + (a - 1.0) * kaw_ref[...])

    tri = _lower_tri_ones(C)
    lane_head = _iota((1, W), 1) // HEAD_DIM
    hc = N_HEADS * C
    rt = _iota((hc, hc), 0) % C
    ct = _iota((hc, hc), 1) % C
    strict_lower = rt > ct
    lower = rt >= ct
    eye = (_iota((hc, hc), 0) == _iota((hc, hc), 1)).astype(F32)

    def stack(m):
        return jnp.concatenate([jnp.where(lane_head == h, m, 0.0) for h in range(N_HEADS)], axis=0)

    chunks = range(n_chunks)
    a_s, b_s, k_s, r_s, v_s, p_last, m, lak, lrb, lrk = ([None] * n_chunks for _ in range(10))
    for c in chunks:
        sl = slice(c * C, (c + 1) * C)
        ld = log_decay[sl]
        cum = _dot_exact_lhs(tri, ld)
        pdec = jnp.exp(cum)
        pinv = jnp.exp(-cum)
        p_last[c] = pdec[C - 1:C, :]
        a_s[c] = stack(-kk[sl] * jnp.exp(cum - ld)).astype(BF16)
        b_s[c] = stack(kk[sl] * a[sl] * pinv).astype(BF16)
        k_s[c] = stack(k[sl] * pinv).astype(BF16)
        r_s[c] = stack(r[sl] * pdec)
        v_s[c] = stack(v[sl]).astype(BF16)
        gram = _dot(jnp.concatenate([a_s[c], r_s[c].astype(BF16)], axis=0),
                    jnp.concatenate([b_s[c], k_s[c]], axis=0), NT)
        m[c] = jnp.where(strict_lower, gram[:hc, :hc], 0.0)
        lak[c] = jnp.where(strict_lower, gram[:hc, hc:], 0.0)
        lrb[c] = jnp.where(lower, gram[hc:, :hc], 0.0)
        lrk[c] = jnp.where(lower, gram[hc:, hc:], 0.0)

    x = [eye + m[c] for c in chunks]
    power = 2
    while power < C:
        m = [_dot1(m[c], m[c]) for c in chunks]
        x = [x[c] + _dot1(x[c], m[c]) for c in chunks]
        power *= 2
    lakv = [_dot1(lak[c], v_s[c]) for c in chunks]
    lrkv = [_dot1(lrk[c], v_s[c]) for c in chunks]
    w1 = [_dot1(x[c], lakv[c]) for c in chunks]
    a2 = [_dot1(x[c], a_s[c]) for c in chunks]
    r2 = [r_s[c] + _dot1(lrb[c], a2[c]) for c in chunks]
    y0 = [_dot1(lrb[c], w1[c]) + lrkv[c] for c in chunks]
    t1 = [_dot1(a2[c], b_s[c], TN) for c in chunks]
    t0 = [_dot1(w1[c], b_s[c], TN) + _dot(v_s[c], k_s[c], TN) for c in chunks]

    s = s_ref[...]
    for c in chunks:
        y = _dot1(r2[c], s, NT) + y0[c]
        y_sc[c * C:(c + 1) * C, :] = sum(y[h * C:(h + 1) * C] for h in range(N_HEADS))
        s = (s + _dot1(s, t1[c]) + t0[c]) * p_last[c]
    s_ref[...] = s

    y = y_sc[...]
    inv_n = 1.0 / HEAD_DIM
    mean = _dot_exact_rhs(y, bd) * inv_n
    yc = y - mean
    var = _dot_exact_rhs(yc * yc, bd) * inv_n
    yn = yc * lax.rsqrt(var + RWKV_GN_EPS) * lnw_ref[...] + lnb_ref[...]
    yn = yn + _dot_exact_rhs(r * k * rk_ref[...], bd) * v
    y_ref[0] = (yn * g).astype(y_ref.dtype)


def _rwkv(c3, col_block, v_first, prm, tb=256):
    bsz, t, _ = c3.shape
    has_vgate = v_first is not None
    blk = lambda w: pl.BlockSpec((1, tb, w), lambda b, i: (b, i, 0))
    cblk = pl.BlockSpec((1, tb, RWKV_COLS), lambda b, i: (b, i, col_block))
    full = lambda a: pl.BlockSpec(a.shape, lambda b, i: (0,) * a.ndim)
    names = ["mu", "w0", "w2", "a0", "a2", "g2", "k_k", "k_a", "r_k", "ln_w", "ln_b"]
    if has_vgate:
        names += ["v0", "v1", "v2"]
    args = [c3] + ([v_first] if has_vgate else []) + [prm[n] for n in names]
    in_specs = [cblk] + ([blk(WIDTH)] if has_vgate else []) + [full(prm[n]) for n in names]
    return pl.pallas_call(
        functools.partial(_rwkv_kernel, n_chunks=tb // RWKV_CHUNK, has_vgate=has_vgate),
        grid=(bsz, t // tb),
        in_specs=in_specs,
        out_specs=[blk(WIDTH), blk(WIDTH)],
        out_shape=[jax.ShapeDtypeStruct((bsz, t, WIDTH), BF16), jax.ShapeDtypeStruct((bsz, t, WIDTH), F32)],
        scratch_shapes=[pltpu.VMEM((WIDTH, WIDTH), F32), pltpu.VMEM((1, RWKV_COLS), F32),
                        pltpu.VMEM((tb, WIDTH), F32)],
        compiler_params=_params(("parallel", "arbitrary")), name="rwkv7",
    )(*args)


def _fox_cum_kernel(f_ref, bias_ref, c_ref, carry_ref, *, tb):
    @pl.when(pl.program_id(1) == 0)
    def _():
        carry_ref[...] = jnp.zeros_like(carry_ref)

    logf = -_softplus(-(f_ref[0] + bias_ref[...]))
    c = _dot_exact_lhs(_lower_tri_ones(tb), logf) + carry_ref[...]
    carry_ref[...] = c[tb - 1:tb, :]
    for h in range(N_HEADS):
        pick = (_iota((LANES, LANES), 0) == h).astype(BF16)
        c1, c2, c3 = _split3(c)
        c_ref[0, h] = LOG2E * (_dot(c1, pick) + (_dot(c2, pick) + _dot(c3, pick)))


def _fox_cum(main3, col_block, bias_row, tb=256):
    bsz, t, _ = main3.shape
    return pl.pallas_call(
        functools.partial(_fox_cum_kernel, tb=tb),
        grid=(bsz, t // tb),
        in_specs=[pl.BlockSpec((1, tb, LANES), lambda b, i: (b, i, col_block)),
                  pl.BlockSpec((1, LANES), lambda b, i: (0, 0))],
        out_specs=pl.BlockSpec((1, N_HEADS, tb, LANES), lambda b, i: (b, 0, i, 0)),
        out_shape=jax.ShapeDtypeStruct((bsz, N_HEADS, t, LANES), F32),
        scratch_shapes=[pltpu.VMEM((1, LANES), F32)],
        compiler_params=_params(("parallel", "arbitrary")), name="fox_cumgate",
    )(main3, bias_row)


ATT_QB = 512
ATT_KB = 512


def _stack_heads(q):
    lane_head = _iota((1, WIDTH), 1) // HEAD_DIM
    zero = jnp.zeros_like(q)
    return jnp.concatenate([jnp.where(lane_head == h, q, zero) for h in range(N_HEADS)], axis=0)


def _softmax_stage_t(s_t, m_sc, l_sc):
    m_old = m_sc[...]
    m_new = jnp.maximum(m_old, jnp.max(s_t, axis=0, keepdims=True))
    alpha = jnp.exp2(m_old - m_new)
    p_t = jnp.exp2(s_t - m_new)
    l_sc[...] = alpha * l_sc[...] + jnp.sum(p_t, axis=0, keepdims=True)
    m_sc[...] = m_new
    return p_t.astype(BF16), alpha


def _pv_stage_t(vt_blk, p_t, alpha, acc_sc):
    for h in range(N_HEADS):
        cs = slice(h * ATT_QB, (h + 1) * ATT_QB)
        pv = _dot(vt_blk[h * HEAD_DIM:(h + 1) * HEAD_DIM, :], p_t[:, cs])
        acc_sc[h] = alpha[:, cs] * acc_sc[h] + pv


def _attention_t(n_blocks, qk, post, post_last, vt_ref, bufs, m_sc, l_sc, acc_sc):
    KB = ATT_KB
    s_buf, p_buf, a_buf = bufs[0:2], bufs[2:4], bufs[4:6]
    _softmax_init_t(m_sc, l_sc, acc_sc)

    def vt_blk(j):
        return vt_ref[0, :, pl.ds(pl.multiple_of(j * KB, KB), KB)]

    def pending(j, par):
        _pv_stage_t(vt_blk(jnp.maximum(j - 1, 0)), p_buf[1 - par][...], a_buf[1 - par][...], acc_sc)

    def step(j, par):
        raw = s_buf[par][...]
        s_buf[1 - par][...] = qk(j + 1)
        p_t, alpha = _softmax_stage_t(post(j, raw), m_sc, l_sc)
        pending(j, par)
        p_buf[par][...] = p_t
        a_buf[par][...] = alpha

    def final(j, par):
        p_t, alpha = _softmax_stage_t(post_last(j, s_buf[par][...]), m_sc, l_sc)
        pending(j, par)
        _pv_stage_t(vt_blk(j), p_t, alpha, acc_sc)

    s_buf[0][...] = qk(0)
    p_buf[1][...] = jnp.zeros_like(p_buf[1])
    a_buf[1][...] = jnp.ones_like(a_buf[1])
    pairs = (n_blocks - 1) // 2

    def pair(t, carry):
        step(2 * t, 0)
        step(2 * t + 1, 1)
        return carry

    lax.fori_loop(0, pairs, pair, 0)
    odd_left = (n_blocks - 1) % 2 == 1

    @pl.when(odd_left)
    def _():
        step(2 * pairs, 0)
        final(2 * pairs + 1, 1)

    @pl.when(jnp.logical_not(odd_left))
    def _():
        final(2 * pairs, 0)


def _attention_bufs():
    cols = N_HEADS * ATT_QB
    return ([pltpu.VMEM((ATT_KB, cols), F32)] * 2 + [pltpu.VMEM((ATT_KB, cols), BF16)] * 2
            + [pltpu.VMEM((1, cols), F32)] * 2)


def _softmax_init_t(m_sc, l_sc, acc_sc):
    m_sc[...] = jnp.full_like(m_sc, MASK_VALUE)
    l_sc[...] = jnp.zeros_like(l_sc)
    acc_sc[...] = jnp.zeros_like(acc_sc)


def _softmax_finish_t(l_sc, acc_sc):
    l = l_sc[...]
    out_t = jnp.concatenate([acc_sc[h] / l[:, h * ATT_QB:(h + 1) * ATT_QB] for h in range(N_HEADS)], axis=0)
    return out_t.T


def _fox_kernel(q_ref, k_ref, vt_ref, ck_ref, o_ref, m_sc, l_sc, acc_sc, *bufs):
    QB, KB = ATT_QB, ATT_KB
    i = pl.program_id(1)
    qstack = _stack_heads(q_ref[0])
    qpos = i * QB + _iota((1, N_HEADS * QB), 1) % QB

    def qk(j):
        return _dot(k_ref[0, pl.ds(pl.multiple_of(j * KB, KB), KB), :], qstack, NT)

    def post(j, s_t):
        ks = pl.ds(pl.multiple_of(j * KB, KB), KB)
        rep = lambda c: jnp.concatenate([c] * (QB // LANES), axis=1)
        return jnp.concatenate([s_t[:, h * QB:(h + 1) * QB] - rep(ck_ref[0, h, ks, :]) for h in range(N_HEADS)], axis=1)

    def post_last(j, s_t):
        return jnp.where(j * KB + _iota((KB, 1), 0) <= qpos, post(j, s_t), MASK_VALUE)

    _attention_t((i * QB) // KB + 1, qk, post, post_last, vt_ref, bufs, m_sc, l_sc, acc_sc)
    o_ref[0] = _softmax_finish_t(l_sc, acc_sc).astype(o_ref.dtype)


def _fox(qkv3, v_t, c_rep):
    bsz, t, _ = qkv3.shape
    cols = N_HEADS * ATT_QB
    return pl.pallas_call(
        _fox_kernel,
        grid=(bsz, t // ATT_QB),
        in_specs=[pl.BlockSpec((1, ATT_QB, WIDTH), lambda b, i: (b, i, 0)),
                  pl.BlockSpec((1, t, WIDTH), lambda b, i: (b, 0, 1)),
                  pl.BlockSpec((1, WIDTH, t), lambda b, i: (b, 0, 0)),
                  pl.BlockSpec((1, N_HEADS, t, LANES), lambda b, i: (b, 0, 0, 0))],
        out_specs=pl.BlockSpec((1, ATT_QB, WIDTH), lambda b, i: (b, i, 0)),
        out_shape=jax.ShapeDtypeStruct((bsz, t, WIDTH), BF16),
        scratch_shapes=[pltpu.VMEM((1, cols), F32), pltpu.VMEM((1, cols), F32),
                        pltpu.VMEM((N_HEADS, HEAD_DIM, ATT_QB), F32)] + _attention_bufs(),
        compiler_params=_params(("parallel", "arbitrary")), name="fox_attention",
    )(qkv3, qkv3, v_t, c_rep)


def _swap_halves(x, half):
    n = x.shape[-1]
    lower = (_iota((1, n), 1) % (2 * half)) < half
    return jnp.where(lower, pltpu.roll(x, n - half, axis=1), pltpu.roll(x, half, axis=1))


def _dsa_prep_kernel(q_ref, k_ref, v_ref, iq_ref, ik_ref, iw_ref, ch_ref, sh_ref, ci_ref, si_ref,
                     selh_ref, sell_ref, qo_ref, ko_ref, vo_ref, q3_ref, k3_ref, wi_ref):
    ch, sh, ci, si = ch_ref[...], sh_ref[...], ci_ref[...], si_ref[...]

    def rope(x, c, s, half):
        return x * c + _swap_halves(x, half) * s

    qo_ref[0] = (rope(q_ref[0], ch, sh, HEAD_DIM // 2) * (LOG2E * HEAD_DIM ** -0.5)).astype(BF16)
    ko_ref[0] = rope(k_ref[0], ch, sh, HEAD_DIM // 2).astype(BF16)
    vo_ref[0] = v_ref[0].astype(BF16)
    qh, ql = _split2(rope(iq_ref[0], ci, si, IDX_DIM // 2) * (IDX_DIM ** -0.5))
    q3_ref[0] = (_dot(qh, selh_ref[...]) + _dot(ql, sell_ref[...])).astype(BF16)
    kh, kl = _split2(rope(ik_ref[0], ci[:, :LANES], si[:, :LANES], IDX_DIM // 2))
    seg = _iota((1, LANES), 1) // IDX_DIM
    k3_ref[0] = jnp.where(seg < 2, kh, jnp.where(seg == 2, kl, jnp.zeros_like(kl)))
    wi_ref[0] = iw_ref[0] * (IDX_HEADS ** -0.5)


def _dsa_prep(main3, idx3, tables, qkv_block0, tb=512):
    bsz, t, _ = main3.shape
    iq_w = IDX_HEADS * IDX_DIM
    r = jnp.arange(iq_w)[:, None]
    c = jnp.arange(IDX_HEADS * LANES)[None, :]
    same = (r // IDX_DIM == c // LANES) & (r % IDX_DIM == c % IDX_DIM)
    seg = (c % LANES) // IDX_DIM
    sel_hi = (same & ((seg == 0) | (seg == 2))).astype(BF16)
    sel_lo = (same & (seg == 1)).astype(BF16)
    mcol = lambda j: pl.BlockSpec((1, tb, WIDTH), lambda b, i, j=j: (b, i, qkv_block0 + j))
    tab = pl.BlockSpec((tb, WIDTH), lambda b, i: (i, 0))
    sel = pl.BlockSpec(sel_hi.shape, lambda b, i: (0, 0))
    out = lambda w: pl.BlockSpec((1, tb, w), lambda b, i: (b, i, 0))
    shape = lambda dt, w: jax.ShapeDtypeStruct((bsz, t, w), dt)
    return pl.pallas_call(
        _dsa_prep_kernel, grid=(bsz, t // tb),
        in_specs=[mcol(0), mcol(1), mcol(2), out(iq_w),
                  pl.BlockSpec((1, tb, LANES), lambda b, i: (b, i, iq_w // LANES)),
                  pl.BlockSpec((1, tb, LANES), lambda b, i: (b, i, iq_w // LANES + 1)),
                  tab, tab, tab, tab, sel, sel],
        out_specs=[out(WIDTH)] * 3 + [out(IDX_HEADS * LANES), out(LANES), out(LANES)],
        out_shape=[shape(BF16, WIDTH)] * 3 + [shape(BF16, IDX_HEADS * LANES), shape(BF16, LANES), shape(F32, LANES)],
        compiler_params=_params(("parallel", "parallel")), name="dsa_prep",
    )(main3, main3, main3, idx3, idx3, idx3, *tables, sel_hi, sel_lo)


def _dsa_kernel(q_ref, k_ref, vt_ref, q3_ref, k3_ref, wt_ref, o_ref, key_sc, tie_sc, m_sc, l_sc, acc_sc,
                *bufs, topk):
    QB, KB = ATT_QB, ATT_KB
    i = pl.program_id(1)
    nkb = (i * QB) // KB + 1
    qpos = i * QB + _iota((1, QB), 1)
    int_min = jnp.int32(-2 ** 31)

    q3 = q3_ref[0]
    qs = jnp.concatenate([q3[:, h * LANES:(h + 1) * LANES] for h in range(IDX_HEADS)], axis=0)
    wt = wt_ref[0]

    def score_block(j, carry):
        ks = pl.ds(pl.multiple_of(j * KB, KB), KB)
        r = _dot(k3_ref[0, ks, :], qs, NT)
        sc = jnp.zeros((KB, QB), F32)
        for h in range(IDX_HEADS):
            sc = sc + jnp.maximum(r[:, h * QB:(h + 1) * QB], 0.0) * wt[h:h + 1, :]
        sc = jnp.where(j * KB + _iota((KB, 1), 0) <= qpos, sc, MASK_VALUE)
        sc = jnp.where(sc == 0.0, 0.0, sc)
        bits = pltpu.bitcast(sc, jnp.int32)
        key_sc[ks, :] = jnp.where(bits < 0, bits ^ jnp.int32(0x7FFFFFFF), bits)
        return carry

    lax.fori_loop(0, nkb, score_block, 0)

    def count_ge(cand):
        def blk(j, acc):
            base = pl.multiple_of(j * KB, KB)
            for r0 in range(0, KB, 64):
                acc = acc + jnp.where(key_sc[pl.ds(base + r0, 64), :] >= cand, 1.0, 0.0)
            return acc
        acc = lax.fori_loop(0, nkb, blk, jnp.zeros((64, QB), F32))
        return jnp.sum(acc, axis=0, keepdims=True)

    kf = float(topk)

    def bit_step(step, v):
        trial = jnp.where(step == 0, jnp.zeros_like(v), v | (jnp.int32(1) << (31 - step)))
        return jnp.where(count_ge(trial) >= kf, trial, v)

    thr = lax.fori_loop(0, 32, bit_step, jnp.full((1, QB), int_min, jnp.int32))
    need = kf - count_ge(thr + 1)

    lt = _lower_tri_ones(KB)
    qstack = _stack_heads(q_ref[0])

    def qk(j):
        return _dot(k_ref[0, pl.ds(pl.multiple_of(j * KB, KB), KB), :], qstack, NT)

    def post(j, s_t):
        ks = pl.ds(pl.multiple_of(j * KB, KB), KB)
        key = key_sc[ks, :]
        tie = key == thr
        tie_b = jnp.where(tie, 1.0, 0.0).astype(BF16)
        rank = tie_sc[...] + _dot(lt, tie_b)
        tie_sc[...] = rank[KB - 1:KB, :]
        sel = (key > thr) | (tie & (rank <= need))
        sel = sel & (j * KB + _iota((KB, 1), 0) <= qpos)
        bias = jnp.where(sel, 0.0, MASK_VALUE)
        return s_t + jnp.concatenate([bias] * N_HEADS, axis=1)

    tie_sc[...] = jnp.zeros_like(tie_sc)
    _attention_t(nkb, qk, post, post, vt_ref, bufs, m_sc, l_sc, acc_sc)
    o_ref[0] = _softmax_finish_t(l_sc, acc_sc).astype(o_ref.dtype)


def _dsa(q, k, v_t, q3, k3, w_t, topk):
    bsz, t, _ = q.shape
    qblk = lambda w: pl.BlockSpec((1, ATT_QB, w), lambda b, i: (b, i, 0))
    seq = lambda w: pl.BlockSpec((1, t, w), lambda b, i: (b, 0, 0))
    cols = N_HEADS * ATT_QB
    return pl.pallas_call(
        functools.partial(_dsa_kernel, topk=topk),
        grid=(bsz, t // ATT_QB),
        in_specs=[qblk(WIDTH), seq(WIDTH), pl.BlockSpec((1, WIDTH, t), lambda b, i: (b, 0, 0)),
                  qblk(IDX_HEADS * LANES), seq(LANES),
                  pl.BlockSpec((1, IDX_HEADS, ATT_QB), lambda b, i: (b, 0, i))],
        out_specs=qblk(WIDTH),
        out_shape=jax.ShapeDtypeStruct((bsz, t, WIDTH), BF16),
        scratch_shapes=[pltpu.VMEM((t, ATT_QB), jnp.int32), pltpu.VMEM((1, ATT_QB), F32), pltpu.VMEM((1, cols), F32),
                        pltpu.VMEM((1, cols), F32), pltpu.VMEM((N_HEADS, HEAD_DIM, ATT_QB), F32)] + _attention_bufs(),
        compiler_params=_params(("parallel", "arbitrary")), name="dsa_attention",
    )(q, k, v_t, q3, k3, w_t)


def _merge_kernel(x_ref, ya_ref, yb_ref, yc_ref, yd_ref, gpre_ref, wg_ref, wb_ref, wo_ref, g_ref, o_ref):
    d = x_ref.shape[-1]
    x = x_ref[...]
    h = _norm_rows(x, gpre_ref[...]).astype(BF16)
    merged = None
    for n, y_ref in enumerate((ya_ref, yb_ref, yc_ref, yd_ref)):
        gate = _sigmoid(_dot(h, wg_ref[:, n * d:(n + 1) * d]))
        term = gate * _dot(y_ref[...], wb_ref[n])
        merged = term if merged is None else merged + term
    mix = _dot(merged.astype(BF16), wo_ref[...])
    o_ref[...] = x + _norm_rows(mix, g_ref[...])


def _merge(x2, ys, g_pre, w_gate, w_branch, w_out, g_post, tm=256):
    n, d = x2.shape
    row = lambda w: pl.BlockSpec((tm, w), lambda i: (i, 0))
    vec = pl.BlockSpec((1, d), lambda i: (0, 0))
    return pl.pallas_call(
        _merge_kernel, grid=(n // tm,),
        in_specs=[row(d)] + [row(WIDTH)] * 4 + [vec, pl.BlockSpec(w_gate.shape, lambda i: (0, 0)),
                                                 pl.BlockSpec(w_branch.shape, lambda i: (0, 0, 0)),
                                                 pl.BlockSpec(w_out.shape, lambda i: (0, 0)), vec],
        out_specs=row(d),
        out_shape=jax.ShapeDtypeStruct((n, d), F32),
        compiler_params=_params(("parallel",)), name="gated_merge",
    )(x2, *ys, g_pre.reshape(1, d), w_gate.astype(BF16), w_branch.astype(BF16), w_out.astype(BF16),
      g_post.reshape(1, d))


def _mlp_kernel(x_ref, gpre_ref, wu_ref, wd_ref, gpost_ref, o_ref, h_ref, acc_ref):
    kf = pl.program_id(1)

    @pl.when(kf == 0)
    def _():
        h_ref[...] = _norm_rows(x_ref[...], gpre_ref[...]).astype(BF16)
        acc_ref[...] = jnp.zeros_like(acc_ref)

    u = jnp.maximum(_dot(h_ref[...], wu_ref[...]), 0.0)
    acc_ref[...] += _dot((u * u).astype(BF16), wd_ref[...])

    @pl.when(kf == pl.num_programs(1) - 1)
    def _():
        o_ref[...] = x_ref[...] + _norm_rows(acc_ref[...], gpost_ref[...])


def _mlp(x2, g_pre, w_up, w_down, g_post, tm=1024, tf=1024):
    n, d = x2.shape
    dff = w_up.shape[1]
    return pl.pallas_call(
        _mlp_kernel, grid=(n // tm, dff // tf),
        in_specs=[pl.BlockSpec((tm, d), lambda i, k: (i, 0)), pl.BlockSpec((1, d), lambda i, k: (0, 0)),
                  pl.BlockSpec((d, tf), lambda i, k: (0, k)), pl.BlockSpec((tf, d), lambda i, k: (k, 0)),
                  pl.BlockSpec((1, d), lambda i, k: (0, 0))],
        out_specs=pl.BlockSpec((tm, d), lambda i, k: (i, 0)),
        out_shape=jax.ShapeDtypeStruct((n, d), F32),
        scratch_shapes=[pltpu.VMEM((tm, d), BF16), pltpu.VMEM((tm, d), F32)],
        compiler_params=_params(("parallel", "arbitrary")), name="mlp",
    )(x2, g_pre.reshape(1, d), w_up.astype(BF16), w_down.astype(BF16), g_post.reshape(1, d))


def _rope_tables(t, dim, groups):
    inv = 1.0 / (ROPE_THETA ** (jnp.arange(0, dim, 2, dtype=F32) / dim))
    ang = jnp.arange(t, dtype=F32)[:, None] * inv[None, :]
    cos, sin = jnp.cos(ang), jnp.sin(ang)
    return jnp.tile(jnp.concatenate([cos, cos], axis=1), (1, groups)), jnp.tile(jnp.concatenate([-sin, sin], axis=1), (1, groups))


def _pad_rows(w, rows, offset):
    return jnp.zeros((rows, w.shape[1]), w.dtype).at[offset:offset + w.shape[0]].set(w)


def _pad_cols(w, cols):
    return jnp.pad(w, ((0, 0), (0, cols - w.shape[1])))


def kernel(x, norm_mix_pre, norm_mix_post, norm_mlp_pre, norm_mlp_post, w_in, w_branch, w_out, hgrn_lb_logits, hgrn_norm_w, fox_f_bias, rwkv_mu, rwkv_w0, rwkv_w2, rwkv_a0, rwkv_a2, rwkv_g2, rwkv_k_k, rwkv_k_a, rwkv_r_k, rwkv_ln_w, rwkv_ln_b, rwkv_v0, rwkv_v1, rwkv_v2, w_up, w_down):
    bsz, t, d = x.shape
    n = bsz * t
    depth = w_in.shape[0]
    W = WIDTH
    topk = min(TOPK_MAX, t // 4)

    lb_soft = jax.nn.softmax(hgrn_lb_logits.astype(F32), axis=0)
    lower_bounds = jnp.cumsum(lb_soft, axis=0) - lb_soft[0:1]
    tables = _rope_tables(t, HEAD_DIM, N_HEADS) + _rope_tables(t, IDX_DIM, IDX_HEADS)

    o_a, o_b, o_iq = 0, 4 * W, 7 * W
    o_ik, o_iw = o_iq + IDX_HEADS * IDX_DIM, o_iq + IDX_HEADS * IDX_DIM + IDX_DIM
    o_c = o_iw + IDX_HEADS
    c_cols = 3 * W + 128
    o_d = o_c + c_cols
    o_df = o_d + 3 * W
    o_g = o_df + N_HEADS
    main_blocks = dict(b_qkv=4, c=2, df=(7 * W + c_cols) // LANES)
    assert main_blocks["c"] * c_cols == 7 * W

    x2 = x.reshape(n, d)
    v_first = None
    for l in range(depth):
        wl = w_in[l]
        w_main = jnp.concatenate([
            wl[:, o_a:o_a + 7 * W], wl[:, o_c:o_c + c_cols], _pad_cols(wl[:, o_df:o_df + N_HEADS], LANES),
            jnp.zeros((d, 256), F32)], axis=1)
        w_idx = jnp.concatenate([
            wl[:, o_iq:o_iq + W], jnp.tile(wl[:, o_ik:o_ik + IDX_DIM], (1, LANES // IDX_DIM)),
            _pad_cols(wl[:, o_iw:o_iw + IDX_HEADS], LANES)], axis=1)
        w_fox = jnp.concatenate([wl[:, o_d:o_d + W] * (LOG2E * HEAD_DIM ** -0.5), wl[:, o_d + W:o_d + 3 * W]], axis=1)

        main2 = _norm_matmul(x2, norm_mix_pre[l], w_main, F32, tm=1024, tn=1024)
        idx2 = _norm_matmul(x2, norm_mix_pre[l], w_idx, F32, tm=1024, tn=w_idx.shape[1], precise=True)
        fox2 = _norm_matmul(x2, norm_mix_pre[l], w_fox, BF16, tm=1024, tn=3 * W)
        main3 = main2.reshape(bsz, t, -1)

        y_a = _hgrn(main3, lower_bounds[l], hgrn_norm_w[l])

        q_b, k_b, v_b, q3, k3, wi = _dsa_prep(main3, idx2.reshape(bsz, t, -1), tables, main_blocks["b_qkv"])
        y_b = _dsa(q_b, k_b, jnp.transpose(v_b, (0, 2, 1)), q3, k3,
                   jnp.transpose(wi[:, :, :IDX_HEADS], (0, 2, 1)), topk)

        prm = dict(
            mu=_pad_cols(rwkv_mu[l][None, :], RWKV_COLS), w0=rwkv_w0[l][None, :],
            w2=_pad_rows(rwkv_w2[l], RWKV_LORA_LANES, 0), a0=rwkv_a0[l][None, :],
            a2=_pad_rows(rwkv_a2[l], RWKV_LORA_LANES, 32), g2=_pad_rows(rwkv_g2[l], RWKV_LORA_LANES, 64),
            k_k=rwkv_k_k[l][None, :], k_a=rwkv_k_a[l][None, :], r_k=rwkv_r_k[l].reshape(1, W),
            ln_w=rwkv_ln_w[l][None, :], ln_b=rwkv_ln_b[l][None, :])
        if l > 0:
            prm.update(v0=rwkv_v0[l - 1][None, :], v1=_pad_cols(rwkv_v1[l - 1], LANES),
                       v2=_pad_rows(rwkv_v2[l - 1], LANES, 0))
        y_c, v_c = _rwkv(main3, main_blocks["c"], v_first, prm)
        if l == 0:
            v_first = v_c

        c_rep = _fox_cum(main3, main_blocks["df"], _pad_cols(fox_f_bias[l][None, :], LANES))
        fox3 = fox2.reshape(bsz, t, 3 * W)
        y_d = _fox(fox3, jnp.transpose(fox3[:, :, 2 * W:], (0, 2, 1)), c_rep)

        ys = [y.reshape(n, W) for y in (y_a, y_b, y_c, y_d)]
        x2 = _merge(x2, ys, norm_mix_pre[l], wl[:, o_g:o_g + 4 * d], w_branch[l], w_out[l], norm_mix_post[l])
        x2 = _mlp(x2, norm_mlp_pre[l], w_up[l], w_down[l], norm_mlp_post[l])
    return x2.reshape(bsz, t, d)
```

```python
import functools

import jax
import jax.numpy as jnp
from jax import lax
from jax.experimental import pallas as pl
from jax.experimental.pallas import tpu as pltpu

F32 = jnp.float32
BF16 = jnp.bfloat16

HEAD_DIM = 64
N_HEADS = 4
WIDTH = N_HEADS * HEAD_DIM
IDX_HEADS = 8
IDX_DIM = 32
TOPK_MAX = 256
ROPE_THETA = 10000.0
NORM_EPS = 1e-6
RWKV_GN_EPS = 64e-5
MASK_VALUE = -1e30
LOG2E = 1.4426950408889634
RWKV_LORA_LANES = 128
RWKV_COLS = 3 * WIDTH + RWKV_LORA_LANES

LANES = 128
VMEM_LIMIT = 48 * 1024 * 1024

NN = ((1,), (0,))
NT = ((1,), (1,))
TN = ((0,), (0,))


def _dot(a, b, dims=NN):
    return lax.dot_general(a, b, (dims, ((), ())), preferred_element_type=F32)


def _split2(x):
    hi = x.astype(BF16)
    lo = (x - hi.astype(F32)).astype(BF16)
    return hi, lo


def _split3(x):
    hi = x.astype(BF16)
    r1 = x - hi.astype(F32)
    mid = r1.astype(BF16)
    lo = (r1 - mid.astype(F32)).astype(BF16)
    return hi, mid, lo


def _dot3(a, b, dims=NN):
    ah, al = _split2(a)
    bh, bl = _split2(b)
    return _dot(ah, bh, dims) + (_dot(ah, bl, dims) + _dot(al, bh, dims))


def _dot_exact_lhs(a_bf16, b, dims=NN):
    b1, b2, b3 = _split3(b)
    return _dot(a_bf16, b1, dims) + (_dot(a_bf16, b2, dims) + _dot(a_bf16, b3, dims))


def _dot_exact_rhs(a, b_bf16, dims=NN):
    a1, a2 = _split2(a)
    return _dot(a1, b_bf16, dims) + _dot(a2, b_bf16, dims)


def _iota(shape, dim):
    return lax.broadcasted_iota(jnp.int32, shape, dim)


def _head_block_ones(n, group):
    return (_iota((n, n), 0) // group == _iota((n, n), 1) // group).astype(BF16)


def _lower_tri_ones(n):
    return (_iota((n, n), 0) >= _iota((n, n), 1)).astype(BF16)


def _sigmoid(x):
    return 1.0 / (1.0 + jnp.exp(-x))


def _softplus(x):
    return jnp.maximum(x, 0.0) + jnp.log(1.0 + jnp.exp(-jnp.abs(x)))


def _params(sem):
    return pltpu.CompilerParams(dimension_semantics=sem, vmem_limit_bytes=VMEM_LIMIT)


def _norm_rows(x, g):
    return x * lax.rsqrt(jnp.mean(x * x, axis=-1, keepdims=True) + NORM_EPS) * g


def _projections_kernel(x_ref, g_ref, wm_ref, wf_ref, wih_ref, wil_ref, om_ref, of_ref, oi_ref):
    h = _norm_rows(x_ref[...], g_ref[...])
    hh, hl = _split2(h)
    om_ref[...] = _dot(hh, wm_ref[...])
    of_ref[...] = _dot(hh, wf_ref[...]).astype(of_ref.dtype)
    oi_ref[...] = _dot(hh, wih_ref[...]) + (_dot(hh, wil_ref[...]) + _dot(hl, wih_ref[...]))


def _projections(x2, g, w_main, w_fox, w_idx, tm=512):
    n, d = x2.shape
    wih, wil = _split2(w_idx)
    weights = [w_main.astype(BF16), w_fox.astype(BF16), wih, wil]
    outs = [(w_main.shape[1], F32), (w_fox.shape[1], BF16), (w_idx.shape[1], F32)]
    return pl.pallas_call(
        _projections_kernel, grid=(n // tm,),
        in_specs=[pl.BlockSpec((tm, d), lambda i: (i, 0)), pl.BlockSpec((1, d), lambda i: (0, 0))]
        + [pl.BlockSpec(w.shape, lambda i: (0, 0)) for w in weights],
        out_specs=[pl.BlockSpec((tm, c), lambda i: (i, 0)) for c, _ in outs],
        out_shape=[jax.ShapeDtypeStruct((n, c), dt) for c, dt in outs],
        compiler_params=_params(("parallel",)), name="norm_proj",
    )(x2, g.reshape(1, d), *weights)


HGRN_CHUNK = 64
HGRN_GROUP = 8


def _dot1(a, b, dims=NN):
    return _dot(a.astype(BF16), b.astype(BF16), dims)


def _hgrn_kernel(q_ref, f_ref, i_ref, g_ref, lb_ref, nw_ref, o_ref, st_ref, *, n_chunks):
    C, G = HGRN_CHUNK, HGRN_GROUP

    @pl.when(pl.program_id(1) == 0)
    def _():
        st_ref[...] = jnp.zeros_like(st_ref)

    lb = lb_ref[...]
    nw = nw_ref[...]
    tri = _lower_tri_ones(C)
    bd = _head_block_ones(WIDTH, HEAD_DIM)
    bd_mask = _iota((WIDTH, WIDTH), 0) // HEAD_DIM == _iota((WIDTH, WIDTH), 1) // HEAD_DIM
    row_top = _iota((G, 1), 0)
    lane_head = _iota((1, WIDTH), 1) // HEAD_DIM

    def chunk(c, carry):
        sl = pl.ds(pl.multiple_of(c * C, C), C)
        fl = f_ref[0, sl, :]
        f = lb + (1.0 - lb) * _sigmoid(fl)
        k = (1.0 - lb) * _sigmoid(-fl)
        b = _dot_exact_lhs(tri, jnp.log(f))
        q = q_ref[0, sl, :] * (HEAD_DIM ** -0.5)
        v = i_ref[0, sl, :]
        st = st_ref[...]
        o_inter = _dot1(q * jnp.exp(b), st, NT)

        n_groups = C // G
        o_rows = [o_inter[g * G:(g + 1) * G] for g in range(n_groups)]
        pieces = []
        for g in range(n_groups):
            s0 = g * G
            q_top, b_top, k_top = q[s0:s0 + G], b[s0:s0 + G], k[s0:s0 + G]
            for u in range(G):
                top = q_top * k_top[u:u + 1] * jnp.exp(jnp.minimum(b_top - b_top[u:u + 1], 0.0))
                pieces.append(jnp.where(row_top >= u, top, 0.0))
        r = _dot(jnp.concatenate(pieces, axis=0).astype(BF16), bd)
        att, v_st = [], []
        for g in range(n_groups - 1):
            s0 = g * G
            b_top, k_top, v_top = b[s0:s0 + G], k[s0:s0 + G], v[s0:s0 + G]
            b_ref = b_top[G - 1:G]
            qp = (q[s0 + G:] * jnp.exp(b[s0 + G:] - b_ref)).astype(BF16)
            kp = k_top * jnp.exp(b_ref - b_top)
            k_st = jnp.concatenate([jnp.where(lane_head == h, kp, 0.0) for h in range(N_HEADS)], axis=0)
            v_st.append(jnp.concatenate([jnp.where(lane_head == h, v_top, 0.0) for h in range(N_HEADS)], axis=0))
            att.append(_dot(qp, k_st.astype(BF16), NT))
        for g in range(n_groups):
            v_top = v[g * G:(g + 1) * G]
            for u in range(G):
                o_rows[g] = o_rows[g] + r[(g * G + u) * G:(g * G + u + 1) * G] * v_top[u:u + 1]
        for g in range(n_groups - 1):
            rest = _dot(att[g].astype(BF16), v_st[g].astype(BF16))
            for g2 in range(g + 1, n_groups):
                o_rows[g2] = o_rows[g2] + rest[(g2 - g - 1) * G:(g2 - g) * G]
        o = jnp.concatenate(o_rows, axis=0)

        b_last = b[C - 1:C, :]
        upd = _dot1(v, k * jnp.exp(b_last - b), TN)
        st_ref[...] = st * jnp.exp(b_last) + jnp.where(bd_mask, upd, 0.0)

        ms = _dot_exact_rhs(o * o, bd) * (1.0 / HEAD_DIM)
        on = o * lax.rsqrt(ms + NORM_EPS) * nw
        gl = g_ref[0, sl, :]
        o_ref[0, sl, :] = (on * (gl * _sigmoid(gl))).astype(o_ref.dtype)
        return carry

    lax.fori_loop(0, n_chunks, chunk, 0)


def _hgrn(main3, lb, norm_w, tb=256):
    bsz, t, _ = main3.shape
    col = lambda j: pl.BlockSpec((1, tb, WIDTH), lambda b, i, j=j: (b, i, j))
    vec = pl.BlockSpec((1, WIDTH), lambda b, i: (0, 0))
    return pl.pallas_call(
        functools.partial(_hgrn_kernel, n_chunks=tb // HGRN_CHUNK),
        grid=(bsz, t // tb),
        in_specs=[col(0), col(1), col(2), col(3), vec, vec],
        out_specs=pl.BlockSpec((1, tb, WIDTH), lambda b, i: (b, i, 0)),
        out_shape=jax.ShapeDtypeStruct((bsz, t, WIDTH), BF16),
        scratch_shapes=[pltpu.VMEM((WIDTH, WIDTH), F32)],
        compiler_params=_params(("parallel", "arbitrary")), name="hgrn2",
    )(main3, main3, main3, main3, lb.reshape(1, WIDTH), jnp.tile(norm_w, N_HEADS).reshape(1, WIDTH))


RWKV_CHUNK = 64


def _rwkv_kernel(*refs, n_chunks, has_vgate):
    if has_vgate:
        (c_ref, vf_ref, mu_ref, w0_ref, w2_ref, a0_ref, a2_ref, g2_ref, kkw_ref, kaw_ref, rk_ref,
         lnw_ref, lnb_ref, v0_ref, v1_ref, v2_ref, y_ref, vout_ref, s_ref, prev_ref, y_sc) = refs
    else:
        (c_ref, mu_ref, w0_ref, w2_ref, a0_ref, a2_ref, g2_ref, kkw_ref, kaw_ref, rk_ref,
         lnw_ref, lnb_ref, y_ref, vout_ref, s_ref, prev_ref, y_sc) = refs
    C = RWKV_CHUNK
    W = WIDTH
    tb = n_chunks * C

    @pl.when(pl.program_id(1) == 0)
    def _():
        s_ref[...] = jnp.zeros_like(s_ref)
        prev_ref[...] = jnp.zeros_like(prev_ref)

    p = c_ref[0]
    shifted = jnp.where(_iota((tb, 1), 0) == 0, prev_ref[...], pltpu.roll(p, 1, axis=0))
    prev_ref[...] = p[tb - 1:tb, :]
    xm = p + (shifted - p) * mu_ref[...]
    r = xm[:, 0:W]
    k = xm[:, W:2 * W]
    v = xm[:, 2 * W:3 * W]
    lora = xm[:, 3 * W:3 * W + RWKV_LORA_LANES]

    bd = _head_block_ones(W, HEAD_DIM)
    w_log = -_softplus(-(w0_ref[...] + _dot3(jnp.tanh(lora), w2_ref[...]))) - 0.5
    log_decay = -jnp.exp(w_log)
    a = _sigmoid(a0_ref[...] + _dot3(lora, a2_ref[...]))
    g = _dot3(_sigmoid(lora), g2_ref[...])
    if has_vgate:
        vg = _dot3(_dot3(v, v1_ref[...]), v2_ref[...])
        v = v + (vf_ref[0] - v) * _sigmoid(v0_ref[...] + vg)
    vout_ref[0] = v
    kk = k * kkw_ref[...]
    kk = kk * lax.rsqrt(jnp.maximum(_dot_exact_rhs(kk * kk, bd), 1e-24))
    k = k * (1.0 + (a - 1.0) * kaw_ref[...])

    tri = _lower_tri_ones(C)
    lane_head = _iota((1, W), 1) // HEAD_DIM
    hc = N_HEADS * C
    rt = _iota((hc, hc), 0) % C
    ct = _iota((hc, hc), 1) % C
    strict_lower = rt > ct
    lower = rt >= ct
    eye = (_iota((hc, hc), 0) == _iota((hc, hc), 1)).astype(F32)

    def stack(m):
        return jnp.concatenate([jnp.where(lane_head == h, m, 0.0) for h in range(N_HEADS)], axis=0)

    chunks = range(n_chunks)
    a_s, b_s, k_s, r_s, v_s, p_last, m, lak, lrb, lrk = ([None] * n_chunks for _ in range(10))
    for c in chunks:
        sl = slice(c * C, (c + 1) * C)
        ld = log_decay[sl]
        cum = _dot_exact_lhs(tri, ld)
        pdec = jnp.exp(cum)
        pinv = jnp.exp(-cum)
        p_last[c] = pdec[C - 1:C, :]
        a_s[c] = stack(-kk[sl] * jnp.exp(cum - ld)).astype(BF16)
        b_s[c] = stack(kk[sl] * a[sl] * pinv).astype(BF16)
        k_s[c] = stack(k[sl] * pinv).astype(BF16)
        r_s[c] = stack(r[sl] * pdec)
        v_s[c] = stack(v[sl]).astype(BF16)
        gram = _dot(jnp.concatenate([a_s[c], r_s[c].astype(BF16)], axis=0),
                    jnp.concatenate([b_s[c], k_s[c]], axis=0), NT)
        m[c] = jnp.where(strict_lower, gram[:hc, :hc], 0.0)
        lak[c] = jnp.where(strict_lower, gram[:hc, hc:], 0.0)
        lrb[c] = jnp.where(lower, gram[hc:, :hc], 0.0)
        lrk[c] = jnp.where(lower, gram[hc:, hc:], 0.0)

    x = [eye + m[c] for c in chunks]
    power = 2
    while power < C:
        m = [_dot1(m[c], m[c]) for c in chunks]
        x = [x[c] + _dot1(x[c], m[c]) for c in chunks]
        power *= 2
    lakv = [_dot1(lak[c], v_s[c]) for c in chunks]
    lrkv = [_dot1(lrk[c], v_s[c]) for c in chunks]
    w1 = [_dot1(x[c], lakv[c]) for c in chunks]
    a2 = [_dot1(x[c], a_s[c]) for c in chunks]
    r2 = [r_s[c] + _dot1(lrb[c], a2[c]) for c in chunks]
    y0 = [_dot1(lrb[c], w1[c]) + lrkv[c] for c in chunks]
    t1 = [_dot1(a2[c], b_s[c], TN) for c in chunks]
    t0 = [_dot1(w1[c], b_s[c], TN) + _dot(v_s[c], k_s[c], TN) for c in chunks]

    s = s_ref[...]
    for c in chunks:
        y = _dot1(r2[c], s, NT) + y0[c]
        y_sc[c * C:(c + 1) * C, :] = sum(y[h * C:(h + 1) * C] for h in range(N_HEADS))
        s = (s + _dot1(s, t1[c]) + t0[c]) * p_last[c]
    s_ref[...] = s

    y = y_sc[...]
    inv_n = 1.0 / HEAD_DIM
    mean = _dot_exact_rhs(y, bd) * inv_n
    yc = y - mean
    var = _dot_exact_rhs(yc * yc, bd) * inv_n
    yn = yc * lax.rsqrt(var + RWKV_GN_EPS) * lnw_ref[...] + lnb_ref[...]
    yn = yn + _dot_exact_rhs(r * k * rk_ref[...], bd) * v
    y_ref[0] = (yn * g).astype(y_ref.dtype)


def _rwkv(c3, col_block, v_first, prm, tb=512):
    bsz, t, _ = c3.shape
    has_vgate = v_first is not None
    blk = lambda w: pl.BlockSpec((1, tb, w), lambda b, i: (b, i, 0))
    cblk = pl.BlockSpec((1, tb, RWKV_COLS), lambda b, i: (b, i, col_block))
    full = lambda a: pl.BlockSpec(a.shape, lambda b, i: (0,) * a.ndim)
    names = ["mu", "w0", "w2", "a0", "a2", "g2", "k_k", "k_a", "r_k", "ln_w", "ln_b"]
    if has_vgate:
        names += ["v0", "v1", "v2"]
    args = [c3] + ([v_first] if has_vgate else []) + [prm[n] for n in names]
    in_specs = [cblk] + ([blk(WIDTH)] if has_vgate else []) + [full(prm[n]) for n in names]
    return pl.pallas_call(
        functools.partial(_rwkv_kernel, n_chunks=tb // RWKV_CHUNK, has_vgate=has_vgate),
        grid=(bsz, t // tb),
        in_specs=in_specs,
        out_specs=[blk(WIDTH), blk(WIDTH)],
        out_shape=[jax.ShapeDtypeStruct((bsz, t, WIDTH), BF16), jax.ShapeDtypeStruct((bsz, t, WIDTH), F32)],
        scratch_shapes=[pltpu.VMEM((WIDTH, WIDTH), F32), pltpu.VMEM((1, RWKV_COLS), F32),
                        pltpu.VMEM((tb, WIDTH), F32)],
        compiler_params=_params(("parallel", "arbitrary")), name="rwkv7",
    )(*args)


def _fox_cum_kernel(f_ref, bias_ref, c_ref, carry_ref, *, tb):
    @pl.when(pl.program_id(1) == 0)
    def _():
        carry_ref[...] = jnp.zeros_like(carry_ref)

    logf = -_softplus(-(f_ref[0] + bias_ref[...]))
    c = _dot_exact_lhs(_lower_tri_ones(tb), logf) + carry_ref[...]
    carry_ref[...] = c[tb - 1:tb, :]
    for h in range(N_HEADS):
        pick = (_iota((LANES, LANES), 0) == h).astype(BF16)
        c1, c2, c3 = _split3(c)
        c_ref[0, h] = LOG2E * (_dot(c1, pick) + (_dot(c2, pick) + _dot(c3, pick)))


def _fox_cum(main3, col_block, bias_row, tb=512):
    bsz, t, _ = main3.shape
    return pl.pallas_call(
        functools.partial(_fox_cum_kernel, tb=tb),
        grid=(bsz, t // tb),
        in_specs=[pl.BlockSpec((1, tb, LANES), lambda b, i: (b, i, col_block)),
                  pl.BlockSpec((1, LANES), lambda b, i: (0, 0))],
        out_specs=pl.BlockSpec((1, N_HEADS, tb, LANES), lambda b, i: (b, 0, i, 0)),
        out_shape=jax.ShapeDtypeStruct((bsz, N_HEADS, t, LANES), F32),
        scratch_shapes=[pltpu.VMEM((1, LANES), F32)],
        compiler_params=_params(("parallel", "arbitrary")), name="fox_cumgate",
    )(main3, bias_row)


ATT_QB = 512
ATT_KB = 512


def _stack_heads(q):
    lane_head = _iota((1, WIDTH), 1) // HEAD_DIM
    zero = jnp.zeros_like(q)
    return jnp.concatenate([jnp.where(lane_head == h, q, zero) for h in range(N_HEADS)], axis=0)


def _softmax_stage_t(s_t, m_sc, l_sc):
    m_old = m_sc[...]
    m_new = jnp.maximum(m_old, jnp.max(s_t, axis=0, keepdims=True))
    alpha = jnp.exp2(m_old - m_new)
    p_t = jnp.exp2(s_t - m_new)
    l_sc[...] = alpha * l_sc[...] + jnp.sum(p_t, axis=0, keepdims=True)
    m_sc[...] = m_new
    return p_t.astype(BF16), alpha


def _pv_stage_t(vt_blk, p_t, alpha, acc_sc):
    for h in range(N_HEADS):
        cs = slice(h * ATT_QB, (h + 1) * ATT_QB)
        pv = _dot(vt_blk[h * HEAD_DIM:(h + 1) * HEAD_DIM, :], p_t[:, cs])
        acc_sc[h] = alpha[:, cs] * acc_sc[h] + pv


def _attention_t(n_blocks, qk, post, post_last, vt_ref, bufs, m_sc, l_sc, acc_sc):
    KB = ATT_KB
    s_buf, p_buf, a_buf = bufs[0:2], bufs[2:4], bufs[4:6]
    _softmax_init_t(m_sc, l_sc, acc_sc)

    def vt_blk(j):
        return vt_ref[0, :, pl.ds(pl.multiple_of(j * KB, KB), KB)]

    def pending(j, par):
        _pv_stage_t(vt_blk(jnp.maximum(j - 1, 0)), p_buf[1 - par][...], a_buf[1 - par][...], acc_sc)

    def step(j, par):
        raw = s_buf[par][...]
        s_buf[1 - par][...] = qk(j + 1)
        p_t, alpha = _softmax_stage_t(post(j, raw), m_sc, l_sc)
        pending(j, par)
        p_buf[par][...] = p_t
        a_buf[par][...] = alpha

    def final(j, par):
        p_t, alpha = _softmax_stage_t(post_last(j, s_buf[par][...]), m_sc, l_sc)
        pending(j, par)
        _pv_stage_t(vt_blk(j), p_t, alpha, acc_sc)

    s_buf[0][...] = qk(0)
    p_buf[1][...] = jnp.zeros_like(p_buf[1])
    a_buf[1][...] = jnp.ones_like(a_buf[1])
    pairs = (n_blocks - 1) // 2

    def pair(t, carry):
        step(2 * t, 0)
        step(2 * t + 1, 1)
        return carry

    lax.fori_loop(0, pairs, pair, 0)
    odd_left = (n_blocks - 1) % 2 == 1

    @pl.when(odd_left)
    def _():
        step(2 * pairs, 0)
        final(2 * pairs + 1, 1)

    @pl.when(jnp.logical_not(odd_left))
    def _():
        final(2 * pairs, 0)


def _attention_bufs():
    cols = N_HEADS * ATT_QB
    return ([pltpu.VMEM((ATT_KB, cols), F32)] * 2 + [pltpu.VMEM((ATT_KB, cols), BF16)] * 2
            + [pltpu.VMEM((1, cols), F32)] * 2)


def _softmax_init_t(m_sc, l_sc, acc_sc):
    m_sc[...] = jnp.full_like(m_sc, MASK_VALUE)
    l_sc[...] = jnp.zeros_like(l_sc)
    acc_sc[...] = jnp.zeros_like(acc_sc)


def _softmax_finish_t(l_sc, acc_sc):
    l = l_sc[...]
    out_t = jnp.concatenate([acc_sc[h] / l[:, h * ATT_QB:(h + 1) * ATT_QB] for h in range(N_HEADS)], axis=0)
    return out_t.T


def _fox_kernel(q_ref, k_ref, vt_ref, ck_ref, o_ref, m_sc, l_sc, acc_sc, *bufs):
    QB, KB = ATT_QB, ATT_KB
    i = pl.program_id(1)
    qstack = _stack_heads(q_ref[0])
    qpos = i * QB + _iota((1, N_HEADS * QB), 1) % QB

    def qk(j):
        return _dot(k_ref[0, pl.ds(pl.multiple_of(j * KB, KB), KB), :], qstack, NT)

    def post(j, s_t):
        ks = pl.ds(pl.multiple_of(j * KB, KB), KB)
        rep = lambda c: jnp.concatenate([c] * (QB // LANES), axis=1)
        return jnp.concatenate([s_t[:, h * QB:(h + 1) * QB] - rep(ck_ref[0, h, ks, :]) for h in range(N_HEADS)], axis=1)

    def post_last(j, s_t):
        return jnp.where(j * KB + _iota((KB, 1), 0) <= qpos, post(j, s_t), MASK_VALUE)

    _attention_t((i * QB) // KB + 1, qk, post, post_last, vt_ref, bufs, m_sc, l_sc, acc_sc)
    o_ref[0] = _softmax_finish_t(l_sc, acc_sc).astype(o_ref.dtype)


def _fox(qkv3, v_t, c_rep):
    bsz, t, _ = qkv3.shape
    cols = N_HEADS * ATT_QB
    return pl.pallas_call(
        _fox_kernel,
        grid=(bsz, t // ATT_QB),
        in_specs=[pl.BlockSpec((1, ATT_QB, WIDTH), lambda b, i: (b, i, 0)),
                  pl.BlockSpec((1, t, WIDTH), lambda b, i: (b, 0, 1)),
                  pl.BlockSpec((1, WIDTH, t), lambda b, i: (b, 0, 0)),
                  pl.BlockSpec((1, N_HEADS, t, LANES), lambda b, i: (b, 0, 0, 0))],
        out_specs=pl.BlockSpec((1, ATT_QB, WIDTH), lambda b, i: (b, i, 0)),
        out_shape=jax.ShapeDtypeStruct((bsz, t, WIDTH), BF16),
        scratch_shapes=[pltpu.VMEM((1, cols), F32), pltpu.VMEM((1, cols), F32),
                        pltpu.VMEM((N_HEADS, HEAD_DIM, ATT_QB), F32)] + _attention_bufs(),
        compiler_params=_params(("parallel", "arbitrary")), name="fox_attention",
    )(qkv3, qkv3, v_t, c_rep)


def _swap_halves(x, half):
    n = x.shape[-1]
    lower = (_iota((1, n), 1) % (2 * half)) < half
    return jnp.where(lower, pltpu.roll(x, n - half, axis=1), pltpu.roll(x, half, axis=1))


def _dsa_prep_kernel(q_ref, k_ref, v_ref, iq_ref, ik_ref, iw_ref, ch_ref, sh_ref, ci_ref, si_ref,
                     selh_ref, sell_ref, qo_ref, ko_ref, vo_ref, q3_ref, k3_ref, wi_ref):
    ch, sh, ci, si = ch_ref[...], sh_ref[...], ci_ref[...], si_ref[...]

    def rope(x, c, s, half):
        return x * c + _swap_halves(x, half) * s

    qo_ref[0] = (rope(q_ref[0], ch, sh, HEAD_DIM // 2) * (LOG2E * HEAD_DIM ** -0.5)).astype(BF16)
    ko_ref[0] = rope(k_ref[0], ch, sh, HEAD_DIM // 2).astype(BF16)
    vo_ref[0] = v_ref[0].astype(BF16)
    qh, ql = _split2(rope(iq_ref[0], ci, si, IDX_DIM // 2) * (IDX_DIM ** -0.5))
    q3_ref[0] = (_dot(qh, selh_ref[...]) + _dot(ql, sell_ref[...])).astype(BF16)
    kh, kl = _split2(rope(ik_ref[0], ci[:, :LANES], si[:, :LANES], IDX_DIM // 2))
    seg = _iota((1, LANES), 1) // IDX_DIM
    k3_ref[0] = jnp.where(seg < 2, kh, jnp.where(seg == 2, kl, jnp.zeros_like(kl)))
    wi_ref[0] = iw_ref[0] * (IDX_HEADS ** -0.5)


def _dsa_prep(main3, idx3, tables, qkv_block0, tb=512):
    bsz, t, _ = main3.shape
    iq_w = IDX_HEADS * IDX_DIM
    r = jnp.arange(iq_w)[:, None]
    c = jnp.arange(IDX_HEADS * LANES)[None, :]
    same = (r // IDX_DIM == c // LANES) & (r % IDX_DIM == c % IDX_DIM)
    seg = (c % LANES) // IDX_DIM
    sel_hi = (same & ((seg == 0) | (seg == 2))).astype(BF16)
    sel_lo = (same & (seg == 1)).astype(BF16)
    mcol = lambda j: pl.BlockSpec((1, tb, WIDTH), lambda b, i, j=j: (b, i, qkv_block0 + j))
    tab = pl.BlockSpec((tb, WIDTH), lambda b, i: (i, 0))
    sel = pl.BlockSpec(sel_hi.shape, lambda b, i: (0, 0))
    out = lambda w: pl.BlockSpec((1, tb, w), lambda b, i: (b, i, 0))
    shape = lambda dt, w: jax.ShapeDtypeStruct((bsz, t, w), dt)
    return pl.pallas_call(
        _dsa_prep_kernel, grid=(bsz, t // tb),
        in_specs=[mcol(0), mcol(1), mcol(2), out(iq_w),
                  pl.BlockSpec((1, tb, LANES), lambda b, i: (b, i, iq_w // LANES)),
                  pl.BlockSpec((1, tb, LANES), lambda b, i: (b, i, iq_w // LANES + 1)),
                  tab, tab, tab, tab, sel, sel],
        out_specs=[out(WIDTH)] * 3 + [out(IDX_HEADS * LANES), out(LANES), out(LANES)],
        out_shape=[shape(BF16, WIDTH)] * 3 + [shape(BF16, IDX_HEADS * LANES), shape(BF16, LANES), shape(F32, LANES)],
        compiler_params=_params(("parallel", "parallel")), name="dsa_prep",
    )(main3, main3, main3, idx3, idx3, idx3, *tables, sel_hi, sel_lo)


def _dsa_kernel(q_ref, k_ref, vt_ref, q3_ref, k3_ref, wt_ref, o_ref, key_sc, tie_sc, m_sc, l_sc, acc_sc,
                *bufs, topk):
    QB, KB = ATT_QB, ATT_KB
    i = pl.program_id(1)
    nkb = (i * QB) // KB + 1
    qpos = i * QB + _iota((1, QB), 1)
    int_min = jnp.int32(-2 ** 31)

    q3 = q3_ref[0]
    qs = jnp.concatenate([q3[:, h * LANES:(h + 1) * LANES] for h in range(IDX_HEADS)], axis=0)
    wt = wt_ref[0]

    def score_block(j, carry):
        ks = pl.ds(pl.multiple_of(j * KB, KB), KB)
        r = _dot(k3_ref[0, ks, :], qs, NT)
        sc = jnp.zeros((KB, QB), F32)
        for h in range(IDX_HEADS):
            sc = sc + jnp.maximum(r[:, h * QB:(h + 1) * QB], 0.0) * wt[h:h + 1, :]
        sc = jnp.where(j * KB + _iota((KB, 1), 0) <= qpos, sc, MASK_VALUE)
        sc = jnp.where(sc == 0.0, 0.0, sc)
        bits = pltpu.bitcast(sc, jnp.int32)
        key_sc[ks, :] = jnp.where(bits < 0, bits ^ jnp.int32(0x7FFFFFFF), bits)
        return carry

    lax.fori_loop(0, nkb, score_block, 0)

    def count_ge(cand):
        def blk(j, acc):
            base = pl.multiple_of(j * KB, KB)
            for r0 in range(0, KB, 64):
                acc = acc + jnp.where(key_sc[pl.ds(base + r0, 64), :] >= cand, 1.0, 0.0)
            return acc
        acc = lax.fori_loop(0, nkb, blk, jnp.zeros((64, QB), F32))
        return jnp.sum(acc, axis=0, keepdims=True)

    kf = float(topk)

    def bit_step(step, v):
        trial = jnp.where(step == 0, jnp.zeros_like(v), v | (jnp.int32(1) << (31 - step)))
        return jnp.where(count_ge(trial) >= kf, trial, v)

    thr = lax.fori_loop(0, 32, bit_step, jnp.full((1, QB), int_min, jnp.int32))
    need = kf - count_ge(thr + 1)

    lt = _lower_tri_ones(KB)
    qstack = _stack_heads(q_ref[0])

    def qk(j):
        return _dot(k_ref[0, pl.ds(pl.multiple_of(j * KB, KB), KB), :], qstack, NT)

    def post(j, s_t):
        ks = pl.ds(pl.multiple_of(j * KB, KB), KB)
        key = key_sc[ks, :]
        tie = key == thr
        tie_b = jnp.where(tie, 1.0, 0.0).astype(BF16)
        rank = tie_sc[...] + _dot(lt, tie_b)
        tie_sc[...] = rank[KB - 1:KB, :]
        sel = (key > thr) | (tie & (rank <= need))
        sel = sel & (j * KB + _iota((KB, 1), 0) <= qpos)
        bias = jnp.where(sel, 0.0, MASK_VALUE)
        return s_t + jnp.concatenate([bias] * N_HEADS, axis=1)

    tie_sc[...] = jnp.zeros_like(tie_sc)
    _attention_t(nkb, qk, post, post, vt_ref, bufs, m_sc, l_sc, acc_sc)
    o_ref[0] = _softmax_finish_t(l_sc, acc_sc).astype(o_ref.dtype)


def _dsa(q, k, v_t, q3, k3, w_t, topk):
    bsz, t, _ = q.shape
    qblk = lambda w: pl.BlockSpec((1, ATT_QB, w), lambda b, i: (b, i, 0))
    seq = lambda w: pl.BlockSpec((1, t, w), lambda b, i: (b, 0, 0))
    cols = N_HEADS * ATT_QB
    return pl.pallas_call(
        functools.partial(_dsa_kernel, topk=topk),
        grid=(bsz, t // ATT_QB),
        in_specs=[qblk(WIDTH), seq(WIDTH), pl.BlockSpec((1, WIDTH, t), lambda b, i: (b, 0, 0)),
                  qblk(IDX_HEADS * LANES), seq(LANES),
                  pl.BlockSpec((1, IDX_HEADS, ATT_QB), lambda b, i: (b, 0, i))],
        out_specs=qblk(WIDTH),
        out_shape=jax.ShapeDtypeStruct((bsz, t, WIDTH), BF16),
        scratch_shapes=[pltpu.VMEM((t, ATT_QB), jnp.int32), pltpu.VMEM((1, ATT_QB), F32), pltpu.VMEM((1, cols), F32),
                        pltpu.VMEM((1, cols), F32), pltpu.VMEM((N_HEADS, HEAD_DIM, ATT_QB), F32)] + _attention_bufs(),
        compiler_params=_params(("parallel", "arbitrary")), name="dsa_attention",
    )(q, k, v_t, q3, k3, w_t)


def _merge_kernel(x_ref, ya_ref, yb_ref, yc_ref, yd_ref, gpre_ref, wg_ref, wb_ref, wo_ref, g_ref, o_ref):
    d = x_ref.shape[-1]
    x = x_ref[...]
    h = _norm_rows(x, gpre_ref[...]).astype(BF16)
    merged = None
    for n, y_ref in enumerate((ya_ref, yb_ref, yc_ref, yd_ref)):
        gate = _sigmoid(_dot(h, wg_ref[:, n * d:(n + 1) * d]))
        term = gate * _dot(y_ref[...], wb_ref[n])
        merged = term if merged is None else merged + term
    mix = _dot(merged.astype(BF16), wo_ref[...])
    o_ref[...] = x + _norm_rows(mix, g_ref[...])


def _merge(x2, ys, g_pre, w_gate, w_branch, w_out, g_post, tm=256):
    n, d = x2.shape
    row = lambda w: pl.BlockSpec((tm, w), lambda i: (i, 0))
    vec = pl.BlockSpec((1, d), lambda i: (0, 0))
    return pl.pallas_call(
        _merge_kernel, grid=(n // tm,),
        in_specs=[row(d)] + [row(WIDTH)] * 4 + [vec, pl.BlockSpec(w_gate.shape, lambda i: (0, 0)),
                                                 pl.BlockSpec(w_branch.shape, lambda i: (0, 0, 0)),
                                                 pl.BlockSpec(w_out.shape, lambda i: (0, 0)), vec],
        out_specs=row(d),
        out_shape=jax.ShapeDtypeStruct((n, d), F32),
        compiler_params=_params(("parallel",)), name="gated_merge",
    )(x2, *ys, g_pre.reshape(1, d), w_gate.astype(BF16), w_branch.astype(BF16), w_out.astype(BF16),
      g_post.reshape(1, d))


def _mlp_kernel(x_ref, gpre_ref, wu_ref, wd_ref, gpost_ref, o_ref, h_ref, acc_ref):
    kf = pl.program_id(1)

    @pl.when(kf == 0)
    def _():
        h_ref[...] = _norm_rows(x_ref[...], gpre_ref[...]).astype(BF16)
        acc_ref[...] = jnp.zeros_like(acc_ref)

    u = jnp.maximum(_dot(h_ref[...], wu_ref[...]), 0.0)
    acc_ref[...] += _dot((u * u).astype(BF16), wd_ref[...])

    @pl.when(kf == pl.num_programs(1) - 1)
    def _():
        o_ref[...] = x_ref[...] + _norm_rows(acc_ref[...], gpost_ref[...])


def _mlp(x2, g_pre, w_up, w_down, g_post, tm=1024, tf=1024):
    n, d = x2.shape
    dff = w_up.shape[1]
    return pl.pallas_call(
        _mlp_kernel, grid=(n // tm, dff // tf),
        in_specs=[pl.BlockSpec((tm, d), lambda i, k: (i, 0)), pl.BlockSpec((1, d), lambda i, k: (0, 0)),
                  pl.BlockSpec((d, tf), lambda i, k: (0, k)), pl.BlockSpec((tf, d), lambda i, k: (k, 0)),
                  pl.BlockSpec((1, d), lambda i, k: (0, 0))],
        out_specs=pl.BlockSpec((tm, d), lambda i, k: (i, 0)),
        out_shape=jax.ShapeDtypeStruct((n, d), F32),
        scratch_shapes=[pltpu.VMEM((tm, d), BF16), pltpu.VMEM((tm, d), F32)],
        compiler_params=_params(("parallel", "arbitrary")), name="mlp",
    )(x2, g_pre.reshape(1, d), w_up.astype(BF16), w_down.astype(BF16), g_post.reshape(1, d))


def _rope_tables(t, dim, groups):
    inv = 1.0 / (ROPE_THETA ** (jnp.arange(0, dim, 2, dtype=F32) / dim))
    ang = jnp.arange(t, dtype=F32)[:, None] * inv[None, :]
    cos, sin = jnp.cos(ang), jnp.sin(ang)
    return jnp.tile(jnp.concatenate([cos, cos], axis=1), (1, groups)), jnp.tile(jnp.concatenate([-sin, sin], axis=1), (1, groups))


def _pad_rows(w, rows, offset):
    return jnp.zeros((rows, w.shape[1]), w.dtype).at[offset:offset + w.shape[0]].set(w)


def _pad_cols(w, cols):
    return jnp.pad(w, ((0, 0), (0, cols - w.shape[1])))


def kernel(x, norm_mix_pre, norm_mix_post, norm_mlp_pre, norm_mlp_post, w_in, w_branch, w_out, hgrn_lb_logits, hgrn_norm_w, fox_f_bias, rwkv_mu, rwkv_w0, rwkv_w2, rwkv_a0, rwkv_a2, rwkv_g2, rwkv_k_k, rwkv_k_a, rwkv_r_k, rwkv_ln_w, rwkv_ln_b, rwkv_v0, rwkv_v1, rwkv_v2, w_up, w_down):
    bsz, t, d = x.shape
    n = bsz * t
    depth = w_in.shape[0]
    W = WIDTH
    topk = min(TOPK_MAX, t // 4)

    lb_soft = jax.nn.softmax(hgrn_lb_logits.astype(F32), axis=0)
    lower_bounds = jnp.cumsum(lb_soft, axis=0) - lb_soft[0:1]
    tables = _rope_tables(t, HEAD_DIM, N_HEADS) + _rope_tables(t, IDX_DIM, IDX_HEADS)

    o_a, o_b, o_iq = 0, 4 * W, 7 * W
    o_ik, o_iw = o_iq + IDX_HEADS * IDX_DIM, o_iq + IDX_HEADS * IDX_DIM + IDX_DIM
    o_c = o_iw + IDX_HEADS
    c_cols = 3 * W + 128
    o_d = o_c + c_cols
    o_df = o_d + 3 * W
    o_g = o_df + N_HEADS
    main_blocks = dict(b_qkv=4, c=2, df=(7 * W + c_cols) // LANES)
    assert main_blocks["c"] * c_cols == 7 * W

    x2 = x.reshape(n, d)
    v_first = None
    for l in range(depth):
        wl = w_in[l]
        w_main = jnp.concatenate([
            wl[:, o_a:o_a + 7 * W], wl[:, o_c:o_c + c_cols], _pad_cols(wl[:, o_df:o_df + N_HEADS], LANES),
            jnp.zeros((d, 256), F32)], axis=1)
        w_idx = jnp.concatenate([
            wl[:, o_iq:o_iq + W], jnp.tile(wl[:, o_ik:o_ik + IDX_DIM], (1, LANES // IDX_DIM)),
            _pad_cols(wl[:, o_iw:o_iw + IDX_HEADS], LANES)], axis=1)
        w_fox = jnp.concatenate([wl[:, o_d:o_d + W] * (LOG2E * HEAD_DIM ** -0.5), wl[:, o_d + W:o_d + 3 * W]], axis=1)

        main2, fox2, idx2 = _projections(x2, norm_mix_pre[l], w_main, w_fox, w_idx)
        main3 = main2.reshape(bsz, t, -1)

        y_a = _hgrn(main3, lower_bounds[l], hgrn_norm_w[l])

        q_b, k_b, v_b, q3, k3, wi = _dsa_prep(main3, idx2.reshape(bsz, t, -1), tables, main_blocks["b_qkv"])
        y_b = _dsa(q_b, k_b, jnp.transpose(v_b, (0, 2, 1)), q3, k3,
                   jnp.transpose(wi[:, :, :IDX_HEADS], (0, 2, 1)), topk)

        prm = dict(
            mu=_pad_cols(rwkv_mu[l][None, :], RWKV_COLS), w0=rwkv_w0[l][None, :],
            w2=_pad_rows(rwkv_w2[l], RWKV_LORA_LANES, 0), a0=rwkv_a0[l][None, :],
            a2=_pad_rows(rwkv_a2[l], RWKV_LORA_LANES, 32), g2=_pad_rows(rwkv_g2[l], RWKV_LORA_LANES, 64),
            k_k=rwkv_k_k[l][None, :], k_a=rwkv_k_a[l][None, :], r_k=rwkv_r_k[l].reshape(1, W),
            ln_w=rwkv_ln_w[l][None, :], ln_b=rwkv_ln_b[l][None, :])
        if l > 0:
            prm.update(v0=rwkv_v0[l - 1][None, :], v1=_pad_cols(rwkv_v1[l - 1], LANES),
                       v2=_pad_rows(rwkv_v2[l - 1], LANES, 0))
        y_c, v_c = _rwkv(main3, main_blocks["c"], v_first, prm)
        if l == 0:
            v_first = v_c

        c_rep = _fox_cum(main3, main_blocks["df"], _pad_cols(fox_f_bias[l][None, :], LANES))
        fox3 = fox2.reshape(bsz, t, 3 * W)
        y_d = _fox(fox3, jnp.transpose(fox3[:, :, 2 * W:], (0, 2, 1)), c_rep)

        ys = [y.reshape(n, W) for y in (y_a, y_b, y_c, y_d)]
        x2 = _merge(x2, ys, norm_mix_pre[l], wl[:, o_g:o_g + 4 * d], w_branch[l], w_out[l], norm_mix_post[l])
        x2 = _mlp(x2, norm_mlp_pre[l], w_up[l], w_down[l], norm_mlp_post[l])
    return x2.reshape(bsz, t, d)
```

```python
import functools

import jax
import jax.numpy as jnp
from jax import lax
from jax.experimental import pallas as pl
from jax.experimental.pallas import tpu as pltpu

F32 = jnp.float32
BF16 = jnp.bfloat16

HEAD_DIM = 64
N_HEADS = 4
WIDTH = N_HEADS * HEAD_DIM
IDX_HEADS = 8
IDX_DIM = 32
TOPK_MAX = 256
ROPE_THETA = 10000.0
NORM_EPS = 1e-6
RWKV_GN_EPS = 64e-5
MASK_VALUE = -1e30
LOG2E = 1.4426950408889634
RWKV_LORA_LANES = 128
RWKV_COLS = 3 * WIDTH + RWKV_LORA_LANES

LANES = 128
VMEM_LIMIT = 48 * 1024 * 1024

NN = ((1,), (0,))
NT = ((1,), (1,))
TN = ((0,), (0,))


def _dot(a, b, dims=NN):
    return lax.dot_general(a, b, (dims, ((), ())), preferred_element_type=F32)


def _split2(x):
    hi = x.astype(BF16)
    lo = (x - hi.astype(F32)).astype(BF16)
    return hi, lo


def _split3(x):
    hi = x.astype(BF16)
    r1 = x - hi.astype(F32)
    mid = r1.astype(BF16)
    lo = (r1 - mid.astype(F32)).astype(BF16)
    return hi, mid, lo


def _dot3(a, b, dims=NN):
    ah, al = _split2(a)
    bh, bl = _split2(b)
    return _dot(ah, bh, dims) + (_dot(ah, bl, dims) + _dot(al, bh, dims))


def _dot_exact_lhs(a_bf16, b, dims=NN):
    b1, b2, b3 = _split3(b)
    return _dot(a_bf16, b1, dims) + (_dot(a_bf16, b2, dims) + _dot(a_bf16, b3, dims))


def _dot_exact_rhs(a, b_bf16, dims=NN):
    a1, a2 = _split2(a)
    return _dot(a1, b_bf16, dims) + _dot(a2, b_bf16, dims)


def _iota(shape, dim):
    return lax.broadcasted_iota(jnp.int32, shape, dim)


def _head_block_ones(n, group):
    return (_iota((n, n), 0) // group == _iota((n, n), 1) // group).astype(BF16)


def _lower_tri_ones(n):
    return (_iota((n, n), 0) >= _iota((n, n), 1)).astype(BF16)


def _sigmoid(x):
    return 1.0 / (1.0 + jnp.exp(-x))


def _softplus(x):
    return jnp.maximum(x, 0.0) + jnp.log(1.0 + jnp.exp(-jnp.abs(x)))


def _params(sem):
    return pltpu.CompilerParams(dimension_semantics=sem, vmem_limit_bytes=VMEM_LIMIT)


def _norm_rows(x, g):
    return x * lax.rsqrt(jnp.mean(x * x, axis=-1, keepdims=True) + NORM_EPS) * g


def _projections_kernel(x_ref, g_ref, wm_ref, wf_ref, wih_ref, wil_ref, om_ref, of_ref, ovt_ref, oi_ref):
    h = _norm_rows(x_ref[...], g_ref[...])
    hh, hl = _split2(h)
    om_ref[...] = _dot(hh, wm_ref[...])
    fox = _dot(hh, wf_ref[...])
    of_ref[...] = fox[:, :2 * WIDTH].astype(of_ref.dtype)
    ovt_ref[0] = fox[:, 2 * WIDTH:].T.astype(ovt_ref.dtype)
    oi_ref[...] = _dot(hh, wih_ref[...]) + (_dot(hh, wil_ref[...]) + _dot(hl, wih_ref[...]))


def _projections(x2, t, g, w_main, w_fox, w_idx, tm=512):
    n, d = x2.shape
    tiles = t // tm
    wih, wil = _split2(w_idx)
    weights = [w_main.astype(BF16), w_fox.astype(BF16), wih, wil]
    rows = lambda c: pl.BlockSpec((tm, c), lambda i: (i, 0))
    return pl.pallas_call(
        _projections_kernel, grid=(n // tm,),
        in_specs=[rows(d), pl.BlockSpec((1, d), lambda i: (0, 0))]
        + [pl.BlockSpec(w.shape, lambda i: (0, 0)) for w in weights],
        out_specs=[rows(w_main.shape[1]), rows(2 * WIDTH),
                   pl.BlockSpec((1, WIDTH, tm), lambda i: (i // tiles, 0, i % tiles)), rows(w_idx.shape[1])],
        out_shape=[jax.ShapeDtypeStruct((n, w_main.shape[1]), F32), jax.ShapeDtypeStruct((n, 2 * WIDTH), BF16),
                   jax.ShapeDtypeStruct((n // t, WIDTH, t), BF16), jax.ShapeDtypeStruct((n, w_idx.shape[1]), F32)],
        compiler_params=_params(("parallel",)), name="norm_proj",
    )(x2, g.reshape(1, d), *weights)


HGRN_CHUNK = 64
HGRN_GROUP = 8


def _dot1(a, b, dims=NN):
    return _dot(a.astype(BF16), b.astype(BF16), dims)


def _hgrn_kernel(q_ref, f_ref, i_ref, g_ref, lb_ref, nw_ref, o_ref, st_ref, *, n_chunks):
    C, G = HGRN_CHUNK, HGRN_GROUP

    @pl.when(pl.program_id(1) == 0)
    def _():
        st_ref[...] = jnp.zeros_like(st_ref)

    lb = lb_ref[...]
    nw = nw_ref[...]
    tri = _lower_tri_ones(C)
    bd = _head_block_ones(WIDTH, HEAD_DIM)
    bd_mask = _iota((WIDTH, WIDTH), 0) // HEAD_DIM == _iota((WIDTH, WIDTH), 1) // HEAD_DIM
    row_top = _iota((G, 1), 0)
    lane_head = _iota((1, WIDTH), 1) // HEAD_DIM

    def chunk(c, carry):
        sl = pl.ds(pl.multiple_of(c * C, C), C)
        fl = f_ref[0, sl, :]
        f = lb + (1.0 - lb) * _sigmoid(fl)
        k = (1.0 - lb) * _sigmoid(-fl)
        b = _dot_exact_lhs(tri, jnp.log(f))
        q = q_ref[0, sl, :] * (HEAD_DIM ** -0.5)
        v = i_ref[0, sl, :]
        st = st_ref[...]
        o_inter = _dot1(q * jnp.exp(b), st, NT)

        n_groups = C // G
        o_rows = [o_inter[g * G:(g + 1) * G] for g in range(n_groups)]
        pieces = []
        for g in range(n_groups):
            s0 = g * G
            q_top, b_top, k_top = q[s0:s0 + G], b[s0:s0 + G], k[s0:s0 + G]
            for u in range(G):
                top = q_top * k_top[u:u + 1] * jnp.exp(jnp.minimum(b_top - b_top[u:u + 1], 0.0))
                pieces.append(jnp.where(row_top >= u, top, 0.0))
        r = _dot(jnp.concatenate(pieces, axis=0).astype(BF16), bd)
        att, v_st = [], []
        for g in range(n_groups - 1):
            s0 = g * G
            b_top, k_top, v_top = b[s0:s0 + G], k[s0:s0 + G], v[s0:s0 + G]
            b_ref = b_top[G - 1:G]
            qp = (q[s0 + G:] * jnp.exp(b[s0 + G:] - b_ref)).astype(BF16)
            kp = k_top * jnp.exp(b_ref - b_top)
            k_st = jnp.concatenate([jnp.where(lane_head == h, kp, 0.0) for h in range(N_HEADS)], axis=0)
            v_st.append(jnp.concatenate([jnp.where(lane_head == h, v_top, 0.0) for h in range(N_HEADS)], axis=0))
            att.append(_dot(qp, k_st.astype(BF16), NT))
        for g in range(n_groups):
            v_top = v[g * G:(g + 1) * G]
            for u in range(G):
                o_rows[g] = o_rows[g] + r[(g * G + u) * G:(g * G + u + 1) * G] * v_top[u:u + 1]
        for g in range(n_groups - 1):
            rest = _dot(att[g].astype(BF16), v_st[g].astype(BF16))
            for g2 in range(g + 1, n_groups):
                o_rows[g2] = o_rows[g2] + rest[(g2 - g - 1) * G:(g2 - g) * G]
        o = jnp.concatenate(o_rows, axis=0)

        b_last = b[C - 1:C, :]
        upd = _dot1(v, k * jnp.exp(b_last - b), TN)
        st_ref[...] = st * jnp.exp(b_last) + jnp.where(bd_mask, upd, 0.0)

        ms = _dot_exact_rhs(o * o, bd) * (1.0 / HEAD_DIM)
        on = o * lax.rsqrt(ms + NORM_EPS) * nw
        gl = g_ref[0, sl, :]
        o_ref[0, sl, :] = (on * (gl * _sigmoid(gl))).astype(o_ref.dtype)
        return carry

    lax.fori_loop(0, n_chunks, chunk, 0)


def _hgrn(main3, lb, norm_w, tb=256):
    bsz, t, _ = main3.shape
    col = lambda j: pl.BlockSpec((1, tb, WIDTH), lambda b, i, j=j: (b, i, j))
    vec = pl.BlockSpec((1, WIDTH), lambda b, i: (0, 0))
    return pl.pallas_call(
        functools.partial(_hgrn_kernel, n_chunks=tb // HGRN_CHUNK),
        grid=(bsz, t // tb),
        in_specs=[col(0), col(1), col(2), col(3), vec, vec],
        out_specs=pl.BlockSpec((1, tb, WIDTH), lambda b, i: (b, i, 0)),
        out_shape=jax.ShapeDtypeStruct((bsz, t, WIDTH), BF16),
        scratch_shapes=[pltpu.VMEM((WIDTH, WIDTH), F32)],
        compiler_params=_params(("parallel", "arbitrary")), name="hgrn2",
    )(main3, main3, main3, main3, lb.reshape(1, WIDTH), jnp.tile(norm_w, N_HEADS).reshape(1, WIDTH))


RWKV_CHUNK = 64


def _rwkv_kernel(*refs, n_chunks, has_vgate):
    if has_vgate:
        (c_ref, vf_ref, mu_ref, w0_ref, w2_ref, a0_ref, a2_ref, g2_ref, kkw_ref, kaw_ref, rk_ref,
         lnw_ref, lnb_ref, v0_ref, v1_ref, v2_ref, y_ref, vout_ref, s_ref, prev_ref, y_sc) = refs
    else:
        (c_ref, mu_ref, w0_ref, w2_ref, a0_ref, a2_ref, g2_ref, kkw_ref, kaw_ref, rk_ref,
         lnw_ref, lnb_ref, y_ref, vout_ref, s_ref, prev_ref, y_sc) = refs
    C = RWKV_CHUNK
    W = WIDTH
    tb = n_chunks * C

    @pl.when(pl.program_id(1) == 0)
    def _():
        s_ref[...] = jnp.zeros_like(s_ref)
        prev_ref[...] = jnp.zeros_like(prev_ref)

    p = c_ref[0]
    shifted = jnp.where(_iota((tb, 1), 0) == 0, prev_ref[...], pltpu.roll(p, 1, axis=0))
    prev_ref[...] = p[tb - 1:tb, :]
    xm = p + (shifted - p) * mu_ref[...]
    r = xm[:, 0:W]
    k = xm[:, W:2 * W]
    v = xm[:, 2 * W:3 * W]
    lora = xm[:, 3 * W:3 * W + RWKV_LORA_LANES]

    bd = _head_block_ones(W, HEAD_DIM)
    w_log = -_softplus(-(w0_ref[...] + _dot3(jnp.tanh(lora), w2_ref[...]))) - 0.5
    log_decay = -jnp.exp(w_log)
    a = _sigmoid(a0_ref[...] + _dot3(lora, a2_ref[...]))
    g = _dot3(_sigmoid(lora), g2_ref[...])
    if has_vgate:
        vg = _dot3(_dot3(v, v1_ref[...]), v2_ref[...])
        v = v + (vf_ref[0] - v) * _sigmoid(v0_ref[...] + vg)
    vout_ref[0] = v
    kk = k * kkw_ref[...]
    kk = kk * lax.rsqrt(jnp.maximum(_dot_exact_rhs(kk * kk, bd), 1e-24))
    k = k * (1.0 + (a - 1.0) * kaw_ref[...])

    tri = _lower_tri_ones(C)
    lane_head = _iota((1, W), 1) // HEAD_DIM
    hc = N_HEADS * C
    rt = _iota((hc, hc), 0) % C
    ct = _iota((hc, hc), 1) % C
    strict_lower = rt > ct
    lower = rt >= ct
    eye = (_iota((hc, hc), 0) == _iota((hc, hc), 1)).astype(F32)

    def stack(m):
        return jnp.concatenate([jnp.where(lane_head == h, m, 0.0) for h in range(N_HEADS)], axis=0)

    chunks = range(n_chunks)
    a_s, b_s, k_s, r_s, v_s, p_last, m, lak, lrb, lrk = ([None] * n_chunks for _ in range(10))
    for c in chunks:
        sl = slice(c * C, (c + 1) * C)
        ld = log_decay[sl]
        cum = _dot_exact_lhs(tri, ld)
        pdec = jnp.exp(cum)
        pinv = jnp.exp(-cum)
        p_last[c] = pdec[C - 1:C, :]
        a_s[c] = stack(-kk[sl] * jnp.exp(cum - ld)).astype(BF16)
        b_s[c] = stack(kk[sl] * a[sl] * pinv).astype(BF16)
        k_s[c] = stack(k[sl] * pinv).astype(BF16)
        r_s[c] = stack(r[sl] * pdec)
        v_s[c] = stack(v[sl]).astype(BF16)
        gram = _dot(jnp.concatenate([a_s[c], r_s[c].astype(BF16)], axis=0),
                    jnp.concatenate([b_s[c], k_s[c]], axis=0), NT)
        m[c] = jnp.where(strict_lower, gram[:hc, :hc], 0.0)
        lak[c] = jnp.where(strict_lower, gram[:hc, hc:], 0.0)
        lrb[c] = jnp.where(lower, gram[hc:, :hc], 0.0)
        lrk[c] = jnp.where(lower, gram[hc:, hc:], 0.0)

    x = [eye + m[c] for c in chunks]
    power = 2
    while power < C:
        m = [_dot1(m[c], m[c]) for c in chunks]
        x = [x[c] + _dot1(x[c], m[c]) for c in chunks]
        power *= 2
    lakv = [_dot1(lak[c], v_s[c]) for c in chunks]
    lrkv = [_dot1(lrk[c], v_s[c]) for c in chunks]
    w1 = [_dot1(x[c], lakv[c]) for c in chunks]
    a2 = [_dot1(x[c], a_s[c]) for c in chunks]
    r2 = [r_s[c] + _dot1(lrb[c], a2[c]) for c in chunks]
    y0 = [_dot1(lrb[c], w1[c]) + lrkv[c] for c in chunks]
    t1 = [_dot1(a2[c], b_s[c], TN) for c in chunks]
    t0 = [_dot1(w1[c], b_s[c], TN) + _dot(v_s[c], k_s[c], TN) for c in chunks]

    s = s_ref[...]
    for c in chunks:
        y = _dot1(r2[c], s, NT) + y0[c]
        y_sc[c * C:(c + 1) * C, :] = sum(y[h * C:(h + 1) * C] for h in range(N_HEADS))
        s = (s + _dot1(s, t1[c]) + t0[c]) * p_last[c]
    s_ref[...] = s

    y = y_sc[...]
    inv_n = 1.0 / HEAD_DIM
    mean = _dot_exact_rhs(y, bd) * inv_n
    yc = y - mean
    var = _dot_exact_rhs(yc * yc, bd) * inv_n
    yn = yc * lax.rsqrt(var + RWKV_GN_EPS) * lnw_ref[...] + lnb_ref[...]
    yn = yn + _dot_exact_rhs(r * k * rk_ref[...], bd) * v
    y_ref[0] = (yn * g).astype(y_ref.dtype)


def _rwkv(c3, col_block, v_first, prm, tb=512):
    bsz, t, _ = c3.shape
    has_vgate = v_first is not None
    blk = lambda w: pl.BlockSpec((1, tb, w), lambda b, i: (b, i, 0))
    cblk = pl.BlockSpec((1, tb, RWKV_COLS), lambda b, i: (b, i, col_block))
    full = lambda a: pl.BlockSpec(a.shape, lambda b, i: (0,) * a.ndim)
    names = ["mu", "w0", "w2", "a0", "a2", "g2", "k_k", "k_a", "r_k", "ln_w", "ln_b"]
    if has_vgate:
        names += ["v0", "v1", "v2"]
    args = [c3] + ([v_first] if has_vgate else []) + [prm[n] for n in names]
    in_specs = [cblk] + ([blk(WIDTH)] if has_vgate else []) + [full(prm[n]) for n in names]
    return pl.pallas_call(
        functools.partial(_rwkv_kernel, n_chunks=tb // RWKV_CHUNK, has_vgate=has_vgate),
        grid=(bsz, t // tb),
        in_specs=in_specs,
        out_specs=[blk(WIDTH), blk(WIDTH)],
        out_shape=[jax.ShapeDtypeStruct((bsz, t, WIDTH), BF16), jax.ShapeDtypeStruct((bsz, t, WIDTH), F32)],
        scratch_shapes=[pltpu.VMEM((WIDTH, WIDTH), F32), pltpu.VMEM((1, RWKV_COLS), F32),
                        pltpu.VMEM((tb, WIDTH), F32)],
        compiler_params=_params(("parallel", "arbitrary")), name="rwkv7",
    )(*args)


def _fox_cum_kernel(f_ref, bias_ref, c_ref, carry_ref, *, tb):
    @pl.when(pl.program_id(1) == 0)
    def _():
        carry_ref[...] = jnp.zeros_like(carry_ref)

    logf = -_softplus(-(f_ref[0] + bias_ref[...]))
    c = _dot_exact_lhs(_lower_tri_ones(tb), logf) + carry_ref[...]
    carry_ref[...] = c[tb - 1:tb, :]
    for h in range(N_HEADS):
        pick = (_iota((LANES, LANES), 0) == h).astype(BF16)
        c1, c2, c3 = _split3(c)
        c_ref[0, h] = LOG2E * (_dot(c1, pick) + (_dot(c2, pick) + _dot(c3, pick)))


def _fox_cum(main3, col_block, bias_row, tb=512):
    bsz, t, _ = main3.shape
    return pl.pallas_call(
        functools.partial(_fox_cum_kernel, tb=tb),
        grid=(bsz, t // tb),
        in_specs=[pl.BlockSpec((1, tb, LANES), lambda b, i: (b, i, col_block)),
                  pl.BlockSpec((1, LANES), lambda b, i: (0, 0))],
        out_specs=pl.BlockSpec((1, N_HEADS, tb, LANES), lambda b, i: (b, 0, i, 0)),
        out_shape=jax.ShapeDtypeStruct((bsz, N_HEADS, t, LANES), F32),
        scratch_shapes=[pltpu.VMEM((1, LANES), F32)],
        compiler_params=_params(("parallel", "arbitrary")), name="fox_cumgate",
    )(main3, bias_row)


ATT_QB = 512
ATT_KB = 512


def _stack_heads(q):
    lane_head = _iota((1, WIDTH), 1) // HEAD_DIM
    zero = jnp.zeros_like(q)
    return jnp.concatenate([jnp.where(lane_head == h, q, zero) for h in range(N_HEADS)], axis=0)


def _softmax_stage_t(s_t, m_sc, l_sc):
    m_old = m_sc[...]
    m_new = jnp.maximum(m_old, jnp.max(s_t, axis=0, keepdims=True))
    alpha = jnp.exp2(m_old - m_new)
    p_t = jnp.exp2(s_t - m_new)
    l_sc[...] = alpha * l_sc[...] + jnp.sum(p_t, axis=0, keepdims=True)
    m_sc[...] = m_new
    return p_t.astype(BF16), alpha


def _pv_stage_t(vt_blk, p_t, alpha, acc_sc):
    for h in range(N_HEADS):
        cs = slice(h * ATT_QB, (h + 1) * ATT_QB)
        pv = _dot(vt_blk[h * HEAD_DIM:(h + 1) * HEAD_DIM, :], p_t[:, cs])
        acc_sc[h] = alpha[:, cs] * acc_sc[h] + pv


def _attention_t(n_blocks, qk, post, post_last, vt_ref, bufs, m_sc, l_sc, acc_sc):
    KB = ATT_KB
    s_buf, p_buf, a_buf = bufs[0:2], bufs[2:4], bufs[4:6]
    _softmax_init_t(m_sc, l_sc, acc_sc)

    def vt_blk(j):
        return vt_ref[0, :, pl.ds(pl.multiple_of(j * KB, KB), KB)]

    def pending(j, par):
        _pv_stage_t(vt_blk(jnp.maximum(j - 1, 0)), p_buf[1 - par][...], a_buf[1 - par][...], acc_sc)

    def step(j, par):
        raw = s_buf[par][...]
        s_buf[1 - par][...] = qk(j + 1)
        p_t, alpha = _softmax_stage_t(post(j, raw), m_sc, l_sc)
        pending(j, par)
        p_buf[par][...] = p_t
        a_buf[par][...] = alpha

    def final(j, par):
        p_t, alpha = _softmax_stage_t(post_last(j, s_buf[par][...]), m_sc, l_sc)
        pending(j, par)
        _pv_stage_t(vt_blk(j), p_t, alpha, acc_sc)

    s_buf[0][...] = qk(0)
    p_buf[1][...] = jnp.zeros_like(p_buf[1])
    a_buf[1][...] = jnp.ones_like(a_buf[1])
    pairs = (n_blocks - 1) // 2

    def pair(t, carry):
        step(2 * t, 0)
        step(2 * t + 1, 1)
        return carry

    lax.fori_loop(0, pairs, pair, 0)
    odd_left = (n_blocks - 1) % 2 == 1

    @pl.when(odd_left)
    def _():
        step(2 * pairs, 0)
        final(2 * pairs + 1, 1)

    @pl.when(jnp.logical_not(odd_left))
    def _():
        final(2 * pairs, 0)


def _attention_bufs():
    cols = N_HEADS * ATT_QB
    return ([pltpu.VMEM((ATT_KB, cols), F32)] * 2 + [pltpu.VMEM((ATT_KB, cols), BF16)] * 2
            + [pltpu.VMEM((1, cols), F32)] * 2)


def _softmax_init_t(m_sc, l_sc, acc_sc):
    m_sc[...] = jnp.full_like(m_sc, MASK_VALUE)
    l_sc[...] = jnp.zeros_like(l_sc)
    acc_sc[...] = jnp.zeros_like(acc_sc)


def _softmax_finish_t(l_sc, acc_sc):
    l = l_sc[...]
    out_t = jnp.concatenate([acc_sc[h] / l[:, h * ATT_QB:(h + 1) * ATT_QB] for h in range(N_HEADS)], axis=0)
    return out_t.T


def _fox_kernel(q_ref, k_ref, vt_ref, ck_ref, o_ref, m_sc, l_sc, acc_sc, *bufs):
    QB, KB = ATT_QB, ATT_KB
    i = pl.program_id(1)
    qstack = _stack_heads(q_ref[0])
    qpos = i * QB + _iota((1, N_HEADS * QB), 1) % QB

    def qk(j):
        return _dot(k_ref[0, pl.ds(pl.multiple_of(j * KB, KB), KB), :], qstack, NT)

    def post(j, s_t):
        ks = pl.ds(pl.multiple_of(j * KB, KB), KB)
        rep = lambda c: jnp.concatenate([c] * (QB // LANES), axis=1)
        return jnp.concatenate([s_t[:, h * QB:(h + 1) * QB] - rep(ck_ref[0, h, ks, :]) for h in range(N_HEADS)], axis=1)

    def post_last(j, s_t):
        return jnp.where(j * KB + _iota((KB, 1), 0) <= qpos, post(j, s_t), MASK_VALUE)

    _attention_t((i * QB) // KB + 1, qk, post, post_last, vt_ref, bufs, m_sc, l_sc, acc_sc)
    o_ref[0] = _softmax_finish_t(l_sc, acc_sc).astype(o_ref.dtype)


def _fox(qk3, v_t, c_rep):
    bsz, t, _ = qk3.shape
    cols = N_HEADS * ATT_QB
    return pl.pallas_call(
        _fox_kernel,
        grid=(bsz, t // ATT_QB),
        in_specs=[pl.BlockSpec((1, ATT_QB, WIDTH), lambda b, i: (b, i, 0)),
                  pl.BlockSpec((1, t, WIDTH), lambda b, i: (b, 0, 1)),
                  pl.BlockSpec((1, WIDTH, t), lambda b, i: (b, 0, 0)),
                  pl.BlockSpec((1, N_HEADS, t, LANES), lambda b, i: (b, 0, 0, 0))],
        out_specs=pl.BlockSpec((1, ATT_QB, WIDTH), lambda b, i: (b, i, 0)),
        out_shape=jax.ShapeDtypeStruct((bsz, t, WIDTH), BF16),
        scratch_shapes=[pltpu.VMEM((1, cols), F32), pltpu.VMEM((1, cols), F32),
                        pltpu.VMEM((N_HEADS, HEAD_DIM, ATT_QB), F32)] + _attention_bufs(),
        compiler_params=_params(("parallel", "arbitrary")), name="fox_attention",
    )(qk3, qk3, v_t, c_rep)


def _swap_halves(x, half):
    n = x.shape[-1]
    lower = (_iota((1, n), 1) % (2 * half)) < half
    return jnp.where(lower, pltpu.roll(x, n - half, axis=1), pltpu.roll(x, half, axis=1))


def _dsa_prep_kernel(q_ref, k_ref, v_ref, iq_ref, ik_ref, iw_ref, ch_ref, sh_ref, ci_ref, si_ref,
                     selh_ref, sell_ref, qo_ref, ko_ref, vo_ref, q3_ref, k3_ref, wi_ref):
    ch, sh, ci, si = ch_ref[...], sh_ref[...], ci_ref[...], si_ref[...]

    def rope(x, c, s, half):
        return x * c + _swap_halves(x, half) * s

    qo_ref[0] = (rope(q_ref[0], ch, sh, HEAD_DIM // 2) * (LOG2E * HEAD_DIM ** -0.5)).astype(BF16)
    ko_ref[0] = rope(k_ref[0], ch, sh, HEAD_DIM // 2).astype(BF16)
    vo_ref[0] = v_ref[0].T.astype(BF16)
    qh, ql = _split2(rope(iq_ref[0], ci, si, IDX_DIM // 2) * (IDX_DIM ** -0.5))
    q3_ref[0] = (_dot(qh, selh_ref[...]) + _dot(ql, sell_ref[...])).astype(BF16)
    kh, kl = _split2(rope(ik_ref[0], ci[:, :LANES], si[:, :LANES], IDX_DIM // 2))
    seg = _iota((1, LANES), 1) // IDX_DIM
    k3_ref[0] = jnp.where(seg < 2, kh, jnp.where(seg == 2, kl, jnp.zeros_like(kl)))
    wi_ref[0] = (iw_ref[0] * (IDX_HEADS ** -0.5)).T


def _dsa_prep(main3, idx3, tables, qkv_block0, tb=512):
    bsz, t, _ = main3.shape
    iq_w = IDX_HEADS * IDX_DIM
    r = jnp.arange(iq_w)[:, None]
    c = jnp.arange(IDX_HEADS * LANES)[None, :]
    same = (r // IDX_DIM == c // LANES) & (r % IDX_DIM == c % IDX_DIM)
    seg = (c % LANES) // IDX_DIM
    sel_hi = (same & ((seg == 0) | (seg == 2))).astype(BF16)
    sel_lo = (same & (seg == 1)).astype(BF16)
    mcol = lambda j: pl.BlockSpec((1, tb, WIDTH), lambda b, i, j=j: (b, i, qkv_block0 + j))
    tab = pl.BlockSpec((tb, WIDTH), lambda b, i: (i, 0))
    sel = pl.BlockSpec(sel_hi.shape, lambda b, i: (0, 0))
    out = lambda w: pl.BlockSpec((1, tb, w), lambda b, i: (b, i, 0))
    out_t = lambda w: pl.BlockSpec((1, w, tb), lambda b, i: (b, 0, i))
    shape = lambda dt, w: jax.ShapeDtypeStruct((bsz, t, w), dt)
    return pl.pallas_call(
        _dsa_prep_kernel, grid=(bsz, t // tb),
        in_specs=[mcol(0), mcol(1), mcol(2), out(iq_w),
                  pl.BlockSpec((1, tb, LANES), lambda b, i: (b, i, iq_w // LANES)),
                  pl.BlockSpec((1, tb, LANES), lambda b, i: (b, i, iq_w // LANES + 1)),
                  tab, tab, tab, tab, sel, sel],
        out_specs=[out(WIDTH), out(WIDTH), out_t(WIDTH), out(IDX_HEADS * LANES), out(LANES), out_t(LANES)],
        out_shape=[shape(BF16, WIDTH), shape(BF16, WIDTH), jax.ShapeDtypeStruct((bsz, WIDTH, t), BF16),
                   shape(BF16, IDX_HEADS * LANES), shape(BF16, LANES), jax.ShapeDtypeStruct((bsz, LANES, t), F32)],
        compiler_params=_params(("parallel", "parallel")), name="dsa_prep",
    )(main3, main3, main3, idx3, idx3, idx3, *tables, sel_hi, sel_lo)


def _dsa_kernel(q_ref, k_ref, vt_ref, q3_ref, k3_ref, wt_ref, o_ref, key_sc, tie_sc, m_sc, l_sc, acc_sc,
                *bufs, topk):
    QB, KB = ATT_QB, ATT_KB
    i = pl.program_id(1)
    nkb = (i * QB) // KB + 1
    qpos = i * QB + _iota((1, QB), 1)
    int_min = jnp.int32(-2 ** 31)

    q3 = q3_ref[0]
    qs = jnp.concatenate([q3[:, h * LANES:(h + 1) * LANES] for h in range(IDX_HEADS)], axis=0)
    wt = wt_ref[0]

    def score_block(j, carry):
        ks = pl.ds(pl.multiple_of(j * KB, KB), KB)
        r = _dot(k3_ref[0, ks, :], qs, NT)
        sc = jnp.zeros((KB, QB), F32)
        for h in range(IDX_HEADS):
            sc = sc + jnp.maximum(r[:, h * QB:(h + 1) * QB], 0.0) * wt[h:h + 1, :]
        sc = jnp.where(j * KB + _iota((KB, 1), 0) <= qpos, sc, MASK_VALUE)
        sc = jnp.where(sc == 0.0, 0.0, sc)
        bits = pltpu.bitcast(sc, jnp.int32)
        key_sc[ks, :] = jnp.where(bits < 0, bits ^ jnp.int32(0x7FFFFFFF), bits)
        return carry

    lax.fori_loop(0, nkb, score_block, 0)

    def count_ge(cand):
        def blk(j, acc):
            base = pl.multiple_of(j * KB, KB)
            for r0 in range(0, KB, 64):
                acc = acc + jnp.where(key_sc[pl.ds(base + r0, 64), :] >= cand, 1.0, 0.0)
            return acc
        acc = lax.fori_loop(0, nkb, blk, jnp.zeros((64, QB), F32))
        return jnp.sum(acc, axis=0, keepdims=True)

    kf = float(topk)

    def bit_step(step, v):
        trial = jnp.where(step == 0, jnp.zeros_like(v), v | (jnp.int32(1) << (31 - step)))
        return jnp.where(count_ge(trial) >= kf, trial, v)

    thr = lax.fori_loop(0, 32, bit_step, jnp.full((1, QB), int_min, jnp.int32))
    need = kf - count_ge(thr + 1)

    lt = _lower_tri_ones(KB)
    qstack = _stack_heads(q_ref[0])

    def qk(j):
        return _dot(k_ref[0, pl.ds(pl.multiple_of(j * KB, KB), KB), :], qstack, NT)

    def post(j, s_t):
        ks = pl.ds(pl.multiple_of(j * KB, KB), KB)
        key = key_sc[ks, :]
        tie = key == thr
        tie_b = jnp.where(tie, 1.0, 0.0).astype(BF16)
        rank = tie_sc[...] + _dot(lt, tie_b)
        tie_sc[...] = rank[KB - 1:KB, :]
        sel = (key > thr) | (tie & (rank <= need))
        sel = sel & (j * KB + _iota((KB, 1), 0) <= qpos)
        bias = jnp.where(sel, 0.0, MASK_VALUE)
        return s_t + jnp.concatenate([bias] * N_HEADS, axis=1)

    tie_sc[...] = jnp.zeros_like(tie_sc)
    _attention_t(nkb, qk, post, post, vt_ref, bufs, m_sc, l_sc, acc_sc)
    o_ref[0] = _softmax_finish_t(l_sc, acc_sc).astype(o_ref.dtype)


def _dsa(q, k, v_t, q3, k3, w_t, topk):
    bsz, t, _ = q.shape
    qblk = lambda w: pl.BlockSpec((1, ATT_QB, w), lambda b, i: (b, i, 0))
    seq = lambda w: pl.BlockSpec((1, t, w), lambda b, i: (b, 0, 0))
    cols = N_HEADS * ATT_QB
    return pl.pallas_call(
        functools.partial(_dsa_kernel, topk=topk),
        grid=(bsz, t // ATT_QB),
        in_specs=[qblk(WIDTH), seq(WIDTH), pl.BlockSpec((1, WIDTH, t), lambda b, i: (b, 0, 0)),
                  qblk(IDX_HEADS * LANES), seq(LANES),
                  pl.BlockSpec((1, IDX_HEADS, ATT_QB), lambda b, i: (b, 0, i))],
        out_specs=qblk(WIDTH),
        out_shape=jax.ShapeDtypeStruct((bsz, t, WIDTH), BF16),
        scratch_shapes=[pltpu.VMEM((t, ATT_QB), jnp.int32), pltpu.VMEM((1, ATT_QB), F32), pltpu.VMEM((1, cols), F32),
                        pltpu.VMEM((1, cols), F32), pltpu.VMEM((N_HEADS, HEAD_DIM, ATT_QB), F32)] + _attention_bufs(),
        compiler_params=_params(("parallel", "arbitrary")), name="dsa_attention",
    )(q, k, v_t, q3, k3, w_t)


def _merge_kernel(x_ref, ya_ref, yb_ref, yc_ref, yd_ref, gpre_ref, wg_ref, wb_ref, wo_ref, g_ref, o_ref):
    d = x_ref.shape[-1]
    x = x_ref[...]
    h = _norm_rows(x, gpre_ref[...]).astype(BF16)
    merged = None
    for n, y_ref in enumerate((ya_ref, yb_ref, yc_ref, yd_ref)):
        gate = _sigmoid(_dot(h, wg_ref[:, n * d:(n + 1) * d]))
        term = gate * _dot(y_ref[...], wb_ref[n])
        merged = term if merged is None else merged + term
    mix = _dot(merged.astype(BF16), wo_ref[...])
    o_ref[...] = x + _norm_rows(mix, g_ref[...])


def _merge(x2, ys, g_pre, w_gate, w_branch, w_out, g_post, tm=256):
    n, d = x2.shape
    row = lambda w: pl.BlockSpec((tm, w), lambda i: (i, 0))
    vec = pl.BlockSpec((1, d), lambda i: (0, 0))
    return pl.pallas_call(
        _merge_kernel, grid=(n // tm,),
        in_specs=[row(d)] + [row(WIDTH)] * 4 + [vec, pl.BlockSpec(w_gate.shape, lambda i: (0, 0)),
                                                 pl.BlockSpec(w_branch.shape, lambda i: (0, 0, 0)),
                                                 pl.BlockSpec(w_out.shape, lambda i: (0, 0)), vec],
        out_specs=row(d),
        out_shape=jax.ShapeDtypeStruct((n, d), F32),
        compiler_params=_params(("parallel",)), name="gated_merge",
    )(x2, *ys, g_pre.reshape(1, d), w_gate.astype(BF16), w_branch.astype(BF16), w_out.astype(BF16),
      g_post.reshape(1, d))


def _mlp_kernel(x_ref, gpre_ref, wu_ref, wd_ref, gpost_ref, o_ref, h_ref, acc_ref):
    kf = pl.program_id(1)

    @pl.when(kf == 0)
    def _():
        h_ref[...] = _norm_rows(x_ref[...], gpre_ref[...]).astype(BF16)
        acc_ref[...] = jnp.zeros_like(acc_ref)

    u = jnp.maximum(_dot(h_ref[...], wu_ref[...]), 0.0)
    acc_ref[...] += _dot((u * u).astype(BF16), wd_ref[...])

    @pl.when(kf == pl.num_programs(1) - 1)
    def _():
        o_ref[...] = x_ref[...] + _norm_rows(acc_ref[...], gpost_ref[...])


def _mlp(x2, g_pre, w_up, w_down, g_post, tm=1024, tf=1024):
    n, d = x2.shape
    dff = w_up.shape[1]
    return pl.pallas_call(
        _mlp_kernel, grid=(n // tm, dff // tf),
        in_specs=[pl.BlockSpec((tm, d), lambda i, k: (i, 0)), pl.BlockSpec((1, d), lambda i, k: (0, 0)),
                  pl.BlockSpec((d, tf), lambda i, k: (0, k)), pl.BlockSpec((tf, d), lambda i, k: (k, 0)),
                  pl.BlockSpec((1, d), lambda i, k: (0, 0))],
        out_specs=pl.BlockSpec((tm, d), lambda i, k: (i, 0)),
        out_shape=jax.ShapeDtypeStruct((n, d), F32),
        scratch_shapes=[pltpu.VMEM((tm, d), BF16), pltpu.VMEM((tm, d), F32)],
        compiler_params=_params(("parallel", "arbitrary")), name="mlp",
    )(x2, g_pre.reshape(1, d), w_up.astype(BF16), w_down.astype(BF16), g_post.reshape(1, d))


def _rope_tables(t, dim, groups):
    inv = 1.0 / (ROPE_THETA ** (jnp.arange(0, dim, 2, dtype=F32) / dim))
    ang = jnp.arange(t, dtype=F32)[:, None] * inv[None, :]
    cos, sin = jnp.cos(ang), jnp.sin(ang)
    return jnp.tile(jnp.concatenate([cos, cos], axis=1), (1, groups)), jnp.tile(jnp.concatenate([-sin, sin], axis=1), (1, groups))


def _pad_rows(w, rows, offset):
    return jnp.zeros((rows, w.shape[1]), w.dtype).at[offset:offset + w.shape[0]].set(w)


def _pad_cols(w, cols):
    return jnp.pad(w, ((0, 0), (0, cols - w.shape[1])))


def kernel(x, norm_mix_pre, norm_mix_post, norm_mlp_pre, norm_mlp_post, w_in, w_branch, w_out, hgrn_lb_logits, hgrn_norm_w, fox_f_bias, rwkv_mu, rwkv_w0, rwkv_w2, rwkv_a0, rwkv_a2, rwkv_g2, rwkv_k_k, rwkv_k_a, rwkv_r_k, rwkv_ln_w, rwkv_ln_b, rwkv_v0, rwkv_v1, rwkv_v2, w_up, w_down):
    bsz, t, d = x.shape
    n = bsz * t
    depth = w_in.shape[0]
    W = WIDTH
    topk = min(TOPK_MAX, t // 4)

    lb_soft = jax.nn.softmax(hgrn_lb_logits.astype(F32), axis=0)
    lower_bounds = jnp.cumsum(lb_soft, axis=0) - lb_soft[0:1]
    tables = _rope_tables(t, HEAD_DIM, N_HEADS) + _rope_tables(t, IDX_DIM, IDX_HEADS)

    o_a, o_b, o_iq = 0, 4 * W, 7 * W
    o_ik, o_iw = o_iq + IDX_HEADS * IDX_DIM, o_iq + IDX_HEADS * IDX_DIM + IDX_DIM
    o_c = o_iw + IDX_HEADS
    c_cols = 3 * W + 128
    o_d = o_c + c_cols
    o_df = o_d + 3 * W
    o_g = o_df + N_HEADS
    main_blocks = dict(b_qkv=4, c=2, df=(7 * W + c_cols) // LANES)
    assert main_blocks["c"] * c_cols == 7 * W

    x2 = x.reshape(n, d)
    w_in_bf16 = w_in.astype(BF16)
    v_first = None
    for l in range(depth):
        wl, wlb = w_in[l], w_in_bf16[l]
        w_main = jnp.concatenate([
            wlb[:, o_a:o_a + 7 * W], wlb[:, o_c:o_c + c_cols], _pad_cols(wlb[:, o_df:o_df + N_HEADS], LANES),
            jnp.zeros((d, 256), BF16)], axis=1)
        w_idx = jnp.concatenate([
            wl[:, o_iq:o_iq + W], jnp.tile(wl[:, o_ik:o_ik + IDX_DIM], (1, LANES // IDX_DIM)),
            _pad_cols(wl[:, o_iw:o_iw + IDX_HEADS], LANES)], axis=1)
        w_fox = jnp.concatenate([(wl[:, o_d:o_d + W] * (LOG2E * HEAD_DIM ** -0.5)).astype(BF16),
                                 wlb[:, o_d + W:o_d + 3 * W]], axis=1)

        main2, fox_qk, fox_vt, idx2 = _projections(x2, t, norm_mix_pre[l], w_main, w_fox, w_idx)
        main3 = main2.reshape(bsz, t, -1)

        y_a = _hgrn(main3, lower_bounds[l], hgrn_norm_w[l])

        y_b = _dsa(*_dsa_prep(main3, idx2.reshape(bsz, t, -1), tables, main_blocks["b_qkv"]), topk)

        prm = dict(
            mu=_pad_cols(rwkv_mu[l][None, :], RWKV_COLS), w0=rwkv_w0[l][None, :],
            w2=_pad_rows(rwkv_w2[l], RWKV_LORA_LANES, 0), a0=rwkv_a0[l][None, :],
            a2=_pad_rows(rwkv_a2[l], RWKV_LORA_LANES, 32), g2=_pad_rows(rwkv_g2[l], RWKV_LORA_LANES, 64),
            k_k=rwkv_k_k[l][None, :], k_a=rwkv_k_a[l][None, :], r_k=rwkv_r_k[l].reshape(1, W),
            ln_w=rwkv_ln_w[l][None, :], ln_b=rwkv_ln_b[l][None, :])
        if l > 0:
            prm.update(v0=rwkv_v0[l - 1][None, :], v1=_pad_cols(rwkv_v1[l - 1], LANES),
                       v2=_pad_rows(rwkv_v2[l - 1], LANES, 0))
        y_c, v_c = _rwkv(main3, main_blocks["c"], v_first, prm)
        if l == 0:
            v_first = v_c

        c_rep = _fox_cum(main3, main_blocks["df"], _pad_cols(fox_f_bias[l][None, :], LANES))
        y_d = _fox(fox_qk.reshape(bsz, t, 2 * W), fox_vt, c_rep)

        ys = [y.reshape(n, W) for y in (y_a, y_b, y_c, y_d)]
        x2 = _merge(x2, ys, norm_mix_pre[l], wlb[:, o_g:o_g + 4 * d], w_branch[l], w_out[l], norm_mix_post[l])
        x2 = _mlp(x2, norm_mlp_pre[l], w_up[l], w_down[l], norm_mlp_post[l])
    return x2.reshape(bsz, t, d)
```

```python
import functools

import jax
import jax.numpy as jnp
from jax import lax
from jax.experimental import pallas as pl
from jax.experimental.pallas import tpu as pltpu

F32 = jnp.float32
BF16 = jnp.bfloat16

HEAD_DIM = 64
N_HEADS = 4
WIDTH = N_HEADS * HEAD_DIM
IDX_HEADS = 8
IDX_DIM = 32
TOPK_MAX = 256
ROPE_THETA = 10000.0
NORM_EPS = 1e-6
RWKV_GN_EPS = 64e-5
MASK_VALUE = -1e30
LOG2E = 1.4426950408889634
RWKV_LORA_LANES = 128
RWKV_COLS = 3 * WIDTH + RWKV_LORA_LANES

LANES = 128
VMEM_LIMIT = 48 * 1024 * 1024

NN = ((1,), (0,))
NT = ((1,), (1,))
TN = ((0,), (0,))


def _dot(a, b, dims=NN):
    return lax.dot_general(a, b, (dims, ((), ())), preferred_element_type=F32)


def _split2(x):
    hi = x.astype(BF16)
    lo = (x - hi.astype(F32)).astype(BF16)
    return hi, lo


def _split3(x):
    hi = x.astype(BF16)
    r1 = x - hi.astype(F32)
    mid = r1.astype(BF16)
    lo = (r1 - mid.astype(F32)).astype(BF16)
    return hi, mid, lo


def _dot3(a, b, dims=NN):
    ah, al = _split2(a)
    bh, bl = _split2(b)
    return _dot(ah, bh, dims) + (_dot(ah, bl, dims) + _dot(al, bh, dims))


def _dot_exact_lhs(a_bf16, b, dims=NN):
    b1, b2, b3 = _split3(b)
    return _dot(a_bf16, b1, dims) + (_dot(a_bf16, b2, dims) + _dot(a_bf16, b3, dims))


def _dot_exact_rhs(a, b_bf16, dims=NN):
    a1, a2 = _split2(a)
    return _dot(a1, b_bf16, dims) + _dot(a2, b_bf16, dims)


def _iota(shape, dim):
    return lax.broadcasted_iota(jnp.int32, shape, dim)


def _head_block_ones(n, group):
    return (_iota((n, n), 0) // group == _iota((n, n), 1) // group).astype(BF16)


def _lower_tri_ones(n):
    return (_iota((n, n), 0) >= _iota((n, n), 1)).astype(BF16)


def _sigmoid(x):
    return 1.0 / (1.0 + jnp.exp(-x))


def _softplus(x):
    return jnp.maximum(x, 0.0) + jnp.log(1.0 + jnp.exp(-jnp.abs(x)))


def _params(sem):
    return pltpu.CompilerParams(dimension_semantics=sem, vmem_limit_bytes=VMEM_LIMIT)


def _norm_rows(x, g):
    return x * lax.rsqrt(jnp.mean(x * x, axis=-1, keepdims=True) + NORM_EPS) * g


def _projections_kernel(x_ref, g_ref, wm_ref, wf_ref, wih_ref, wil_ref, om_ref, of_ref, ovt_ref, oi_ref):
    h = _norm_rows(x_ref[...], g_ref[...])
    hh, hl = _split2(h)
    om_ref[...] = _dot(hh, wm_ref[...])
    fox = _dot(hh, wf_ref[...])
    of_ref[...] = fox[:, :2 * WIDTH].astype(of_ref.dtype)
    ovt_ref[0] = fox[:, 2 * WIDTH:].T.astype(ovt_ref.dtype)
    oi_ref[...] = _dot(hh, wih_ref[...]) + (_dot(hh, wil_ref[...]) + _dot(hl, wih_ref[...]))


def _projections(x2, t, g, w_main, w_fox, w_idx, tm=512):
    n, d = x2.shape
    tiles = t // tm
    wih, wil = _split2(w_idx)
    weights = [w_main.astype(BF16), w_fox.astype(BF16), wih, wil]
    rows = lambda c: pl.BlockSpec((tm, c), lambda i: (i, 0))
    return pl.pallas_call(
        _projections_kernel, grid=(n // tm,),
        in_specs=[rows(d), pl.BlockSpec((1, d), lambda i: (0, 0))]
        + [pl.BlockSpec(w.shape, lambda i: (0, 0)) for w in weights],
        out_specs=[rows(w_main.shape[1]), rows(2 * WIDTH),
                   pl.BlockSpec((1, WIDTH, tm), lambda i: (i // tiles, 0, i % tiles)), rows(w_idx.shape[1])],
        out_shape=[jax.ShapeDtypeStruct((n, w_main.shape[1]), F32), jax.ShapeDtypeStruct((n, 2 * WIDTH), BF16),
                   jax.ShapeDtypeStruct((n // t, WIDTH, t), BF16), jax.ShapeDtypeStruct((n, w_idx.shape[1]), F32)],
        compiler_params=_params(("parallel",)), name="norm_proj",
    )(x2, g.reshape(1, d), *weights)


HGRN_CHUNK = 64
HGRN_GROUP = 8


def _dot1(a, b, dims=NN):
    return _dot(a.astype(BF16), b.astype(BF16), dims)


def _hgrn_kernel(q_ref, f_ref, i_ref, g_ref, lb_ref, nw_ref, o_ref, st_ref, *, n_chunks):
    C, G = HGRN_CHUNK, HGRN_GROUP

    @pl.when(pl.program_id(1) == 0)
    def _():
        st_ref[...] = jnp.zeros_like(st_ref)

    lb = lb_ref[...]
    nw = nw_ref[...]
    tri = _lower_tri_ones(C)
    bd = _head_block_ones(WIDTH, HEAD_DIM)
    bd_mask = _iota((WIDTH, WIDTH), 0) // HEAD_DIM == _iota((WIDTH, WIDTH), 1) // HEAD_DIM
    row_top = _iota((G, 1), 0)
    lane_head = _iota((1, WIDTH), 1) // HEAD_DIM

    def chunk(c, carry):
        sl = pl.ds(pl.multiple_of(c * C, C), C)
        fl = f_ref[0, sl, :]
        f = lb + (1.0 - lb) * _sigmoid(fl)
        k = (1.0 - lb) * _sigmoid(-fl)
        b = _dot_exact_lhs(tri, jnp.log(f))
        q = q_ref[0, sl, :] * (HEAD_DIM ** -0.5)
        v = i_ref[0, sl, :]
        st = st_ref[...]
        o_inter = _dot1(q * jnp.exp(b), st, NT)

        n_groups = C // G
        o_rows = [o_inter[g * G:(g + 1) * G] for g in range(n_groups)]
        pieces = []
        for g in range(n_groups):
            s0 = g * G
            q_top, b_top, k_top = q[s0:s0 + G], b[s0:s0 + G], k[s0:s0 + G]
            for u in range(G):
                top = q_top * k_top[u:u + 1] * jnp.exp(jnp.minimum(b_top - b_top[u:u + 1], 0.0))
                pieces.append(jnp.where(row_top >= u, top, 0.0))
        r = _dot(jnp.concatenate(pieces, axis=0).astype(BF16), bd)
        att, v_st = [], []
        for g in range(n_groups - 1):
            s0 = g * G
            b_top, k_top, v_top = b[s0:s0 + G], k[s0:s0 + G], v[s0:s0 + G]
            b_ref = b_top[G - 1:G]
            qp = (q[s0 + G:] * jnp.exp(b[s0 + G:] - b_ref)).astype(BF16)
            kp = k_top * jnp.exp(b_ref - b_top)
            k_st = jnp.concatenate([jnp.where(lane_head == h, kp, 0.0) for h in range(N_HEADS)], axis=0)
            v_st.append(jnp.concatenate([jnp.where(lane_head == h, v_top, 0.0) for h in range(N_HEADS)], axis=0))
            att.append(_dot(qp, k_st.astype(BF16), NT))
        for g in range(n_groups):
            v_top = v[g * G:(g + 1) * G]
            for u in range(G):
                o_rows[g] = o_rows[g] + r[(g * G + u) * G:(g * G + u + 1) * G] * v_top[u:u + 1]
        for g in range(n_groups - 1):
            rest = _dot(att[g].astype(BF16), v_st[g].astype(BF16))
            for g2 in range(g + 1, n_groups):
                o_rows[g2] = o_rows[g2] + rest[(g2 - g - 1) * G:(g2 - g) * G]
        o = jnp.concatenate(o_rows, axis=0)

        b_last = b[C - 1:C, :]
        upd = _dot1(v, k * jnp.exp(b_last - b), TN)
        st_ref[...] = st * jnp.exp(b_last) + jnp.where(bd_mask, upd, 0.0)

        ms = _dot_exact_rhs(o * o, bd) * (1.0 / HEAD_DIM)
        on = o * lax.rsqrt(ms + NORM_EPS) * nw
        gl = g_ref[0, sl, :]
        o_ref[0, sl, :] = (on * (gl * _sigmoid(gl))).astype(o_ref.dtype)
        return carry

    lax.fori_loop(0, n_chunks, chunk, 0)


def _hgrn(main3, lb, norm_w, tb=256):
    bsz, t, _ = main3.shape
    col = lambda j: pl.BlockSpec((1, tb, WIDTH), lambda b, i, j=j: (b, i, j))
    vec = pl.BlockSpec((1, WIDTH), lambda b, i: (0, 0))
    return pl.pallas_call(
        functools.partial(_hgrn_kernel, n_chunks=tb // HGRN_CHUNK),
        grid=(bsz, t // tb),
        in_specs=[col(0), col(1), col(2), col(3), vec, vec],
        out_specs=pl.BlockSpec((1, tb, WIDTH), lambda b, i: (b, i, 0)),
        out_shape=jax.ShapeDtypeStruct((bsz, t, WIDTH), BF16),
        scratch_shapes=[pltpu.VMEM((WIDTH, WIDTH), F32)],
        compiler_params=_params(("parallel", "arbitrary")), name="hgrn2",
    )(main3, main3, main3, main3, lb.reshape(1, WIDTH), jnp.tile(norm_w, N_HEADS).reshape(1, WIDTH))


RWKV_CHUNK = 64


def _rwkv_kernel(*refs, n_chunks, has_vgate):
    if has_vgate:
        (c_ref, vf_ref, mu_ref, w0_ref, w2_ref, a0_ref, a2_ref, g2_ref, kkw_ref, kaw_ref, rk_ref,
         lnw_ref, lnb_ref, v0_ref, v1_ref, v2_ref, y_ref, vout_ref, s_ref, prev_ref, y_sc) = refs
    else:
        (c_ref, mu_ref, w0_ref, w2_ref, a0_ref, a2_ref, g2_ref, kkw_ref, kaw_ref, rk_ref,
         lnw_ref, lnb_ref, y_ref, vout_ref, s_ref, prev_ref, y_sc) = refs
    C = RWKV_CHUNK
    W = WIDTH
    tb = n_chunks * C

    @pl.when(pl.program_id(1) == 0)
    def _():
        s_ref[...] = jnp.zeros_like(s_ref)
        prev_ref[...] = jnp.zeros_like(prev_ref)

    p = c_ref[0]
    shifted = jnp.where(_iota((tb, 1), 0) == 0, prev_ref[...], pltpu.roll(p, 1, axis=0))
    prev_ref[...] = p[tb - 1:tb, :]
    xm = p + (shifted - p) * mu_ref[...]
    r = xm[:, 0:W]
    k = xm[:, W:2 * W]
    v = xm[:, 2 * W:3 * W]
    lora = xm[:, 3 * W:3 * W + RWKV_LORA_LANES]

    bd = _head_block_ones(W, HEAD_DIM)
    w_log = -_softplus(-(w0_ref[...] + _dot3(jnp.tanh(lora), w2_ref[...]))) - 0.5
    log_decay = -jnp.exp(w_log)
    a = _sigmoid(a0_ref[...] + _dot3(lora, a2_ref[...]))
    g = _dot3(_sigmoid(lora), g2_ref[...])
    if has_vgate:
        vg = _dot3(_dot3(v, v1_ref[...]), v2_ref[...])
        v = v + (vf_ref[0] - v) * _sigmoid(v0_ref[...] + vg)
    vout_ref[0] = v
    kk = k * kkw_ref[...]
    kk = kk * lax.rsqrt(jnp.maximum(_dot_exact_rhs(kk * kk, bd), 1e-24))
    k = k * (1.0 + (a - 1.0) * kaw_ref[...])

    tri = _lower_tri_ones(C)
    lane_head = _iota((1, W), 1) // HEAD_DIM
    hc = N_HEADS * C
    rt = _iota((hc, hc), 0) % C
    ct = _iota((hc, hc), 1) % C
    strict_lower = rt > ct
    lower = rt >= ct
    eye = (_iota((hc, hc), 0) == _iota((hc, hc), 1)).astype(F32)

    def stack(m):
        return jnp.concatenate([jnp.where(lane_head == h, m, 0.0) for h in range(N_HEADS)], axis=0)

    chunks = range(n_chunks)
    a_s, b_s, k_s, r_s, v_s, p_last, m, lak, lrb, lrk = ([None] * n_chunks for _ in range(10))
    for c in chunks:
        sl = slice(c * C, (c + 1) * C)
        ld = log_decay[sl]
        cum = _dot_exact_lhs(tri, ld)
        pdec = jnp.exp(cum)
        pinv = jnp.exp(-cum)
        p_last[c] = pdec[C - 1:C, :]
        a_s[c] = stack(-kk[sl] * jnp.exp(cum - ld)).astype(BF16)
        b_s[c] = stack(kk[sl] * a[sl] * pinv).astype(BF16)
        k_s[c] = stack(k[sl] * pinv).astype(BF16)
        r_s[c] = stack(r[sl] * pdec)
        v_s[c] = stack(v[sl]).astype(BF16)
        gram = _dot(jnp.concatenate([a_s[c], r_s[c].astype(BF16)], axis=0),
                    jnp.concatenate([b_s[c], k_s[c]], axis=0), NT)
        m[c] = jnp.where(strict_lower, gram[:hc, :hc], 0.0)
        lak[c] = jnp.where(strict_lower, gram[:hc, hc:], 0.0)
        lrb[c] = jnp.where(lower, gram[hc:, :hc], 0.0)
        lrk[c] = jnp.where(lower, gram[hc:, hc:], 0.0)

    x = [eye + m[c] for c in chunks]
    power = 2
    while power < C:
        m = [_dot1(m[c], m[c]) for c in chunks]
        x = [x[c] + _dot1(x[c], m[c]) for c in chunks]
        power *= 2
    lakv = [_dot1(lak[c], v_s[c]) for c in chunks]
    lrkv = [_dot1(lrk[c], v_s[c]) for c in chunks]
    w1 = [_dot1(x[c], lakv[c]) for c in chunks]
    a2 = [_dot1(x[c], a_s[c]) for c in chunks]
    r2 = [r_s[c] + _dot1(lrb[c], a2[c]) for c in chunks]
    y0 = [_dot1(lrb[c], w1[c]) + lrkv[c] for c in chunks]
    t1 = [_dot1(a2[c], b_s[c], TN) for c in chunks]
    t0 = [_dot1(w1[c], b_s[c], TN) + _dot(v_s[c], k_s[c], TN) for c in chunks]

    s = s_ref[...]
    for c in chunks:
        y = _dot1(r2[c], s, NT) + y0[c]
        y_sc[c * C:(c + 1) * C, :] = sum(y[h * C:(h + 1) * C] for h in range(N_HEADS))
        s = (s + _dot1(s, t1[c]) + t0[c]) * p_last[c]
    s_ref[...] = s

    y = y_sc[...]
    inv_n = 1.0 / HEAD_DIM
    mean = _dot_exact_rhs(y, bd) * inv_n
    yc = y - mean
    var = _dot_exact_rhs(yc * yc, bd) * inv_n
    yn = yc * lax.rsqrt(var + RWKV_GN_EPS) * lnw_ref[...] + lnb_ref[...]
    yn = yn + _dot_exact_rhs(r * k * rk_ref[...], bd) * v
    y_ref[0] = (yn * g).astype(y_ref.dtype)


def _rwkv(c3, col_block, v_first, prm, tb=512):
    bsz, t, _ = c3.shape
    has_vgate = v_first is not None
    blk = lambda w: pl.BlockSpec((1, tb, w), lambda b, i: (b, i, 0))
    cblk = pl.BlockSpec((1, tb, RWKV_COLS), lambda b, i: (b, i, col_block))
    full = lambda a: pl.BlockSpec(a.shape, lambda b, i: (0,) * a.ndim)
    names = ["mu", "w0", "w2", "a0", "a2", "g2", "k_k", "k_a", "r_k", "ln_w", "ln_b"]
    if has_vgate:
        names += ["v0", "v1", "v2"]
    args = [c3] + ([v_first] if has_vgate else []) + [prm[n] for n in names]
    in_specs = [cblk] + ([blk(WIDTH)] if has_vgate else []) + [full(prm[n]) for n in names]
    return pl.pallas_call(
        functools.partial(_rwkv_kernel, n_chunks=tb // RWKV_CHUNK, has_vgate=has_vgate),
        grid=(bsz, t // tb),
        in_specs=in_specs,
        out_specs=[blk(WIDTH), blk(WIDTH)],
        out_shape=[jax.ShapeDtypeStruct((bsz, t, WIDTH), BF16), jax.ShapeDtypeStruct((bsz, t, WIDTH), F32)],
        scratch_shapes=[pltpu.VMEM((WIDTH, WIDTH), F32), pltpu.VMEM((1, RWKV_COLS), F32),
                        pltpu.VMEM((tb, WIDTH), F32)],
        compiler_params=_params(("parallel", "arbitrary")), name="rwkv7",
    )(*args)


def _fox_cum_kernel(f_ref, bias_ref, c_ref, carry_ref, *, tb):
    @pl.when(pl.program_id(1) == 0)
    def _():
        carry_ref[...] = jnp.zeros_like(carry_ref)

    logf = -_softplus(-(f_ref[0] + bias_ref[...]))
    c = _dot_exact_lhs(_lower_tri_ones(tb), logf) + carry_ref[...]
    carry_ref[...] = c[tb - 1:tb, :]
    for h in range(N_HEADS):
        pick = (_iota((LANES, LANES), 0) == h).astype(BF16)
        c1, c2, c3 = _split3(c)
        c_ref[0, h] = LOG2E * (_dot(c1, pick) + (_dot(c2, pick) + _dot(c3, pick)))


def _fox_cum(main3, col_block, bias_row, tb=512):
    bsz, t, _ = main3.shape
    return pl.pallas_call(
        functools.partial(_fox_cum_kernel, tb=tb),
        grid=(bsz, t // tb),
        in_specs=[pl.BlockSpec((1, tb, LANES), lambda b, i: (b, i, col_block)),
                  pl.BlockSpec((1, LANES), lambda b, i: (0, 0))],
        out_specs=pl.BlockSpec((1, N_HEADS, tb, LANES), lambda b, i: (b, 0, i, 0)),
        out_shape=jax.ShapeDtypeStruct((bsz, N_HEADS, t, LANES), F32),
        scratch_shapes=[pltpu.VMEM((1, LANES), F32)],
        compiler_params=_params(("parallel", "arbitrary")), name="fox_cumgate",
    )(main3, bias_row)


ATT_QB = 512
ATT_KB = 512


def _stack_heads(q):
    lane_head = _iota((1, WIDTH), 1) // HEAD_DIM
    zero = jnp.zeros_like(q)
    return jnp.concatenate([jnp.where(lane_head == h, q, zero) for h in range(N_HEADS)], axis=0)


def _softmax_stage_t(s_t, m_sc, l_sc):
    m_old = m_sc[...]
    m_new = jnp.maximum(m_old, jnp.max(s_t, axis=0, keepdims=True))
    alpha = jnp.exp2(m_old - m_new)
    p_t = jnp.exp2(s_t - m_new)
    l_sc[...] = alpha * l_sc[...] + jnp.sum(p_t, axis=0, keepdims=True)
    m_sc[...] = m_new
    return p_t.astype(BF16), alpha


def _pv_stage_t(vt_blk, p_t, alpha, acc_sc):
    for h in range(N_HEADS):
        cs = slice(h * ATT_QB, (h + 1) * ATT_QB)
        pv = _dot(vt_blk[h * HEAD_DIM:(h + 1) * HEAD_DIM, :], p_t[:, cs])
        acc_sc[h] = alpha[:, cs] * acc_sc[h] + pv


def _attention_t(n_blocks, qk, post, post_last, vt_ref, bufs, m_sc, l_sc, acc_sc):
    KB = ATT_KB
    s_buf, p_buf, a_buf = bufs[0:2], bufs[2:4], bufs[4:6]
    _softmax_init_t(m_sc, l_sc, acc_sc)

    def vt_blk(j):
        return vt_ref[0, :, pl.ds(pl.multiple_of(j * KB, KB), KB)]

    def pending(j, par):
        _pv_stage_t(vt_blk(jnp.maximum(j - 1, 0)), p_buf[1 - par][...], a_buf[1 - par][...], acc_sc)

    def step(j, par):
        raw = s_buf[par][...]
        s_buf[1 - par][...] = qk(j + 1)
        p_t, alpha = _softmax_stage_t(post(j, raw), m_sc, l_sc)
        pending(j, par)
        p_buf[par][...] = p_t
        a_buf[par][...] = alpha

    def final(j, par):
        p_t, alpha = _softmax_stage_t(post_last(j, s_buf[par][...]), m_sc, l_sc)
        pending(j, par)
        _pv_stage_t(vt_blk(j), p_t, alpha, acc_sc)

    s_buf[0][...] = qk(0)
    p_buf[1][...] = jnp.zeros_like(p_buf[1])
    a_buf[1][...] = jnp.ones_like(a_buf[1])
    pairs = (n_blocks - 1) // 2

    def pair(t, carry):
        step(2 * t, 0)
        step(2 * t + 1, 1)
        return carry

    lax.fori_loop(0, pairs, pair, 0)
    odd_left = (n_blocks - 1) % 2 == 1

    @pl.when(odd_left)
    def _():
        step(2 * pairs, 0)
        final(2 * pairs + 1, 1)

    @pl.when(jnp.logical_not(odd_left))
    def _():
        final(2 * pairs, 0)


def _attention_bufs():
    cols = N_HEADS * ATT_QB
    return ([pltpu.VMEM((ATT_KB, cols), F32)] * 2 + [pltpu.VMEM((ATT_KB, cols), BF16)] * 2
            + [pltpu.VMEM((1, cols), F32)] * 2)


def _softmax_init_t(m_sc, l_sc, acc_sc):
    m_sc[...] = jnp.full_like(m_sc, MASK_VALUE)
    l_sc[...] = jnp.zeros_like(l_sc)
    acc_sc[...] = jnp.zeros_like(acc_sc)


def _softmax_finish_t(l_sc, acc_sc):
    l = l_sc[...]
    out_t = jnp.concatenate([acc_sc[h] / l[:, h * ATT_QB:(h + 1) * ATT_QB] for h in range(N_HEADS)], axis=0)
    return out_t.T


def _fox_kernel(q_ref, k_ref, vt_ref, ck_ref, o_ref, m_sc, l_sc, acc_sc, *bufs):
    QB, KB = ATT_QB, ATT_KB
    i = pl.program_id(1)
    qstack = _stack_heads(q_ref[0])
    qpos = i * QB + _iota((1, N_HEADS * QB), 1) % QB

    def qk(j):
        return _dot(k_ref[0, pl.ds(pl.multiple_of(j * KB, KB), KB), :], qstack, NT)

    def post(j, s_t):
        ks = pl.ds(pl.multiple_of(j * KB, KB), KB)
        rep = lambda c: jnp.concatenate([c] * (QB // LANES), axis=1)
        return jnp.concatenate([s_t[:, h * QB:(h + 1) * QB] - rep(ck_ref[0, h, ks, :]) for h in range(N_HEADS)], axis=1)

    def post_last(j, s_t):
        return jnp.where(j * KB + _iota((KB, 1), 0) <= qpos, post(j, s_t), MASK_VALUE)

    _attention_t((i * QB) // KB + 1, qk, post, post_last, vt_ref, bufs, m_sc, l_sc, acc_sc)
    o_ref[0] = _softmax_finish_t(l_sc, acc_sc).astype(o_ref.dtype)


def _fox(qk3, v_t, c_rep):
    bsz, t, _ = qk3.shape
    cols = N_HEADS * ATT_QB
    return pl.pallas_call(
        _fox_kernel,
        grid=(bsz, t // ATT_QB),
        in_specs=[pl.BlockSpec((1, ATT_QB, WIDTH), lambda b, i: (b, i, 0)),
                  pl.BlockSpec((1, t, WIDTH), lambda b, i: (b, 0, 1)),
                  pl.BlockSpec((1, WIDTH, t), lambda b, i: (b, 0, 0)),
                  pl.BlockSpec((1, N_HEADS, t, LANES), lambda b, i: (b, 0, 0, 0))],
        out_specs=pl.BlockSpec((1, ATT_QB, WIDTH), lambda b, i: (b, i, 0)),
        out_shape=jax.ShapeDtypeStruct((bsz, t, WIDTH), BF16),
        scratch_shapes=[pltpu.VMEM((1, cols), F32), pltpu.VMEM((1, cols), F32),
                        pltpu.VMEM((N_HEADS, HEAD_DIM, ATT_QB), F32)] + _attention_bufs(),
        compiler_params=_params(("parallel", "arbitrary")), name="fox_attention",
    )(qk3, qk3, v_t, c_rep)


def _swap_halves(x, half):
    n = x.shape[-1]
    lower = (_iota((1, n), 1) % (2 * half)) < half
    return jnp.where(lower, pltpu.roll(x, n - half, axis=1), pltpu.roll(x, half, axis=1))


def _dsa_prep_kernel(q_ref, k_ref, v_ref, iq_ref, ik_ref, iw_ref, ch_ref, sh_ref, ci_ref, si_ref,
                     selh_ref, sell_ref, qo_ref, ko_ref, vo_ref, q3_ref, k3_ref, wi_ref):
    ch, sh, ci, si = ch_ref[...], sh_ref[...], ci_ref[...], si_ref[...]

    def rope(x, c, s, half):
        return x * c + _swap_halves(x, half) * s

    qo_ref[0] = (rope(q_ref[0], ch, sh, HEAD_DIM // 2) * (LOG2E * HEAD_DIM ** -0.5)).astype(BF16)
    ko_ref[0] = rope(k_ref[0], ch, sh, HEAD_DIM // 2).astype(BF16)
    vo_ref[0] = v_ref[0].T.astype(BF16)
    qh, ql = _split2(rope(iq_ref[0], ci, si, IDX_DIM // 2) * (IDX_DIM ** -0.5))
    q3_ref[0] = (_dot(qh, selh_ref[...]) + _dot(ql, sell_ref[...])).astype(BF16)
    kh, kl = _split2(rope(ik_ref[0], ci[:, :LANES], si[:, :LANES], IDX_DIM // 2))
    seg = _iota((1, LANES), 1) // IDX_DIM
    k3_ref[0] = jnp.where(seg < 2, kh, jnp.where(seg == 2, kl, jnp.zeros_like(kl)))
    wi_ref[0] = (iw_ref[0] * (IDX_HEADS ** -0.5)).T


def _dsa_prep(main3, idx3, tables, qkv_block0, tb=512):
    bsz, t, _ = main3.shape
    iq_w = IDX_HEADS * IDX_DIM
    r = jnp.arange(iq_w)[:, None]
    c = jnp.arange(IDX_HEADS * LANES)[None, :]
    same = (r // IDX_DIM == c // LANES) & (r % IDX_DIM == c % IDX_DIM)
    seg = (c % LANES) // IDX_DIM
    sel_hi = (same & ((seg == 0) | (seg == 2))).astype(BF16)
    sel_lo = (same & (seg == 1)).astype(BF16)
    mcol = lambda j: pl.BlockSpec((1, tb, WIDTH), lambda b, i, j=j: (b, i, qkv_block0 + j))
    tab = pl.BlockSpec((tb, WIDTH), lambda b, i: (i, 0))
    sel = pl.BlockSpec(sel_hi.shape, lambda b, i: (0, 0))
    out = lambda w: pl.BlockSpec((1, tb, w), lambda b, i: (b, i, 0))
    out_t = lambda w: pl.BlockSpec((1, w, tb), lambda b, i: (b, 0, i))
    shape = lambda dt, w: jax.ShapeDtypeStruct((bsz, t, w), dt)
    return pl.pallas_call(
        _dsa_prep_kernel, grid=(bsz, t // tb),
        in_specs=[mcol(0), mcol(1), mcol(2), out(iq_w),
                  pl.BlockSpec((1, tb, LANES), lambda b, i: (b, i, iq_w // LANES)),
                  pl.BlockSpec((1, tb, LANES), lambda b, i: (b, i, iq_w // LANES + 1)),
                  tab, tab, tab, tab, sel, sel],
        out_specs=[out(WIDTH), out(WIDTH), out_t(WIDTH), out(IDX_HEADS * LANES), out(LANES), out_t(LANES)],
        out_shape=[shape(BF16, WIDTH), shape(BF16, WIDTH), jax.ShapeDtypeStruct((bsz, WIDTH, t), BF16),
                   shape(BF16, IDX_HEADS * LANES), shape(BF16, LANES), jax.ShapeDtypeStruct((bsz, LANES, t), F32)],
        compiler_params=_params(("parallel", "parallel")), name="dsa_prep",
    )(main3, main3, main3, idx3, idx3, idx3, *tables, sel_hi, sel_lo)


def _dsa_kernel(q_ref, k_ref, vt_ref, q3_ref, k3_ref, wt_ref, o_ref, key_sc, tie_sc, m_sc, l_sc, acc_sc,
                *bufs, topk):
    QB, KB = ATT_QB, ATT_KB
    i = pl.program_id(1)
    nkb = (i * QB) // KB + 1
    qpos = i * QB + _iota((1, QB), 1)
    int_min = jnp.int32(-2 ** 31)

    q3 = q3_ref[0]
    qs = jnp.concatenate([q3[:, h * LANES:(h + 1) * LANES] for h in range(IDX_HEADS)], axis=0)
    wt = wt_ref[0]

    def score_block(j, carry):
        ks = pl.ds(pl.multiple_of(j * KB, KB), KB)
        r = _dot(k3_ref[0, ks, :], qs, NT)
        sc = jnp.zeros((KB, QB), F32)
        for h in range(IDX_HEADS):
            sc = sc + jnp.maximum(r[:, h * QB:(h + 1) * QB], 0.0) * wt[h:h + 1, :]
        sc = jnp.where(j * KB + _iota((KB, 1), 0) <= qpos, sc, MASK_VALUE)
        sc = jnp.where(sc == 0.0, 0.0, sc)
        bits = pltpu.bitcast(sc, jnp.int32)
        key_sc[ks, :] = jnp.where(bits < 0, bits ^ jnp.int32(0x7FFFFFFF), bits)
        return carry

    lax.fori_loop(0, nkb, score_block, 0)

    def count_ge(cand):
        def blk(j, acc):
            base = pl.multiple_of(j * KB, KB)
            for r0 in range(0, KB, 64):
                acc = acc + jnp.where(key_sc[pl.ds(base + r0, 64), :] >= cand, 1.0, 0.0)
            return acc
        acc = lax.fori_loop(0, nkb, blk, jnp.zeros((64, QB), F32))
        return jnp.sum(acc, axis=0, keepdims=True)

    kf = float(topk)

    def bit_step(step, v):
        trial = jnp.where(step == 0, jnp.zeros_like(v), v | (jnp.int32(1) << (31 - step)))
        return jnp.where(count_ge(trial) >= kf, trial, v)

    thr = lax.fori_loop(0, 32, bit_step, jnp.full((1, QB), int_min, jnp.int32))
    need = kf - count_ge(thr + 1)

    lt = _lower_tri_ones(KB)
    qstack = _stack_heads(q_ref[0])

    def qk(j):
        return _dot(k_ref[0, pl.ds(pl.multiple_of(j * KB, KB), KB), :], qstack, NT)

    def post(j, s_t):
        ks = pl.ds(pl.multiple_of(j * KB, KB), KB)
        key = key_sc[ks, :]
        tie = key == thr
        tie_b = jnp.where(tie, 1.0, 0.0).astype(BF16)
        rank = tie_sc[...] + _dot(lt, tie_b)
        tie_sc[...] = rank[KB - 1:KB, :]
        sel = (key > thr) | (tie & (rank <= need))
        sel = sel & (j * KB + _iota((KB, 1), 0) <= qpos)
        bias = jnp.where(sel, 0.0, MASK_VALUE)
        return s_t + jnp.concatenate([bias] * N_HEADS, axis=1)

    tie_sc[...] = jnp.zeros_like(tie_sc)
    _attention_t(nkb, qk, post, post, vt_ref, bufs, m_sc, l_sc, acc_sc)
    o_ref[0] = _softmax_finish_t(l_sc, acc_sc).astype(o_ref.dtype)


def _dsa(q, k, v_t, q3, k3, w_t, topk):
    bsz, t, _ = q.shape
    qblk = lambda w: pl.BlockSpec((1, ATT_QB, w), lambda b, i: (b, i, 0))
    seq = lambda w: pl.BlockSpec((1, t, w), lambda b, i: (b, 0, 0))
    cols = N_HEADS * ATT_QB
    return pl.pallas_call(
        functools.partial(_dsa_kernel, topk=topk),
        grid=(bsz, t // ATT_QB),
        in_specs=[qblk(WIDTH), seq(WIDTH), pl.BlockSpec((1, WIDTH, t), lambda b, i: (b, 0, 0)),
                  qblk(IDX_HEADS * LANES), seq(LANES),
                  pl.BlockSpec((1, IDX_HEADS, ATT_QB), lambda b, i: (b, 0, i))],
        out_specs=qblk(WIDTH),
        out_shape=jax.ShapeDtypeStruct((bsz, t, WIDTH), BF16),
        scratch_shapes=[pltpu.VMEM((t, ATT_QB), jnp.int32), pltpu.VMEM((1, ATT_QB), F32), pltpu.VMEM((1, cols), F32),
                        pltpu.VMEM((1, cols), F32), pltpu.VMEM((N_HEADS, HEAD_DIM, ATT_QB), F32)] + _attention_bufs(),
        compiler_params=_params(("parallel", "arbitrary")), name="dsa_attention",
    )(q, k, v_t, q3, k3, w_t)


def _merge_kernel(x_ref, ya_ref, yb_ref, yc_ref, yd_ref, gpre_ref, wg_ref, wb_ref, wo_ref, g_ref, o_ref):
    d = x_ref.shape[-1]
    x = x_ref[...]
    h = _norm_rows(x, gpre_ref[...]).astype(BF16)
    merged = None
    for n, y_ref in enumerate((ya_ref, yb_ref, yc_ref, yd_ref)):
        gate = _sigmoid(_dot(h, wg_ref[:, n * d:(n + 1) * d]))
        term = gate * _dot(y_ref[...], wb_ref[n])
        merged = term if merged is None else merged + term
    mix = _dot(merged.astype(BF16), wo_ref[...])
    o_ref[...] = x + _norm_rows(mix, g_ref[...])


def _merge(x2, ys, g_pre, w_gate, w_branch, w_out, g_post, tm=256):
    n, d = x2.shape
    row = lambda w: pl.BlockSpec((tm, w), lambda i: (i, 0))
    vec = pl.BlockSpec((1, d), lambda i: (0, 0))
    return pl.pallas_call(
        _merge_kernel, grid=(n // tm,),
        in_specs=[row(d)] + [row(WIDTH)] * 4 + [vec, pl.BlockSpec(w_gate.shape, lambda i: (0, 0)),
                                                 pl.BlockSpec(w_branch.shape, lambda i: (0, 0, 0)),
                                                 pl.BlockSpec(w_out.shape, lambda i: (0, 0)), vec],
        out_specs=row(d),
        out_shape=jax.ShapeDtypeStruct((n, d), F32),
        compiler_params=_params(("parallel",)), name="gated_merge",
    )(x2, *ys, g_pre.reshape(1, d), w_gate.astype(BF16), w_branch.astype(BF16), w_out.astype(BF16),
      g_post.reshape(1, d))


MLP_VMEM_LIMIT = 56 * 1024 * 1024


def _mlp_kernel(x_ref, gpre_ref, wu_ref, wd_ref, gpost_ref, o_ref, *, tf):
    x = x_ref[...]
    h = _norm_rows(x, gpre_ref[...]).astype(BF16)
    acc = None
    for k in range(wu_ref.shape[1] // tf):
        u = jnp.maximum(_dot(h, wu_ref[:, k * tf:(k + 1) * tf]), 0.0)
        part = _dot((u * u).astype(BF16), wd_ref[k * tf:(k + 1) * tf, :])
        acc = part if acc is None else acc + part
    o_ref[...] = x + _norm_rows(acc, gpost_ref[...])


def _mlp(x2, g_pre, w_up, w_down, g_post, tm=512, tf=1024):
    n, d = x2.shape
    return pl.pallas_call(
        functools.partial(_mlp_kernel, tf=tf), grid=(n // tm,),
        in_specs=[pl.BlockSpec((tm, d), lambda i: (i, 0)), pl.BlockSpec((1, d), lambda i: (0, 0)),
                  pl.BlockSpec(w_up.shape, lambda i: (0, 0)), pl.BlockSpec(w_down.shape, lambda i: (0, 0)),
                  pl.BlockSpec((1, d), lambda i: (0, 0))],
        out_specs=pl.BlockSpec((tm, d), lambda i: (i, 0)),
        out_shape=jax.ShapeDtypeStruct((n, d), F32),
        compiler_params=pltpu.CompilerParams(dimension_semantics=("parallel",), vmem_limit_bytes=MLP_VMEM_LIMIT),
        name="mlp",
    )(x2, g_pre.reshape(1, d), w_up.astype(BF16), w_down.astype(BF16), g_post.reshape(1, d))


def _rope_tables(t, dim, groups):
    inv = 1.0 / (ROPE_THETA ** (jnp.arange(0, dim, 2, dtype=F32) / dim))
    ang = jnp.arange(t, dtype=F32)[:, None] * inv[None, :]
    cos, sin = jnp.cos(ang), jnp.sin(ang)
    return jnp.tile(jnp.concatenate([cos, cos], axis=1), (1, groups)), jnp.tile(jnp.concatenate([-sin, sin], axis=1), (1, groups))


def _pad_rows(w, rows, offset):
    return jnp.zeros((rows, w.shape[1]), w.dtype).at[offset:offset + w.shape[0]].set(w)


def _pad_cols(w, cols):
    return jnp.pad(w, ((0, 0), (0, cols - w.shape[1])))


def kernel(x, norm_mix_pre, norm_mix_post, norm_mlp_pre, norm_mlp_post, w_in, w_branch, w_out, hgrn_lb_logits, hgrn_norm_w, fox_f_bias, rwkv_mu, rwkv_w0, rwkv_w2, rwkv_a0, rwkv_a2, rwkv_g2, rwkv_k_k, rwkv_k_a, rwkv_r_k, rwkv_ln_w, rwkv_ln_b, rwkv_v0, rwkv_v1, rwkv_v2, w_up, w_down):
    bsz, t, d = x.shape
    n = bsz * t
    depth = w_in.shape[0]
    W = WIDTH
    topk = min(TOPK_MAX, t // 4)

    lb_soft = jax.nn.softmax(hgrn_lb_logits.astype(F32), axis=0)
    lower_bounds = jnp.cumsum(lb_soft, axis=0) - lb_soft[0:1]
    tables = _rope_tables(t, HEAD_DIM, N_HEADS) + _rope_tables(t, IDX_DIM, IDX_HEADS)

    o_a, o_b, o_iq = 0, 4 * W, 7 * W
    o_ik, o_iw = o_iq + IDX_HEADS * IDX_DIM, o_iq + IDX_HEADS * IDX_DIM + IDX_DIM
    o_c = o_iw + IDX_HEADS
    c_cols = 3 * W + 128
    o_d = o_c + c_cols
    o_df = o_d + 3 * W
    o_g = o_df + N_HEADS
    main_blocks = dict(b_qkv=4, c=2, df=(7 * W + c_cols) // LANES)
    assert main_blocks["c"] * c_cols == 7 * W

    x2 = x.reshape(n, d)
    w_in_bf16 = w_in.astype(BF16)
    v_first = None
    for l in range(depth):
        wl, wlb = w_in[l], w_in_bf16[l]
        w_main = jnp.concatenate([
            wlb[:, o_a:o_a + 7 * W], wlb[:, o_c:o_c + c_cols], _pad_cols(wlb[:, o_df:o_df + N_HEADS], LANES),
            jnp.zeros((d, 256), BF16)], axis=1)
        w_idx = jnp.concatenate([
            wl[:, o_iq:o_iq + W], jnp.tile(wl[:, o_ik:o_ik + IDX_DIM], (1, LANES // IDX_DIM)),
            _pad_cols(wl[:, o_iw:o_iw + IDX_HEADS], LANES)], axis=1)
        w_fox = jnp.concatenate([(wl[:, o_d:o_d + W] * (LOG2E * HEAD_DIM ** -0.5)).astype(BF16),
                                 wlb[:, o_d + W:o_d + 3 * W]], axis=1)

        main2, fox_qk, fox_vt, idx2 = _projections(x2, t, norm_mix_pre[l], w_main, w_fox, w_idx)
        main3 = main2.reshape(bsz, t, -1)

        y_a = _hgrn(main3, lower_bounds[l], hgrn_norm_w[l])

        y_b = _dsa(*_dsa_prep(main3, idx2.reshape(bsz, t, -1), tables, main_blocks["b_qkv"]), topk)

        prm = dict(
            mu=_pad_cols(rwkv_mu[l][None, :], RWKV_COLS), w0=rwkv_w0[l][None, :],
            w2=_pad_rows(rwkv_w2[l], RWKV_LORA_LANES, 0), a0=rwkv_a0[l][None, :],
            a2=_pad_rows(rwkv_a2[l], RWKV_LORA_LANES, 32), g2=_pad_rows(rwkv_g2[l], RWKV_LORA_LANES, 64),
            k_k=rwkv_k_k[l][None, :], k_a=rwkv_k_a[l][None, :], r_k=rwkv_r_k[l].reshape(1, W),
            ln_w=rwkv_ln_w[l][None, :], ln_b=rwkv_ln_b[l][None, :])
        if l > 0:
            prm.update(v0=rwkv_v0[l - 1][None, :], v1=_pad_cols(rwkv_v1[l - 1], LANES),
                       v2=_pad_rows(rwkv_v2[l - 1], LANES, 0))
        y_c, v_c = _rwkv(main3, main_blocks["c"], v_first, prm)
        if l == 0:
            v_first = v_c

        c_rep = _fox_cum(main3, main_blocks["df"], _pad_cols(fox_f_bias[l][None, :], LANES))
        y_d = _fox(fox_qk.reshape(bsz, t, 2 * W), fox_vt, c_rep)

        ys = [y.reshape(n, W) for y in (y_a, y_b, y_c, y_d)]
        x2 = _merge(x2, ys, norm_mix_pre[l], wlb[:, o_g:o_g + 4 * d], w_branch[l], w_out[l], norm_mix_post[l])
        x2 = _mlp(x2, norm_mlp_pre[l], w_up[l], w_down[l], norm_mlp_post[l])
    return x2.reshape(bsz, t, d)
```

```python
import functools

import jax
import jax.numpy as jnp
from jax import lax
from jax.experimental import pallas as pl
from jax.experimental.pallas import tpu as pltpu

F32 = jnp.float32
BF16 = jnp.bfloat16

HEAD_DIM = 64
N_HEADS = 4
WIDTH = N_HEADS * HEAD_DIM
IDX_HEADS = 8
IDX_DIM = 32
TOPK_MAX = 256
ROPE_THETA = 10000.0
NORM_EPS = 1e-6
RWKV_GN_EPS = 64e-5
MASK_VALUE = -1e30
LOG2E = 1.4426950408889634
RWKV_LORA_LANES = 128
RWKV_COLS = 3 * WIDTH + RWKV_LORA_LANES

LANES = 128
VMEM_LIMIT = 48 * 1024 * 1024

NN = ((1,), (0,))
NT = ((1,), (1,))
TN = ((0,), (0,))


def _dot(a, b, dims=NN):
    return lax.dot_general(a, b, (dims, ((), ())), preferred_element_type=F32)


def _split2(x):
    hi = x.astype(BF16)
    lo = (x - hi.astype(F32)).astype(BF16)
    return hi, lo


def _split3(x):
    hi = x.astype(BF16)
    r1 = x - hi.astype(F32)
    mid = r1.astype(BF16)
    lo = (r1 - mid.astype(F32)).astype(BF16)
    return hi, mid, lo


def _dot3(a, b, dims=NN):
    ah, al = _split2(a)
    bh, bl = _split2(b)
    return _dot(ah, bh, dims) + (_dot(ah, bl, dims) + _dot(al, bh, dims))


def _dot_exact_lhs(a_bf16, b, dims=NN):
    b1, b2, b3 = _split3(b)
    return _dot(a_bf16, b1, dims) + (_dot(a_bf16, b2, dims) + _dot(a_bf16, b3, dims))


def _dot_exact_rhs(a, b_bf16, dims=NN):
    a1, a2 = _split2(a)
    return _dot(a1, b_bf16, dims) + _dot(a2, b_bf16, dims)


def _iota(shape, dim):
    return lax.broadcasted_iota(jnp.int32, shape, dim)


def _head_block_ones(n, group):
    return (_iota((n, n), 0) // group == _iota((n, n), 1) // group).astype(BF16)


def _lower_tri_ones(n):
    return (_iota((n, n), 0) >= _iota((n, n), 1)).astype(BF16)


def _sigmoid(x):
    return 1.0 / (1.0 + jnp.exp(-x))


def _softplus(x):
    return jnp.maximum(x, 0.0) + jnp.log(1.0 + jnp.exp(-jnp.abs(x)))


def _params(sem):
    return pltpu.CompilerParams(dimension_semantics=sem, vmem_limit_bytes=VMEM_LIMIT)


def _norm_rows(x, g):
    return x * lax.rsqrt(jnp.mean(x * x, axis=-1, keepdims=True) + NORM_EPS) * g


def _projections_kernel(x_ref, g_ref, wm_ref, wf_ref, wih_ref, wil_ref, om_ref, of_ref, ovt_ref, oi_ref):
    h = _norm_rows(x_ref[...], g_ref[...])
    hh, hl = _split2(h)
    om_ref[...] = _dot(hh, wm_ref[...])
    fox = _dot(hh, wf_ref[...])
    of_ref[...] = fox[:, :2 * WIDTH].astype(of_ref.dtype)
    ovt_ref[0] = fox[:, 2 * WIDTH:].T.astype(ovt_ref.dtype)
    oi_ref[...] = _dot(hh, wih_ref[...]) + (_dot(hh, wil_ref[...]) + _dot(hl, wih_ref[...]))


def _projections(x2, t, g, w_main, w_fox, w_idx, tm=512):
    n, d = x2.shape
    tiles = t // tm
    wih, wil = _split2(w_idx)
    weights = [w_main.astype(BF16), w_fox.astype(BF16), wih, wil]
    rows = lambda c: pl.BlockSpec((tm, c), lambda i: (i, 0))
    return pl.pallas_call(
        _projections_kernel, grid=(n // tm,),
        in_specs=[rows(d), pl.BlockSpec((1, d), lambda i: (0, 0))]
        + [pl.BlockSpec(w.shape, lambda i: (0, 0)) for w in weights],
        out_specs=[rows(w_main.shape[1]), rows(2 * WIDTH),
                   pl.BlockSpec((1, WIDTH, tm), lambda i: (i // tiles, 0, i % tiles)), rows(w_idx.shape[1])],
        out_shape=[jax.ShapeDtypeStruct((n, w_main.shape[1]), F32), jax.ShapeDtypeStruct((n, 2 * WIDTH), BF16),
                   jax.ShapeDtypeStruct((n // t, WIDTH, t), BF16), jax.ShapeDtypeStruct((n, w_idx.shape[1]), F32)],
        compiler_params=_params(("parallel",)), name="norm_proj",
    )(x2, g.reshape(1, d), *weights)


HGRN_CHUNK = 64
HGRN_GROUP = 8


def _dot1(a, b, dims=NN):
    return _dot(a.astype(BF16), b.astype(BF16), dims)


def _hgrn_kernel(q_ref, f_ref, i_ref, g_ref, lb_ref, nw_ref, o_ref, st_ref, *, n_chunks):
    C, G = HGRN_CHUNK, HGRN_GROUP

    @pl.when(pl.program_id(1) == 0)
    def _():
        st_ref[...] = jnp.zeros_like(st_ref)

    lb = lb_ref[...]
    nw = nw_ref[...]
    tri = _lower_tri_ones(C)
    bd = _head_block_ones(WIDTH, HEAD_DIM)
    bd_mask = _iota((WIDTH, WIDTH), 0) // HEAD_DIM == _iota((WIDTH, WIDTH), 1) // HEAD_DIM
    row_top = _iota((G, 1), 0)
    lane_head = _iota((1, WIDTH), 1) // HEAD_DIM

    def chunk(c, carry):
        sl = pl.ds(pl.multiple_of(c * C, C), C)
        fl = f_ref[0, sl, :]
        f = lb + (1.0 - lb) * _sigmoid(fl)
        k = (1.0 - lb) * _sigmoid(-fl)
        b = _dot_exact_lhs(tri, jnp.log(f))
        q = q_ref[0, sl, :] * (HEAD_DIM ** -0.5)
        v = i_ref[0, sl, :]
        st = st_ref[...]
        o_inter = _dot1(q * jnp.exp(b), st, NT)

        n_groups = C // G
        o_rows = [o_inter[g * G:(g + 1) * G] for g in range(n_groups)]
        pieces = []
        for g in range(n_groups):
            s0 = g * G
            q_top, b_top, k_top = q[s0:s0 + G], b[s0:s0 + G], k[s0:s0 + G]
            for u in range(G):
                top = q_top * k_top[u:u + 1] * jnp.exp(jnp.minimum(b_top - b_top[u:u + 1], 0.0))
                pieces.append(jnp.where(row_top >= u, top, 0.0))
        r = _dot(jnp.concatenate(pieces, axis=0).astype(BF16), bd)
        att, v_st = [], []
        for g in range(n_groups - 1):
            s0 = g * G
            b_top, k_top, v_top = b[s0:s0 + G], k[s0:s0 + G], v[s0:s0 + G]
            b_ref = b_top[G - 1:G]
            qp = (q[s0 + G:] * jnp.exp(b[s0 + G:] - b_ref)).astype(BF16)
            kp = k_top * jnp.exp(b_ref - b_top)
            k_st = jnp.concatenate([jnp.where(lane_head == h, kp, 0.0) for h in range(N_HEADS)], axis=0)
            v_st.append(jnp.concatenate([jnp.where(lane_head == h, v_top, 0.0) for h in range(N_HEADS)], axis=0))
            att.append(_dot(qp, k_st.astype(BF16), NT))
        for g in range(n_groups):
            v_top = v[g * G:(g + 1) * G]
            for u in range(G):
                o_rows[g] = o_rows[g] + r[(g * G + u) * G:(g * G + u + 1) * G] * v_top[u:u + 1]
        for g in range(n_groups - 1):
            rest = _dot(att[g].astype(BF16), v_st[g].astype(BF16))
            for g2 in range(g + 1, n_groups):
                o_rows[g2] = o_rows[g2] + rest[(g2 - g - 1) * G:(g2 - g) * G]
        o = jnp.concatenate(o_rows, axis=0)

        b_last = b[C - 1:C, :]
        upd = _dot1(v, k * jnp.exp(b_last - b), TN)
        st_ref[...] = st * jnp.exp(b_last) + jnp.where(bd_mask, upd, 0.0)

        ms = _dot_exact_rhs(o * o, bd) * (1.0 / HEAD_DIM)
        on = o * lax.rsqrt(ms + NORM_EPS) * nw
        gl = g_ref[0, sl, :]
        o_ref[0, sl, :] = (on * (gl * _sigmoid(gl))).astype(o_ref.dtype)
        return carry

    lax.fori_loop(0, n_chunks, chunk, 0, unroll=True)


def _hgrn(main3, lb, norm_w, tb=512):
    bsz, t, _ = main3.shape
    col = lambda j: pl.BlockSpec((1, tb, WIDTH), lambda b, i, j=j: (b, i, j))
    vec = pl.BlockSpec((1, WIDTH), lambda b, i: (0, 0))
    return pl.pallas_call(
        functools.partial(_hgrn_kernel, n_chunks=tb // HGRN_CHUNK),
        grid=(bsz, t // tb),
        in_specs=[col(0), col(1), col(2), col(3), vec, vec],
        out_specs=pl.BlockSpec((1, tb, WIDTH), lambda b, i: (b, i, 0)),
        out_shape=jax.ShapeDtypeStruct((bsz, t, WIDTH), BF16),
        scratch_shapes=[pltpu.VMEM((WIDTH, WIDTH), F32)],
        compiler_params=_params(("parallel", "arbitrary")), name="hgrn2",
    )(main3, main3, main3, main3, lb.reshape(1, WIDTH), jnp.tile(norm_w, N_HEADS).reshape(1, WIDTH))


RWKV_CHUNK = 64


def _rwkv_kernel(*refs, n_chunks, has_vgate):
    if has_vgate:
        (c_ref, vf_ref, mu_ref, w0_ref, w2_ref, a0_ref, a2_ref, g2_ref, kkw_ref, kaw_ref, rk_ref,
         lnw_ref, lnb_ref, v0_ref, v1_ref, v2_ref, y_ref, vout_ref, s_ref, prev_ref, y_sc) = refs
    else:
        (c_ref, mu_ref, w0_ref, w2_ref, a0_ref, a2_ref, g2_ref, kkw_ref, kaw_ref, rk_ref,
         lnw_ref, lnb_ref, y_ref, vout_ref, s_ref, prev_ref, y_sc) = refs
    C = RWKV_CHUNK
    W = WIDTH
    tb = n_chunks * C

    @pl.when(pl.program_id(1) == 0)
    def _():
        s_ref[...] = jnp.zeros_like(s_ref)
        prev_ref[...] = jnp.zeros_like(prev_ref)

    p = c_ref[0]
    shifted = jnp.where(_iota((tb, 1), 0) == 0, prev_ref[...], pltpu.roll(p, 1, axis=0))
    prev_ref[...] = p[tb - 1:tb, :]
    xm = p + (shifted - p) * mu_ref[...]
    r = xm[:, 0:W]
    k = xm[:, W:2 * W]
    v = xm[:, 2 * W:3 * W]
    lora = xm[:, 3 * W:3 * W + RWKV_LORA_LANES]

    bd = _head_block_ones(W, HEAD_DIM)
    w_log = -_softplus(-(w0_ref[...] + _dot3(jnp.tanh(lora), w2_ref[...]))) - 0.5
    log_decay = -jnp.exp(w_log)
    a = _sigmoid(a0_ref[...] + _dot3(lora, a2_ref[...]))
    g = _dot3(_sigmoid(lora), g2_ref[...])
    if has_vgate:
        vg = _dot3(_dot3(v, v1_ref[...]), v2_ref[...])
        v = v + (vf_ref[0] - v) * _sigmoid(v0_ref[...] + vg)
    vout_ref[0] = v
    kk = k * kkw_ref[...]
    kk = kk * lax.rsqrt(jnp.maximum(_dot_exact_rhs(kk * kk, bd), 1e-24))
    k = k * (1.0 + (a - 1.0) * kaw_ref[...])

    tri = _lower_tri_ones(C)
    lane_head = _iota((1, W), 1) // HEAD_DIM
    hc = N_HEADS * C
    rt = _iota((hc, hc), 0) % C
    ct = _iota((hc, hc), 1) % C
    strict_lower = rt > ct
    lower = rt >= ct
    eye = (_iota((hc, hc), 0) == _iota((hc, hc), 1)).astype(F32)

    def stack(m):
        return jnp.concatenate([jnp.where(lane_head == h, m, 0.0) for h in range(N_HEADS)], axis=0)

    chunks = range(n_chunks)
    a_s, b_s, k_s, r_s, v_s, p_last, m, lak, lrb, lrk = ([None] * n_chunks for _ in range(10))
    for c in chunks:
        sl = slice(c * C, (c + 1) * C)
        ld = log_decay[sl]
        cum = _dot_exact_lhs(tri, ld)
        pdec = jnp.exp(cum)
        pinv = jnp.exp(-cum)
        p_last[c] = pdec[C - 1:C, :]
        a_s[c] = stack(-kk[sl] * jnp.exp(cum - ld)).astype(BF16)
        b_s[c] = stack(kk[sl] * a[sl] * pinv).astype(BF16)
        k_s[c] = stack(k[sl] * pinv).astype(BF16)
        r_s[c] = stack(r[sl] * pdec)
        v_s[c] = stack(v[sl]).astype(BF16)
        gram = _dot(jnp.concatenate([a_s[c], r_s[c].astype(BF16)], axis=0),
                    jnp.concatenate([b_s[c], k_s[c]], axis=0), NT)
        m[c] = jnp.where(strict_lower, gram[:hc, :hc], 0.0)
        lak[c] = jnp.where(strict_lower, gram[:hc, hc:], 0.0)
        lrb[c] = jnp.where(lower, gram[hc:, :hc], 0.0)
        lrk[c] = jnp.where(lower, gram[hc:, hc:], 0.0)

    x = [eye + m[c] for c in chunks]
    power = 2
    while power < C:
        m = [_dot1(m[c], m[c]) for c in chunks]
        x = [x[c] + _dot1(x[c], m[c]) for c in chunks]
        power *= 2
    lakv = [_dot1(lak[c], v_s[c]) for c in chunks]
    lrkv = [_dot1(lrk[c], v_s[c]) for c in chunks]
    w1 = [_dot1(x[c], lakv[c]) for c in chunks]
    a2 = [_dot1(x[c], a_s[c]) for c in chunks]
    r2 = [r_s[c] + _dot1(lrb[c], a2[c]) for c in chunks]
    y0 = [_dot1(lrb[c], w1[c]) + lrkv[c] for c in chunks]
    t1 = [_dot1(a2[c], b_s[c], TN) for c in chunks]
    t0 = [_dot1(w1[c], b_s[c], TN) + _dot(v_s[c], k_s[c], TN) for c in chunks]

    s = s_ref[...]
    for c in chunks:
        y = _dot1(r2[c], s, NT) + y0[c]
        y_sc[c * C:(c + 1) * C, :] = sum(y[h * C:(h + 1) * C] for h in range(N_HEADS))
        s = (s + _dot1(s, t1[c]) + t0[c]) * p_last[c]
    s_ref[...] = s

    y = y_sc[...]
    inv_n = 1.0 / HEAD_DIM
    mean = _dot_exact_rhs(y, bd) * inv_n
    yc = y - mean
    var = _dot_exact_rhs(yc * yc, bd) * inv_n
    yn = yc * lax.rsqrt(var + RWKV_GN_EPS) * lnw_ref[...] + lnb_ref[...]
    yn = yn + _dot_exact_rhs(r * k * rk_ref[...], bd) * v
    y_ref[0] = (yn * g).astype(y_ref.dtype)


def _rwkv(c3, col_block, v_first, prm, tb=512):
    bsz, t, _ = c3.shape
    has_vgate = v_first is not None
    blk = lambda w: pl.BlockSpec((1, tb, w), lambda b, i: (b, i, 0))
    cblk = pl.BlockSpec((1, tb, RWKV_COLS), lambda b, i: (b, i, col_block))
    full = lambda a: pl.BlockSpec(a.shape, lambda b, i: (0,) * a.ndim)
    names = ["mu", "w0", "w2", "a0", "a2", "g2", "k_k", "k_a", "r_k", "ln_w", "ln_b"]
    if has_vgate:
        names += ["v0", "v1", "v2"]
    args = [c3] + ([v_first] if has_vgate else []) + [prm[n] for n in names]
    in_specs = [cblk] + ([blk(WIDTH)] if has_vgate else []) + [full(prm[n]) for n in names]
    return pl.pallas_call(
        functools.partial(_rwkv_kernel, n_chunks=tb // RWKV_CHUNK, has_vgate=has_vgate),
        grid=(bsz, t // tb),
        in_specs=in_specs,
        out_specs=[blk(WIDTH), blk(WIDTH)],
        out_shape=[jax.ShapeDtypeStruct((bsz, t, WIDTH), BF16), jax.ShapeDtypeStruct((bsz, t, WIDTH), F32)],
        scratch_shapes=[pltpu.VMEM((WIDTH, WIDTH), F32), pltpu.VMEM((1, RWKV_COLS), F32),
                        pltpu.VMEM((tb, WIDTH), F32)],
        compiler_params=_params(("parallel", "arbitrary")), name="rwkv7",
    )(*args)


def _fox_cum_kernel(f_ref, bias_ref, c_ref, carry_ref, *, tb):
    @pl.when(pl.program_id(1) == 0)
    def _():
        carry_ref[...] = jnp.zeros_like(carry_ref)

    logf = -_softplus(-(f_ref[0] + bias_ref[...]))
    c = _dot_exact_lhs(_lower_tri_ones(tb), logf) + carry_ref[...]
    carry_ref[...] = c[tb - 1:tb, :]
    for h in range(N_HEADS):
        pick = (_iota((LANES, LANES), 0) == h).astype(BF16)
        c1, c2, c3 = _split3(c)
        c_ref[0, h] = LOG2E * (_dot(c1, pick) + (_dot(c2, pick) + _dot(c3, pick)))


def _fox_cum(main3, col_block, bias_row, tb=512):
    bsz, t, _ = main3.shape
    return pl.pallas_call(
        functools.partial(_fox_cum_kernel, tb=tb),
        grid=(bsz, t // tb),
        in_specs=[pl.BlockSpec((1, tb, LANES), lambda b, i: (b, i, col_block)),
                  pl.BlockSpec((1, LANES), lambda b, i: (0, 0))],
        out_specs=pl.BlockSpec((1, N_HEADS, tb, LANES), lambda b, i: (b, 0, i, 0)),
        out_shape=jax.ShapeDtypeStruct((bsz, N_HEADS, t, LANES), F32),
        scratch_shapes=[pltpu.VMEM((1, LANES), F32)],
        compiler_params=_params(("parallel", "arbitrary")), name="fox_cumgate",
    )(main3, bias_row)


ATT_QB = 512
ATT_KB = 512


def _stack_heads(q):
    lane_head = _iota((1, WIDTH), 1) // HEAD_DIM
    zero = jnp.zeros_like(q)
    return jnp.concatenate([jnp.where(lane_head == h, q, zero) for h in range(N_HEADS)], axis=0)


def _softmax_stage_t(s_t, m_sc, l_sc):
    m_old = m_sc[...]
    m_new = jnp.maximum(m_old, jnp.max(s_t, axis=0, keepdims=True))
    alpha = jnp.exp2(m_old - m_new)
    p_t = jnp.exp2(s_t - m_new)
    l_sc[...] = alpha * l_sc[...] + jnp.sum(p_t, axis=0, keepdims=True)
    m_sc[...] = m_new
    return p_t.astype(BF16), alpha


def _pv_stage_t(vt_blk, p_t, alpha, acc_sc):
    for h in range(N_HEADS):
        cs = slice(h * ATT_QB, (h + 1) * ATT_QB)
        pv = _dot(vt_blk[h * HEAD_DIM:(h + 1) * HEAD_DIM, :], p_t[:, cs])
        acc_sc[h] = alpha[:, cs] * acc_sc[h] + pv


def _attention_t(n_blocks, qk, post, post_last, vt_ref, bufs, m_sc, l_sc, acc_sc):
    KB = ATT_KB
    s_buf, p_buf, a_buf = bufs[0:2], bufs[2:4], bufs[4:6]
    _softmax_init_t(m_sc, l_sc, acc_sc)

    def vt_blk(j):
        return vt_ref[0, :, pl.ds(pl.multiple_of(j * KB, KB), KB)]

    def pending(j, par):
        _pv_stage_t(vt_blk(jnp.maximum(j - 1, 0)), p_buf[1 - par][...], a_buf[1 - par][...], acc_sc)

    def step(j, par):
        raw = s_buf[par][...]
        s_buf[1 - par][...] = qk(j + 1)
        p_t, alpha = _softmax_stage_t(post(j, raw), m_sc, l_sc)
        pending(j, par)
        p_buf[par][...] = p_t
        a_buf[par][...] = alpha

    def final(j, par):
        p_t, alpha = _softmax_stage_t(post_last(j, s_buf[par][...]), m_sc, l_sc)
        pending(j, par)
        _pv_stage_t(vt_blk(j), p_t, alpha, acc_sc)

    s_buf[0][...] = qk(0)
    p_buf[1][...] = jnp.zeros_like(p_buf[1])
    a_buf[1][...] = jnp.ones_like(a_buf[1])
    pairs = (n_blocks - 1) // 2

    def pair(t, carry):
        step(2 * t, 0)
        step(2 * t + 1, 1)
        return carry

    lax.fori_loop(0, pairs, pair, 0)
    odd_left = (n_blocks - 1) % 2 == 1

    @pl.when(odd_left)
    def _():
        step(2 * pairs, 0)
        final(2 * pairs + 1, 1)

    @pl.when(jnp.logical_not(odd_left))
    def _():
        final(2 * pairs, 0)


def _attention_bufs():
    cols = N_HEADS * ATT_QB
    return ([pltpu.VMEM((ATT_KB, cols), F32)] * 2 + [pltpu.VMEM((ATT_KB, cols), BF16)] * 2
            + [pltpu.VMEM((1, cols), F32)] * 2)


def _softmax_init_t(m_sc, l_sc, acc_sc):
    m_sc[...] = jnp.full_like(m_sc, MASK_VALUE)
    l_sc[...] = jnp.zeros_like(l_sc)
    acc_sc[...] = jnp.zeros_like(acc_sc)


def _softmax_finish_t(l_sc, acc_sc):
    l = l_sc[...]
    out_t = jnp.concatenate([acc_sc[h] / l[:, h * ATT_QB:(h + 1) * ATT_QB] for h in range(N_HEADS)], axis=0)
    return out_t.T


def _fox_kernel(q_ref, k_ref, vt_ref, ck_ref, o_ref, m_sc, l_sc, acc_sc, *bufs):
    QB, KB = ATT_QB, ATT_KB
    i = pl.program_id(1)
    qstack = _stack_heads(q_ref[0])
    qpos = i * QB + _iota((1, N_HEADS * QB), 1) % QB

    def qk(j):
        return _dot(k_ref[0, pl.ds(pl.multiple_of(j * KB, KB), KB), :], qstack, NT)

    def post(j, s_t):
        ks = pl.ds(pl.multiple_of(j * KB, KB), KB)
        rep = lambda c: jnp.concatenate([c] * (QB // LANES), axis=1)
        return jnp.concatenate([s_t[:, h * QB:(h + 1) * QB] - rep(ck_ref[0, h, ks, :]) for h in range(N_HEADS)], axis=1)

    def post_last(j, s_t):
        return jnp.where(j * KB + _iota((KB, 1), 0) <= qpos, post(j, s_t), MASK_VALUE)

    _attention_t((i * QB) // KB + 1, qk, post, post_last, vt_ref, bufs, m_sc, l_sc, acc_sc)
    o_ref[0] = _softmax_finish_t(l_sc, acc_sc).astype(o_ref.dtype)


def _fox(qk3, v_t, c_rep):
    bsz, t, _ = qk3.shape
    cols = N_HEADS * ATT_QB
    return pl.pallas_call(
        _fox_kernel,
        grid=(bsz, t // ATT_QB),
        in_specs=[pl.BlockSpec((1, ATT_QB, WIDTH), lambda b, i: (b, i, 0)),
                  pl.BlockSpec((1, t, WIDTH), lambda b, i: (b, 0, 1)),
                  pl.BlockSpec((1, WIDTH, t), lambda b, i: (b, 0, 0)),
                  pl.BlockSpec((1, N_HEADS, t, LANES), lambda b, i: (b, 0, 0, 0))],
        out_specs=pl.BlockSpec((1, ATT_QB, WIDTH), lambda b, i: (b, i, 0)),
        out_shape=jax.ShapeDtypeStruct((bsz, t, WIDTH), BF16),
        scratch_shapes=[pltpu.VMEM((1, cols), F32), pltpu.VMEM((1, cols), F32),
                        pltpu.VMEM((N_HEADS, HEAD_DIM, ATT_QB), F32)] + _attention_bufs(),
        compiler_params=_params(("parallel", "arbitrary")), name="fox_attention",
    )(qk3, qk3, v_t, c_rep)


def _swap_halves(x, half):
    n = x.shape[-1]
    lower = (_iota((1, n), 1) % (2 * half)) < half
    return jnp.where(lower, pltpu.roll(x, n - half, axis=1), pltpu.roll(x, half, axis=1))


def _dsa_prep_kernel(q_ref, k_ref, v_ref, iq_ref, ik_ref, iw_ref, ch_ref, sh_ref, ci_ref, si_ref,
                     selh_ref, sell_ref, qo_ref, ko_ref, vo_ref, q3_ref, k3_ref, wi_ref):
    ch, sh, ci, si = ch_ref[...], sh_ref[...], ci_ref[...], si_ref[...]

    def rope(x, c, s, half):
        return x * c + _swap_halves(x, half) * s

    qo_ref[0] = (rope(q_ref[0], ch, sh, HEAD_DIM // 2) * (LOG2E * HEAD_DIM ** -0.5)).astype(BF16)
    ko_ref[0] = rope(k_ref[0], ch, sh, HEAD_DIM // 2).astype(BF16)
    vo_ref[0] = v_ref[0].T.astype(BF16)
    qh, ql = _split2(rope(iq_ref[0], ci, si, IDX_DIM // 2) * (IDX_DIM ** -0.5))
    q3_ref[0] = (_dot(qh, selh_ref[...]) + _dot(ql, sell_ref[...])).astype(BF16)
    kh, kl = _split2(rope(ik_ref[0], ci[:, :LANES], si[:, :LANES], IDX_DIM // 2))
    seg = _iota((1, LANES), 1) // IDX_DIM
    k3_ref[0] = jnp.where(seg < 2, kh, jnp.where(seg == 2, kl, jnp.zeros_like(kl)))
    wi_ref[0] = (iw_ref[0] * (IDX_HEADS ** -0.5)).T


def _dsa_prep(main3, idx3, tables, qkv_block0, tb=512):
    bsz, t, _ = main3.shape
    iq_w = IDX_HEADS * IDX_DIM
    r = jnp.arange(iq_w)[:, None]
    c = jnp.arange(IDX_HEADS * LANES)[None, :]
    same = (r // IDX_DIM == c // LANES) & (r % IDX_DIM == c % IDX_DIM)
    seg = (c % LANES) // IDX_DIM
    sel_hi = (same & ((seg == 0) | (seg == 2))).astype(BF16)
    sel_lo = (same & (seg == 1)).astype(BF16)
    mcol = lambda j: pl.BlockSpec((1, tb, WIDTH), lambda b, i, j=j: (b, i, qkv_block0 + j))
    tab = pl.BlockSpec((tb, WIDTH), lambda b, i: (i, 0))
    sel = pl.BlockSpec(sel_hi.shape, lambda b, i: (0, 0))
    out = lambda w: pl.BlockSpec((1, tb, w), lambda b, i: (b, i, 0))
    out_t = lambda w: pl.BlockSpec((1, w, tb), lambda b, i: (b, 0, i))
    shape = lambda dt, w: jax.ShapeDtypeStruct((bsz, t, w), dt)
    return pl.pallas_call(
        _dsa_prep_kernel, grid=(bsz, t // tb),
        in_specs=[mcol(0), mcol(1), mcol(2), out(iq_w),
                  pl.BlockSpec((1, tb, LANES), lambda b, i: (b, i, iq_w // LANES)),
                  pl.BlockSpec((1, tb, LANES), lambda b, i: (b, i, iq_w // LANES + 1)),
                  tab, tab, tab, tab, sel, sel],
        out_specs=[out(WIDTH), out(WIDTH), out_t(WIDTH), out(IDX_HEADS * LANES), out(LANES), out_t(LANES)],
        out_shape=[shape(BF16, WIDTH), shape(BF16, WIDTH), jax.ShapeDtypeStruct((bsz, WIDTH, t), BF16),
                   shape(BF16, IDX_HEADS * LANES), shape(BF16, LANES), jax.ShapeDtypeStruct((bsz, LANES, t), F32)],
        compiler_params=_params(("parallel", "parallel")), name="dsa_prep",
    )(main3, main3, main3, idx3, idx3, idx3, *tables, sel_hi, sel_lo)


def _dsa_kernel(q_ref, k_ref, vt_ref, q3_ref, k3_ref, wt_ref, o_ref, key_sc, tie_sc, m_sc, l_sc, acc_sc,
                *bufs, topk):
    QB, KB = ATT_QB, ATT_KB
    i = pl.program_id(1)
    nkb = (i * QB) // KB + 1
    qpos = i * QB + _iota((1, QB), 1)
    int_min = jnp.int32(-2 ** 31)

    q3 = q3_ref[0]
    qs = jnp.concatenate([q3[:, h * LANES:(h + 1) * LANES] for h in range(IDX_HEADS)], axis=0)
    wt = wt_ref[0]

    def score_block(j, carry):
        ks = pl.ds(pl.multiple_of(j * KB, KB), KB)
        r = _dot(k3_ref[0, ks, :], qs, NT)
        sc = jnp.zeros((KB, QB), F32)
        for h in range(IDX_HEADS):
            sc = sc + jnp.maximum(r[:, h * QB:(h + 1) * QB], 0.0) * wt[h:h + 1, :]
        sc = jnp.where(j * KB + _iota((KB, 1), 0) <= qpos, sc, MASK_VALUE)
        sc = jnp.where(sc == 0.0, 0.0, sc)
        bits = pltpu.bitcast(sc, jnp.int32)
        key_sc[ks, :] = jnp.where(bits < 0, bits ^ jnp.int32(0x7FFFFFFF), bits)
        return carry

    lax.fori_loop(0, nkb, score_block, 0)

    def count_ge(cand):
        def blk(j, acc):
            base = pl.multiple_of(j * KB, KB)
            for r0 in range(0, KB, 64):
                acc = acc + jnp.where(key_sc[pl.ds(base + r0, 64), :] >= cand, 1.0, 0.0)
            return acc
        acc = lax.fori_loop(0, nkb, blk, jnp.zeros((64, QB), F32))
        return jnp.sum(acc, axis=0, keepdims=True)

    kf = float(topk)

    def bit_step(step, v):
        trial = jnp.where(step == 0, jnp.zeros_like(v), v | (jnp.int32(1) << (31 - step)))
        return jnp.where(count_ge(trial) >= kf, trial, v)

    thr = lax.fori_loop(0, 32, bit_step, jnp.full((1, QB), int_min, jnp.int32))
    need = kf - count_ge(thr + 1)

    lt = _lower_tri_ones(KB)
    qstack = _stack_heads(q_ref[0])

    def qk(j):
        return _dot(k_ref[0, pl.ds(pl.multiple_of(j * KB, KB), KB), :], qstack, NT)

    def post(j, s_t):
        ks = pl.ds(pl.multiple_of(j * KB, KB), KB)
        key = key_sc[ks, :]
        tie = key == thr
        tie_b = jnp.where(tie, 1.0, 0.0).astype(BF16)
        rank = tie_sc[...] + _dot(lt, tie_b)
        tie_sc[...] = rank[KB - 1:KB, :]
        sel = (key > thr) | (tie & (rank <= need))
        sel = sel & (j * KB + _iota((KB, 1), 0) <= qpos)
        bias = jnp.where(sel, 0.0, MASK_VALUE)
        return s_t + jnp.concatenate([bias] * N_HEADS, axis=1)

    tie_sc[...] = jnp.zeros_like(tie_sc)
    _attention_t(nkb, qk, post, post, vt_ref, bufs, m_sc, l_sc, acc_sc)
    o_ref[0] = _softmax_finish_t(l_sc, acc_sc).astype(o_ref.dtype)


def _dsa(q, k, v_t, q3, k3, w_t, topk):
    bsz, t, _ = q.shape
    qblk = lambda w: pl.BlockSpec((1, ATT_QB, w), lambda b, i: (b, i, 0))
    seq = lambda w: pl.BlockSpec((1, t, w), lambda b, i: (b, 0, 0))
    cols = N_HEADS * ATT_QB
    return pl.pallas_call(
        functools.partial(_dsa_kernel, topk=topk),
        grid=(bsz, t // ATT_QB),
        in_specs=[qblk(WIDTH), seq(WIDTH), pl.BlockSpec((1, WIDTH, t), lambda b, i: (b, 0, 0)),
                  qblk(IDX_HEADS * LANES), seq(LANES),
                  pl.BlockSpec((1, IDX_HEADS, ATT_QB), lambda b, i: (b, 0, i))],
        out_specs=qblk(WIDTH),
        out_shape=jax.ShapeDtypeStruct((bsz, t, WIDTH), BF16),
        scratch_shapes=[pltpu.VMEM((t, ATT_QB), jnp.int32), pltpu.VMEM((1, ATT_QB), F32), pltpu.VMEM((1, cols), F32),
                        pltpu.VMEM((1, cols), F32), pltpu.VMEM((N_HEADS, HEAD_DIM, ATT_QB), F32)] + _attention_bufs(),
        compiler_params=_params(("parallel", "arbitrary")), name="dsa_attention",
    )(q, k, v_t, q3, k3, w_t)


def _merge_kernel(x_ref, ya_ref, yb_ref, yc_ref, yd_ref, gpre_ref, wg_ref, wb_ref, wo_ref, g_ref, o_ref):
    d = x_ref.shape[-1]
    x = x_ref[...]
    h = _norm_rows(x, gpre_ref[...]).astype(BF16)
    merged = None
    for n, y_ref in enumerate((ya_ref, yb_ref, yc_ref, yd_ref)):
        gate = _sigmoid(_dot(h, wg_ref[:, n * d:(n + 1) * d]))
        term = gate * _dot(y_ref[...], wb_ref[n])
        merged = term if merged is None else merged + term
    mix = _dot(merged.astype(BF16), wo_ref[...])
    o_ref[...] = x + _norm_rows(mix, g_ref[...])


def _merge(x2, ys, g_pre, w_gate, w_branch, w_out, g_post, tm=256):
    n, d = x2.shape
    row = lambda w: pl.BlockSpec((tm, w), lambda i: (i, 0))
    vec = pl.BlockSpec((1, d), lambda i: (0, 0))
    return pl.pallas_call(
        _merge_kernel, grid=(n // tm,),
        in_specs=[row(d)] + [row(WIDTH)] * 4 + [vec, pl.BlockSpec(w_gate.shape, lambda i: (0, 0)),
                                                 pl.BlockSpec(w_branch.shape, lambda i: (0, 0, 0)),
                                                 pl.BlockSpec(w_out.shape, lambda i: (0, 0)), vec],
        out_specs=row(d),
        out_shape=jax.ShapeDtypeStruct((n, d), F32),
        compiler_params=_params(("parallel",)), name="gated_merge",
    )(x2, *ys, g_pre.reshape(1, d), w_gate.astype(BF16), w_branch.astype(BF16), w_out.astype(BF16),
      g_post.reshape(1, d))


MLP_VMEM_LIMIT = 56 * 1024 * 1024


def _mlp_kernel(x_ref, gpre_ref, wu_ref, wd_ref, gpost_ref, o_ref, *, tf):
    x = x_ref[...]
    h = _norm_rows(x, gpre_ref[...]).astype(BF16)
    acc = None
    for k in range(wu_ref.shape[1] // tf):
        u = jnp.maximum(_dot(h, wu_ref[:, k * tf:(k + 1) * tf]), 0.0)
        part = _dot((u * u).astype(BF16), wd_ref[k * tf:(k + 1) * tf, :])
        acc = part if acc is None else acc + part
    o_ref[...] = x + _norm_rows(acc, gpost_ref[...])


def _mlp(x2, g_pre, w_up, w_down, g_post, tm=512, tf=1024):
    n, d = x2.shape
    return pl.pallas_call(
        functools.partial(_mlp_kernel, tf=tf), grid=(n // tm,),
        in_specs=[pl.BlockSpec((tm, d), lambda i: (i, 0)), pl.BlockSpec((1, d), lambda i: (0, 0)),
                  pl.BlockSpec(w_up.shape, lambda i: (0, 0)), pl.BlockSpec(w_down.shape, lambda i: (0, 0)),
                  pl.BlockSpec((1, d), lambda i: (0, 0))],
        out_specs=pl.BlockSpec((tm, d), lambda i: (i, 0)),
        out_shape=jax.ShapeDtypeStruct((n, d), F32),
        compiler_params=pltpu.CompilerParams(dimension_semantics=("parallel",), vmem_limit_bytes=MLP_VMEM_LIMIT),
        name="mlp",
    )(x2, g_pre.reshape(1, d), w_up.astype(BF16), w_down.astype(BF16), g_post.reshape(1, d))


def _rope_tables(t, dim, groups):
    inv = 1.0 / (ROPE_THETA ** (jnp.arange(0, dim, 2, dtype=F32) / dim))
    ang = jnp.arange(t, dtype=F32)[:, None] * inv[None, :]
    cos, sin = jnp.cos(ang), jnp.sin(ang)
    return jnp.tile(jnp.concatenate([cos, cos], axis=1), (1, groups)), jnp.tile(jnp.concatenate([-sin, sin], axis=1), (1, groups))


def _pad_rows(w, rows, offset):
    return jnp.zeros((rows, w.shape[1]), w.dtype).at[offset:offset + w.shape[0]].set(w)


def _pad_cols(w, cols):
    return jnp.pad(w, ((0, 0), (0, cols - w.shape[1])))


def kernel(x, norm_mix_pre, norm_mix_post, norm_mlp_pre, norm_mlp_post, w_in, w_branch, w_out, hgrn_lb_logits, hgrn_norm_w, fox_f_bias, rwkv_mu, rwkv_w0, rwkv_w2, rwkv_a0, rwkv_a2, rwkv_g2, rwkv_k_k, rwkv_k_a, rwkv_r_k, rwkv_ln_w, rwkv_ln_b, rwkv_v0, rwkv_v1, rwkv_v2, w_up, w_down):
    bsz, t, d = x.shape
    n = bsz * t
    depth = w_in.shape[0]
    W = WIDTH
    topk = min(TOPK_MAX, t // 4)

    lb_soft = jax.nn.softmax(hgrn_lb_logits.astype(F32), axis=0)
    lower_bounds = jnp.cumsum(lb_soft, axis=0) - lb_soft[0:1]
    tables = _rope_tables(t, HEAD_DIM, N_HEADS) + _rope_tables(t, IDX_DIM, IDX_HEADS)

    o_a, o_b, o_iq = 0, 4 * W, 7 * W
    o_ik, o_iw = o_iq + IDX_HEADS * IDX_DIM, o_iq + IDX_HEADS * IDX_DIM + IDX_DIM
    o_c = o_iw + IDX_HEADS
    c_cols = 3 * W + 128
    o_d = o_c + c_cols
    o_df = o_d + 3 * W
    o_g = o_df + N_HEADS
    main_blocks = dict(b_qkv=4, c=2, df=(7 * W + c_cols) // LANES)
    assert main_blocks["c"] * c_cols == 7 * W

    x2 = x.reshape(n, d)
    w_in_bf16 = w_in.astype(BF16)
    v_first = None
    for l in range(depth):
        wl, wlb = w_in[l], w_in_bf16[l]
        w_main = jnp.concatenate([
            wlb[:, o_a:o_a + 7 * W], wlb[:, o_c:o_c + c_cols], _pad_cols(wlb[:, o_df:o_df + N_HEADS], LANES),
            jnp.zeros((d, 256), BF16)], axis=1)
        w_idx = jnp.concatenate([
            wl[:, o_iq:o_iq + W], jnp.tile(wl[:, o_ik:o_ik + IDX_DIM], (1, LANES // IDX_DIM)),
            _pad_cols(wl[:, o_iw:o_iw + IDX_HEADS], LANES)], axis=1)
        w_fox = jnp.concatenate([(wl[:, o_d:o_d + W] * (LOG2E * HEAD_DIM ** -0.5)).astype(BF16),
                                 wlb[:, o_d + W:o_d + 3 * W]], axis=1)

        main2, fox_qk, fox_vt, idx2 = _projections(x2, t, norm_mix_pre[l], w_main, w_fox, w_idx)
        main3 = main2.reshape(bsz, t, -1)

        y_a = _hgrn(main3, lower_bounds[l], hgrn_norm_w[l])

        y_b = _dsa(*_dsa_prep(main3, idx2.reshape(bsz, t, -1), tables, main_blocks["b_qkv"]), topk)

        prm = dict(
            mu=_pad_cols(rwkv_mu[l][None, :], RWKV_COLS), w0=rwkv_w0[l][None, :],
            w2=_pad_rows(rwkv_w2[l], RWKV_LORA_LANES, 0), a0=rwkv_a0[l][None, :],
            a2=_pad_rows(rwkv_a2[l], RWKV_LORA_LANES, 32), g2=_pad_rows(rwkv_g2[l], RWKV_LORA_LANES, 64),
            k_k=rwkv_k_k[l][None, :], k_a=rwkv_k_a[l][None, :], r_k=rwkv_r_k[l].reshape(1, W),
            ln_w=rwkv_ln_w[l][None, :], ln_b=rwkv_ln_b[l][None, :])
        if l > 0:
            prm.update(v0=rwkv_v0[l - 1][None, :], v1=_pad_cols(rwkv_v1[l - 1], LANES),
                       v2=_pad_rows(rwkv_v2[l - 1], LANES, 0))
        y_c, v_c = _rwkv(main3, main_blocks["c"], v_first, prm)
        if l == 0:
            v_first = v_c

        c_rep = _fox_cum(main3, main_blocks["df"], _pad_cols(fox_f_bias[l][None, :], LANES))
        y_d = _fox(fox_qk.reshape(bsz, t, 2 * W), fox_vt, c_rep)

        ys = [y.reshape(n, W) for y in (y_a, y_b, y_c, y_d)]
        x2 = _merge(x2, ys, norm_mix_pre[l], wlb[:, o_g:o_g + 4 * d], w_branch[l], w_out[l], norm_mix_post[l])
        x2 = _mlp(x2, norm_mlp_pre[l], w_up[l], w_down[l], norm_mlp_post[l])
    return x2.reshape(bsz, t, d)
```

```python
import functools

import jax
import jax.numpy as jnp
from jax import lax
from jax.experimental import pallas as pl
from jax.experimental.pallas import tpu as pltpu

F32 = jnp.float32
BF16 = jnp.bfloat16

HEAD_DIM = 64
N_HEADS = 4
WIDTH = N_HEADS * HEAD_DIM
IDX_HEADS = 8
IDX_DIM = 32
TOPK_MAX = 256
ROPE_THETA = 10000.0
NORM_EPS = 1e-6
RWKV_GN_EPS = 64e-5
MASK_VALUE = -1e30
LOG2E = 1.4426950408889634
RWKV_LORA_LANES = 128
RWKV_COLS = 3 * WIDTH + RWKV_LORA_LANES

LANES = 128
VMEM_LIMIT = 48 * 1024 * 1024

NN = ((1,), (0,))
NT = ((1,), (1,))
TN = ((0,), (0,))


def _dot(a, b, dims=NN):
    return lax.dot_general(a, b, (dims, ((), ())), preferred_element_type=F32)


def _split2(x):
    hi = x.astype(BF16)
    lo = (x - hi.astype(F32)).astype(BF16)
    return hi, lo


def _split3(x):
    hi = x.astype(BF16)
    r1 = x - hi.astype(F32)
    mid = r1.astype(BF16)
    lo = (r1 - mid.astype(F32)).astype(BF16)
    return hi, mid, lo


def _dot3(a, b, dims=NN):
    ah, al = _split2(a)
    bh, bl = _split2(b)
    return _dot(ah, bh, dims) + (_dot(ah, bl, dims) + _dot(al, bh, dims))


def _dot_exact_lhs(a_bf16, b, dims=NN):
    b1, b2, b3 = _split3(b)
    return _dot(a_bf16, b1, dims) + (_dot(a_bf16, b2, dims) + _dot(a_bf16, b3, dims))


def _dot_exact_rhs(a, b_bf16, dims=NN):
    a1, a2 = _split2(a)
    return _dot(a1, b_bf16, dims) + _dot(a2, b_bf16, dims)


def _iota(shape, dim):
    return lax.broadcasted_iota(jnp.int32, shape, dim)


def _head_block_ones(n, group):
    return (_iota((n, n), 0) // group == _iota((n, n), 1) // group).astype(BF16)


def _lower_tri_ones(n):
    return (_iota((n, n), 0) >= _iota((n, n), 1)).astype(BF16)


def _sigmoid(x):
    return 1.0 / (1.0 + jnp.exp(-x))


def _softplus(x):
    return jnp.maximum(x, 0.0) + jnp.log(1.0 + jnp.exp(-jnp.abs(x)))


def _params(sem):
    return pltpu.CompilerParams(dimension_semantics=sem, vmem_limit_bytes=VMEM_LIMIT)


def _norm_rows(x, g):
    return x * lax.rsqrt(jnp.mean(x * x, axis=-1, keepdims=True) + NORM_EPS) * g


def _projections_kernel(x_ref, g_ref, wm_ref, wf_ref, wih_ref, wil_ref, om_ref, of_ref, ovt_ref, oi_ref):
    h = _norm_rows(x_ref[...], g_ref[...])
    hh, hl = _split2(h)
    om_ref[...] = _dot(hh, wm_ref[...])
    fox = _dot(hh, wf_ref[...])
    of_ref[...] = fox[:, :2 * WIDTH].astype(of_ref.dtype)
    ovt_ref[0] = fox[:, 2 * WIDTH:].T.astype(ovt_ref.dtype)
    oi_ref[...] = _dot(hh, wih_ref[...]) + (_dot(hh, wil_ref[...]) + _dot(hl, wih_ref[...]))


def _projections(x2, t, g, w_main, w_fox, w_idx, tm=512):
    n, d = x2.shape
    tiles = t // tm
    wih, wil = _split2(w_idx)
    weights = [w_main.astype(BF16), w_fox.astype(BF16), wih, wil]
    rows = lambda c: pl.BlockSpec((tm, c), lambda i: (i, 0))
    return pl.pallas_call(
        _projections_kernel, grid=(n // tm,),
        in_specs=[rows(d), pl.BlockSpec((1, d), lambda i: (0, 0))]
        + [pl.BlockSpec(w.shape, lambda i: (0, 0)) for w in weights],
        out_specs=[rows(w_main.shape[1]), rows(2 * WIDTH),
                   pl.BlockSpec((1, WIDTH, tm), lambda i: (i // tiles, 0, i % tiles)), rows(w_idx.shape[1])],
        out_shape=[jax.ShapeDtypeStruct((n, w_main.shape[1]), F32), jax.ShapeDtypeStruct((n, 2 * WIDTH), BF16),
                   jax.ShapeDtypeStruct((n // t, WIDTH, t), BF16), jax.ShapeDtypeStruct((n, w_idx.shape[1]), F32)],
        compiler_params=_params(("parallel",)), name="norm_proj",
    )(x2, g.reshape(1, d), *weights)


HGRN_CHUNK = 64
HGRN_GROUP = 8


def _dot1(a, b, dims=NN):
    return _dot(a.astype(BF16), b.astype(BF16), dims)


def _hgrn_kernel(q_ref, f_ref, i_ref, g_ref, lb_ref, nw_ref, o_ref, st_ref, *, n_chunks):
    C, G = HGRN_CHUNK, HGRN_GROUP

    @pl.when(pl.program_id(1) == 0)
    def _():
        st_ref[...] = jnp.zeros_like(st_ref)

    lb = lb_ref[...]
    nw = nw_ref[...]
    tri = _lower_tri_ones(C)
    bd = _head_block_ones(WIDTH, HEAD_DIM)
    bd_mask = _iota((WIDTH, WIDTH), 0) // HEAD_DIM == _iota((WIDTH, WIDTH), 1) // HEAD_DIM
    row_top = _iota((G, 1), 0)
    lane_head = _iota((1, WIDTH), 1) // HEAD_DIM

    def chunk(c, carry):
        sl = pl.ds(pl.multiple_of(c * C, C), C)
        fl = f_ref[0, sl, :]
        f = lb + (1.0 - lb) * _sigmoid(fl)
        k = (1.0 - lb) * _sigmoid(-fl)
        b = _dot_exact_lhs(tri, jnp.log(f))
        q = q_ref[0, sl, :] * (HEAD_DIM ** -0.5)
        v = i_ref[0, sl, :]
        st = st_ref[...]
        o_inter = _dot1(q * jnp.exp(b), st, NT)

        n_groups = C // G
        o_rows = [o_inter[g * G:(g + 1) * G] for g in range(n_groups)]
        pieces = []
        for g in range(n_groups):
            s0 = g * G
            q_top, b_top, k_top = q[s0:s0 + G], b[s0:s0 + G], k[s0:s0 + G]
            for u in range(G):
                top = q_top * k_top[u:u + 1] * jnp.exp(jnp.minimum(b_top - b_top[u:u + 1], 0.0))
                pieces.append(jnp.where(row_top >= u, top, 0.0))
        r = _dot(jnp.concatenate(pieces, axis=0).astype(BF16), bd)
        att, v_st = [], []
        for g in range(n_groups - 1):
            s0 = g * G
            b_top, k_top, v_top = b[s0:s0 + G], k[s0:s0 + G], v[s0:s0 + G]
            b_ref = b_top[G - 1:G]
            qp = (q[s0 + G:] * jnp.exp(b[s0 + G:] - b_ref)).astype(BF16)
            kp = k_top * jnp.exp(b_ref - b_top)
            k_st = jnp.concatenate([jnp.where(lane_head == h, kp, 0.0) for h in range(N_HEADS)], axis=0)
            v_st.append(jnp.concatenate([jnp.where(lane_head == h, v_top, 0.0) for h in range(N_HEADS)], axis=0))
            att.append(_dot(qp, k_st.astype(BF16), NT))
        for g in range(n_groups):
            v_top = v[g * G:(g + 1) * G]
            for u in range(G):
                o_rows[g] = o_rows[g] + r[(g * G + u) * G:(g * G + u + 1) * G] * v_top[u:u + 1]
        for g in range(n_groups - 1):
            rest = _dot(att[g].astype(BF16), v_st[g].astype(BF16))
            for g2 in range(g + 1, n_groups):
                o_rows[g2] = o_rows[g2] + rest[(g2 - g - 1) * G:(g2 - g) * G]
        o = jnp.concatenate(o_rows, axis=0)

        b_last = b[C - 1:C, :]
        upd = _dot1(v, k * jnp.exp(b_last - b), TN)
        st_ref[...] = st * jnp.exp(b_last) + jnp.where(bd_mask, upd, 0.0)

        ms = _dot_exact_rhs(o * o, bd) * (1.0 / HEAD_DIM)
        on = o * lax.rsqrt(ms + NORM_EPS) * nw
        gl = g_ref[0, sl, :]
        o_ref[0, sl, :] = (on * (gl * _sigmoid(gl))).astype(o_ref.dtype)
        return carry

    lax.fori_loop(0, n_chunks, chunk, 0, unroll=True)


def _hgrn(main3, lb, norm_w, tb=512):
    bsz, t, _ = main3.shape
    col = lambda j: pl.BlockSpec((1, tb, WIDTH), lambda b, i, j=j: (b, i, j))
    vec = pl.BlockSpec((1, WIDTH), lambda b, i: (0, 0))
    return pl.pallas_call(
        functools.partial(_hgrn_kernel, n_chunks=tb // HGRN_CHUNK),
        grid=(bsz, t // tb),
        in_specs=[col(0), col(1), col(2), col(3), vec, vec],
        out_specs=pl.BlockSpec((1, tb, WIDTH), lambda b, i: (b, i, 0)),
        out_shape=jax.ShapeDtypeStruct((bsz, t, WIDTH), BF16),
        scratch_shapes=[pltpu.VMEM((WIDTH, WIDTH), F32)],
        compiler_params=_params(("parallel", "arbitrary")), name="hgrn2",
    )(main3, main3, main3, main3, lb.reshape(1, WIDTH), jnp.tile(norm_w, N_HEADS).reshape(1, WIDTH))


RWKV_CHUNK = 64


def _rwkv_kernel(*refs, n_chunks, has_vgate):
    if has_vgate:
        (c_ref, vf_ref, mu_ref, w0_ref, w2_ref, a0_ref, a2_ref, g2_ref, kkw_ref, kaw_ref, rk_ref,
         lnw_ref, lnb_ref, v0_ref, v1_ref, v2_ref, y_ref, vout_ref, s_ref, prev_ref, y_sc) = refs
    else:
        (c_ref, mu_ref, w0_ref, w2_ref, a0_ref, a2_ref, g2_ref, kkw_ref, kaw_ref, rk_ref,
         lnw_ref, lnb_ref, y_ref, vout_ref, s_ref, prev_ref, y_sc) = refs
    C = RWKV_CHUNK
    W = WIDTH
    tb = n_chunks * C

    @pl.when(pl.program_id(1) == 0)
    def _():
        s_ref[...] = jnp.zeros_like(s_ref)
        prev_ref[...] = jnp.zeros_like(prev_ref)

    p = c_ref[0]
    shifted = jnp.where(_iota((tb, 1), 0) == 0, prev_ref[...], pltpu.roll(p, 1, axis=0))
    prev_ref[...] = p[tb - 1:tb, :]
    xm = p + (shifted - p) * mu_ref[...]
    r = xm[:, 0:W]
    k = xm[:, W:2 * W]
    v = xm[:, 2 * W:3 * W]
    lora = xm[:, 3 * W:3 * W + RWKV_LORA_LANES]

    bd = _head_block_ones(W, HEAD_DIM)
    w_log = -_softplus(-(w0_ref[...] + _dot3(jnp.tanh(lora), w2_ref[...]))) - 0.5
    log_decay = -jnp.exp(w_log)
    a = _sigmoid(a0_ref[...] + _dot3(lora, a2_ref[...]))
    g = _dot3(_sigmoid(lora), g2_ref[...])
    if has_vgate:
        vg = _dot3(_dot3(v, v1_ref[...]), v2_ref[...])
        v = v + (vf_ref[0] - v) * _sigmoid(v0_ref[...] + vg)
    vout_ref[0] = v
    kk = k * kkw_ref[...]
    kk = kk * lax.rsqrt(jnp.maximum(_dot_exact_rhs(kk * kk, bd), 1e-24))
    k = k * (1.0 + (a - 1.0) * kaw_ref[...])

    tri = _lower_tri_ones(C)
    lane_head = _iota((1, W), 1) // HEAD_DIM
    hc = N_HEADS * C
    rt = _iota((hc, hc), 0) % C
    ct = _iota((hc, hc), 1) % C
    strict_lower = rt > ct
    lower = rt >= ct
    eye = (_iota((hc, hc), 0) == _iota((hc, hc), 1)).astype(F32)

    def stack(m):
        return jnp.concatenate([jnp.where(lane_head == h, m, 0.0) for h in range(N_HEADS)], axis=0)

    chunks = range(n_chunks)
    a_s, b_s, k_s, r_s, v_s, p_last, m, lak, lrb, lrk = ([None] * n_chunks for _ in range(10))
    for c in chunks:
        sl = slice(c * C, (c + 1) * C)
        ld = log_decay[sl]
        cum = _dot_exact_lhs(tri, ld)
        pdec = jnp.exp(cum)
        pinv = jnp.exp(-cum)
        p_last[c] = pdec[C - 1:C, :]
        a_s[c] = stack(-kk[sl] * jnp.exp(cum - ld)).astype(BF16)
        b_s[c] = stack(kk[sl] * a[sl] * pinv).astype(BF16)
        k_s[c] = stack(k[sl] * pinv).astype(BF16)
        r_s[c] = stack(r[sl] * pdec)
        v_s[c] = stack(v[sl]).astype(BF16)
        gram = _dot(jnp.concatenate([a_s[c], r_s[c].astype(BF16)], axis=0),
                    jnp.concatenate([b_s[c], k_s[c]], axis=0), NT)
        m[c] = jnp.where(strict_lower, gram[:hc, :hc], 0.0)
        lak[c] = jnp.where(strict_lower, gram[:hc, hc:], 0.0)
        lrb[c] = jnp.where(lower, gram[hc:, :hc], 0.0)
        lrk[c] = jnp.where(lower, gram[hc:, hc:], 0.0)

    x = [eye + m[c] for c in chunks]
    power = 2
    while power < C:
        m = [_dot1(m[c], m[c]) for c in chunks]
        x = [x[c] + _dot1(x[c], m[c]) for c in chunks]
        power *= 2
    lakv = [_dot1(lak[c], v_s[c]) for c in chunks]
    lrkv = [_dot1(lrk[c], v_s[c]) for c in chunks]
    w1 = [_dot1(x[c], lakv[c]) for c in chunks]
    a2 = [_dot1(x[c], a_s[c]) for c in chunks]
    r2 = [r_s[c] + _dot1(lrb[c], a2[c]) for c in chunks]
    y0 = [_dot1(lrb[c], w1[c]) + lrkv[c] for c in chunks]
    t1 = [_dot1(a2[c], b_s[c], TN) for c in chunks]
    t0 = [_dot1(w1[c], b_s[c], TN) + _dot(v_s[c], k_s[c], TN) for c in chunks]

    s = s_ref[...]
    for c in chunks:
        y = _dot1(r2[c], s, NT) + y0[c]
        y_sc[c * C:(c + 1) * C, :] = sum(y[h * C:(h + 1) * C] for h in range(N_HEADS))
        s = (s + _dot1(s, t1[c]) + t0[c]) * p_last[c]
    s_ref[...] = s

    y = y_sc[...]
    inv_n = 1.0 / HEAD_DIM
    mean = _dot_exact_rhs(y, bd) * inv_n
    yc = y - mean
    var = _dot_exact_rhs(yc * yc, bd) * inv_n
    yn = yc * lax.rsqrt(var + RWKV_GN_EPS) * lnw_ref[...] + lnb_ref[...]
    yn = yn + _dot_exact_rhs(r * k * rk_ref[...], bd) * v
    y_ref[0] = (yn * g).astype(y_ref.dtype)


def _rwkv(c3, col_block, v_first, prm, tb=512):
    bsz, t, _ = c3.shape
    has_vgate = v_first is not None
    blk = lambda w: pl.BlockSpec((1, tb, w), lambda b, i: (b, i, 0))
    cblk = pl.BlockSpec((1, tb, RWKV_COLS), lambda b, i: (b, i, col_block))
    full = lambda a: pl.BlockSpec(a.shape, lambda b, i: (0,) * a.ndim)
    names = ["mu", "w0", "w2", "a0", "a2", "g2", "k_k", "k_a", "r_k", "ln_w", "ln_b"]
    if has_vgate:
        names += ["v0", "v1", "v2"]
    args = [c3] + ([v_first] if has_vgate else []) + [prm[n] for n in names]
    in_specs = [cblk] + ([blk(WIDTH)] if has_vgate else []) + [full(prm[n]) for n in names]
    return pl.pallas_call(
        functools.partial(_rwkv_kernel, n_chunks=tb // RWKV_CHUNK, has_vgate=has_vgate),
        grid=(bsz, t // tb),
        in_specs=in_specs,
        out_specs=[blk(WIDTH), blk(WIDTH)],
        out_shape=[jax.ShapeDtypeStruct((bsz, t, WIDTH), BF16), jax.ShapeDtypeStruct((bsz, t, WIDTH), F32)],
        scratch_shapes=[pltpu.VMEM((WIDTH, WIDTH), F32), pltpu.VMEM((1, RWKV_COLS), F32),
                        pltpu.VMEM((tb, WIDTH), F32)],
        compiler_params=_params(("parallel", "arbitrary")), name="rwkv7",
    )(*args)


def _fox_cum_kernel(f_ref, bias_ref, c_ref, carry_ref, *, tb):
    @pl.when(pl.program_id(1) == 0)
    def _():
        carry_ref[...] = jnp.zeros_like(carry_ref)

    logf = -_softplus(-(f_ref[0] + bias_ref[...]))
    c = _dot_exact_lhs(_lower_tri_ones(tb), logf) + carry_ref[...]
    carry_ref[...] = c[tb - 1:tb, :]
    for h in range(N_HEADS):
        pick = (_iota((LANES, LANES), 0) == h).astype(BF16)
        c1, c2, c3 = _split3(c)
        c_ref[0, h] = LOG2E * (_dot(c1, pick) + (_dot(c2, pick) + _dot(c3, pick)))


def _fox_cum(main3, col_block, bias_row, tb=512):
    bsz, t, _ = main3.shape
    return pl.pallas_call(
        functools.partial(_fox_cum_kernel, tb=tb),
        grid=(bsz, t // tb),
        in_specs=[pl.BlockSpec((1, tb, LANES), lambda b, i: (b, i, col_block)),
                  pl.BlockSpec((1, LANES), lambda b, i: (0, 0))],
        out_specs=pl.BlockSpec((1, N_HEADS, tb, LANES), lambda b, i: (b, 0, i, 0)),
        out_shape=jax.ShapeDtypeStruct((bsz, N_HEADS, t, LANES), F32),
        scratch_shapes=[pltpu.VMEM((1, LANES), F32)],
        compiler_params=_params(("parallel", "arbitrary")), name="fox_cumgate",
    )(main3, bias_row)


ATT_QB = 512
ATT_KB = 512


def _stack_heads(q):
    lane_head = _iota((1, WIDTH), 1) // HEAD_DIM
    zero = jnp.zeros_like(q)
    return jnp.concatenate([jnp.where(lane_head == h, q, zero) for h in range(N_HEADS)], axis=0)


def _softmax_stage_t(s_t, m_sc, l_sc):
    m_old = m_sc[...]
    m_new = jnp.maximum(m_old, jnp.max(s_t, axis=0, keepdims=True))
    alpha = jnp.exp2(m_old - m_new)
    p_t = jnp.exp2(s_t - m_new)
    l_sc[...] = alpha * l_sc[...] + jnp.sum(p_t, axis=0, keepdims=True)
    m_sc[...] = m_new
    return p_t.astype(BF16), alpha


def _pv_stage_t(vt_blk, p_t, alpha, acc_sc):
    for h in range(N_HEADS):
        cs = slice(h * ATT_QB, (h + 1) * ATT_QB)
        pv = _dot(vt_blk[h * HEAD_DIM:(h + 1) * HEAD_DIM, :], p_t[:, cs])
        acc_sc[h] = alpha[:, cs] * acc_sc[h] + pv


def _attention_t(n_blocks, qk, post, post_last, vt_ref, bufs, m_sc, l_sc, acc_sc):
    KB = ATT_KB
    s_buf, p_buf, a_buf = bufs[0:2], bufs[2:4], bufs[4:6]
    _softmax_init_t(m_sc, l_sc, acc_sc)

    def vt_blk(j):
        return vt_ref[0, :, pl.ds(pl.multiple_of(j * KB, KB), KB)]

    def pending(j, par):
        _pv_stage_t(vt_blk(jnp.maximum(j - 1, 0)), p_buf[1 - par][...], a_buf[1 - par][...], acc_sc)

    def step(j, par):
        raw = s_buf[par][...]
        s_buf[1 - par][...] = qk(j + 1)
        p_t, alpha = _softmax_stage_t(post(j, raw), m_sc, l_sc)
        pending(j, par)
        p_buf[par][...] = p_t
        a_buf[par][...] = alpha

    def final(j, par):
        p_t, alpha = _softmax_stage_t(post_last(j, s_buf[par][...]), m_sc, l_sc)
        pending(j, par)
        _pv_stage_t(vt_blk(j), p_t, alpha, acc_sc)

    s_buf[0][...] = qk(0)
    p_buf[1][...] = jnp.zeros_like(p_buf[1])
    a_buf[1][...] = jnp.ones_like(a_buf[1])
    pairs = (n_blocks - 1) // 2

    def pair(t, carry):
        step(2 * t, 0)
        step(2 * t + 1, 1)
        return carry

    lax.fori_loop(0, pairs, pair, 0)
    odd_left = (n_blocks - 1) % 2 == 1

    @pl.when(odd_left)
    def _():
        step(2 * pairs, 0)
        final(2 * pairs + 1, 1)

    @pl.when(jnp.logical_not(odd_left))
    def _():
        final(2 * pairs, 0)


def _attention_bufs():
    cols = N_HEADS * ATT_QB
    return ([pltpu.VMEM((ATT_KB, cols), F32)] * 2 + [pltpu.VMEM((ATT_KB, cols), BF16)] * 2
            + [pltpu.VMEM((1, cols), F32)] * 2)


def _softmax_init_t(m_sc, l_sc, acc_sc):
    m_sc[...] = jnp.full_like(m_sc, MASK_VALUE)
    l_sc[...] = jnp.zeros_like(l_sc)
    acc_sc[...] = jnp.zeros_like(acc_sc)


def _softmax_finish_t(l_sc, acc_sc):
    l = l_sc[...]
    out_t = jnp.concatenate([acc_sc[h] / l[:, h * ATT_QB:(h + 1) * ATT_QB] for h in range(N_HEADS)], axis=0)
    return out_t.T


def _fox_kernel(q_ref, k_ref, vt_ref, ck_ref, o_ref, m_sc, l_sc, acc_sc, *bufs):
    QB, KB = ATT_QB, ATT_KB
    i = pl.program_id(1)
    qstack = _stack_heads(q_ref[0])
    qpos = i * QB + _iota((1, N_HEADS * QB), 1) % QB

    def qk(j):
        return _dot(k_ref[0, pl.ds(pl.multiple_of(j * KB, KB), KB), :], qstack, NT)

    def post(j, s_t):
        ks = pl.ds(pl.multiple_of(j * KB, KB), KB)
        rep = lambda c: jnp.concatenate([c] * (QB // LANES), axis=1)
        return jnp.concatenate([s_t[:, h * QB:(h + 1) * QB] - rep(ck_ref[0, h, ks, :]) for h in range(N_HEADS)], axis=1)

    def post_last(j, s_t):
        return jnp.where(j * KB + _iota((KB, 1), 0) <= qpos, post(j, s_t), MASK_VALUE)

    _attention_t((i * QB) // KB + 1, qk, post, post_last, vt_ref, bufs, m_sc, l_sc, acc_sc)
    o_ref[0] = _softmax_finish_t(l_sc, acc_sc).astype(o_ref.dtype)


def _fox(qk3, v_t, c_rep):
    bsz, t, _ = qk3.shape
    cols = N_HEADS * ATT_QB
    return pl.pallas_call(
        _fox_kernel,
        grid=(bsz, t // ATT_QB),
        in_specs=[pl.BlockSpec((1, ATT_QB, WIDTH), lambda b, i: (b, i, 0)),
                  pl.BlockSpec((1, t, WIDTH), lambda b, i: (b, 0, 1)),
                  pl.BlockSpec((1, WIDTH, t), lambda b, i: (b, 0, 0)),
                  pl.BlockSpec((1, N_HEADS, t, LANES), lambda b, i: (b, 0, 0, 0))],
        out_specs=pl.BlockSpec((1, ATT_QB, WIDTH), lambda b, i: (b, i, 0)),
        out_shape=jax.ShapeDtypeStruct((bsz, t, WIDTH), BF16),
        scratch_shapes=[pltpu.VMEM((1, cols), F32), pltpu.VMEM((1, cols), F32),
                        pltpu.VMEM((N_HEADS, HEAD_DIM, ATT_QB), F32)] + _attention_bufs(),
        compiler_params=_params(("parallel", "arbitrary")), name="fox_attention",
    )(qk3, qk3, v_t, c_rep)


def _swap_halves(x, half):
    n = x.shape[-1]
    lower = (_iota((1, n), 1) % (2 * half)) < half
    return jnp.where(lower, pltpu.roll(x, n - half, axis=1), pltpu.roll(x, half, axis=1))


def _dsa_prep_kernel(q_ref, k_ref, v_ref, iq_ref, ik_ref, iw_ref, ch_ref, sh_ref, ci_ref, si_ref,
                     selh_ref, sell_ref, qo_ref, ko_ref, vo_ref, q3_ref, k3_ref, wi_ref):
    ch, sh, ci, si = ch_ref[...], sh_ref[...], ci_ref[...], si_ref[...]

    def rope(x, c, s, half):
        return x * c + _swap_halves(x, half) * s

    qo_ref[0] = (rope(q_ref[0], ch, sh, HEAD_DIM // 2) * (LOG2E * HEAD_DIM ** -0.5)).astype(BF16)
    ko_ref[0] = rope(k_ref[0], ch, sh, HEAD_DIM // 2).astype(BF16)
    vo_ref[0] = v_ref[0].T.astype(BF16)
    qh, ql = _split2(rope(iq_ref[0], ci, si, IDX_DIM // 2) * (IDX_DIM ** -0.5))
    q3_ref[0] = (_dot(qh, selh_ref[...]) + _dot(ql, sell_ref[...])).astype(BF16)
    kh, kl = _split2(rope(ik_ref[0], ci[:, :LANES], si[:, :LANES], IDX_DIM // 2))
    seg = _iota((1, LANES), 1) // IDX_DIM
    k3_ref[0] = jnp.where(seg < 2, kh, jnp.where(seg == 2, kl, jnp.zeros_like(kl)))
    wi_ref[0] = (iw_ref[0] * (IDX_HEADS ** -0.5)).T


def _dsa_prep(main3, idx3, tables, qkv_block0, tb=512):
    bsz, t, _ = main3.shape
    iq_w = IDX_HEADS * IDX_DIM
    r = jnp.arange(iq_w)[:, None]
    c = jnp.arange(IDX_HEADS * LANES)[None, :]
    same = (r // IDX_DIM == c // LANES) & (r % IDX_DIM == c % IDX_DIM)
    seg = (c % LANES) // IDX_DIM
    sel_hi = (same & ((seg == 0) | (seg == 2))).astype(BF16)
    sel_lo = (same & (seg == 1)).astype(BF16)
    mcol = lambda j: pl.BlockSpec((1, tb, WIDTH), lambda b, i, j=j: (b, i, qkv_block0 + j))
    tab = pl.BlockSpec((tb, WIDTH), lambda b, i: (i, 0))
    sel = pl.BlockSpec(sel_hi.shape, lambda b, i: (0, 0))
    out = lambda w: pl.BlockSpec((1, tb, w), lambda b, i: (b, i, 0))
    out_t = lambda w: pl.BlockSpec((1, w, tb), lambda b, i: (b, 0, i))
    shape = lambda dt, w: jax.ShapeDtypeStruct((bsz, t, w), dt)
    return pl.pallas_call(
        _dsa_prep_kernel, grid=(bsz, t // tb),
        in_specs=[mcol(0), mcol(1), mcol(2), out(iq_w),
                  pl.BlockSpec((1, tb, LANES), lambda b, i: (b, i, iq_w // LANES)),
                  pl.BlockSpec((1, tb, LANES), lambda b, i: (b, i, iq_w // LANES + 1)),
                  tab, tab, tab, tab, sel, sel],
        out_specs=[out(WIDTH), out(WIDTH), out_t(WIDTH), out(IDX_HEADS * LANES), out(LANES), out_t(LANES)],
        out_shape=[shape(BF16, WIDTH), shape(BF16, WIDTH), jax.ShapeDtypeStruct((bsz, WIDTH, t), BF16),
                   shape(BF16, IDX_HEADS * LANES), shape(BF16, LANES), jax.ShapeDtypeStruct((bsz, LANES, t), F32)],
        compiler_params=_params(("parallel", "parallel")), name="dsa_prep",
    )(main3, main3, main3, idx3, idx3, idx3, *tables, sel_hi, sel_lo)


def _dsa_kernel(q_ref, k_ref, vt_ref, q3_ref, k3_ref, wt_ref, o_ref, key_sc, tie_sc, m_sc, l_sc, acc_sc,
                *bufs, topk):
    QB, KB = ATT_QB, ATT_KB
    i = pl.program_id(1)
    nkb = (i * QB) // KB + 1
    qpos = i * QB + _iota((1, QB), 1)
    int_min = jnp.int32(-2 ** 31)

    q3 = q3_ref[0]
    qs = jnp.concatenate([q3[:, h * LANES:(h + 1) * LANES] for h in range(IDX_HEADS)], axis=0)
    wt = wt_ref[0]

    def score_block(j):
        ks = pl.ds(pl.multiple_of(j * KB, KB), KB)
        r = _dot(k3_ref[0, ks, :], qs, NT)
        sc = jnp.zeros((KB, QB), F32)
        for h in range(IDX_HEADS):
            sc = sc + jnp.maximum(r[:, h * QB:(h + 1) * QB], 0.0) * wt[h:h + 1, :]
        sc = jnp.where(j * KB + _iota((KB, 1), 0) <= qpos, sc, MASK_VALUE)
        sc = jnp.where(sc == 0.0, 0.0, sc)
        bits = pltpu.bitcast(sc, jnp.int32)
        key_sc[ks, :] = jnp.where(bits < 0, bits ^ jnp.int32(0x7FFFFFFF), bits)

    def score_pair(t, carry):
        score_block(2 * t)
        score_block(2 * t + 1)
        return carry

    lax.fori_loop(0, nkb // 2, score_pair, 0)

    @pl.when(nkb % 2 == 1)
    def _():
        score_block(nkb - 1)

    def count_ge(cand):
        def blk(j, acc):
            base = pl.multiple_of(j * KB, KB)
            for r0 in range(0, KB, 64):
                acc = acc + jnp.where(key_sc[pl.ds(base + r0, 64), :] >= cand, 1.0, 0.0)
            return acc
        acc = lax.fori_loop(0, nkb, blk, jnp.zeros((64, QB), F32))
        return jnp.sum(acc, axis=0, keepdims=True)

    kf = float(topk)

    def bit_step(step, v):
        trial = jnp.where(step == 0, jnp.zeros_like(v), v | (jnp.int32(1) << (31 - step)))
        return jnp.where(count_ge(trial) >= kf, trial, v)

    thr = lax.fori_loop(0, 32, bit_step, jnp.full((1, QB), int_min, jnp.int32))
    need = kf - count_ge(thr + 1)

    lt = _lower_tri_ones(KB)
    qstack = _stack_heads(q_ref[0])

    def qk(j):
        return _dot(k_ref[0, pl.ds(pl.multiple_of(j * KB, KB), KB), :], qstack, NT)

    def post(j, s_t):
        ks = pl.ds(pl.multiple_of(j * KB, KB), KB)
        key = key_sc[ks, :]
        tie = key == thr
        tie_b = jnp.where(tie, 1.0, 0.0).astype(BF16)
        rank = tie_sc[...] + _dot(lt, tie_b)
        tie_sc[...] = rank[KB - 1:KB, :]
        sel = (key > thr) | (tie & (rank <= need))
        sel = sel & (j * KB + _iota((KB, 1), 0) <= qpos)
        bias = jnp.where(sel, 0.0, MASK_VALUE)
        return s_t + jnp.concatenate([bias] * N_HEADS, axis=1)

    tie_sc[...] = jnp.zeros_like(tie_sc)
    _attention_t(nkb, qk, post, post, vt_ref, bufs, m_sc, l_sc, acc_sc)
    o_ref[0] = _softmax_finish_t(l_sc, acc_sc).astype(o_ref.dtype)


def _dsa(q, k, v_t, q3, k3, w_t, topk):
    bsz, t, _ = q.shape
    qblk = lambda w: pl.BlockSpec((1, ATT_QB, w), lambda b, i: (b, i, 0))
    seq = lambda w: pl.BlockSpec((1, t, w), lambda b, i: (b, 0, 0))
    cols = N_HEADS * ATT_QB
    return pl.pallas_call(
        functools.partial(_dsa_kernel, topk=topk),
        grid=(bsz, t // ATT_QB),
        in_specs=[qblk(WIDTH), seq(WIDTH), pl.BlockSpec((1, WIDTH, t), lambda b, i: (b, 0, 0)),
                  qblk(IDX_HEADS * LANES), seq(LANES),
                  pl.BlockSpec((1, IDX_HEADS, ATT_QB), lambda b, i: (b, 0, i))],
        out_specs=qblk(WIDTH),
        out_shape=jax.ShapeDtypeStruct((bsz, t, WIDTH), BF16),
        scratch_shapes=[pltpu.VMEM((t, ATT_QB), jnp.int32), pltpu.VMEM((1, ATT_QB), F32), pltpu.VMEM((1, cols), F32),
                        pltpu.VMEM((1, cols), F32), pltpu.VMEM((N_HEADS, HEAD_DIM, ATT_QB), F32)] + _attention_bufs(),
        compiler_params=_params(("parallel", "arbitrary")), name="dsa_attention",
    )(q, k, v_t, q3, k3, w_t)


def _merge_kernel(x_ref, ya_ref, yb_ref, yc_ref, yd_ref, gpre_ref, wg_ref, wb_ref, wo_ref, g_ref, o_ref):
    d = x_ref.shape[-1]
    x = x_ref[...]
    h = _norm_rows(x, gpre_ref[...]).astype(BF16)
    merged = None
    for n, y_ref in enumerate((ya_ref, yb_ref, yc_ref, yd_ref)):
        gate = _sigmoid(_dot(h, wg_ref[:, n * d:(n + 1) * d]))
        term = gate * _dot(y_ref[...], wb_ref[n])
        merged = term if merged is None else merged + term
    mix = _dot(merged.astype(BF16), wo_ref[...])
    o_ref[...] = x + _norm_rows(mix, g_ref[...])


def _merge(x2, ys, g_pre, w_gate, w_branch, w_out, g_post, tm=256):
    n, d = x2.shape
    row = lambda w: pl.BlockSpec((tm, w), lambda i: (i, 0))
    vec = pl.BlockSpec((1, d), lambda i: (0, 0))
    return pl.pallas_call(
        _merge_kernel, grid=(n // tm,),
        in_specs=[row(d)] + [row(WIDTH)] * 4 + [vec, pl.BlockSpec(w_gate.shape, lambda i: (0, 0)),
                                                 pl.BlockSpec(w_branch.shape, lambda i: (0, 0, 0)),
                                                 pl.BlockSpec(w_out.shape, lambda i: (0, 0)), vec],
        out_specs=row(d),
        out_shape=jax.ShapeDtypeStruct((n, d), F32),
        compiler_params=_params(("parallel",)), name="gated_merge",
    )(x2, *ys, g_pre.reshape(1, d), w_gate.astype(BF16), w_branch.astype(BF16), w_out.astype(BF16),
      g_post.reshape(1, d))


MLP_VMEM_LIMIT = 56 * 1024 * 1024


def _mlp_kernel(x_ref, gpre_ref, wu_ref, wd_ref, gpost_ref, o_ref, *, tf):
    x = x_ref[...]
    h = _norm_rows(x, gpre_ref[...]).astype(BF16)
    acc = None
    for k in range(wu_ref.shape[1] // tf):
        u = jnp.maximum(_dot(h, wu_ref[:, k * tf:(k + 1) * tf]), 0.0)
        part = _dot((u * u).astype(BF16), wd_ref[k * tf:(k + 1) * tf, :])
        acc = part if acc is None else acc + part
    o_ref[...] = x + _norm_rows(acc, gpost_ref[...])


def _mlp(x2, g_pre, w_up, w_down, g_post, tm=512, tf=1024):
    n, d = x2.shape
    return pl.pallas_call(
        functools.partial(_mlp_kernel, tf=tf), grid=(n // tm,),
        in_specs=[pl.BlockSpec((tm, d), lambda i: (i, 0)), pl.BlockSpec((1, d), lambda i: (0, 0)),
                  pl.BlockSpec(w_up.shape, lambda i: (0, 0)), pl.BlockSpec(w_down.shape, lambda i: (0, 0)),
                  pl.BlockSpec((1, d), lambda i: (0, 0))],
        out_specs=pl.BlockSpec((tm, d), lambda i: (i, 0)),
        out_shape=jax.ShapeDtypeStruct((n, d), F32),
        compiler_params=pltpu.CompilerParams(dimension_semantics=("parallel",), vmem_limit_bytes=MLP_VMEM_LIMIT),
        name="mlp",
    )(x2, g_pre.reshape(1, d), w_up.astype(BF16), w_down.astype(BF16), g_post.reshape(1, d))


def _rope_tables(t, dim, groups):
    inv = 1.0 / (ROPE_THETA ** (jnp.arange(0, dim, 2, dtype=F32) / dim))
    ang = jnp.arange(t, dtype=F32)[:, None] * inv[None, :]
    cos, sin = jnp.cos(ang), jnp.sin(ang)
    return jnp.tile(jnp.concatenate([cos, cos], axis=1), (1, groups)), jnp.tile(jnp.concatenate([-sin, sin], axis=1), (1, groups))


def _pad_rows(w, rows, offset):
    return jnp.zeros((rows, w.shape[1]), w.dtype).at[offset:offset + w.shape[0]].set(w)


def _pad_cols(w, cols):
    return jnp.pad(w, ((0, 0), (0, cols - w.shape[1])))


def kernel(x, norm_mix_pre, norm_mix_post, norm_mlp_pre, norm_mlp_post, w_in, w_branch, w_out, hgrn_lb_logits, hgrn_norm_w, fox_f_bias, rwkv_mu, rwkv_w0, rwkv_w2, rwkv_a0, rwkv_a2, rwkv_g2, rwkv_k_k, rwkv_k_a, rwkv_r_k, rwkv_ln_w, rwkv_ln_b, rwkv_v0, rwkv_v1, rwkv_v2, w_up, w_down):
    bsz, t, d = x.shape
    n = bsz * t
    depth = w_in.shape[0]
    W = WIDTH
    topk = min(TOPK_MAX, t // 4)

    lb_soft = jax.nn.softmax(hgrn_lb_logits.astype(F32), axis=0)
    lower_bounds = jnp.cumsum(lb_soft, axis=0) - lb_soft[0:1]
    tables = _rope_tables(t, HEAD_DIM, N_HEADS) + _rope_tables(t, IDX_DIM, IDX_HEADS)

    assert d == 4 * W and t % ATT_QB == 0 and t % ATT_KB == 0
    lora_a, lora_g = rwkv_w2.shape[1], rwkv_w2.shape[1] + rwkv_a2.shape[1]
    o_a, o_iq = 0, 7 * W
    o_ik, o_iw = o_iq + IDX_HEADS * IDX_DIM, o_iq + IDX_HEADS * IDX_DIM + IDX_DIM
    o_c = o_iw + IDX_HEADS
    c_cols = RWKV_COLS
    o_d = o_c + c_cols
    o_df = o_d + 3 * W
    o_g = o_df + N_HEADS
    main_pad = 2 * LANES
    main_blocks = dict(b_qkv=4, c=2, df=(7 * W + c_cols) // LANES)
    assert main_blocks["c"] * c_cols == 7 * W

    x2 = x.reshape(n, d)
    w_in_bf16 = w_in.astype(BF16)
    v_first = None
    for l in range(depth):
        wl, wlb = w_in[l], w_in_bf16[l]
        w_main = jnp.concatenate([
            wlb[:, o_a:o_a + 7 * W], wlb[:, o_c:o_c + c_cols], _pad_cols(wlb[:, o_df:o_df + N_HEADS], LANES),
            jnp.zeros((d, main_pad), BF16)], axis=1)
        w_idx = jnp.concatenate([
            wl[:, o_iq:o_iq + W], jnp.tile(wl[:, o_ik:o_ik + IDX_DIM], (1, LANES // IDX_DIM)),
            _pad_cols(wl[:, o_iw:o_iw + IDX_HEADS], LANES)], axis=1)
        w_fox = jnp.concatenate([(wl[:, o_d:o_d + W] * (LOG2E * HEAD_DIM ** -0.5)).astype(BF16),
                                 wlb[:, o_d + W:o_d + 3 * W]], axis=1)

        main2, fox_qk, fox_vt, idx2 = _projections(x2, t, norm_mix_pre[l], w_main, w_fox, w_idx)
        main3 = main2.reshape(bsz, t, -1)

        y_a = _hgrn(main3, lower_bounds[l], hgrn_norm_w[l])

        y_b = _dsa(*_dsa_prep(main3, idx2.reshape(bsz, t, -1), tables, main_blocks["b_qkv"]), topk)

        prm = dict(
            mu=_pad_cols(rwkv_mu[l][None, :], RWKV_COLS), w0=rwkv_w0[l][None, :],
            w2=_pad_rows(rwkv_w2[l], RWKV_LORA_LANES, 0), a0=rwkv_a0[l][None, :],
            a2=_pad_rows(rwkv_a2[l], RWKV_LORA_LANES, lora_a), g2=_pad_rows(rwkv_g2[l], RWKV_LORA_LANES, lora_g),
            k_k=rwkv_k_k[l][None, :], k_a=rwkv_k_a[l][None, :], r_k=rwkv_r_k[l].reshape(1, W),
            ln_w=rwkv_ln_w[l][None, :], ln_b=rwkv_ln_b[l][None, :])
        if l > 0:
            prm.update(v0=rwkv_v0[l - 1][None, :], v1=_pad_cols(rwkv_v1[l - 1], LANES),
                       v2=_pad_rows(rwkv_v2[l - 1], LANES, 0))
        y_c, v_c = _rwkv(main3, main_blocks["c"], v_first, prm)
        if l == 0:
            v_first = v_c

        c_rep = _fox_cum(main3, main_blocks["df"], _pad_cols(fox_f_bias[l][None, :], LANES))
        y_d = _fox(fox_qk.reshape(bsz, t, 2 * W), fox_vt, c_rep)

        ys = [y.reshape(n, W) for y in (y_a, y_b, y_c, y_d)]
        x2 = _merge(x2, ys, norm_mix_pre[l], wlb[:, o_g:o_g + 4 * d], w_branch[l], w_out[l], norm_mix_post[l])
        x2 = _mlp(x2, norm_mlp_pre[l], w_up[l], w_down[l], norm_mlp_post[l])
    return x2.reshape(bsz, t, d)
```

```python
import functools

import jax
import jax.numpy as jnp
from jax import lax
from jax.experimental import pallas as pl
from jax.experimental.pallas import tpu as pltpu

F32 = jnp.float32
BF16 = jnp.bfloat16

HEAD_DIM = 64
N_HEADS = 4
WIDTH = N_HEADS * HEAD_DIM
IDX_HEADS = 8
IDX_DIM = 32
TOPK_MAX = 256
ROPE_THETA = 10000.0
NORM_EPS = 1e-6
RWKV_GN_EPS = 64e-5
MASK_VALUE = -1e30
LOG2E = 1.4426950408889634
RWKV_LORA_LANES = 128
RWKV_COLS = 3 * WIDTH + RWKV_LORA_LANES

LANES = 128
VMEM_LIMIT = 48 * 1024 * 1024

NN = ((1,), (0,))
NT = ((1,), (1,))
TN = ((0,), (0,))


def _dot(a, b, dims=NN):
    return lax.dot_general(a, b, (dims, ((), ())), preferred_element_type=F32)


def _split2(x):
    hi = x.astype(BF16)
    lo = (x - hi.astype(F32)).astype(BF16)
    return hi, lo


def _split3(x):
    hi = x.astype(BF16)
    r1 = x - hi.astype(F32)
    mid = r1.astype(BF16)
    lo = (r1 - mid.astype(F32)).astype(BF16)
    return hi, mid, lo


def _dot3(a, b, dims=NN):
    ah, al = _split2(a)
    bh, bl = _split2(b)
    return _dot(ah, bh, dims) + (_dot(ah, bl, dims) + _dot(al, bh, dims))


def _dot_exact_lhs(a_bf16, b, dims=NN):
    b1, b2, b3 = _split3(b)
    return _dot(a_bf16, b1, dims) + (_dot(a_bf16, b2, dims) + _dot(a_bf16, b3, dims))


def _dot_exact_rhs(a, b_bf16, dims=NN):
    a1, a2 = _split2(a)
    return _dot(a1, b_bf16, dims) + _dot(a2, b_bf16, dims)


def _iota(shape, dim):
    return lax.broadcasted_iota(jnp.int32, shape, dim)


def _head_block_ones(n, group):
    return (_iota((n, n), 0) // group == _iota((n, n), 1) // group).astype(BF16)


def _lower_tri_ones(n):
    return (_iota((n, n), 0) >= _iota((n, n), 1)).astype(BF16)


def _sigmoid(x):
    return 1.0 / (1.0 + jnp.exp(-x))


def _softplus(x):
    return jnp.maximum(x, 0.0) + jnp.log(1.0 + jnp.exp(-jnp.abs(x)))


def _params(sem):
    return pltpu.CompilerParams(dimension_semantics=sem, vmem_limit_bytes=VMEM_LIMIT)


def _norm_rows(x, g):
    return x * lax.rsqrt(jnp.mean(x * x, axis=-1, keepdims=True) + NORM_EPS) * g


def _projections_kernel(x_ref, g_ref, wm_ref, wf_ref, wih_ref, wil_ref, om_ref, of_ref, ovt_ref, oi_ref):
    h = _norm_rows(x_ref[...], g_ref[...])
    hh, hl = _split2(h)
    om_ref[...] = _dot(hh, wm_ref[...])
    fox = _dot(hh, wf_ref[...])
    of_ref[...] = fox[:, :2 * WIDTH].astype(of_ref.dtype)
    ovt_ref[0] = fox[:, 2 * WIDTH:].T.astype(ovt_ref.dtype)
    oi_ref[...] = _dot(hh, wih_ref[...]) + (_dot(hh, wil_ref[...]) + _dot(hl, wih_ref[...]))


def _projections(x2, t, g, w_main, w_fox, w_idx, tm=512):
    n, d = x2.shape
    tiles = t // tm
    wih, wil = _split2(w_idx)
    weights = [w_main.astype(BF16), w_fox.astype(BF16), wih, wil]
    rows = lambda c: pl.BlockSpec((tm, c), lambda i: (i, 0))
    return pl.pallas_call(
        _projections_kernel, grid=(n // tm,),
        in_specs=[rows(d), pl.BlockSpec((1, d), lambda i: (0, 0))]
        + [pl.BlockSpec(w.shape, lambda i: (0, 0)) for w in weights],
        out_specs=[rows(w_main.shape[1]), rows(2 * WIDTH),
                   pl.BlockSpec((1, WIDTH, tm), lambda i: (i // tiles, 0, i % tiles)), rows(w_idx.shape[1])],
        out_shape=[jax.ShapeDtypeStruct((n, w_main.shape[1]), F32), jax.ShapeDtypeStruct((n, 2 * WIDTH), BF16),
                   jax.ShapeDtypeStruct((n // t, WIDTH, t), BF16), jax.ShapeDtypeStruct((n, w_idx.shape[1]), F32)],
        compiler_params=_params(("parallel",)), name="norm_proj",
    )(x2, g.reshape(1, d), *weights)


HGRN_CHUNK = 64
HGRN_GROUP = 8


def _dot1(a, b, dims=NN):
    return _dot(a.astype(BF16), b.astype(BF16), dims)


def _hgrn_kernel(q_ref, f_ref, i_ref, g_ref, lb_ref, nw_ref, o_ref, st_ref, *, n_chunks):
    C, G = HGRN_CHUNK, HGRN_GROUP

    @pl.when(pl.program_id(1) == 0)
    def _():
        st_ref[...] = jnp.zeros_like(st_ref)

    lb = lb_ref[...]
    nw = nw_ref[...]
    tri = _lower_tri_ones(C)
    bd = _head_block_ones(WIDTH, HEAD_DIM)
    bd_mask = _iota((WIDTH, WIDTH), 0) // HEAD_DIM == _iota((WIDTH, WIDTH), 1) // HEAD_DIM
    row_top = _iota((G, 1), 0)
    lane_head = _iota((1, WIDTH), 1) // HEAD_DIM

    def chunk(c, carry):
        sl = pl.ds(pl.multiple_of(c * C, C), C)
        fl = f_ref[0, sl, :]
        f = lb + (1.0 - lb) * _sigmoid(fl)
        k = (1.0 - lb) * _sigmoid(-fl)
        b = _dot_exact_lhs(tri, jnp.log(f))
        q = q_ref[0, sl, :] * (HEAD_DIM ** -0.5)
        v = i_ref[0, sl, :]
        st = st_ref[...]
        o_inter = _dot1(q * jnp.exp(b), st, NT)

        n_groups = C // G
        o_rows = [o_inter[g * G:(g + 1) * G] for g in range(n_groups)]
        pieces = []
        for g in range(n_groups):
            s0 = g * G
            q_top, b_top, k_top = q[s0:s0 + G], b[s0:s0 + G], k[s0:s0 + G]
            for u in range(G):
                top = q_top * k_top[u:u + 1] * jnp.exp(jnp.minimum(b_top - b_top[u:u + 1], 0.0))
                pieces.append(jnp.where(row_top >= u, top, 0.0))
        r = _dot(jnp.concatenate(pieces, axis=0).astype(BF16), bd)
        att, v_st = [], []
        for g in range(n_groups - 1):
            s0 = g * G
            b_top, k_top, v_top = b[s0:s0 + G], k[s0:s0 + G], v[s0:s0 + G]
            b_ref = b_top[G - 1:G]
            qp = (q[s0 + G:] * jnp.exp(b[s0 + G:] - b_ref)).astype(BF16)
            kp = k_top * jnp.exp(b_ref - b_top)
            k_st = jnp.concatenate([jnp.where(lane_head == h, kp, 0.0) for h in range(N_HEADS)], axis=0)
            v_st.append(jnp.concatenate([jnp.where(lane_head == h, v_top, 0.0) for h in range(N_HEADS)], axis=0))
            att.append(_dot(qp, k_st.astype(BF16), NT))
        for g in range(n_groups):
            v_top = v[g * G:(g + 1) * G]
            for u in range(G):
                o_rows[g] = o_rows[g] + r[(g * G + u) * G:(g * G + u + 1) * G] * v_top[u:u + 1]
        for g in range(n_groups - 1):
            rest = _dot(att[g].astype(BF16), v_st[g].astype(BF16))
            for g2 in range(g + 1, n_groups):
                o_rows[g2] = o_rows[g2] + rest[(g2 - g - 1) * G:(g2 - g) * G]
        o = jnp.concatenate(o_rows, axis=0)

        b_last = b[C - 1:C, :]
        upd = _dot1(v, k * jnp.exp(b_last - b), TN)
        st_ref[...] = st * jnp.exp(b_last) + jnp.where(bd_mask, upd, 0.0)

        ms = _dot_exact_rhs(o * o, bd) * (1.0 / HEAD_DIM)
        on = o * lax.rsqrt(ms + NORM_EPS) * nw
        gl = g_ref[0, sl, :]
        o_ref[0, sl, :] = (on * (gl * _sigmoid(gl))).astype(o_ref.dtype)
        return carry

    lax.fori_loop(0, n_chunks, chunk, 0, unroll=True)


def _hgrn(main3, lb, norm_w, tb=512):
    bsz, t, _ = main3.shape
    col = lambda j: pl.BlockSpec((1, tb, WIDTH), lambda b, i, j=j: (b, i, j))
    vec = pl.BlockSpec((1, WIDTH), lambda b, i: (0, 0))
    return pl.pallas_call(
        functools.partial(_hgrn_kernel, n_chunks=tb // HGRN_CHUNK),
        grid=(bsz, t // tb),
        in_specs=[col(0), col(1), col(2), col(3), vec, vec],
        out_specs=pl.BlockSpec((1, tb, WIDTH), lambda b, i: (b, i, 0)),
        out_shape=jax.ShapeDtypeStruct((bsz, t, WIDTH), BF16),
        scratch_shapes=[pltpu.VMEM((WIDTH, WIDTH), F32)],
        compiler_params=_params(("parallel", "arbitrary")), name="hgrn2",
    )(main3, main3, main3, main3, lb.reshape(1, WIDTH), jnp.tile(norm_w, N_HEADS).reshape(1, WIDTH))


RWKV_CHUNK = 64


def _rwkv_kernel(*refs, n_chunks, has_vgate):
    if has_vgate:
        (c_ref, vf_ref, mu_ref, w0_ref, w2_ref, a0_ref, a2_ref, g2_ref, kkw_ref, kaw_ref, rk_ref,
         lnw_ref, lnb_ref, v0_ref, v1_ref, v2_ref, y_ref, vout_ref, s_ref, prev_ref, y_sc) = refs
    else:
        (c_ref, mu_ref, w0_ref, w2_ref, a0_ref, a2_ref, g2_ref, kkw_ref, kaw_ref, rk_ref,
         lnw_ref, lnb_ref, y_ref, vout_ref, s_ref, prev_ref, y_sc) = refs
    C = RWKV_CHUNK
    W = WIDTH
    tb = n_chunks * C

    @pl.when(pl.program_id(1) == 0)
    def _():
        s_ref[...] = jnp.zeros_like(s_ref)
        prev_ref[...] = jnp.zeros_like(prev_ref)

    p = c_ref[0]
    shifted = jnp.where(_iota((tb, 1), 0) == 0, prev_ref[...], pltpu.roll(p, 1, axis=0))
    prev_ref[...] = p[tb - 1:tb, :]
    xm = p + (shifted - p) * mu_ref[...]
    r = xm[:, 0:W]
    k = xm[:, W:2 * W]
    v = xm[:, 2 * W:3 * W]
    lora = xm[:, 3 * W:3 * W + RWKV_LORA_LANES]

    bd = _head_block_ones(W, HEAD_DIM)
    w_log = -_softplus(-(w0_ref[...] + _dot3(jnp.tanh(lora), w2_ref[...]))) - 0.5
    log_decay = -jnp.exp(w_log)
    a = _sigmoid(a0_ref[...] + _dot3(lora, a2_ref[...]))
    g = _dot3(_sigmoid(lora), g2_ref[...])
    if has_vgate:
        vg = _dot3(_dot3(v, v1_ref[...]), v2_ref[...])
        v = v + (vf_ref[0] - v) * _sigmoid(v0_ref[...] + vg)
    vout_ref[0] = v
    kk = k * kkw_ref[...]
    kk = kk * lax.rsqrt(jnp.maximum(_dot_exact_rhs(kk * kk, bd), 1e-24))
    k = k * (1.0 + (a - 1.0) * kaw_ref[...])

    tri = _lower_tri_ones(C)
    lane_head = _iota((1, W), 1) // HEAD_DIM
    hc = N_HEADS * C
    rt = _iota((hc, hc), 0) % C
    ct = _iota((hc, hc), 1) % C
    strict_lower = rt > ct
    lower = rt >= ct
    eye = (_iota((hc, hc), 0) == _iota((hc, hc), 1)).astype(F32)

    def stack(m):
        return jnp.concatenate([jnp.where(lane_head == h, m, 0.0) for h in range(N_HEADS)], axis=0)

    chunks = range(n_chunks)
    a_s, b_s, k_s, r_s, v_s, p_last, m, lak, lrb, lrk = ([None] * n_chunks for _ in range(10))
    for c in chunks:
        sl = slice(c * C, (c + 1) * C)
        ld = log_decay[sl]
        cum = _dot_exact_lhs(tri, ld)
        pdec = jnp.exp(cum)
        pinv = jnp.exp(-cum)
        p_last[c] = pdec[C - 1:C, :]
        a_s[c] = stack(-kk[sl] * jnp.exp(cum - ld)).astype(BF16)
        b_s[c] = stack(kk[sl] * a[sl] * pinv).astype(BF16)
        k_s[c] = stack(k[sl] * pinv).astype(BF16)
        r_s[c] = stack(r[sl] * pdec)
        v_s[c] = stack(v[sl]).astype(BF16)
        gram = _dot(jnp.concatenate([a_s[c], r_s[c].astype(BF16)], axis=0),
                    jnp.concatenate([b_s[c], k_s[c]], axis=0), NT)
        m[c] = jnp.where(strict_lower, gram[:hc, :hc], 0.0)
        lak[c] = jnp.where(strict_lower, gram[:hc, hc:], 0.0)
        lrb[c] = jnp.where(lower, gram[hc:, :hc], 0.0)
        lrk[c] = jnp.where(lower, gram[hc:, hc:], 0.0)

    x = [eye + m[c] for c in chunks]
    power = 2
    while power < C:
        m = [_dot1(m[c], m[c]) for c in chunks]
        x = [x[c] + _dot1(x[c], m[c]) for c in chunks]
        power *= 2
    lakv = [_dot1(lak[c], v_s[c]) for c in chunks]
    lrkv = [_dot1(lrk[c], v_s[c]) for c in chunks]
    w1 = [_dot1(x[c], lakv[c]) for c in chunks]
    a2 = [_dot1(x[c], a_s[c]) for c in chunks]
    r2 = [r_s[c] + _dot1(lrb[c], a2[c]) for c in chunks]
    y0 = [_dot1(lrb[c], w1[c]) + lrkv[c] for c in chunks]
    t1 = [_dot1(a2[c], b_s[c], TN) for c in chunks]
    t0 = [_dot1(w1[c], b_s[c], TN) + _dot(v_s[c], k_s[c], TN) for c in chunks]

    s = s_ref[...]
    for c in chunks:
        y = _dot1(r2[c], s, NT) + y0[c]
        y_sc[c * C:(c + 1) * C, :] = sum(y[h * C:(h + 1) * C] for h in range(N_HEADS))
        s = (s + _dot1(s, t1[c]) + t0[c]) * p_last[c]
    s_ref[...] = s

    y = y_sc[...]
    inv_n = 1.0 / HEAD_DIM
    mean = _dot_exact_rhs(y, bd) * inv_n
    yc = y - mean
    var = _dot_exact_rhs(yc * yc, bd) * inv_n
    yn = yc * lax.rsqrt(var + RWKV_GN_EPS) * lnw_ref[...] + lnb_ref[...]
    yn = yn + _dot_exact_rhs(r * k * rk_ref[...], bd) * v
    y_ref[0] = (yn * g).astype(y_ref.dtype)


def _rwkv(c3, col_block, v_first, prm, tb=512):
    bsz, t, _ = c3.shape
    has_vgate = v_first is not None
    blk = lambda w: pl.BlockSpec((1, tb, w), lambda b, i: (b, i, 0))
    cblk = pl.BlockSpec((1, tb, RWKV_COLS), lambda b, i: (b, i, col_block))
    full = lambda a: pl.BlockSpec(a.shape, lambda b, i: (0,) * a.ndim)
    names = ["mu", "w0", "w2", "a0", "a2", "g2", "k_k", "k_a", "r_k", "ln_w", "ln_b"]
    if has_vgate:
        names += ["v0", "v1", "v2"]
    args = [c3] + ([v_first] if has_vgate else []) + [prm[n] for n in names]
    in_specs = [cblk] + ([blk(WIDTH)] if has_vgate else []) + [full(prm[n]) for n in names]
    return pl.pallas_call(
        functools.partial(_rwkv_kernel, n_chunks=tb // RWKV_CHUNK, has_vgate=has_vgate),
        grid=(bsz, t // tb),
        in_specs=in_specs,
        out_specs=[blk(WIDTH), blk(WIDTH)],
        out_shape=[jax.ShapeDtypeStruct((bsz, t, WIDTH), BF16), jax.ShapeDtypeStruct((bsz, t, WIDTH), F32)],
        scratch_shapes=[pltpu.VMEM((WIDTH, WIDTH), F32), pltpu.VMEM((1, RWKV_COLS), F32),
                        pltpu.VMEM((tb, WIDTH), F32)],
        compiler_params=_params(("parallel", "arbitrary")), name="rwkv7",
    )(*args)


def _fox_cum_kernel(f_ref, bias_ref, c_ref, carry_ref, *, tb):
    @pl.when(pl.program_id(1) == 0)
    def _():
        carry_ref[...] = jnp.zeros_like(carry_ref)

    logf = -_softplus(-(f_ref[0] + bias_ref[...]))
    c = _dot_exact_lhs(_lower_tri_ones(tb), logf) + carry_ref[...]
    carry_ref[...] = c[tb - 1:tb, :]
    for h in range(N_HEADS):
        pick = (_iota((LANES, LANES), 0) == h).astype(BF16)
        c1, c2, c3 = _split3(c)
        c_ref[0, h] = LOG2E * (_dot(c1, pick) + (_dot(c2, pick) + _dot(c3, pick)))


def _fox_cum(main3, col_block, bias_row, tb=512):
    bsz, t, _ = main3.shape
    return pl.pallas_call(
        functools.partial(_fox_cum_kernel, tb=tb),
        grid=(bsz, t // tb),
        in_specs=[pl.BlockSpec((1, tb, LANES), lambda b, i: (b, i, col_block)),
                  pl.BlockSpec((1, LANES), lambda b, i: (0, 0))],
        out_specs=pl.BlockSpec((1, N_HEADS, tb, LANES), lambda b, i: (b, 0, i, 0)),
        out_shape=jax.ShapeDtypeStruct((bsz, N_HEADS, t, LANES), F32),
        scratch_shapes=[pltpu.VMEM((1, LANES), F32)],
        compiler_params=_params(("parallel", "arbitrary")), name="fox_cumgate",
    )(main3, bias_row)


ATT_QB = 512
ATT_KB = 512


def _stack_heads(q):
    lane_head = _iota((1, WIDTH), 1) // HEAD_DIM
    zero = jnp.zeros_like(q)
    return jnp.concatenate([jnp.where(lane_head == h, q, zero) for h in range(N_HEADS)], axis=0)


def _softmax_stage_t(s_t, m_sc, l_sc):
    m_old = m_sc[...]
    m_new = jnp.maximum(m_old, jnp.max(s_t, axis=0, keepdims=True))
    alpha = jnp.exp2(m_old - m_new)
    p_t = jnp.exp2(s_t - m_new)
    l_sc[...] = alpha * l_sc[...] + jnp.sum(p_t, axis=0, keepdims=True)
    m_sc[...] = m_new
    return p_t.astype(BF16), alpha


def _pv_stage_t(vt_blk, p_t, alpha, acc_sc):
    for h in range(N_HEADS):
        cs = slice(h * ATT_QB, (h + 1) * ATT_QB)
        pv = _dot(vt_blk[h * HEAD_DIM:(h + 1) * HEAD_DIM, :], p_t[:, cs])
        acc_sc[h] = alpha[:, cs] * acc_sc[h] + pv


def _attention_t(n_blocks, qk, post, post_last, vt_ref, bufs, m_sc, l_sc, acc_sc):
    KB = ATT_KB
    s_buf, p_buf, a_buf = bufs[0:2], bufs[2:4], bufs[4:6]
    _softmax_init_t(m_sc, l_sc, acc_sc)

    def vt_blk(j):
        return vt_ref[0, :, pl.ds(pl.multiple_of(j * KB, KB), KB)]

    def pending(j, par):
        _pv_stage_t(vt_blk(jnp.maximum(j - 1, 0)), p_buf[1 - par][...], a_buf[1 - par][...], acc_sc)

    def step(j, par):
        raw = s_buf[par][...]
        s_buf[1 - par][...] = qk(j + 1)
        p_t, alpha = _softmax_stage_t(post(j, raw), m_sc, l_sc)
        pending(j, par)
        p_buf[par][...] = p_t
        a_buf[par][...] = alpha

    def final(j, par):
        p_t, alpha = _softmax_stage_t(post_last(j, s_buf[par][...]), m_sc, l_sc)
        pending(j, par)
        _pv_stage_t(vt_blk(j), p_t, alpha, acc_sc)

    s_buf[0][...] = qk(0)
    p_buf[1][...] = jnp.zeros_like(p_buf[1])
    a_buf[1][...] = jnp.ones_like(a_buf[1])
    pairs = (n_blocks - 1) // 2

    def pair(t, carry):
        step(2 * t, 0)
        step(2 * t + 1, 1)
        return carry

    lax.fori_loop(0, pairs, pair, 0)
    odd_left = (n_blocks - 1) % 2 == 1

    @pl.when(odd_left)
    def _():
        step(2 * pairs, 0)
        final(2 * pairs + 1, 1)

    @pl.when(jnp.logical_not(odd_left))
    def _():
        final(2 * pairs, 0)


def _attention_bufs():
    cols = N_HEADS * ATT_QB
    return ([pltpu.VMEM((ATT_KB, cols), F32)] * 2 + [pltpu.VMEM((ATT_KB, cols), BF16)] * 2
            + [pltpu.VMEM((1, cols), F32)] * 2)


def _softmax_init_t(m_sc, l_sc, acc_sc):
    m_sc[...] = jnp.full_like(m_sc, MASK_VALUE)
    l_sc[...] = jnp.zeros_like(l_sc)
    acc_sc[...] = jnp.zeros_like(acc_sc)


def _softmax_finish_t(l_sc, acc_sc):
    l = l_sc[...]
    out_t = jnp.concatenate([acc_sc[h] / l[:, h * ATT_QB:(h + 1) * ATT_QB] for h in range(N_HEADS)], axis=0)
    return out_t.T


def _fox_kernel(q_ref, k_ref, vt_ref, ck_ref, o_ref, m_sc, l_sc, acc_sc, *bufs):
    QB, KB = ATT_QB, ATT_KB
    i = pl.program_id(1)
    qstack = _stack_heads(q_ref[0])
    qpos = i * QB + _iota((1, N_HEADS * QB), 1) % QB

    def qk(j):
        return _dot(k_ref[0, pl.ds(pl.multiple_of(j * KB, KB), KB), :], qstack, NT)

    def post(j, s_t):
        ks = pl.ds(pl.multiple_of(j * KB, KB), KB)
        rep = lambda c: jnp.concatenate([c] * (QB // LANES), axis=1)
        return jnp.concatenate([s_t[:, h * QB:(h + 1) * QB] - rep(ck_ref[0, h, ks, :]) for h in range(N_HEADS)], axis=1)

    def post_last(j, s_t):
        return jnp.where(j * KB + _iota((KB, 1), 0) <= qpos, post(j, s_t), MASK_VALUE)

    _attention_t((i * QB) // KB + 1, qk, post, post_last, vt_ref, bufs, m_sc, l_sc, acc_sc)
    o_ref[0] = _softmax_finish_t(l_sc, acc_sc).astype(o_ref.dtype)


def _fox(qk3, v_t, c_rep):
    bsz, t, _ = qk3.shape
    cols = N_HEADS * ATT_QB
    return pl.pallas_call(
        _fox_kernel,
        grid=(bsz, t // ATT_QB),
        in_specs=[pl.BlockSpec((1, ATT_QB, WIDTH), lambda b, i: (b, i, 0)),
                  pl.BlockSpec((1, t, WIDTH), lambda b, i: (b, 0, 1)),
                  pl.BlockSpec((1, WIDTH, t), lambda b, i: (b, 0, 0)),
                  pl.BlockSpec((1, N_HEADS, t, LANES), lambda b, i: (b, 0, 0, 0))],
        out_specs=pl.BlockSpec((1, ATT_QB, WIDTH), lambda b, i: (b, i, 0)),
        out_shape=jax.ShapeDtypeStruct((bsz, t, WIDTH), BF16),
        scratch_shapes=[pltpu.VMEM((1, cols), F32), pltpu.VMEM((1, cols), F32),
                        pltpu.VMEM((N_HEADS, HEAD_DIM, ATT_QB), F32)] + _attention_bufs(),
        compiler_params=_params(("parallel", "arbitrary")), name="fox_attention",
    )(qk3, qk3, v_t, c_rep)


def _swap_halves(x, half):
    n = x.shape[-1]
    lower = (_iota((1, n), 1) % (2 * half)) < half
    return jnp.where(lower, pltpu.roll(x, n - half, axis=1), pltpu.roll(x, half, axis=1))


def _dsa_prep_kernel(q_ref, k_ref, v_ref, iq_ref, ik_ref, iw_ref, ch_ref, sh_ref, ci_ref, si_ref,
                     selh_ref, sell_ref, qo_ref, ko_ref, vo_ref, q3_ref, k3_ref, wi_ref):
    ch, sh, ci, si = ch_ref[...], sh_ref[...], ci_ref[...], si_ref[...]

    def rope(x, c, s, half):
        return x * c + _swap_halves(x, half) * s

    qo_ref[0] = (rope(q_ref[0], ch, sh, HEAD_DIM // 2) * (LOG2E * HEAD_DIM ** -0.5)).astype(BF16)
    ko_ref[0] = rope(k_ref[0], ch, sh, HEAD_DIM // 2).astype(BF16)
    vo_ref[0] = v_ref[0].T.astype(BF16)
    qh, ql = _split2(rope(iq_ref[0], ci, si, IDX_DIM // 2) * (IDX_DIM ** -0.5))
    q3_ref[0] = (_dot(qh, selh_ref[...]) + _dot(ql, sell_ref[...])).astype(BF16)
    kh, kl = _split2(rope(ik_ref[0], ci[:, :LANES], si[:, :LANES], IDX_DIM // 2))
    seg = _iota((1, LANES), 1) // IDX_DIM
    k3_ref[0] = jnp.where(seg < 2, kh, jnp.where(seg == 2, kl, jnp.zeros_like(kl)))
    wi_ref[0] = (iw_ref[0] * (IDX_HEADS ** -0.5)).T


def _dsa_prep(main3, idx3, tables, qkv_block0, tb=512):
    bsz, t, _ = main3.shape
    iq_w = IDX_HEADS * IDX_DIM
    r = jnp.arange(iq_w)[:, None]
    c = jnp.arange(IDX_HEADS * LANES)[None, :]
    same = (r // IDX_DIM == c // LANES) & (r % IDX_DIM == c % IDX_DIM)
    seg = (c % LANES) // IDX_DIM
    sel_hi = (same & ((seg == 0) | (seg == 2))).astype(BF16)
    sel_lo = (same & (seg == 1)).astype(BF16)
    mcol = lambda j: pl.BlockSpec((1, tb, WIDTH), lambda b, i, j=j: (b, i, qkv_block0 + j))
    tab = pl.BlockSpec((tb, WIDTH), lambda b, i: (i, 0))
    sel = pl.BlockSpec(sel_hi.shape, lambda b, i: (0, 0))
    out = lambda w: pl.BlockSpec((1, tb, w), lambda b, i: (b, i, 0))
    out_t = lambda w: pl.BlockSpec((1, w, tb), lambda b, i: (b, 0, i))
    shape = lambda dt, w: jax.ShapeDtypeStruct((bsz, t, w), dt)
    return pl.pallas_call(
        _dsa_prep_kernel, grid=(bsz, t // tb),
        in_specs=[mcol(0), mcol(1), mcol(2), out(iq_w),
                  pl.BlockSpec((1, tb, LANES), lambda b, i: (b, i, iq_w // LANES)),
                  pl.BlockSpec((1, tb, LANES), lambda b, i: (b, i, iq_w // LANES + 1)),
                  tab, tab, tab, tab, sel, sel],
        out_specs=[out(WIDTH), out(WIDTH), out_t(WIDTH), out(IDX_HEADS * LANES), out(LANES), out_t(LANES)],
        out_shape=[shape(BF16, WIDTH), shape(BF16, WIDTH), jax.ShapeDtypeStruct((bsz, WIDTH, t), BF16),
                   shape(BF16, IDX_HEADS * LANES), shape(BF16, LANES), jax.ShapeDtypeStruct((bsz, LANES, t), F32)],
        compiler_params=_params(("parallel", "parallel")), name="dsa_prep",
    )(main3, main3, main3, idx3, idx3, idx3, *tables, sel_hi, sel_lo)


def _dsa_kernel(q_ref, k_ref, vt_ref, q3_ref, k3_ref, wt_ref, o_ref, key_sc, tie_sc, m_sc, l_sc, acc_sc,
                *bufs, topk):
    QB, KB = ATT_QB, ATT_KB
    i = pl.program_id(1)
    nkb = (i * QB) // KB + 1
    qpos = i * QB + _iota((1, QB), 1)
    int_min = jnp.int32(-2 ** 31)

    q3 = q3_ref[0]
    qs = jnp.concatenate([q3[:, h * LANES:(h + 1) * LANES] for h in range(IDX_HEADS)], axis=0)
    wt = wt_ref[0]

    def score_block(j):
        ks = pl.ds(pl.multiple_of(j * KB, KB), KB)
        r = _dot(k3_ref[0, ks, :], qs, NT)
        sc = jnp.zeros((KB, QB), F32)
        for h in range(IDX_HEADS):
            sc = sc + jnp.maximum(r[:, h * QB:(h + 1) * QB], 0.0) * wt[h:h + 1, :]
        sc = jnp.where(j * KB + _iota((KB, 1), 0) <= qpos, sc, MASK_VALUE)
        sc = jnp.where(sc == 0.0, 0.0, sc)
        bits = pltpu.bitcast(sc, jnp.int32)
        key_sc[ks, :] = jnp.where(bits < 0, bits ^ jnp.int32(0x7FFFFFFF), bits)

    def score_pair(t, carry):
        score_block(2 * t)
        score_block(2 * t + 1)
        return carry

    lax.fori_loop(0, nkb // 2, score_pair, 0)

    @pl.when(nkb % 2 == 1)
    def _():
        score_block(nkb - 1)

    def count_ge(cand):
        def blk(j, acc):
            base = pl.multiple_of(j * KB, KB)
            for r0 in range(0, KB, 64):
                acc = acc + jnp.where(key_sc[pl.ds(base + r0, 64), :] >= cand, 1.0, 0.0)
            return acc
        acc = lax.fori_loop(0, nkb, blk, jnp.zeros((64, QB), F32))
        return jnp.sum(acc, axis=0, keepdims=True)

    kf = float(topk)

    def bit_step(step, v):
        trial = jnp.where(step == 0, jnp.zeros_like(v), v | (jnp.int32(1) << (31 - step)))
        return jnp.where(count_ge(trial) >= kf, trial, v)

    thr = lax.fori_loop(0, 32, bit_step, jnp.full((1, QB), int_min, jnp.int32))
    need = kf - count_ge(thr + 1)

    lt = _lower_tri_ones(KB)
    qstack = _stack_heads(q_ref[0])

    def qk(j):
        return _dot(k_ref[0, pl.ds(pl.multiple_of(j * KB, KB), KB), :], qstack, NT)

    def post(j, s_t):
        ks = pl.ds(pl.multiple_of(j * KB, KB), KB)
        key = key_sc[ks, :]
        tie = key == thr
        tie_b = jnp.where(tie, 1.0, 0.0).astype(BF16)
        rank = tie_sc[...] + _dot(lt, tie_b)
        tie_sc[...] = rank[KB - 1:KB, :]
        sel = (key > thr) | (tie & (rank <= need))
        sel = sel & (j * KB + _iota((KB, 1), 0) <= qpos)
        bias = jnp.where(sel, 0.0, MASK_VALUE)
        return s_t + jnp.concatenate([bias] * N_HEADS, axis=1)

    tie_sc[...] = jnp.zeros_like(tie_sc)
    _attention_t(nkb, qk, post, post, vt_ref, bufs, m_sc, l_sc, acc_sc)
    o_ref[0] = _softmax_finish_t(l_sc, acc_sc).astype(o_ref.dtype)


def _dsa(q, k, v_t, q3, k3, w_t, topk):
    bsz, t, _ = q.shape
    qblk = lambda w: pl.BlockSpec((1, ATT_QB, w), lambda b, i: (b, i, 0))
    seq = lambda w: pl.BlockSpec((1, t, w), lambda b, i: (b, 0, 0))
    cols = N_HEADS * ATT_QB
    return pl.pallas_call(
        functools.partial(_dsa_kernel, topk=topk),
        grid=(bsz, t // ATT_QB),
        in_specs=[qblk(WIDTH), seq(WIDTH), pl.BlockSpec((1, WIDTH, t), lambda b, i: (b, 0, 0)),
                  qblk(IDX_HEADS * LANES), seq(LANES),
                  pl.BlockSpec((1, IDX_HEADS, ATT_QB), lambda b, i: (b, 0, i))],
        out_specs=qblk(WIDTH),
        out_shape=jax.ShapeDtypeStruct((bsz, t, WIDTH), BF16),
        scratch_shapes=[pltpu.VMEM((t, ATT_QB), jnp.int32), pltpu.VMEM((1, ATT_QB), F32), pltpu.VMEM((1, cols), F32),
                        pltpu.VMEM((1, cols), F32), pltpu.VMEM((N_HEADS, HEAD_DIM, ATT_QB), F32)] + _attention_bufs(),
        compiler_params=_params(("parallel", "arbitrary")), name="dsa_attention",
    )(q, k, v_t, q3, k3, w_t)


def _merge_kernel(x_ref, ya_ref, yb_ref, yc_ref, yd_ref, gpre_ref, wg_ref, wb_ref, wo_ref, g_ref, o_ref):
    d = x_ref.shape[-1]
    x = x_ref[...]
    h = _norm_rows(x, gpre_ref[...]).astype(BF16)
    merged = None
    for n, y_ref in enumerate((ya_ref, yb_ref, yc_ref, yd_ref)):
        gate = _sigmoid(_dot(h, wg_ref[:, n * d:(n + 1) * d]))
        term = gate * _dot(y_ref[...], wb_ref[n])
        merged = term if merged is None else merged + term
    mix = _dot(merged.astype(BF16), wo_ref[...])
    o_ref[...] = x + _norm_rows(mix, g_ref[...])


def _merge(x2, ys, g_pre, w_gate, w_branch, w_out, layer, g_post, tm=256):
    n, d = x2.shape
    row = lambda w: pl.BlockSpec((tm, w), lambda i: (i, 0))
    vec = pl.BlockSpec((1, d), lambda i: (0, 0))
    return pl.pallas_call(
        _merge_kernel, grid=(n // tm,),
        in_specs=[row(d)] + [row(WIDTH)] * 4 + [vec, pl.BlockSpec(w_gate.shape, lambda i: (0, 0)),
                                                 pl.BlockSpec((None,) + w_branch.shape[1:], lambda i: (layer, 0, 0, 0)),
                                                 pl.BlockSpec((None,) + w_out.shape[1:], lambda i: (layer, 0, 0)), vec],
        out_specs=row(d),
        out_shape=jax.ShapeDtypeStruct((n, d), F32),
        compiler_params=_params(("parallel",)), name="gated_merge",
    )(x2, *ys, g_pre.reshape(1, d), w_gate.astype(BF16), w_branch, w_out, g_post.reshape(1, d))


MLP_VMEM_LIMIT = 56 * 1024 * 1024


def _mlp_kernel(x_ref, gpre_ref, wu_ref, wd_ref, gpost_ref, o_ref, *, tf):
    x = x_ref[...]
    h = _norm_rows(x, gpre_ref[...]).astype(BF16)
    acc = None
    for k in range(wu_ref.shape[1] // tf):
        u = jnp.maximum(_dot(h, wu_ref[:, k * tf:(k + 1) * tf]), 0.0)
        part = _dot((u * u).astype(BF16), wd_ref[k * tf:(k + 1) * tf, :])
        acc = part if acc is None else acc + part
    o_ref[...] = x + _norm_rows(acc, gpost_ref[...])


def _mlp(x2, g_pre, w_up, w_down, layer, g_post, tm=512, tf=1024):
    n, d = x2.shape
    return pl.pallas_call(
        functools.partial(_mlp_kernel, tf=tf), grid=(n // tm,),
        in_specs=[pl.BlockSpec((tm, d), lambda i: (i, 0)), pl.BlockSpec((1, d), lambda i: (0, 0)),
                  pl.BlockSpec((None,) + w_up.shape[1:], lambda i: (layer, 0, 0)),
                  pl.BlockSpec((None,) + w_down.shape[1:], lambda i: (layer, 0, 0)),
                  pl.BlockSpec((1, d), lambda i: (0, 0))],
        out_specs=pl.BlockSpec((tm, d), lambda i: (i, 0)),
        out_shape=jax.ShapeDtypeStruct((n, d), F32),
        compiler_params=pltpu.CompilerParams(dimension_semantics=("parallel",), vmem_limit_bytes=MLP_VMEM_LIMIT),
        name="mlp",
    )(x2, g_pre.reshape(1, d), w_up, w_down, g_post.reshape(1, d))


def _rope_tables(t, dim, groups):
    inv = 1.0 / (ROPE_THETA ** (jnp.arange(0, dim, 2, dtype=F32) / dim))
    ang = jnp.arange(t, dtype=F32)[:, None] * inv[None, :]
    cos, sin = jnp.cos(ang), jnp.sin(ang)
    return jnp.tile(jnp.concatenate([cos, cos], axis=1), (1, groups)), jnp.tile(jnp.concatenate([-sin, sin], axis=1), (1, groups))


def _pad_rows(w, rows, offset):
    return jnp.zeros((rows, w.shape[1]), w.dtype).at[offset:offset + w.shape[0]].set(w)


def _pad_cols(w, cols):
    return jnp.pad(w, ((0, 0), (0, cols - w.shape[1])))


def kernel(x, norm_mix_pre, norm_mix_post, norm_mlp_pre, norm_mlp_post, w_in, w_branch, w_out, hgrn_lb_logits, hgrn_norm_w, fox_f_bias, rwkv_mu, rwkv_w0, rwkv_w2, rwkv_a0, rwkv_a2, rwkv_g2, rwkv_k_k, rwkv_k_a, rwkv_r_k, rwkv_ln_w, rwkv_ln_b, rwkv_v0, rwkv_v1, rwkv_v2, w_up, w_down):
    bsz, t, d = x.shape
    n = bsz * t
    depth = w_in.shape[0]
    W = WIDTH
    topk = min(TOPK_MAX, t // 4)

    lb_soft = jax.nn.softmax(hgrn_lb_logits.astype(F32), axis=0)
    lower_bounds = jnp.cumsum(lb_soft, axis=0) - lb_soft[0:1]
    tables = _rope_tables(t, HEAD_DIM, N_HEADS) + _rope_tables(t, IDX_DIM, IDX_HEADS)

    assert d == 4 * W and t % ATT_QB == 0 and t % ATT_KB == 0
    lora_a, lora_g = rwkv_w2.shape[1], rwkv_w2.shape[1] + rwkv_a2.shape[1]
    o_a, o_iq = 0, 7 * W
    o_ik, o_iw = o_iq + IDX_HEADS * IDX_DIM, o_iq + IDX_HEADS * IDX_DIM + IDX_DIM
    o_c = o_iw + IDX_HEADS
    c_cols = RWKV_COLS
    o_d = o_c + c_cols
    o_df = o_d + 3 * W
    o_g = o_df + N_HEADS
    main_pad = 2 * LANES
    main_blocks = dict(b_qkv=4, c=2, df=(7 * W + c_cols) // LANES)
    assert main_blocks["c"] * c_cols == 7 * W

    x2 = x.reshape(n, d)
    w_in_bf16 = w_in.astype(BF16)
    w_branch_bf16, w_out_bf16 = w_branch.astype(BF16), w_out.astype(BF16)
    w_up_bf16, w_down_bf16 = w_up.astype(BF16), w_down.astype(BF16)
    v_first = None
    for l in range(depth):
        wl, wlb = w_in[l], w_in_bf16[l]
        w_main = jnp.concatenate([
            wlb[:, o_a:o_a + 7 * W], wlb[:, o_c:o_c + c_cols], _pad_cols(wlb[:, o_df:o_df + N_HEADS], LANES),
            jnp.zeros((d, main_pad), BF16)], axis=1)
        w_idx = jnp.concatenate([
            wl[:, o_iq:o_iq + W], jnp.tile(wl[:, o_ik:o_ik + IDX_DIM], (1, LANES // IDX_DIM)),
            _pad_cols(wl[:, o_iw:o_iw + IDX_HEADS], LANES)], axis=1)
        w_fox = jnp.concatenate([(wl[:, o_d:o_d + W] * (LOG2E * HEAD_DIM ** -0.5)).astype(BF16),
                                 wlb[:, o_d + W:o_d + 3 * W]], axis=1)

        main2, fox_qk, fox_vt, idx2 = _projections(x2, t, norm_mix_pre[l], w_main, w_fox, w_idx)
        main3 = main2.reshape(bsz, t, -1)

        y_a = _hgrn(main3, lower_bounds[l], hgrn_norm_w[l])

        y_b = _dsa(*_dsa_prep(main3, idx2.reshape(bsz, t, -1), tables, main_blocks["b_qkv"]), topk)

        prm = dict(
            mu=_pad_cols(rwkv_mu[l][None, :], RWKV_COLS), w0=rwkv_w0[l][None, :],
            w2=_pad_rows(rwkv_w2[l], RWKV_LORA_LANES, 0), a0=rwkv_a0[l][None, :],
            a2=_pad_rows(rwkv_a2[l], RWKV_LORA_LANES, lora_a), g2=_pad_rows(rwkv_g2[l], RWKV_LORA_LANES, lora_g),
            k_k=rwkv_k_k[l][None, :], k_a=rwkv_k_a[l][None, :], r_k=rwkv_r_k[l].reshape(1, W),
            ln_w=rwkv_ln_w[l][None, :], ln_b=rwkv_ln_b[l][None, :])
        if l > 0:
            prm.update(v0=rwkv_v0[l - 1][None, :], v1=_pad_cols(rwkv_v1[l - 1], LANES),
                       v2=_pad_rows(rwkv_v2[l - 1], LANES, 0))
        y_c, v_c = _rwkv(main3, main_blocks["c"], v_first, prm)
        if l == 0:
            v_first = v_c

        c_rep = _fox_cum(main3, main_blocks["df"], _pad_cols(fox_f_bias[l][None, :], LANES))
        y_d = _fox(fox_qk.reshape(bsz, t, 2 * W), fox_vt, c_rep)

        ys = [y.reshape(n, W) for y in (y_a, y_b, y_c, y_d)]
        x2 = _merge(x2, ys, norm_mix_pre[l], wlb[:, o_g:o_g + 4 * d], w_branch_bf16, w_out_bf16, l, norm_mix_post[l])
        x2 = _mlp(x2, norm_mlp_pre[l], w_up_bf16, w_down_bf16, l, norm_mlp_post[l])
    return x2.reshape(bsz, t, d)
```

```python
import functools

import jax
import jax.numpy as jnp
from jax import lax
from jax.experimental import pallas as pl
from jax.experimental.pallas import tpu as pltpu

F32 = jnp.float32
BF16 = jnp.bfloat16

HEAD_DIM = 64
N_HEADS = 4
WIDTH = N_HEADS * HEAD_DIM
IDX_HEADS = 8
IDX_DIM = 32
TOPK_MAX = 256
ROPE_THETA = 10000.0
NORM_EPS = 1e-6
RWKV_GN_EPS = 64e-5
MASK_VALUE = -1e30
LOG2E = 1.4426950408889634
RWKV_LORA_LANES = 128
RWKV_COLS = 3 * WIDTH + RWKV_LORA_LANES

LANES = 128
VMEM_LIMIT = 48 * 1024 * 1024

NN = ((1,), (0,))
NT = ((1,), (1,))
TN = ((0,), (0,))


def _dot(a, b, dims=NN):
    return lax.dot_general(a, b, (dims, ((), ())), preferred_element_type=F32)


def _split2(x):
    hi = x.astype(BF16)
    lo = (x - hi.astype(F32)).astype(BF16)
    return hi, lo


def _split3(x):
    hi = x.astype(BF16)
    r1 = x - hi.astype(F32)
    mid = r1.astype(BF16)
    lo = (r1 - mid.astype(F32)).astype(BF16)
    return hi, mid, lo


def _dot3(a, b, dims=NN):
    ah, al = _split2(a)
    bh, bl = _split2(b)
    return _dot(ah, bh, dims) + (_dot(ah, bl, dims) + _dot(al, bh, dims))


def _dot_exact_lhs(a_bf16, b, dims=NN):
    b1, b2, b3 = _split3(b)
    return _dot(a_bf16, b1, dims) + (_dot(a_bf16, b2, dims) + _dot(a_bf16, b3, dims))


def _dot_exact_rhs(a, b_bf16, dims=NN):
    a1, a2 = _split2(a)
    return _dot(a1, b_bf16, dims) + _dot(a2, b_bf16, dims)


def _iota(shape, dim):
    return lax.broadcasted_iota(jnp.int32, shape, dim)


def _head_block_ones(n, group):
    return (_iota((n, n), 0) // group == _iota((n, n), 1) // group).astype(BF16)


def _lower_tri_ones(n):
    return (_iota((n, n), 0) >= _iota((n, n), 1)).astype(BF16)


def _sigmoid(x):
    return 1.0 / (1.0 + jnp.exp(-x))


def _softplus(x):
    return jnp.maximum(x, 0.0) + jnp.log(1.0 + jnp.exp(-jnp.abs(x)))


def _params(sem):
    return pltpu.CompilerParams(dimension_semantics=sem, vmem_limit_bytes=VMEM_LIMIT)


def _norm_rows(x, g):
    return x * lax.rsqrt(jnp.mean(x * x, axis=-1, keepdims=True) + NORM_EPS) * g


def _projections_kernel(x_ref, g_ref, wm_ref, wf_ref, wih_ref, wil_ref, om_ref, of_ref, ovt_ref, oi_ref):
    h = _norm_rows(x_ref[...], g_ref[...])
    hh, hl = _split2(h)
    om_ref[...] = _dot(hh, wm_ref[...])
    fox = _dot(hh, wf_ref[...])
    of_ref[...] = fox[:, :2 * WIDTH].astype(of_ref.dtype)
    ovt_ref[0] = fox[:, 2 * WIDTH:].T.astype(ovt_ref.dtype)
    oi_ref[...] = _dot(hh, wih_ref[...]) + (_dot(hh, wil_ref[...]) + _dot(hl, wih_ref[...]))


def _projections(x2, t, g, weights, layer, tm=512):
    n, d = x2.shape
    tiles = t // tm
    main_cols, idx_cols = weights[0].shape[2], weights[2].shape[2]
    rows = lambda c: pl.BlockSpec((tm, c), lambda i: (i, 0))
    return pl.pallas_call(
        _projections_kernel, grid=(n // tm,),
        in_specs=[rows(d), pl.BlockSpec((1, d), lambda i: (0, 0))]
        + [pl.BlockSpec((None,) + w.shape[1:], lambda i: (layer, 0, 0)) for w in weights],
        out_specs=[rows(main_cols), rows(2 * WIDTH),
                   pl.BlockSpec((1, WIDTH, tm), lambda i: (i // tiles, 0, i % tiles)), rows(idx_cols)],
        out_shape=[jax.ShapeDtypeStruct((n, main_cols), F32), jax.ShapeDtypeStruct((n, 2 * WIDTH), BF16),
                   jax.ShapeDtypeStruct((n // t, WIDTH, t), BF16), jax.ShapeDtypeStruct((n, idx_cols), F32)],
        compiler_params=_params(("parallel",)), name="norm_proj",
    )(x2, g.reshape(1, d), *weights)


HGRN_CHUNK = 64
HGRN_GROUP = 8


def _dot1(a, b, dims=NN):
    return _dot(a.astype(BF16), b.astype(BF16), dims)


def _hgrn_kernel(q_ref, f_ref, i_ref, g_ref, lb_ref, nw_ref, o_ref, st_ref, *, n_chunks):
    C, G = HGRN_CHUNK, HGRN_GROUP

    @pl.when(pl.program_id(1) == 0)
    def _():
        st_ref[...] = jnp.zeros_like(st_ref)

    lb = lb_ref[...]
    nw = nw_ref[...]
    tri = _lower_tri_ones(C)
    bd = _head_block_ones(WIDTH, HEAD_DIM)
    bd_mask = _iota((WIDTH, WIDTH), 0) // HEAD_DIM == _iota((WIDTH, WIDTH), 1) // HEAD_DIM
    row_top = _iota((G, 1), 0)
    lane_head = _iota((1, WIDTH), 1) // HEAD_DIM

    def chunk(c, carry):
        sl = pl.ds(pl.multiple_of(c * C, C), C)
        fl = f_ref[0, sl, :]
        f = lb + (1.0 - lb) * _sigmoid(fl)
        k = (1.0 - lb) * _sigmoid(-fl)
        b = _dot_exact_lhs(tri, jnp.log(f))
        q = q_ref[0, sl, :] * (HEAD_DIM ** -0.5)
        v = i_ref[0, sl, :]
        st = st_ref[...]
        o_inter = _dot1(q * jnp.exp(b), st, NT)

        n_groups = C // G
        o_rows = [o_inter[g * G:(g + 1) * G] for g in range(n_groups)]
        pieces = []
        for g in range(n_groups):
            s0 = g * G
            q_top, b_top, k_top = q[s0:s0 + G], b[s0:s0 + G], k[s0:s0 + G]
            for u in range(G):
                top = q_top * k_top[u:u + 1] * jnp.exp(jnp.minimum(b_top - b_top[u:u + 1], 0.0))
                pieces.append(jnp.where(row_top >= u, top, 0.0))
        r = _dot(jnp.concatenate(pieces, axis=0).astype(BF16), bd)
        att, v_st = [], []
        for g in range(n_groups - 1):
            s0 = g * G
            b_top, k_top, v_top = b[s0:s0 + G], k[s0:s0 + G], v[s0:s0 + G]
            b_ref = b_top[G - 1:G]
            qp = (q[s0 + G:] * jnp.exp(b[s0 + G:] - b_ref)).astype(BF16)
            kp = k_top * jnp.exp(b_ref - b_top)
            k_st = jnp.concatenate([jnp.where(lane_head == h, kp, 0.0) for h in range(N_HEADS)], axis=0)
            v_st.append(jnp.concatenate([jnp.where(lane_head == h, v_top, 0.0) for h in range(N_HEADS)], axis=0))
            att.append(_dot(qp, k_st.astype(BF16), NT))
        for g in range(n_groups):
            v_top = v[g * G:(g + 1) * G]
            for u in range(G):
                o_rows[g] = o_rows[g] + r[(g * G + u) * G:(g * G + u + 1) * G] * v_top[u:u + 1]
        for g in range(n_groups - 1):
            rest = _dot(att[g].astype(BF16), v_st[g].astype(BF16))
            for g2 in range(g + 1, n_groups):
                o_rows[g2] = o_rows[g2] + rest[(g2 - g - 1) * G:(g2 - g) * G]
        o = jnp.concatenate(o_rows, axis=0)

        b_last = b[C - 1:C, :]
        upd = _dot1(v, k * jnp.exp(b_last - b), TN)
        st_ref[...] = st * jnp.exp(b_last) + jnp.where(bd_mask, upd, 0.0)

        ms = _dot_exact_rhs(o * o, bd) * (1.0 / HEAD_DIM)
        on = o * lax.rsqrt(ms + NORM_EPS) * nw
        gl = g_ref[0, sl, :]
        o_ref[0, sl, :] = (on * (gl * _sigmoid(gl))).astype(o_ref.dtype)
        return carry

    lax.fori_loop(0, n_chunks, chunk, 0, unroll=True)


def _hgrn(main3, lb, norm_w, tb=512):
    bsz, t, _ = main3.shape
    col = lambda j: pl.BlockSpec((1, tb, WIDTH), lambda b, i, j=j: (b, i, j))
    vec = pl.BlockSpec((1, WIDTH), lambda b, i: (0, 0))
    return pl.pallas_call(
        functools.partial(_hgrn_kernel, n_chunks=tb // HGRN_CHUNK),
        grid=(bsz, t // tb),
        in_specs=[col(0), col(1), col(2), col(3), vec, vec],
        out_specs=pl.BlockSpec((1, tb, WIDTH), lambda b, i: (b, i, 0)),
        out_shape=jax.ShapeDtypeStruct((bsz, t, WIDTH), BF16),
        scratch_shapes=[pltpu.VMEM((WIDTH, WIDTH), F32)],
        compiler_params=_params(("parallel", "arbitrary")), name="hgrn2",
    )(main3, main3, main3, main3, lb.reshape(1, WIDTH), jnp.tile(norm_w, N_HEADS).reshape(1, WIDTH))


RWKV_CHUNK = 64


def _rwkv_kernel(*refs, n_chunks, has_vgate):
    if has_vgate:
        (c_ref, vf_ref, mu_ref, w0_ref, w2_ref, a0_ref, a2_ref, g2_ref, kkw_ref, kaw_ref, rk_ref,
         lnw_ref, lnb_ref, v0_ref, v1_ref, v2_ref, y_ref, vout_ref, s_ref, prev_ref, y_sc) = refs
    else:
        (c_ref, mu_ref, w0_ref, w2_ref, a0_ref, a2_ref, g2_ref, kkw_ref, kaw_ref, rk_ref,
         lnw_ref, lnb_ref, y_ref, vout_ref, s_ref, prev_ref, y_sc) = refs
    C = RWKV_CHUNK
    W = WIDTH
    tb = n_chunks * C

    @pl.when(pl.program_id(1) == 0)
    def _():
        s_ref[...] = jnp.zeros_like(s_ref)
        prev_ref[...] = jnp.zeros_like(prev_ref)

    p = c_ref[0]
    shifted = jnp.where(_iota((tb, 1), 0) == 0, prev_ref[...], pltpu.roll(p, 1, axis=0))
    prev_ref[...] = p[tb - 1:tb, :]
    xm = p + (shifted - p) * mu_ref[...]
    r = xm[:, 0:W]
    k = xm[:, W:2 * W]
    v = xm[:, 2 * W:3 * W]
    lora = xm[:, 3 * W:3 * W + RWKV_LORA_LANES]

    bd = _head_block_ones(W, HEAD_DIM)
    w_log = -_softplus(-(w0_ref[...] + _dot3(jnp.tanh(lora), w2_ref[...]))) - 0.5
    log_decay = -jnp.exp(w_log)
    a = _sigmoid(a0_ref[...] + _dot3(lora, a2_ref[...]))
    g = _dot3(_sigmoid(lora), g2_ref[...])
    if has_vgate:
        vg = _dot3(_dot3(v, v1_ref[...]), v2_ref[...])
        v = v + (vf_ref[0] - v) * _sigmoid(v0_ref[...] + vg)
    vout_ref[0] = v
    kk = k * kkw_ref[...]
    kk = kk * lax.rsqrt(jnp.maximum(_dot_exact_rhs(kk * kk, bd), 1e-24))
    k = k * (1.0 + (a - 1.0) * kaw_ref[...])

    tri = _lower_tri_ones(C)
    lane_head = _iota((1, W), 1) // HEAD_DIM
    hc = N_HEADS * C
    rt = _iota((hc, hc), 0) % C
    ct = _iota((hc, hc), 1) % C
    strict_lower = rt > ct
    lower = rt >= ct
    eye = (_iota((hc, hc), 0) == _iota((hc, hc), 1)).astype(F32)

    def stack(m):
        return jnp.concatenate([jnp.where(lane_head == h, m, 0.0) for h in range(N_HEADS)], axis=0)

    chunks = range(n_chunks)
    a_s, b_s, k_s, r_s, v_s, p_last, m, lak, lrb, lrk = ([None] * n_chunks for _ in range(10))
    for c in chunks:
        sl = slice(c * C, (c + 1) * C)
        ld = log_decay[sl]
        cum = _dot_exact_lhs(tri, ld)
        pdec = jnp.exp(cum)
        pinv = jnp.exp(-cum)
        p_last[c] = pdec[C - 1:C, :]
        a_s[c] = stack(-kk[sl] * jnp.exp(cum - ld)).astype(BF16)
        b_s[c] = stack(kk[sl] * a[sl] * pinv).astype(BF16)
        k_s[c] = stack(k[sl] * pinv).astype(BF16)
        r_s[c] = stack(r[sl] * pdec)
        v_s[c] = stack(v[sl]).astype(BF16)
        gram = _dot(jnp.concatenate([a_s[c], r_s[c].astype(BF16)], axis=0),
                    jnp.concatenate([b_s[c], k_s[c]], axis=0), NT)
        m[c] = jnp.where(strict_lower, gram[:hc, :hc], 0.0)
        lak[c] = jnp.where(strict_lower, gram[:hc, hc:], 0.0)
        lrb[c] = jnp.where(lower, gram[hc:, :hc], 0.0)
        lrk[c] = jnp.where(lower, gram[hc:, hc:], 0.0)

    x = [eye + m[c] for c in chunks]
    power = 2
    while power < C:
        m = [_dot1(m[c], m[c]) for c in chunks]
        x = [x[c] + _dot1(x[c], m[c]) for c in chunks]
        power *= 2
    lakv = [_dot1(lak[c], v_s[c]) for c in chunks]
    lrkv = [_dot1(lrk[c], v_s[c]) for c in chunks]
    w1 = [_dot1(x[c], lakv[c]) for c in chunks]
    a2 = [_dot1(x[c], a_s[c]) for c in chunks]
    r2 = [r_s[c] + _dot1(lrb[c], a2[c]) for c in chunks]
    y0 = [_dot1(lrb[c], w1[c]) + lrkv[c] for c in chunks]
    t1 = [_dot1(a2[c], b_s[c], TN) for c in chunks]
    t0 = [_dot1(w1[c], b_s[c], TN) + _dot(v_s[c], k_s[c], TN) for c in chunks]

    s = s_ref[...]
    for c in chunks:
        y = _dot1(r2[c], s, NT) + y0[c]
        y_sc[c * C:(c + 1) * C, :] = sum(y[h * C:(h + 1) * C] for h in range(N_HEADS))
        s = (s + _dot1(s, t1[c]) + t0[c]) * p_last[c]
    s_ref[...] = s

    y = y_sc[...]
    inv_n = 1.0 / HEAD_DIM
    mean = _dot_exact_rhs(y, bd) * inv_n
    yc = y - mean
    var = _dot_exact_rhs(yc * yc, bd) * inv_n
    yn = yc * lax.rsqrt(var + RWKV_GN_EPS) * lnw_ref[...] + lnb_ref[...]
    yn = yn + _dot_exact_rhs(r * k * rk_ref[...], bd) * v
    y_ref[0] = (yn * g).astype(y_ref.dtype)


def _rwkv(c3, col_block, v_first, prm, tb=512):
    bsz, t, _ = c3.shape
    has_vgate = v_first is not None
    blk = lambda w: pl.BlockSpec((1, tb, w), lambda b, i: (b, i, 0))
    cblk = pl.BlockSpec((1, tb, RWKV_COLS), lambda b, i: (b, i, col_block))
    full = lambda a: pl.BlockSpec(a.shape, lambda b, i: (0,) * a.ndim)
    names = ["mu", "w0", "w2", "a0", "a2", "g2", "k_k", "k_a", "r_k", "ln_w", "ln_b"]
    if has_vgate:
        names += ["v0", "v1", "v2"]
    args = [c3] + ([v_first] if has_vgate else []) + [prm[n] for n in names]
    in_specs = [cblk] + ([blk(WIDTH)] if has_vgate else []) + [full(prm[n]) for n in names]
    return pl.pallas_call(
        functools.partial(_rwkv_kernel, n_chunks=tb // RWKV_CHUNK, has_vgate=has_vgate),
        grid=(bsz, t // tb),
        in_specs=in_specs,
        out_specs=[blk(WIDTH), blk(WIDTH)],
        out_shape=[jax.ShapeDtypeStruct((bsz, t, WIDTH), BF16), jax.ShapeDtypeStruct((bsz, t, WIDTH), F32)],
        scratch_shapes=[pltpu.VMEM((WIDTH, WIDTH), F32), pltpu.VMEM((1, RWKV_COLS), F32),
                        pltpu.VMEM((tb, WIDTH), F32)],
        compiler_params=_params(("parallel", "arbitrary")), name="rwkv7",
    )(*args)


def _fox_cum_kernel(f_ref, bias_ref, c_ref, carry_ref, *, tb):
    @pl.when(pl.program_id(1) == 0)
    def _():
        carry_ref[...] = jnp.zeros_like(carry_ref)

    logf = -_softplus(-(f_ref[0] + bias_ref[...]))
    c = _dot_exact_lhs(_lower_tri_ones(tb), logf) + carry_ref[...]
    carry_ref[...] = c[tb - 1:tb, :]
    for h in range(N_HEADS):
        pick = (_iota((LANES, LANES), 0) == h).astype(BF16)
        c1, c2, c3 = _split3(c)
        c_ref[0, h] = LOG2E * (_dot(c1, pick) + (_dot(c2, pick) + _dot(c3, pick)))


def _fox_cum(main3, col_block, bias_row, tb=512):
    bsz, t, _ = main3.shape
    return pl.pallas_call(
        functools.partial(_fox_cum_kernel, tb=tb),
        grid=(bsz, t // tb),
        in_specs=[pl.BlockSpec((1, tb, LANES), lambda b, i: (b, i, col_block)),
                  pl.BlockSpec((1, LANES), lambda b, i: (0, 0))],
        out_specs=pl.BlockSpec((1, N_HEADS, tb, LANES), lambda b, i: (b, 0, i, 0)),
        out_shape=jax.ShapeDtypeStruct((bsz, N_HEADS, t, LANES), F32),
        scratch_shapes=[pltpu.VMEM((1, LANES), F32)],
        compiler_params=_params(("parallel", "arbitrary")), name="fox_cumgate",
    )(main3, bias_row)


ATT_QB = 512
ATT_KB = 512


def _stack_heads(q):
    lane_head = _iota((1, WIDTH), 1) // HEAD_DIM
    zero = jnp.zeros_like(q)
    return jnp.concatenate([jnp.where(lane_head == h, q, zero) for h in range(N_HEADS)], axis=0)


def _softmax_stage_t(s_t, m_sc, l_sc):
    m_old = m_sc[...]
    m_new = jnp.maximum(m_old, jnp.max(s_t, axis=0, keepdims=True))
    alpha = jnp.exp2(m_old - m_new)
    p_t = jnp.exp2(s_t - m_new)
    l_sc[...] = alpha * l_sc[...] + jnp.sum(p_t, axis=0, keepdims=True)
    m_sc[...] = m_new
    return p_t.astype(BF16), alpha


def _pv_stage_t(vt_blk, p_t, alpha, acc_sc):
    for h in range(N_HEADS):
        cs = slice(h * ATT_QB, (h + 1) * ATT_QB)
        pv = _dot(vt_blk[h * HEAD_DIM:(h + 1) * HEAD_DIM, :], p_t[:, cs])
        acc_sc[h] = alpha[:, cs] * acc_sc[h] + pv


def _attention_t(n_blocks, qk, post, post_last, vt_ref, bufs, m_sc, l_sc, acc_sc):
    KB = ATT_KB
    s_buf, p_buf, a_buf = bufs[0:2], bufs[2:4], bufs[4:6]
    _softmax_init_t(m_sc, l_sc, acc_sc)

    def vt_blk(j):
        return vt_ref[0, :, pl.ds(pl.multiple_of(j * KB, KB), KB)]

    def pending(j, par):
        _pv_stage_t(vt_blk(jnp.maximum(j - 1, 0)), p_buf[1 - par][...], a_buf[1 - par][...], acc_sc)

    def step(j, par):
        raw = s_buf[par][...]
        s_buf[1 - par][...] = qk(j + 1)
        p_t, alpha = _softmax_stage_t(post(j, raw), m_sc, l_sc)
        pending(j, par)
        p_buf[par][...] = p_t
        a_buf[par][...] = alpha

    def final(j, par):
        p_t, alpha = _softmax_stage_t(post_last(j, s_buf[par][...]), m_sc, l_sc)
        pending(j, par)
        _pv_stage_t(vt_blk(j), p_t, alpha, acc_sc)

    s_buf[0][...] = qk(0)
    p_buf[1][...] = jnp.zeros_like(p_buf[1])
    a_buf[1][...] = jnp.ones_like(a_buf[1])
    pairs = (n_blocks - 1) // 2

    def pair(t, carry):
        step(2 * t, 0)
        step(2 * t + 1, 1)
        return carry

    lax.fori_loop(0, pairs, pair, 0)
    odd_left = (n_blocks - 1) % 2 == 1

    @pl.when(odd_left)
    def _():
        step(2 * pairs, 0)
        final(2 * pairs + 1, 1)

    @pl.when(jnp.logical_not(odd_left))
    def _():
        final(2 * pairs, 0)


def _attention_bufs():
    cols = N_HEADS * ATT_QB
    return ([pltpu.VMEM((ATT_KB, cols), F32)] * 2 + [pltpu.VMEM((ATT_KB, cols), BF16)] * 2
            + [pltpu.VMEM((1, cols), F32)] * 2)


def _softmax_init_t(m_sc, l_sc, acc_sc):
    m_sc[...] = jnp.full_like(m_sc, MASK_VALUE)
    l_sc[...] = jnp.zeros_like(l_sc)
    acc_sc[...] = jnp.zeros_like(acc_sc)


def _softmax_finish_t(l_sc, acc_sc):
    l = l_sc[...]
    out_t = jnp.concatenate([acc_sc[h] / l[:, h * ATT_QB:(h + 1) * ATT_QB] for h in range(N_HEADS)], axis=0)
    return out_t.T


def _fox_kernel(q_ref, k_ref, vt_ref, ck_ref, o_ref, m_sc, l_sc, acc_sc, *bufs):
    QB, KB = ATT_QB, ATT_KB
    i = pl.program_id(1)
    qstack = _stack_heads(q_ref[0])
    qpos = i * QB + _iota((1, N_HEADS * QB), 1) % QB

    def qk(j):
        return _dot(k_ref[0, pl.ds(pl.multiple_of(j * KB, KB), KB), :], qstack, NT)

    def post(j, s_t):
        ks = pl.ds(pl.multiple_of(j * KB, KB), KB)
        rep = lambda c: jnp.concatenate([c] * (QB // LANES), axis=1)
        return jnp.concatenate([s_t[:, h * QB:(h + 1) * QB] - rep(ck_ref[0, h, ks, :]) for h in range(N_HEADS)], axis=1)

    def post_last(j, s_t):
        return jnp.where(j * KB + _iota((KB, 1), 0) <= qpos, post(j, s_t), MASK_VALUE)

    _attention_t((i * QB) // KB + 1, qk, post, post_last, vt_ref, bufs, m_sc, l_sc, acc_sc)
    o_ref[0] = _softmax_finish_t(l_sc, acc_sc).astype(o_ref.dtype)


def _fox(qk3, v_t, c_rep):
    bsz, t, _ = qk3.shape
    cols = N_HEADS * ATT_QB
    return pl.pallas_call(
        _fox_kernel,
        grid=(bsz, t // ATT_QB),
        in_specs=[pl.BlockSpec((1, ATT_QB, WIDTH), lambda b, i: (b, i, 0)),
                  pl.BlockSpec((1, t, WIDTH), lambda b, i: (b, 0, 1)),
                  pl.BlockSpec((1, WIDTH, t), lambda b, i: (b, 0, 0)),
                  pl.BlockSpec((1, N_HEADS, t, LANES), lambda b, i: (b, 0, 0, 0))],
        out_specs=pl.BlockSpec((1, ATT_QB, WIDTH), lambda b, i: (b, i, 0)),
        out_shape=jax.ShapeDtypeStruct((bsz, t, WIDTH), BF16),
        scratch_shapes=[pltpu.VMEM((1, cols), F32), pltpu.VMEM((1, cols), F32),
                        pltpu.VMEM((N_HEADS, HEAD_DIM, ATT_QB), F32)] + _attention_bufs(),
        compiler_params=_params(("parallel", "arbitrary")), name="fox_attention",
    )(qk3, qk3, v_t, c_rep)


def _swap_halves(x, half):
    n = x.shape[-1]
    lower = (_iota((1, n), 1) % (2 * half)) < half
    return jnp.where(lower, pltpu.roll(x, n - half, axis=1), pltpu.roll(x, half, axis=1))


def _dsa_prep_kernel(q_ref, k_ref, v_ref, iq_ref, ik_ref, iw_ref, ch_ref, sh_ref, ci_ref, si_ref,
                     selh_ref, sell_ref, qo_ref, ko_ref, vo_ref, q3_ref, k3_ref, wi_ref):
    ch, sh, ci, si = ch_ref[...], sh_ref[...], ci_ref[...], si_ref[...]

    def rope(x, c, s, half):
        return x * c + _swap_halves(x, half) * s

    qo_ref[0] = (rope(q_ref[0], ch, sh, HEAD_DIM // 2) * (LOG2E * HEAD_DIM ** -0.5)).astype(BF16)
    ko_ref[0] = rope(k_ref[0], ch, sh, HEAD_DIM // 2).astype(BF16)
    vo_ref[0] = v_ref[0].T.astype(BF16)
    qh, ql = _split2(rope(iq_ref[0], ci, si, IDX_DIM // 2) * (IDX_DIM ** -0.5))
    q3_ref[0] = (_dot(qh, selh_ref[...]) + _dot(ql, sell_ref[...])).astype(BF16)
    kh, kl = _split2(rope(ik_ref[0], ci[:, :LANES], si[:, :LANES], IDX_DIM // 2))
    seg = _iota((1, LANES), 1) // IDX_DIM
    k3_ref[0] = jnp.where(seg < 2, kh, jnp.where(seg == 2, kl, jnp.zeros_like(kl)))
    wi_ref[0] = (iw_ref[0] * (IDX_HEADS ** -0.5)).T


def _dsa_prep(main3, idx3, tables, qkv_block0, tb=512):
    bsz, t, _ = main3.shape
    iq_w = IDX_HEADS * IDX_DIM
    r = jnp.arange(iq_w)[:, None]
    c = jnp.arange(IDX_HEADS * LANES)[None, :]
    same = (r // IDX_DIM == c // LANES) & (r % IDX_DIM == c % IDX_DIM)
    seg = (c % LANES) // IDX_DIM
    sel_hi = (same & ((seg == 0) | (seg == 2))).astype(BF16)
    sel_lo = (same & (seg == 1)).astype(BF16)
    mcol = lambda j: pl.BlockSpec((1, tb, WIDTH), lambda b, i, j=j: (b, i, qkv_block0 + j))
    tab = pl.BlockSpec((tb, WIDTH), lambda b, i: (i, 0))
    sel = pl.BlockSpec(sel_hi.shape, lambda b, i: (0, 0))
    out = lambda w: pl.BlockSpec((1, tb, w), lambda b, i: (b, i, 0))
    out_t = lambda w: pl.BlockSpec((1, w, tb), lambda b, i: (b, 0, i))
    shape = lambda dt, w: jax.ShapeDtypeStruct((bsz, t, w), dt)
    return pl.pallas_call(
        _dsa_prep_kernel, grid=(bsz, t // tb),
        in_specs=[mcol(0), mcol(1), mcol(2), out(iq_w),
                  pl.BlockSpec((1, tb, LANES), lambda b, i: (b, i, iq_w // LANES)),
                  pl.BlockSpec((1, tb, LANES), lambda b, i: (b, i, iq_w // LANES + 1)),
                  tab, tab, tab, tab, sel, sel],
        out_specs=[out(WIDTH), out(WIDTH), out_t(WIDTH), out(IDX_HEADS * LANES), out(LANES), out_t(LANES)],
        out_shape=[shape(BF16, WIDTH), shape(BF16, WIDTH), jax.ShapeDtypeStruct((bsz, WIDTH, t), BF16),
                   shape(BF16, IDX_HEADS * LANES), shape(BF16, LANES), jax.ShapeDtypeStruct((bsz, LANES, t), F32)],
        compiler_params=_params(("parallel", "parallel")), name="dsa_prep",
    )(main3, main3, main3, idx3, idx3, idx3, *tables, sel_hi, sel_lo)


def _dsa_kernel(q_ref, k_ref, vt_ref, q3_ref, k3_ref, wt_ref, o_ref, key_sc, tie_sc, m_sc, l_sc, acc_sc,
                *bufs, topk):
    QB, KB = ATT_QB, ATT_KB
    i = pl.program_id(1)
    nkb = (i * QB) // KB + 1
    qpos = i * QB + _iota((1, QB), 1)
    int_min = jnp.int32(-2 ** 31)

    q3 = q3_ref[0]
    qs = jnp.concatenate([q3[:, h * LANES:(h + 1) * LANES] for h in range(IDX_HEADS)], axis=0)
    wt = wt_ref[0]

    def score_block(j):
        ks = pl.ds(pl.multiple_of(j * KB, KB), KB)
        r = _dot(k3_ref[0, ks, :], qs, NT)
        sc = jnp.zeros((KB, QB), F32)
        for h in range(IDX_HEADS):
            sc = sc + jnp.maximum(r[:, h * QB:(h + 1) * QB], 0.0) * wt[h:h + 1, :]
        sc = jnp.where(j * KB + _iota((KB, 1), 0) <= qpos, sc, MASK_VALUE)
        sc = jnp.where(sc == 0.0, 0.0, sc)
        bits = pltpu.bitcast(sc, jnp.int32)
        key_sc[ks, :] = jnp.where(bits < 0, bits ^ jnp.int32(0x7FFFFFFF), bits)

    def score_pair(t, carry):
        score_block(2 * t)
        score_block(2 * t + 1)
        return carry

    lax.fori_loop(0, nkb // 2, score_pair, 0)

    @pl.when(nkb % 2 == 1)
    def _():
        score_block(nkb - 1)

    def count_ge(cand):
        def blk(j, acc):
            base = pl.multiple_of(j * KB, KB)
            for r0 in range(0, KB, 64):
                acc = acc + jnp.where(key_sc[pl.ds(base + r0, 64), :] >= cand, 1.0, 0.0)
            return acc
        acc = lax.fori_loop(0, nkb, blk, jnp.zeros((64, QB), F32))
        return jnp.sum(acc, axis=0, keepdims=True)

    kf = float(topk)

    def bit_step(step, v):
        trial = jnp.where(step == 0, jnp.zeros_like(v), v | (jnp.int32(1) << (31 - step)))
        return jnp.where(count_ge(trial) >= kf, trial, v)

    thr = lax.fori_loop(0, 32, bit_step, jnp.full((1, QB), int_min, jnp.int32))
    need = kf - count_ge(thr + 1)

    lt = _lower_tri_ones(KB)
    qstack = _stack_heads(q_ref[0])

    def qk(j):
        return _dot(k_ref[0, pl.ds(pl.multiple_of(j * KB, KB), KB), :], qstack, NT)

    def post(j, s_t):
        ks = pl.ds(pl.multiple_of(j * KB, KB), KB)
        key = key_sc[ks, :]
        tie = key == thr
        tie_b = jnp.where(tie, 1.0, 0.0).astype(BF16)
        rank = tie_sc[...] + _dot(lt, tie_b)
        tie_sc[...] = rank[KB - 1:KB, :]
        sel = (key > thr) | (tie & (rank <= need))
        sel = sel & (j * KB + _iota((KB, 1), 0) <= qpos)
        bias = jnp.where(sel, 0.0, MASK_VALUE)
        return s_t + jnp.concatenate([bias] * N_HEADS, axis=1)

    tie_sc[...] = jnp.zeros_like(tie_sc)
    _attention_t(nkb, qk, post, post, vt_ref, bufs, m_sc, l_sc, acc_sc)
    o_ref[0] = _softmax_finish_t(l_sc, acc_sc).astype(o_ref.dtype)


def _dsa(q, k, v_t, q3, k3, w_t, topk):
    bsz, t, _ = q.shape
    qblk = lambda w: pl.BlockSpec((1, ATT_QB, w), lambda b, i: (b, i, 0))
    seq = lambda w: pl.BlockSpec((1, t, w), lambda b, i: (b, 0, 0))
    cols = N_HEADS * ATT_QB
    return pl.pallas_call(
        functools.partial(_dsa_kernel, topk=topk),
        grid=(bsz, t // ATT_QB),
        in_specs=[qblk(WIDTH), seq(WIDTH), pl.BlockSpec((1, WIDTH, t), lambda b, i: (b, 0, 0)),
                  qblk(IDX_HEADS * LANES), seq(LANES),
                  pl.BlockSpec((1, IDX_HEADS, ATT_QB), lambda b, i: (b, 0, i))],
        out_specs=qblk(WIDTH),
        out_shape=jax.ShapeDtypeStruct((bsz, t, WIDTH), BF16),
        scratch_shapes=[pltpu.VMEM((t, ATT_QB), jnp.int32), pltpu.VMEM((1, ATT_QB), F32), pltpu.VMEM((1, cols), F32),
                        pltpu.VMEM((1, cols), F32), pltpu.VMEM((N_HEADS, HEAD_DIM, ATT_QB), F32)] + _attention_bufs(),
        compiler_params=_params(("parallel", "arbitrary")), name="dsa_attention",
    )(q, k, v_t, q3, k3, w_t)


def _merge_kernel(x_ref, ya_ref, yb_ref, yc_ref, yd_ref, gpre_ref, wg_ref, wb_ref, wo_ref, g_ref, o_ref):
    d = x_ref.shape[-1]
    x = x_ref[...]
    h = _norm_rows(x, gpre_ref[...]).astype(BF16)
    merged = None
    for n, y_ref in enumerate((ya_ref, yb_ref, yc_ref, yd_ref)):
        gate = _sigmoid(_dot(h, wg_ref[:, n * d:(n + 1) * d]))
        term = gate * _dot(y_ref[...], wb_ref[n])
        merged = term if merged is None else merged + term
    mix = _dot(merged.astype(BF16), wo_ref[...])
    o_ref[...] = x + _norm_rows(mix, g_ref[...])


def _merge(x2, ys, g_pre, w_gate, w_branch, w_out, layer, g_post, tm=256):
    n, d = x2.shape
    row = lambda w: pl.BlockSpec((tm, w), lambda i: (i, 0))
    vec = pl.BlockSpec((1, d), lambda i: (0, 0))
    return pl.pallas_call(
        _merge_kernel, grid=(n // tm,),
        in_specs=[row(d)] + [row(WIDTH)] * 4 + [vec, pl.BlockSpec((None,) + w_gate.shape[1:], lambda i: (layer, 0, 0)),
                                                 pl.BlockSpec((None,) + w_branch.shape[1:], lambda i: (layer, 0, 0, 0)),
                                                 pl.BlockSpec((None,) + w_out.shape[1:], lambda i: (layer, 0, 0)), vec],
        out_specs=row(d),
        out_shape=jax.ShapeDtypeStruct((n, d), F32),
        compiler_params=_params(("parallel",)), name="gated_merge",
    )(x2, *ys, g_pre.reshape(1, d), w_gate, w_branch, w_out, g_post.reshape(1, d))


MLP_VMEM_LIMIT = 56 * 1024 * 1024


def _mlp_kernel(x_ref, gpre_ref, wu_ref, wd_ref, gpost_ref, o_ref, *, tf):
    x = x_ref[...]
    h = _norm_rows(x, gpre_ref[...]).astype(BF16)
    acc = None
    for k in range(wu_ref.shape[1] // tf):
        u = jnp.maximum(_dot(h, wu_ref[:, k * tf:(k + 1) * tf]), 0.0)
        part = _dot((u * u).astype(BF16), wd_ref[k * tf:(k + 1) * tf, :])
        acc = part if acc is None else acc + part
    o_ref[...] = x + _norm_rows(acc, gpost_ref[...])


def _mlp(x2, g_pre, w_up, w_down, layer, g_post, tm=512, tf=1024):
    n, d = x2.shape
    return pl.pallas_call(
        functools.partial(_mlp_kernel, tf=tf), grid=(n // tm,),
        in_specs=[pl.BlockSpec((tm, d), lambda i: (i, 0)), pl.BlockSpec((1, d), lambda i: (0, 0)),
                  pl.BlockSpec((None,) + w_up.shape[1:], lambda i: (layer, 0, 0)),
                  pl.BlockSpec((None,) + w_down.shape[1:], lambda i: (layer, 0, 0)),
                  pl.BlockSpec((1, d), lambda i: (0, 0))],
        out_specs=pl.BlockSpec((tm, d), lambda i: (i, 0)),
        out_shape=jax.ShapeDtypeStruct((n, d), F32),
        compiler_params=pltpu.CompilerParams(dimension_semantics=("parallel",), vmem_limit_bytes=MLP_VMEM_LIMIT),
        name="mlp",
    )(x2, g_pre.reshape(1, d), w_up, w_down, g_post.reshape(1, d))


def _rope_tables(t, dim, groups):
    inv = 1.0 / (ROPE_THETA ** (jnp.arange(0, dim, 2, dtype=F32) / dim))
    ang = jnp.arange(t, dtype=F32)[:, None] * inv[None, :]
    cos, sin = jnp.cos(ang), jnp.sin(ang)
    return jnp.tile(jnp.concatenate([cos, cos], axis=1), (1, groups)), jnp.tile(jnp.concatenate([-sin, sin], axis=1), (1, groups))


def _pad_rows(w, rows, offset):
    return jnp.zeros((rows, w.shape[1]), w.dtype).at[offset:offset + w.shape[0]].set(w)


def _pad_cols(w, cols):
    return jnp.pad(w, ((0, 0), (0, cols - w.shape[1])))


def kernel(x, norm_mix_pre, norm_mix_post, norm_mlp_pre, norm_mlp_post, w_in, w_branch, w_out, hgrn_lb_logits, hgrn_norm_w, fox_f_bias, rwkv_mu, rwkv_w0, rwkv_w2, rwkv_a0, rwkv_a2, rwkv_g2, rwkv_k_k, rwkv_k_a, rwkv_r_k, rwkv_ln_w, rwkv_ln_b, rwkv_v0, rwkv_v1, rwkv_v2, w_up, w_down):
    bsz, t, d = x.shape
    n = bsz * t
    depth = w_in.shape[0]
    W = WIDTH
    topk = min(TOPK_MAX, t // 4)

    lb_soft = jax.nn.softmax(hgrn_lb_logits.astype(F32), axis=0)
    lower_bounds = jnp.cumsum(lb_soft, axis=0) - lb_soft[0:1]
    tables = _rope_tables(t, HEAD_DIM, N_HEADS) + _rope_tables(t, IDX_DIM, IDX_HEADS)

    assert d == 4 * W and t % ATT_QB == 0 and t % ATT_KB == 0
    lora_a, lora_g = rwkv_w2.shape[1], rwkv_w2.shape[1] + rwkv_a2.shape[1]
    o_a, o_iq = 0, 7 * W
    o_ik, o_iw = o_iq + IDX_HEADS * IDX_DIM, o_iq + IDX_HEADS * IDX_DIM + IDX_DIM
    o_c = o_iw + IDX_HEADS
    c_cols = RWKV_COLS
    o_d = o_c + c_cols
    o_df = o_d + 3 * W
    o_g = o_df + N_HEADS
    main_pad = 2 * LANES
    main_blocks = dict(b_qkv=4, c=2, df=(7 * W + c_cols) // LANES)
    assert main_blocks["c"] * c_cols == 7 * W

    w_in_bf16 = w_in.astype(BF16)
    cols = lambda w, o, width: w[:, :, o:o + width]
    pad_last = lambda w, width: jnp.pad(w, ((0, 0), (0, 0), (0, width - w.shape[2])))
    w_main = jnp.concatenate([
        cols(w_in_bf16, o_a, 7 * W), cols(w_in_bf16, o_c, c_cols), pad_last(cols(w_in_bf16, o_df, N_HEADS), LANES),
        jnp.zeros((depth, d, main_pad), BF16)], axis=2)
    w_idx = jnp.concatenate([
        cols(w_in, o_iq, W), jnp.tile(cols(w_in, o_ik, IDX_DIM), (1, 1, LANES // IDX_DIM)),
        pad_last(cols(w_in, o_iw, IDX_HEADS), LANES)], axis=2)
    w_fox = jnp.concatenate([(cols(w_in, o_d, W) * (LOG2E * HEAD_DIM ** -0.5)).astype(BF16),
                             cols(w_in_bf16, o_d + W, 2 * W)], axis=2)
    proj_weights = (w_main, w_fox) + _split2(w_idx)
    w_gate = cols(w_in_bf16, o_g, 4 * d)
    w_branch_bf16, w_out_bf16 = w_branch.astype(BF16), w_out.astype(BF16)
    w_up_bf16, w_down_bf16 = w_up.astype(BF16), w_down.astype(BF16)

    x2 = x.reshape(n, d)
    v_first = None
    for l in range(depth):
        main2, fox_qk, fox_vt, idx2 = _projections(x2, t, norm_mix_pre[l], proj_weights, l)
        main3 = main2.reshape(bsz, t, -1)

        y_a = _hgrn(main3, lower_bounds[l], hgrn_norm_w[l])

        y_b = _dsa(*_dsa_prep(main3, idx2.reshape(bsz, t, -1), tables, main_blocks["b_qkv"]), topk)

        prm = dict(
            mu=_pad_cols(rwkv_mu[l][None, :], RWKV_COLS), w0=rwkv_w0[l][None, :],
            w2=_pad_rows(rwkv_w2[l], RWKV_LORA_LANES, 0), a0=rwkv_a0[l][None, :],
            a2=_pad_rows(rwkv_a2[l], RWKV_LORA_LANES, lora_a), g2=_pad_rows(rwkv_g2[l], RWKV_LORA_LANES, lora_g),
            k_k=rwkv_k_k[l][None, :], k_a=rwkv_k_a[l][None, :], r_k=rwkv_r_k[l].reshape(1, W),
            ln_w=rwkv_ln_w[l][None, :], ln_b=rwkv_ln_b[l][None, :])
        if l > 0:
            prm.update(v0=rwkv_v0[l - 1][None, :], v1=_pad_cols(rwkv_v1[l - 1], LANES),
                       v2=_pad_rows(rwkv_v2[l - 1], LANES, 0))
        y_c, v_c = _rwkv(main3, main_blocks["c"], v_first, prm)
        if l == 0:
            v_first = v_c

        c_rep = _fox_cum(main3, main_blocks["df"], _pad_cols(fox_f_bias[l][None, :], LANES))
        y_d = _fox(fox_qk.reshape(bsz, t, 2 * W), fox_vt, c_rep)

        ys = [y.reshape(n, W) for y in (y_a, y_b, y_c, y_d)]
        x2 = _merge(x2, ys, norm_mix_pre[l], w_gate, w_branch_bf16, w_out_bf16, l, norm_mix_post[l])
        x2 = _mlp(x2, norm_mlp_pre[l], w_up_bf16, w_down_bf16, l, norm_mlp_post[l])
    return x2.reshape(bsz, t, d)
```

```python
import functools

import jax
import jax.numpy as jnp
from jax import lax
from jax.experimental import pallas as pl
from jax.experimental.pallas import tpu as pltpu

F32 = jnp.float32
BF16 = jnp.bfloat16

HEAD_DIM = 64
N_HEADS = 4
WIDTH = N_HEADS * HEAD_DIM
IDX_HEADS = 8
IDX_DIM = 32
TOPK_MAX = 256
ROPE_THETA = 10000.0
NORM_EPS = 1e-6
RWKV_GN_EPS = 64e-5
MASK_VALUE = -1e30
LOG2E = 1.4426950408889634
RWKV_LORA_LANES = 128
RWKV_COLS = 3 * WIDTH + RWKV_LORA_LANES

LANES = 128
VMEM_LIMIT = 48 * 1024 * 1024

NN = ((1,), (0,))
NT = ((1,), (1,))
TN = ((0,), (0,))


def _dot(a, b, dims=NN):
    return lax.dot_general(a, b, (dims, ((), ())), preferred_element_type=F32)


def _split2(x):
    hi = x.astype(BF16)
    lo = (x - hi.astype(F32)).astype(BF16)
    return hi, lo


def _split3(x):
    hi = x.astype(BF16)
    r1 = x - hi.astype(F32)
    mid = r1.astype(BF16)
    lo = (r1 - mid.astype(F32)).astype(BF16)
    return hi, mid, lo


def _dot3(a, b, dims=NN):
    ah, al = _split2(a)
    bh, bl = _split2(b)
    return _dot(ah, bh, dims) + (_dot(ah, bl, dims) + _dot(al, bh, dims))


def _dot_exact_lhs(a_bf16, b, dims=NN):
    b1, b2, b3 = _split3(b)
    return _dot(a_bf16, b1, dims) + (_dot(a_bf16, b2, dims) + _dot(a_bf16, b3, dims))


def _dot_exact_rhs(a, b_bf16, dims=NN):
    a1, a2 = _split2(a)
    return _dot(a1, b_bf16, dims) + _dot(a2, b_bf16, dims)


def _iota(shape, dim):
    return lax.broadcasted_iota(jnp.int32, shape, dim)


def _head_block_ones(n, group):
    return (_iota((n, n), 0) // group == _iota((n, n), 1) // group).astype(BF16)


def _lower_tri_ones(n):
    return (_iota((n, n), 0) >= _iota((n, n), 1)).astype(BF16)


def _sigmoid(x):
    return 1.0 / (1.0 + jnp.exp(-x))


def _softplus(x):
    return jnp.maximum(x, 0.0) + jnp.log(1.0 + jnp.exp(-jnp.abs(x)))


def _params(sem):
    return pltpu.CompilerParams(dimension_semantics=sem, vmem_limit_bytes=VMEM_LIMIT)


def _norm_rows(x, g):
    return x * lax.rsqrt(jnp.mean(x * x, axis=-1, keepdims=True) + NORM_EPS) * g


def _projections_kernel(x_ref, g_ref, wm_ref, wf_ref, wih_ref, wil_ref, om_ref, of_ref, ovt_ref, oi_ref):
    h = _norm_rows(x_ref[...], g_ref[...])
    hh, hl = _split2(h)
    om_ref[...] = _dot(hh, wm_ref[...])
    fox = _dot(hh, wf_ref[...])
    of_ref[...] = fox[:, :2 * WIDTH].astype(of_ref.dtype)
    ovt_ref[0] = fox[:, 2 * WIDTH:].T.astype(ovt_ref.dtype)
    oi_ref[...] = _dot(hh, wih_ref[...]) + (_dot(hh, wil_ref[...]) + _dot(hl, wih_ref[...]))


def _projections(x2, t, g, weights, layer, tm=512):
    n, d = x2.shape
    tiles = t // tm
    main_cols, idx_cols = weights[0].shape[2], weights[2].shape[2]
    rows = lambda c: pl.BlockSpec((tm, c), lambda i: (i, 0))
    return pl.pallas_call(
        _projections_kernel, grid=(n // tm,),
        in_specs=[rows(d), pl.BlockSpec((1, d), lambda i: (0, 0))]
        + [pl.BlockSpec((None,) + w.shape[1:], lambda i: (layer, 0, 0)) for w in weights],
        out_specs=[rows(main_cols), rows(2 * WIDTH),
                   pl.BlockSpec((1, WIDTH, tm), lambda i: (i // tiles, 0, i % tiles)), rows(idx_cols)],
        out_shape=[jax.ShapeDtypeStruct((n, main_cols), F32), jax.ShapeDtypeStruct((n, 2 * WIDTH), BF16),
                   jax.ShapeDtypeStruct((n // t, WIDTH, t), BF16), jax.ShapeDtypeStruct((n, idx_cols), F32)],
        compiler_params=_params(("parallel",)), name="norm_proj",
    )(x2, g.reshape(1, d), *weights)


HGRN_CHUNK = 64
HGRN_GROUP = 8


def _dot1(a, b, dims=NN):
    return _dot(a.astype(BF16), b.astype(BF16), dims)


def _hgrn_kernel(q_ref, f_ref, i_ref, g_ref, lb_ref, nw_ref, o_ref, st_ref, *, n_chunks):
    C, G = HGRN_CHUNK, HGRN_GROUP

    @pl.when(pl.program_id(1) == 0)
    def _():
        st_ref[...] = jnp.zeros_like(st_ref)

    lb = lb_ref[...]
    nw = nw_ref[...]
    tri = _lower_tri_ones(C)
    bd = _head_block_ones(WIDTH, HEAD_DIM)
    bd_mask = _iota((WIDTH, WIDTH), 0) // HEAD_DIM == _iota((WIDTH, WIDTH), 1) // HEAD_DIM
    row_top = _iota((G, 1), 0)
    lane_head = _iota((1, WIDTH), 1) // HEAD_DIM

    def chunk(c, carry):
        sl = pl.ds(pl.multiple_of(c * C, C), C)
        fl = f_ref[0, sl, :]
        f = lb + (1.0 - lb) * _sigmoid(fl)
        k = (1.0 - lb) * _sigmoid(-fl)
        b = _dot_exact_lhs(tri, jnp.log(f))
        q = q_ref[0, sl, :] * (HEAD_DIM ** -0.5)
        v = i_ref[0, sl, :]
        st = st_ref[...]
        o_inter = _dot1(q * jnp.exp(b), st, NT)

        n_groups = C // G
        o_rows = [o_inter[g * G:(g + 1) * G] for g in range(n_groups)]
        pieces = []
        for g in range(n_groups):
            s0 = g * G
            q_top, b_top, k_top = q[s0:s0 + G], b[s0:s0 + G], k[s0:s0 + G]
            for u in range(G):
                top = q_top * k_top[u:u + 1] * jnp.exp(jnp.minimum(b_top - b_top[u:u + 1], 0.0))
                pieces.append(jnp.where(row_top >= u, top, 0.0))
        r = _dot(jnp.concatenate(pieces, axis=0).astype(BF16), bd)
        att, v_st = [], []
        for g in range(n_groups - 1):
            s0 = g * G
            b_top, k_top, v_top = b[s0:s0 + G], k[s0:s0 + G], v[s0:s0 + G]
            b_ref = b_top[G - 1:G]
            qp = (q[s0 + G:] * jnp.exp(b[s0 + G:] - b_ref)).astype(BF16)
            kp = k_top * jnp.exp(b_ref - b_top)
            k_st = jnp.concatenate([jnp.where(lane_head == h, kp, 0.0) for h in range(N_HEADS)], axis=0)
            v_st.append(jnp.concatenate([jnp.where(lane_head == h, v_top, 0.0) for h in range(N_HEADS)], axis=0))
            att.append(_dot(qp, k_st.astype(BF16), NT))
        for g in range(n_groups):
            v_top = v[g * G:(g + 1) * G]
            for u in range(G):
                o_rows[g] = o_rows[g] + r[(g * G + u) * G:(g * G + u + 1) * G] * v_top[u:u + 1]
        for g in range(n_groups - 1):
            rest = _dot(att[g].astype(BF16), v_st[g].astype(BF16))
            for g2 in range(g + 1, n_groups):
                o_rows[g2] = o_rows[g2] + rest[(g2 - g - 1) * G:(g2 - g) * G]
        o = jnp.concatenate(o_rows, axis=0)

        b_last = b[C - 1:C, :]
        upd = _dot1(v, k * jnp.exp(b_last - b), TN)
        st_ref[...] = st * jnp.exp(b_last) + jnp.where(bd_mask, upd, 0.0)

        ms = _dot_exact_rhs(o * o, bd) * (1.0 / HEAD_DIM)
        on = o * lax.rsqrt(ms + NORM_EPS) * nw
        gl = g_ref[0, sl, :]
        o_ref[0, sl, :] = (on * (gl * _sigmoid(gl))).astype(o_ref.dtype)
        return carry

    lax.fori_loop(0, n_chunks, chunk, 0, unroll=True)


def _hgrn(main3, lb, norm_w, tb=512):
    bsz, t, _ = main3.shape
    col = lambda j: pl.BlockSpec((1, tb, WIDTH), lambda b, i, j=j: (b, i, j))
    vec = pl.BlockSpec((1, WIDTH), lambda b, i: (0, 0))
    return pl.pallas_call(
        functools.partial(_hgrn_kernel, n_chunks=tb // HGRN_CHUNK),
        grid=(bsz, t // tb),
        in_specs=[col(0), col(1), col(2), col(3), vec, vec],
        out_specs=pl.BlockSpec((1, tb, WIDTH), lambda b, i: (b, i, 0)),
        out_shape=jax.ShapeDtypeStruct((bsz, t, WIDTH), BF16),
        scratch_shapes=[pltpu.VMEM((WIDTH, WIDTH), F32)],
        compiler_params=_params(("parallel", "arbitrary")), name="hgrn2",
    )(main3, main3, main3, main3, lb.reshape(1, WIDTH), jnp.tile(norm_w, N_HEADS).reshape(1, WIDTH))


RWKV_CHUNK = 64


def _rwkv_kernel(*refs, n_chunks, has_vgate):
    if has_vgate:
        (c_ref, vf_ref, mu_ref, w0_ref, w2_ref, a0_ref, a2_ref, g2_ref, kkw_ref, kaw_ref, rk_ref,
         lnw_ref, lnb_ref, v0_ref, v1_ref, v2_ref, y_ref, vout_ref, s_ref, prev_ref, y_sc) = refs
    else:
        (c_ref, mu_ref, w0_ref, w2_ref, a0_ref, a2_ref, g2_ref, kkw_ref, kaw_ref, rk_ref,
         lnw_ref, lnb_ref, y_ref, vout_ref, s_ref, prev_ref, y_sc) = refs
    C = RWKV_CHUNK
    W = WIDTH
    tb = n_chunks * C

    @pl.when(pl.program_id(1) == 0)
    def _():
        s_ref[...] = jnp.zeros_like(s_ref)
        prev_ref[...] = jnp.zeros_like(prev_ref)

    p = c_ref[0]
    shifted = jnp.where(_iota((tb, 1), 0) == 0, prev_ref[...], pltpu.roll(p, 1, axis=0))
    prev_ref[...] = p[tb - 1:tb, :]
    xm = p + (shifted - p) * mu_ref[...]
    r = xm[:, 0:W]
    k = xm[:, W:2 * W]
    v = xm[:, 2 * W:3 * W]
    lora = xm[:, 3 * W:3 * W + RWKV_LORA_LANES]

    bd = _head_block_ones(W, HEAD_DIM)
    w_log = -_softplus(-(w0_ref[...] + _dot3(jnp.tanh(lora), w2_ref[...]))) - 0.5
    log_decay = -jnp.exp(w_log)
    a = _sigmoid(a0_ref[...] + _dot3(lora, a2_ref[...]))
    g = _dot3(_sigmoid(lora), g2_ref[...])
    if has_vgate:
        vg = _dot3(_dot3(v, v1_ref[...]), v2_ref[...])
        v = v + (vf_ref[0] - v) * _sigmoid(v0_ref[...] + vg)
    vout_ref[0] = v
    kk = k * kkw_ref[...]
    kk = kk * lax.rsqrt(jnp.maximum(_dot_exact_rhs(kk * kk, bd), 1e-24))
    k = k * (1.0 + (a - 1.0) * kaw_ref[...])

    tri = _lower_tri_ones(C)
    lane_head = _iota((1, W), 1) // HEAD_DIM
    hc = N_HEADS * C
    rt = _iota((hc, hc), 0) % C
    ct = _iota((hc, hc), 1) % C
    strict_lower = rt > ct
    lower = rt >= ct
    eye = (_iota((hc, hc), 0) == _iota((hc, hc), 1)).astype(F32)

    def stack(m):
        return jnp.concatenate([jnp.where(lane_head == h, m, 0.0) for h in range(N_HEADS)], axis=0)

    chunks = range(n_chunks)
    a_s, b_s, k_s, r_s, v_s, p_last, m, lak, lrb, lrk = ([None] * n_chunks for _ in range(10))
    for c in chunks:
        sl = slice(c * C, (c + 1) * C)
        ld = log_decay[sl]
        cum = _dot_exact_lhs(tri, ld)
        pdec = jnp.exp(cum)
        pinv = jnp.exp(-cum)
        p_last[c] = pdec[C - 1:C, :]
        a_s[c] = stack(-kk[sl] * jnp.exp(cum - ld)).astype(BF16)
        b_s[c] = stack(kk[sl] * a[sl] * pinv).astype(BF16)
        k_s[c] = stack(k[sl] * pinv).astype(BF16)
        r_s[c] = stack(r[sl] * pdec)
        v_s[c] = stack(v[sl]).astype(BF16)
        gram = _dot(jnp.concatenate([a_s[c], r_s[c].astype(BF16)], axis=0),
                    jnp.concatenate([b_s[c], k_s[c]], axis=0), NT)
        m[c] = jnp.where(strict_lower, gram[:hc, :hc], 0.0)
        lak[c] = jnp.where(strict_lower, gram[:hc, hc:], 0.0)
        lrb[c] = jnp.where(lower, gram[hc:, :hc], 0.0)
        lrk[c] = jnp.where(lower, gram[hc:, hc:], 0.0)

    x = [eye + m[c] for c in chunks]
    power = 2
    while power < C:
        m = [_dot1(m[c], m[c]) for c in chunks]
        x = [x[c] + _dot1(x[c], m[c]) for c in chunks]
        power *= 2
    lakv = [_dot1(lak[c], v_s[c]) for c in chunks]
    lrkv = [_dot1(lrk[c], v_s[c]) for c in chunks]
    w1 = [_dot1(x[c], lakv[c]) for c in chunks]
    a2 = [_dot1(x[c], a_s[c]) for c in chunks]
    r2 = [r_s[c] + _dot1(lrb[c], a2[c]) for c in chunks]
    y0 = [_dot1(lrb[c], w1[c]) + lrkv[c] for c in chunks]
    t1 = [_dot1(a2[c], b_s[c], TN) for c in chunks]
    t0 = [_dot1(w1[c], b_s[c], TN) + _dot(v_s[c], k_s[c], TN) for c in chunks]

    s = s_ref[...]
    for c in chunks:
        y = _dot1(r2[c], s, NT) + y0[c]
        y_sc[c * C:(c + 1) * C, :] = sum(y[h * C:(h + 1) * C] for h in range(N_HEADS))
        s = (s + _dot1(s, t1[c]) + t0[c]) * p_last[c]
    s_ref[...] = s

    y = y_sc[...]
    inv_n = 1.0 / HEAD_DIM
    mean = _dot_exact_rhs(y, bd) * inv_n
    yc = y - mean
    var = _dot_exact_rhs(yc * yc, bd) * inv_n
    yn = yc * lax.rsqrt(var + RWKV_GN_EPS) * lnw_ref[...] + lnb_ref[...]
    yn = yn + _dot_exact_rhs(r * k * rk_ref[...], bd) * v
    y_ref[0] = (yn * g).astype(y_ref.dtype)


def _rwkv(c3, col_block, v_first, prm, tb=512):
    bsz, t, _ = c3.shape
    has_vgate = v_first is not None
    blk = lambda w: pl.BlockSpec((1, tb, w), lambda b, i: (b, i, 0))
    cblk = pl.BlockSpec((1, tb, RWKV_COLS), lambda b, i: (b, i, col_block))
    full = lambda a: pl.BlockSpec(a.shape, lambda b, i: (0,) * a.ndim)
    names = ["mu", "w0", "w2", "a0", "a2", "g2", "k_k", "k_a", "r_k", "ln_w", "ln_b"]
    if has_vgate:
        names += ["v0", "v1", "v2"]
    args = [c3] + ([v_first] if has_vgate else []) + [prm[n] for n in names]
    in_specs = [cblk] + ([blk(WIDTH)] if has_vgate else []) + [full(prm[n]) for n in names]
    return pl.pallas_call(
        functools.partial(_rwkv_kernel, n_chunks=tb // RWKV_CHUNK, has_vgate=has_vgate),
        grid=(bsz, t // tb),
        in_specs=in_specs,
        out_specs=[blk(WIDTH), blk(WIDTH)],
        out_shape=[jax.ShapeDtypeStruct((bsz, t, WIDTH), BF16), jax.ShapeDtypeStruct((bsz, t, WIDTH), F32)],
        scratch_shapes=[pltpu.VMEM((WIDTH, WIDTH), F32), pltpu.VMEM((1, RWKV_COLS), F32),
                        pltpu.VMEM((tb, WIDTH), F32)],
        compiler_params=_params(("parallel", "arbitrary")), name="rwkv7",
    )(*args)


def _fox_cum_kernel(f_ref, bias_ref, c_ref, carry_ref, *, tb):
    @pl.when(pl.program_id(1) == 0)
    def _():
        carry_ref[...] = jnp.zeros_like(carry_ref)

    logf = -_softplus(-(f_ref[0] + bias_ref[...]))
    c = _dot_exact_lhs(_lower_tri_ones(tb), logf) + carry_ref[...]
    carry_ref[...] = c[tb - 1:tb, :]
    for h in range(N_HEADS):
        pick = (_iota((LANES, LANES), 0) == h).astype(BF16)
        c1, c2, c3 = _split3(c)
        c_ref[0, h] = LOG2E * (_dot(c1, pick) + (_dot(c2, pick) + _dot(c3, pick)))


def _fox_cum(main3, col_block, bias_row, tb=512):
    bsz, t, _ = main3.shape
    return pl.pallas_call(
        functools.partial(_fox_cum_kernel, tb=tb),
        grid=(bsz, t // tb),
        in_specs=[pl.BlockSpec((1, tb, LANES), lambda b, i: (b, i, col_block)),
                  pl.BlockSpec((1, LANES), lambda b, i: (0, 0))],
        out_specs=pl.BlockSpec((1, N_HEADS, tb, LANES), lambda b, i: (b, 0, i, 0)),
        out_shape=jax.ShapeDtypeStruct((bsz, N_HEADS, t, LANES), F32),
        scratch_shapes=[pltpu.VMEM((1, LANES), F32)],
        compiler_params=_params(("parallel", "arbitrary")), name="fox_cumgate",
    )(main3, bias_row)


ATT_QB = 512
ATT_KB = 512


def _stack_heads(q):
    lane_head = _iota((1, WIDTH), 1) // HEAD_DIM
    zero = jnp.zeros_like(q)
    return jnp.concatenate([jnp.where(lane_head == h, q, zero) for h in range(N_HEADS)], axis=0)


def _softmax_stage_t(s_t, m_sc, l_sc):
    m_old = m_sc[...]
    m_new = jnp.maximum(m_old, jnp.max(s_t, axis=0, keepdims=True))
    alpha = jnp.exp2(m_old - m_new)
    p_t = jnp.exp2(s_t - m_new)
    l_sc[...] = alpha * l_sc[...] + jnp.sum(p_t, axis=0, keepdims=True)
    m_sc[...] = m_new
    return p_t.astype(BF16), alpha


def _pv_stage_t(vt_blk, p_t, alpha, acc_sc):
    for h in range(N_HEADS):
        cs = slice(h * ATT_QB, (h + 1) * ATT_QB)
        pv = _dot(vt_blk[h * HEAD_DIM:(h + 1) * HEAD_DIM, :], p_t[:, cs])
        acc_sc[h] = alpha[:, cs] * acc_sc[h] + pv


def _attention_t(n_blocks, qk, post, post_last, vt_ref, bufs, m_sc, l_sc, acc_sc):
    KB = ATT_KB
    s_buf, p_buf, a_buf = bufs[0:2], bufs[2:4], bufs[4:6]
    _softmax_init_t(m_sc, l_sc, acc_sc)

    def vt_blk(j):
        return vt_ref[0, :, pl.ds(pl.multiple_of(j * KB, KB), KB)]

    def pending(j, par):
        _pv_stage_t(vt_blk(jnp.maximum(j - 1, 0)), p_buf[1 - par][...], a_buf[1 - par][...], acc_sc)

    def step(j, par):
        raw = s_buf[par][...]
        s_buf[1 - par][...] = qk(j + 1)
        p_t, alpha = _softmax_stage_t(post(j, raw), m_sc, l_sc)
        pending(j, par)
        p_buf[par][...] = p_t
        a_buf[par][...] = alpha

    def final(j, par):
        p_t, alpha = _softmax_stage_t(post_last(j, s_buf[par][...]), m_sc, l_sc)
        pending(j, par)
        _pv_stage_t(vt_blk(j), p_t, alpha, acc_sc)

    s_buf[0][...] = qk(0)
    p_buf[1][...] = jnp.zeros_like(p_buf[1])
    a_buf[1][...] = jnp.ones_like(a_buf[1])
    pairs = (n_blocks - 1) // 2

    def pair(t, carry):
        step(2 * t, 0)
        step(2 * t + 1, 1)
        return carry

    lax.fori_loop(0, pairs, pair, 0)
    odd_left = (n_blocks - 1) % 2 == 1

    @pl.when(odd_left)
    def _():
        step(2 * pairs, 0)
        final(2 * pairs + 1, 1)

    @pl.when(jnp.logical_not(odd_left))
    def _():
        final(2 * pairs, 0)


def _attention_bufs():
    cols = N_HEADS * ATT_QB
    return ([pltpu.VMEM((ATT_KB, cols), F32)] * 2 + [pltpu.VMEM((ATT_KB, cols), BF16)] * 2
            + [pltpu.VMEM((1, cols), F32)] * 2)


def _softmax_init_t(m_sc, l_sc, acc_sc):
    m_sc[...] = jnp.full_like(m_sc, MASK_VALUE)
    l_sc[...] = jnp.zeros_like(l_sc)
    acc_sc[...] = jnp.zeros_like(acc_sc)


def _softmax_finish_t(l_sc, acc_sc):
    l = l_sc[...]
    out_t = jnp.concatenate([acc_sc[h] / l[:, h * ATT_QB:(h + 1) * ATT_QB] for h in range(N_HEADS)], axis=0)
    return out_t.T


def _fox_kernel(q_ref, k_ref, vt_ref, ck_ref, o_ref, m_sc, l_sc, acc_sc, *bufs):
    QB, KB = ATT_QB, ATT_KB
    i = pl.program_id(1)
    qstack = _stack_heads(q_ref[0])
    qpos = i * QB + _iota((1, N_HEADS * QB), 1) % QB

    def qk(j):
        return _dot(k_ref[0, pl.ds(pl.multiple_of(j * KB, KB), KB), :], qstack, NT)

    def post(j, s_t):
        ks = pl.ds(pl.multiple_of(j * KB, KB), KB)
        rep = lambda c: jnp.concatenate([c] * (QB // LANES), axis=1)
        return jnp.concatenate([s_t[:, h * QB:(h + 1) * QB] - rep(ck_ref[0, h, ks, :]) for h in range(N_HEADS)], axis=1)

    def post_last(j, s_t):
        return jnp.where(j * KB + _iota((KB, 1), 0) <= qpos, post(j, s_t), MASK_VALUE)

    _attention_t((i * QB) // KB + 1, qk, post, post_last, vt_ref, bufs, m_sc, l_sc, acc_sc)
    o_ref[0] = _softmax_finish_t(l_sc, acc_sc).astype(o_ref.dtype)


def _fox(qk3, v_t, c_rep):
    bsz, t, _ = qk3.shape
    cols = N_HEADS * ATT_QB
    return pl.pallas_call(
        _fox_kernel,
        grid=(bsz, t // ATT_QB),
        in_specs=[pl.BlockSpec((1, ATT_QB, WIDTH), lambda b, i: (b, i, 0)),
                  pl.BlockSpec((1, t, WIDTH), lambda b, i: (b, 0, 1)),
                  pl.BlockSpec((1, WIDTH, t), lambda b, i: (b, 0, 0)),
                  pl.BlockSpec((1, N_HEADS, t, LANES), lambda b, i: (b, 0, 0, 0))],
        out_specs=pl.BlockSpec((1, ATT_QB, WIDTH), lambda b, i: (b, i, 0)),
        out_shape=jax.ShapeDtypeStruct((bsz, t, WIDTH), BF16),
        scratch_shapes=[pltpu.VMEM((1, cols), F32), pltpu.VMEM((1, cols), F32),
                        pltpu.VMEM((N_HEADS, HEAD_DIM, ATT_QB), F32)] + _attention_bufs(),
        compiler_params=_params(("parallel", "arbitrary")), name="fox_attention",
    )(qk3, qk3, v_t, c_rep)


def _swap_halves(x, half):
    n = x.shape[-1]
    lower = (_iota((1, n), 1) % (2 * half)) < half
    return jnp.where(lower, pltpu.roll(x, n - half, axis=1), pltpu.roll(x, half, axis=1))


def _dsa_prep_kernel(q_ref, k_ref, v_ref, iq_ref, ik_ref, iw_ref, ch_ref, sh_ref, ci_ref, si_ref,
                     selh_ref, sell_ref, qo_ref, ko_ref, vo_ref, q3_ref, k3_ref, wi_ref):
    ch, sh, ci, si = ch_ref[...], sh_ref[...], ci_ref[...], si_ref[...]

    def rope(x, c, s, half):
        return x * c + _swap_halves(x, half) * s

    qo_ref[0] = (rope(q_ref[0], ch, sh, HEAD_DIM // 2) * (LOG2E * HEAD_DIM ** -0.5)).astype(BF16)
    ko_ref[0] = rope(k_ref[0], ch, sh, HEAD_DIM // 2).astype(BF16)
    vo_ref[0] = v_ref[0].T.astype(BF16)
    qh, ql = _split2(rope(iq_ref[0], ci, si, IDX_DIM // 2) * (IDX_DIM ** -0.5))
    q3_ref[0] = (_dot(qh, selh_ref[...]) + _dot(ql, sell_ref[...])).astype(BF16)
    kh, kl = _split2(rope(ik_ref[0], ci[:, :LANES], si[:, :LANES], IDX_DIM // 2))
    seg = _iota((1, LANES), 1) // IDX_DIM
    k3_ref[0] = jnp.where(seg < 2, kh, jnp.where(seg == 2, kl, jnp.zeros_like(kl)))
    wi_ref[0] = (iw_ref[0] * (IDX_HEADS ** -0.5)).T


def _dsa_prep(main3, idx3, tables, qkv_block0, tb=512):
    bsz, t, _ = main3.shape
    iq_w = IDX_HEADS * IDX_DIM
    r = jnp.arange(iq_w)[:, None]
    c = jnp.arange(IDX_HEADS * LANES)[None, :]
    same = (r // IDX_DIM == c // LANES) & (r % IDX_DIM == c % IDX_DIM)
    seg = (c % LANES) // IDX_DIM
    sel_hi = (same & ((seg == 0) | (seg == 2))).astype(BF16)
    sel_lo = (same & (seg == 1)).astype(BF16)
    mcol = lambda j: pl.BlockSpec((1, tb, WIDTH), lambda b, i, j=j: (b, i, qkv_block0 + j))
    tab = pl.BlockSpec((tb, WIDTH), lambda b, i: (i, 0))
    sel = pl.BlockSpec(sel_hi.shape, lambda b, i: (0, 0))
    out = lambda w: pl.BlockSpec((1, tb, w), lambda b, i: (b, i, 0))
    out_t = lambda w: pl.BlockSpec((1, w, tb), lambda b, i: (b, 0, i))
    shape = lambda dt, w: jax.ShapeDtypeStruct((bsz, t, w), dt)
    return pl.pallas_call(
        _dsa_prep_kernel, grid=(bsz, t // tb),
        in_specs=[mcol(0), mcol(1), mcol(2), out(iq_w),
                  pl.BlockSpec((1, tb, LANES), lambda b, i: (b, i, iq_w // LANES)),
                  pl.BlockSpec((1, tb, LANES), lambda b, i: (b, i, iq_w // LANES + 1)),
                  tab, tab, tab, tab, sel, sel],
        out_specs=[out(WIDTH), out(WIDTH), out_t(WIDTH), out(IDX_HEADS * LANES), out(LANES), out_t(LANES)],
        out_shape=[shape(BF16, WIDTH), shape(BF16, WIDTH), jax.ShapeDtypeStruct((bsz, WIDTH, t), BF16),
                   shape(BF16, IDX_HEADS * LANES), shape(BF16, LANES), jax.ShapeDtypeStruct((bsz, LANES, t), F32)],
        compiler_params=_params(("parallel", "parallel")), name="dsa_prep",
    )(main3, main3, main3, idx3, idx3, idx3, *tables, sel_hi, sel_lo)


def _dsa_kernel(q_ref, k_ref, vt_ref, q3_ref, k3_ref, wt_ref, o_ref, key_sc, tie_sc, m_sc, l_sc, acc_sc,
                *bufs, topk):
    QB, KB = ATT_QB, ATT_KB
    i = pl.program_id(1)
    nkb = (i * QB) // KB + 1
    qpos = i * QB + _iota((1, QB), 1)
    int_min = jnp.int32(-2 ** 31)

    q3 = q3_ref[0]
    qs = jnp.concatenate([q3[:, h * LANES:(h + 1) * LANES] for h in range(IDX_HEADS)], axis=0)
    wt = wt_ref[0]

    def score_block(j):
        ks = pl.ds(pl.multiple_of(j * KB, KB), KB)
        r = _dot(k3_ref[0, ks, :], qs, NT)
        sc = jnp.zeros((KB, QB), F32)
        for h in range(IDX_HEADS):
            sc = sc + jnp.maximum(r[:, h * QB:(h + 1) * QB], 0.0) * wt[h:h + 1, :]
        sc = jnp.where(j * KB + _iota((KB, 1), 0) <= qpos, sc, MASK_VALUE)
        sc = jnp.where(sc == 0.0, 0.0, sc)
        bits = pltpu.bitcast(sc, jnp.int32)
        key_sc[ks, :] = jnp.where(bits < 0, bits ^ jnp.int32(0x7FFFFFFF), bits)

    def score_pair(t, carry):
        score_block(2 * t)
        score_block(2 * t + 1)
        return carry

    lax.fori_loop(0, nkb // 2, score_pair, 0)

    @pl.when(nkb % 2 == 1)
    def _():
        score_block(nkb - 1)

    def count_ge(cand):
        def blk(j, acc):
            base = pl.multiple_of(j * KB, KB)
            for r0 in range(0, KB, 64):
                acc = acc + jnp.where(key_sc[pl.ds(base + r0, 64), :] >= cand, 1.0, 0.0)
            return acc
        acc = lax.fori_loop(0, nkb, blk, jnp.zeros((64, QB), F32))
        return jnp.sum(acc, axis=0, keepdims=True)

    kf = float(topk)

    def bit_step(step, v):
        trial = jnp.where(step == 0, jnp.zeros_like(v), v | (jnp.int32(1) << (31 - step)))
        return jnp.where(count_ge(trial) >= kf, trial, v)

    thr = lax.fori_loop(0, 32, bit_step, jnp.full((1, QB), int_min, jnp.int32))
    need = kf - count_ge(thr + 1)

    lt = _lower_tri_ones(KB)
    qstack = _stack_heads(q_ref[0])

    def qk(j):
        return _dot(k_ref[0, pl.ds(pl.multiple_of(j * KB, KB), KB), :], qstack, NT)

    def post(j, s_t):
        ks = pl.ds(pl.multiple_of(j * KB, KB), KB)
        key = key_sc[ks, :]
        tie = key == thr
        tie_b = jnp.where(tie, 1.0, 0.0).astype(BF16)
        rank = tie_sc[...] + _dot(lt, tie_b)
        tie_sc[...] = rank[KB - 1:KB, :]
        sel = (key > thr) | (tie & (rank <= need))
        sel = sel & (j * KB + _iota((KB, 1), 0) <= qpos)
        bias = jnp.where(sel, 0.0, MASK_VALUE)
        return s_t + jnp.concatenate([bias] * N_HEADS, axis=1)

    tie_sc[...] = jnp.zeros_like(tie_sc)
    _attention_t(nkb, qk, post, post, vt_ref, bufs, m_sc, l_sc, acc_sc)
    o_ref[0] = _softmax_finish_t(l_sc, acc_sc).astype(o_ref.dtype)


def _dsa(q, k, v_t, q3, k3, w_t, topk):
    bsz, t, _ = q.shape
    qblk = lambda w: pl.BlockSpec((1, ATT_QB, w), lambda b, i: (b, i, 0))
    seq = lambda w: pl.BlockSpec((1, t, w), lambda b, i: (b, 0, 0))
    cols = N_HEADS * ATT_QB
    return pl.pallas_call(
        functools.partial(_dsa_kernel, topk=topk),
        grid=(bsz, t // ATT_QB),
        in_specs=[qblk(WIDTH), seq(WIDTH), pl.BlockSpec((1, WIDTH, t), lambda b, i: (b, 0, 0)),
                  qblk(IDX_HEADS * LANES), seq(LANES),
                  pl.BlockSpec((1, IDX_HEADS, ATT_QB), lambda b, i: (b, 0, i))],
        out_specs=qblk(WIDTH),
        out_shape=jax.ShapeDtypeStruct((bsz, t, WIDTH), BF16),
        scratch_shapes=[pltpu.VMEM((t, ATT_QB), jnp.int32), pltpu.VMEM((1, ATT_QB), F32), pltpu.VMEM((1, cols), F32),
                        pltpu.VMEM((1, cols), F32), pltpu.VMEM((N_HEADS, HEAD_DIM, ATT_QB), F32)] + _attention_bufs(),
        compiler_params=_params(("parallel", "arbitrary")), name="dsa_attention",
    )(q, k, v_t, q3, k3, w_t)


def _merge_kernel(x_ref, ya_ref, yb_ref, yc_ref, yd_ref, gpre_ref, wg_ref, wb_ref, wo_ref, g_ref, o_ref):
    d = x_ref.shape[-1]
    x = x_ref[...]
    h = _norm_rows(x, gpre_ref[...]).astype(BF16)
    merged = None
    for n, y_ref in enumerate((ya_ref, yb_ref, yc_ref, yd_ref)):
        gate = _sigmoid(_dot(h, wg_ref[:, n * d:(n + 1) * d]))
        term = gate * _dot(y_ref[...], wb_ref[n])
        merged = term if merged is None else merged + term
    mix = _dot(merged.astype(BF16), wo_ref[...])
    o_ref[...] = x + _norm_rows(mix, g_ref[...])


def _merge(x2, ys, g_pre, w_gate, w_branch, w_out, layer, g_post, tm=256):
    n, d = x2.shape
    row = lambda w: pl.BlockSpec((tm, w), lambda i: (i, 0))
    vec = pl.BlockSpec((1, d), lambda i: (0, 0))
    return pl.pallas_call(
        _merge_kernel, grid=(n // tm,),
        in_specs=[row(d)] + [row(WIDTH)] * 4 + [vec, pl.BlockSpec((None,) + w_gate.shape[1:], lambda i: (layer, 0, 0)),
                                                 pl.BlockSpec((None,) + w_branch.shape[1:], lambda i: (layer, 0, 0, 0)),
                                                 pl.BlockSpec((None,) + w_out.shape[1:], lambda i: (layer, 0, 0)), vec],
        out_specs=row(d),
        out_shape=jax.ShapeDtypeStruct((n, d), F32),
        compiler_params=_params(("parallel",)), name="gated_merge",
    )(x2, *ys, g_pre.reshape(1, d), w_gate, w_branch, w_out, g_post.reshape(1, d))


MLP_VMEM_LIMIT = 56 * 1024 * 1024


def _mlp_kernel(x_ref, gpre_ref, wu_ref, wd_ref, gpost_ref, o_ref, *, tf):
    x = x_ref[...]
    h = _norm_rows(x, gpre_ref[...]).astype(BF16)
    acc = None
    for k in range(wu_ref.shape[1] // tf):
        u = jnp.maximum(_dot(h, wu_ref[:, k * tf:(k + 1) * tf]), 0.0)
        part = _dot((u * u).astype(BF16), wd_ref[k * tf:(k + 1) * tf, :])
        acc = part if acc is None else acc + part
    o_ref[...] = x + _norm_rows(acc, gpost_ref[...])


def _mlp(x2, g_pre, w_up, w_down, layer, g_post, tm=512, tf=1024):
    n, d = x2.shape
    return pl.pallas_call(
        functools.partial(_mlp_kernel, tf=tf), grid=(n // tm,),
        in_specs=[pl.BlockSpec((tm, d), lambda i: (i, 0)), pl.BlockSpec((1, d), lambda i: (0, 0)),
                  pl.BlockSpec((None,) + w_up.shape[1:], lambda i: (layer, 0, 0)),
                  pl.BlockSpec((None,) + w_down.shape[1:], lambda i: (layer, 0, 0)),
                  pl.BlockSpec((1, d), lambda i: (0, 0))],
        out_specs=pl.BlockSpec((tm, d), lambda i: (i, 0)),
        out_shape=jax.ShapeDtypeStruct((n, d), F32),
        compiler_params=pltpu.CompilerParams(dimension_semantics=("parallel",), vmem_limit_bytes=MLP_VMEM_LIMIT),
        name="mlp",
    )(x2, g_pre.reshape(1, d), w_up, w_down, g_post.reshape(1, d))


def _rope_tables(t, dim, groups):
    inv = 1.0 / (ROPE_THETA ** (jnp.arange(0, dim, 2, dtype=F32) / dim))
    ang = jnp.arange(t, dtype=F32)[:, None] * inv[None, :]
    cos, sin = jnp.cos(ang), jnp.sin(ang)
    return jnp.tile(jnp.concatenate([cos, cos], axis=1), (1, groups)), jnp.tile(jnp.concatenate([-sin, sin], axis=1), (1, groups))


def _pad_rows(w, rows, offset):
    return jnp.zeros((rows, w.shape[1]), w.dtype).at[offset:offset + w.shape[0]].set(w)


def _pad_cols(w, cols):
    return jnp.pad(w, ((0, 0), (0, cols - w.shape[1])))


def kernel(x, norm_mix_pre, norm_mix_post, norm_mlp_pre, norm_mlp_post, w_in, w_branch, w_out, hgrn_lb_logits, hgrn_norm_w, fox_f_bias, rwkv_mu, rwkv_w0, rwkv_w2, rwkv_a0, rwkv_a2, rwkv_g2, rwkv_k_k, rwkv_k_a, rwkv_r_k, rwkv_ln_w, rwkv_ln_b, rwkv_v0, rwkv_v1, rwkv_v2, w_up, w_down):
    bsz, t, d = x.shape
    n = bsz * t
    depth = w_in.shape[0]
    W = WIDTH
    topk = min(TOPK_MAX, t // 4)

    lb_soft = jax.nn.softmax(hgrn_lb_logits.astype(F32), axis=0)
    lower_bounds = jnp.cumsum(lb_soft, axis=0) - lb_soft[0:1]
    tables = _rope_tables(t, HEAD_DIM, N_HEADS) + _rope_tables(t, IDX_DIM, IDX_HEADS)

    assert d == 4 * W and t % ATT_QB == 0 and t % ATT_KB == 0
    lora_a, lora_g = rwkv_w2.shape[1], rwkv_w2.shape[1] + rwkv_a2.shape[1]
    o_a, o_iq = 0, 7 * W
    o_ik, o_iw = o_iq + IDX_HEADS * IDX_DIM, o_iq + IDX_HEADS * IDX_DIM + IDX_DIM
    o_c = o_iw + IDX_HEADS
    c_cols = RWKV_COLS
    o_d = o_c + c_cols
    o_df = o_d + 3 * W
    o_g = o_df + N_HEADS
    main_pad = 2 * LANES
    main_blocks = dict(b_qkv=4, c=2, df=(7 * W + c_cols) // LANES)
    assert main_blocks["c"] * c_cols == 7 * W

    w_cols = jnp.transpose(w_in, (2, 0, 1))

    def cols(o, width, dtype=BF16):
        return jnp.transpose(w_cols[o:o + width].astype(dtype), (1, 2, 0))

    pad_last = lambda w, width: jnp.pad(w, ((0, 0), (0, 0), (0, width - w.shape[2])))
    w_main = jnp.concatenate([
        cols(o_a, 7 * W), cols(o_c, c_cols), pad_last(cols(o_df, N_HEADS), LANES),
        jnp.zeros((depth, d, main_pad), BF16)], axis=2)
    w_idx = jnp.concatenate([
        cols(o_iq, W, F32), jnp.tile(cols(o_ik, IDX_DIM, F32), (1, 1, LANES // IDX_DIM)),
        pad_last(cols(o_iw, IDX_HEADS, F32), LANES)], axis=2)
    w_fox = jnp.concatenate([(cols(o_d, W, F32) * (LOG2E * HEAD_DIM ** -0.5)).astype(BF16),
                             cols(o_d + W, 2 * W)], axis=2)
    proj_weights = (w_main, w_fox) + _split2(w_idx)
    w_gate = cols(o_g, 4 * d)
    w_branch_bf16, w_out_bf16 = w_branch.astype(BF16), w_out.astype(BF16)
    w_up_bf16, w_down_bf16 = w_up.astype(BF16), w_down.astype(BF16)

    x2 = x.reshape(n, d)
    v_first = None
    for l in range(depth):
        main2, fox_qk, fox_vt, idx2 = _projections(x2, t, norm_mix_pre[l], proj_weights, l)
        main3 = main2.reshape(bsz, t, -1)

        y_a = _hgrn(main3, lower_bounds[l], hgrn_norm_w[l])

        y_b = _dsa(*_dsa_prep(main3, idx2.reshape(bsz, t, -1), tables, main_blocks["b_qkv"]), topk)

        prm = dict(
            mu=_pad_cols(rwkv_mu[l][None, :], RWKV_COLS), w0=rwkv_w0[l][None, :],
            w2=_pad_rows(rwkv_w2[l], RWKV_LORA_LANES, 0), a0=rwkv_a0[l][None, :],
            a2=_pad_rows(rwkv_a2[l], RWKV_LORA_LANES, lora_a), g2=_pad_rows(rwkv_g2[l], RWKV_LORA_LANES, lora_g),
            k_k=rwkv_k_k[l][None, :], k_a=rwkv_k_a[l][None, :], r_k=rwkv_r_k[l].reshape(1, W),
            ln_w=rwkv_ln_w[l][None, :], ln_b=rwkv_ln_b[l][None, :])
        if l > 0:
            prm.update(v0=rwkv_v0[l - 1][None, :], v1=_pad_cols(rwkv_v1[l - 1], LANES),
                       v2=_pad_rows(rwkv_v2[l - 1], LANES, 0))
        y_c, v_c = _rwkv(main3, main_blocks["c"], v_first, prm)
        if l == 0:
            v_first = v_c

        c_rep = _fox_cum(main3, main_blocks["df"], _pad_cols(fox_f_bias[l][None, :], LANES))
        y_d = _fox(fox_qk.reshape(bsz, t, 2 * W), fox_vt, c_rep)

        ys = [y.reshape(n, W) for y in (y_a, y_b, y_c, y_d)]
        x2 = _merge(x2, ys, norm_mix_pre[l], w_gate, w_branch_bf16, w_out_bf16, l, norm_mix_post[l])
        x2 = _mlp(x2, norm_mlp_pre[l], w_up_bf16, w_down_bf16, l, norm_mlp_post[l])
    return x2.reshape(bsz, t, d)
```
